```python
import math
import jax, jax.numpy as jnp
from jax import lax
import numpy as np

D_MODEL = 2048
BATCH = 32
SEQ = 256
DEPTH = 2
DEC_BATCH = 8
DEC_SEQ = 1024
PAST_LEN = 256

GRID_W = 64
N_BRANCH = 4
BRANCH_WIDTH = D_MODEL // N_BRANCH
HEAD_DIM = 64
A_QK = 64
A_V = 2 * A_QK
A_HEADS = BRANCH_WIDTH // A_V
B_HEADS = BRANCH_WIDTH // HEAD_DIM
NA_ROWS = 8
NA_COLS = 16
NA_SPAN = 2 * NA_COLS
C_HEADS = BRANCH_WIDTH // HEAD_DIM
C_KV_HEADS = C_HEADS // 4
C_GROUP = C_HEADS // C_KV_HEADS
SWA_WINDOW = 128
BLOCK = 128
HY_WIDTH = BRANCH_WIDTH
HY_ORDER = 2
HY_BANDS = 16
HY_EMB = 1 + 2 * HY_BANDS
HY_FFN = 64
HY_DECAY_MIN = 3.0701
HY_DECAY_MAX = 15.3506
D_FF = 4 * D_MODEL
ROPE_BASE = 10000.0
EPS = 1e-6
NEG_INF = -1e30
A_QK_COLS = 2 * A_HEADS * A_QK
A_V_COLS = A_HEADS * A_V
B_COLS = B_HEADS * HEAD_DIM
C_Q_COLS = C_HEADS * HEAD_DIM
C_KV_COLS = C_KV_HEADS * HEAD_DIM
HY_COLS = 3 * HY_WIDTH
GATE_COLS = N_BRANCH * D_MODEL
IN_SPLITS = (A_QK_COLS, A_QK_COLS, A_V_COLS, B_COLS, B_COLS, B_COLS, C_Q_COLS, C_KV_COLS, C_KV_COLS, HY_COLS, GATE_COLS)
IN_COLS = sum(IN_SPLITS)

kernel_name = 'hybrid_diffusion_prefix_trunk_step'

f32 = jnp.float32


def rmsnorm(x, g):
    xf = x.astype(f32)
    y = xf * lax.rsqrt(jnp.mean(xf * xf, axis=-1, keepdims=True) + EPS)
    return (y * g.astype(f32)).astype(x.dtype)


def adaln(cvec, w_ada, b_ada):
    m = jax.nn.silu(cvec) @ w_ada + b_ada
    return jnp.split(m, 6, axis=-1)


def in_projection(x, shift, scale, g, w_in):
    h = rmsnorm(x, g) * (1 + scale) + shift
    offsets = np.cumsum(IN_SPLITS)[:-1].tolist()
    return jnp.split(h @ w_in, offsets, axis=-1)


def rope_2d(x):
    L, d = x.shape[1], x.shape[-1]
    half = d // 2
    nf = half // 2
    inv = ROPE_BASE ** (-jnp.arange(nf, dtype=f32) / nf)
    t = jnp.arange(L)
    shape = (1, L) + (1,) * (x.ndim - 3) + (nf,)

    def rot(xh, pos):
        ang = (pos.astype(f32)[:, None] * inv[None, :]).reshape(shape)
        cos, sin = jnp.cos(ang), jnp.sin(ang)
        x1, x2 = xh[..., :nf], xh[..., nf:]
        return jnp.concatenate([x1 * cos - x2 * sin, x1 * sin + x2 * cos], axis=-1)

    xf = x.astype(f32)
    out = jnp.concatenate([rot(xf[..., :half], t // GRID_W), rot(xf[..., half:], t % GRID_W)], axis=-1)
    return out.astype(x.dtype)


def over_query_blocks(fn, q):
    B, L = q.shape[:2]
    nb = L // BLOCK
    qb = jnp.moveaxis(q.reshape((B, nb, BLOCK) + q.shape[2:]), 1, 0)
    o = jnp.moveaxis(lax.map(fn, qb), 0, 1)
    return o.reshape((B, L) + o.shape[3:])


def softmax_with_sink(s, sink_gr):
    col = jnp.broadcast_to(sink_gr.astype(f32)[:, :, None, None], s.shape[:-1] + (1,))
    return jax.nn.softmax(jnp.concatenate([s, col], axis=-1), axis=-1)[..., :-1]


def dense_attention(q, k, v, sink):
    G, R, d = q.shape[2:]
    scale = d ** -0.5
    kf, vf = k.astype(f32), v.astype(f32)

    def block(qb):
        s = jnp.einsum('bqgrd,bkgd->bgrqk', qb.astype(f32) * scale, kf)
        pr = jax.nn.softmax(s, axis=-1) if sink is None else softmax_with_sink(s, sink.reshape(G, R))
        return jnp.einsum('bgrqk,bkgd->bqgrd', pr, vf)

    return over_query_blocks(block, q).astype(q.dtype)


def diff_lambda_value(lam_p, li):
    lam_init = 0.8 - 0.6 * math.exp(-0.3 * li)
    lf = lam_p.astype(f32)
    lam = jnp.exp(jnp.sum(lf[0] * lf[1])) - jnp.exp(jnp.sum(lf[2] * lf[3])) + lam_init
    return lam, lam_init


def diff_attention(q, k, v, lam, lam_init, g):
    scale = q.shape[-1] ** -0.5
    kf, vf = k.astype(f32), v.astype(f32)

    def block(qb):
        s = jnp.einsum('bqchd,bkchd->bchqk', qb.astype(f32) * scale, kf)
        pr = jax.nn.softmax(s, axis=-1)
        w = pr[:, 0] - lam * pr[:, 1]
        return jnp.einsum('bhqk,bkhd->bqhd', w, vf)

    o = over_query_blocks(block, q)
    return (rmsnorm(o, g) * (1.0 - lam_init)).astype(q.dtype)


def na_latent(q, k, v, kc, vc, rel_bias):
    B, L, H, d = q.shape
    rows = L // GRID_W
    wr = min(NA_ROWS, rows)
    ncb = GRID_W // NA_COLS
    nk = wr * NA_SPAN
    r = jnp.arange(rows)
    row_idx = jnp.clip(r - wr // 2, 0, rows - wr)[:, None] + jnp.arange(wr)[None, :]
    qcol = jnp.arange(ncb)[:, None] * NA_COLS + jnp.arange(NA_COLS)[None, :]
    span0 = jnp.clip(jnp.arange(ncb) * NA_COLS - NA_COLS // 2, 0, GRID_W - NA_SPAN)
    col_idx = span0[:, None] + jnp.arange(NA_SPAN)[None, :]
    col0 = jnp.clip(qcol - NA_COLS // 2, 0, GRID_W - NA_COLS)
    kcol = col_idx[:, None, :]
    valid = (kcol >= col0[..., None]) & (kcol < col0[..., None] + NA_COLS)
    dr = row_idx - r[:, None] + NA_ROWS - 1
    dc = jnp.clip(kcol - qcol[..., None], -(NA_COLS - 1), NA_COLS - 1) + NA_COLS - 1
    bias = rel_bias.astype(f32)[:, dr[:, None, None, :, None], dc[None, :, :, None, :]]
    bias = bias.reshape(H, rows, ncb, NA_COLS, nk)
    mask = jnp.broadcast_to(valid[:, :, None, :], (ncb, NA_COLS, wr, NA_SPAN)).reshape(ncb, NA_COLS, nk)

    def gather(t):
        t = t.reshape(B, rows, GRID_W, H, d)[:, row_idx]
        t = t[:, :, :, col_idx]
        return jnp.moveaxis(t, 3, 2).reshape(B, rows, ncb, nk, H, d).astype(f32)

    kg, vg = gather(k), gather(v)
    qg = q.reshape(B, rows, ncb, NA_COLS, H, d).astype(f32) * d ** -0.5
    s_loc = jnp.einsum('brnqhd,brnkhd->bhrnqk', qg, kg) + bias[None]
    s_loc = jnp.where(mask[None, None, None], s_loc, NEG_INF)
    s_ctx = jnp.einsum('brnqhd,bkhd->bhrnqk', qg, kc.astype(f32))
    pr = jax.nn.softmax(jnp.concatenate([s_loc, s_ctx], axis=-1), axis=-1)
    o = (jnp.einsum('bhrnqk,brnkhd->brnqhd', pr[..., :nk], vg)
         + jnp.einsum('bhrnqk,bkhd->brnqhd', pr[..., nk:], vc.astype(f32)))
    return o.reshape(B, L, H, d).astype(q.dtype)


def swa_latent(q, k, v, kc, vc, sink):
    B, L, G, R, d = q.shape
    nb = L // BLOCK
    nk = 3 * BLOCK

    def band(t):
        tp = jnp.pad(t, ((0, 0), (BLOCK, BLOCK), (0, 0), (0, 0))).reshape(B, nb + 2, BLOCK, G, t.shape[-1])
        return jnp.concatenate([tp[:, :-2], tp[:, 1:-1], tp[:, 2:]], axis=2).astype(f32)

    kb, vb = band(k), band(v)
    qb = q.reshape(B, nb, BLOCK, G, R, d).astype(f32) * d ** -0.5
    qpos = jnp.arange(nb)[:, None] * BLOCK + jnp.arange(BLOCK)[None, :]
    kpos = jnp.arange(nb)[:, None] * BLOCK - BLOCK + jnp.arange(nk)[None, :]
    valid = ((kpos[:, None, :] >= 0) & (kpos[:, None, :] < L)
             & (jnp.abs(qpos[:, :, None] - kpos[:, None, :]) <= SWA_WINDOW))
    s_loc = jnp.einsum('bnqgrd,bnkgd->bngrqk', qb, kb)
    s_loc = jnp.where(valid[None, :, None, None], s_loc, NEG_INF)
    s_ctx = jnp.einsum('bnqgrd,bkgd->bngrqk', qb, kc.astype(f32))
    pr = softmax_with_sink(jnp.concatenate([s_loc, s_ctx], axis=-1), sink.reshape(G, R))
    o = (jnp.einsum('bngrqk,bnkgd->bnqgrd', pr[..., :nk], vb)
         + jnp.einsum('bngrqk,bkgd->bnqgrd', pr[..., nk:], vc.astype(f32)))
    return o.reshape(B, L, G, R, vc.shape[-1]).astype(q.dtype)


def short_conv3(u, w):
    up = jnp.pad(u, ((0, 0), (1, 1), (0, 0)))
    return up[:, :-2] * w[0] + up[:, 1:-1] * w[1] + up[:, 2:] * w[2]


def hyena_filters(L, p):
    n = jnp.arange(L, dtype=f32)[:, None]
    t = n / max(L - 1, 1)
    w = 2.0 * math.pi * n / L
    bands = jnp.linspace(1e-4, HY_BANDS - 1, HY_BANDS, dtype=f32)[None, :]
    z = jnp.concatenate([t, jnp.cos(bands * w), -jnp.sin(bands * w)], axis=-1)
    freq = p['hy_freq'].astype(f32)
    h = jnp.sin(freq * (z @ p['hy_w1'].astype(f32) + p['hy_b1'].astype(f32)))
    h = jnp.sin(freq * (h @ p['hy_w2'].astype(f32) + p['hy_b2'].astype(f32)))
    h = (h @ p['hy_w3'].astype(f32)) * jnp.exp(-t * jnp.abs(p['hy_decay'].astype(f32)))
    return h.reshape(L, HY_ORDER, 2, HY_WIDTH)


def long_conv_bidir(u, hf, hb, bias):
    L = u.shape[1]
    n = 2 * L
    uf = u.astype(f32)
    Hf = jnp.fft.rfft(hf, n=n, axis=0)[None]
    Hb = jnp.fft.rfft(hb, n=n, axis=0)[None]
    fwd = jnp.fft.irfft(jnp.fft.rfft(uf, n=n, axis=1) * Hf, n=n, axis=1)[:, :L]
    bwd = jnp.fft.irfft(jnp.fft.rfft(uf[:, ::-1], n=n, axis=1) * Hb, n=n, axis=1)[:, :L][:, ::-1]
    return (fwd + bwd + uf * bias.astype(f32)).astype(u.dtype)


def hyena(u, p):
    L = u.shape[1]
    uc = short_conv3(u, p['hy_short'])
    v, x1, x2 = jnp.split(uc, 3, axis=-1)
    h = hyena_filters(L, p)
    z = x1 * long_conv_bidir(v, h[:, 0, 0], h[:, 0, 1], p['hy_bias'][0])
    return x2 * long_conv_bidir(z, h[:, 1, 0], h[:, 1, 1], p['hy_bias'][1])


def context_mixers(parts, p, li):
    aq, ak, av, bq, bk, bv, cq, ck, cv, hy, _ = parts
    B, L = aq.shape[:2]
    q_a = aq.reshape(B, L, 2, A_HEADS, A_QK)
    k_a = ak.reshape(B, L, 2, A_HEADS, A_QK)
    v_a = av.reshape(B, L, A_HEADS, A_V)
    lam, lam_init = diff_lambda_value(p['diff_lambda'], li)
    o_a = diff_attention(q_a, k_a, v_a, lam, lam_init, p['diff_norm_g'])
    k_b = bk.reshape(B, L, B_HEADS, HEAD_DIM)
    v_b = bv.reshape(B, L, B_HEADS, HEAD_DIM)
    o_b = dense_attention(bq.reshape(B, L, B_HEADS, 1, HEAD_DIM), k_b, v_b, None)
    k_c = ck.reshape(B, L, C_KV_HEADS, HEAD_DIM)
    v_c = cv.reshape(B, L, C_KV_HEADS, HEAD_DIM)
    o_c = dense_attention(cq.reshape(B, L, C_KV_HEADS, C_GROUP, HEAD_DIM), k_c, v_c, p['swa_sink'])
    o_d = hyena(hy, p)
    branches = (o_a.reshape(B, L, BRANCH_WIDTH), o_b.reshape(B, L, BRANCH_WIDTH),
                o_c.reshape(B, L, BRANCH_WIDTH), o_d)
    return branches, (k_a, v_a, k_b, v_b, k_c, v_c)


def latent_mixers(parts, cache, p, li):
    aq, ak, av, bq, bk, bv, cq, ck, cv, hy, _ = parts
    ka_c, va_c, kb_c, vb_c, kc_c, vc_c = cache
    B, L = aq.shape[:2]
    q_a = rope_2d(aq.reshape(B, L, 2, A_HEADS, A_QK))
    k_a = rope_2d(ak.reshape(B, L, 2, A_HEADS, A_QK))
    v_a = av.reshape(B, L, A_HEADS, A_V)
    lam, lam_init = diff_lambda_value(p['diff_lambda'], li)
    o_a = diff_attention(q_a, jnp.concatenate([k_a, ka_c.astype(k_a.dtype)], axis=1),
                         jnp.concatenate([v_a, va_c.astype(v_a.dtype)], axis=1), lam, lam_init, p['diff_norm_g'])
    o_b = na_latent(bq.reshape(B, L, B_HEADS, HEAD_DIM), bk.reshape(B, L, B_HEADS, HEAD_DIM),
                    bv.reshape(B, L, B_HEADS, HEAD_DIM), kb_c, vb_c, p['na_bias'])
    q_c = rope_2d(cq.reshape(B, L, C_KV_HEADS, C_GROUP, HEAD_DIM))
    k_c = rope_2d(ck.reshape(B, L, C_KV_HEADS, HEAD_DIM))
    o_c = swa_latent(q_c, k_c, cv.reshape(B, L, C_KV_HEADS, HEAD_DIM), kc_c, vc_c, p['swa_sink'])
    o_d = hyena(hy, p)
    return (o_a.reshape(B, L, BRANCH_WIDTH), o_b.reshape(B, L, BRANCH_WIDTH),
            o_c.reshape(B, L, BRANCH_WIDTH), o_d)


def merge_branches(branches, gate_logits, w_branch, w_out):
    o = jnp.stack(branches, axis=2)
    g = jax.nn.sigmoid(gate_logits).reshape(o.shape[:3] + (D_MODEL,))
    proj = jnp.einsum('blnw,nwd->blnd', o, w_branch)
    return jnp.sum(g * proj, axis=2) @ w_out


def channel_mixer(x, shift, scale, gate, g, w_up, w_down):
    h = rmsnorm(x, g) * (1 + scale) + shift
    a = jax.nn.relu(h @ w_up)
    return x + gate * ((a * a) @ w_down)


def setup_inputs(seed: int = 0) -> dict:
    key = jax.random.key(seed)
    ks = jax.random.split(key, 33)

    def nrm(i, shape, s):
        return jax.random.normal(ks[i], shape, jnp.float32) * s

    D = D_MODEL
    decay_base = jnp.broadcast_to(jnp.linspace(HY_DECAY_MIN, HY_DECAY_MAX, 2 * HY_ORDER * HY_WIDTH, dtype=jnp.float32),
                                  (DEPTH, 2 * HY_ORDER * HY_WIDTH))
    return {
        'x_prompt': nrm(0, (BATCH, SEQ, D), 1.0),
        'x_sample': nrm(1, (DEC_BATCH, DEC_SEQ, D), 1.0),
        'cache_a_k': nrm(2, (DEC_BATCH, DEPTH, PAST_LEN, 2, A_HEADS, A_QK), 1.0),
        'cache_a_v': nrm(3, (DEC_BATCH, DEPTH, PAST_LEN, A_HEADS, A_V), 1.0),
        'cache_b_k': nrm(4, (DEC_BATCH, DEPTH, PAST_LEN, B_HEADS, HEAD_DIM), 1.0),
        'cache_b_v': nrm(5, (DEC_BATCH, DEPTH, PAST_LEN, B_HEADS, HEAD_DIM), 1.0),
        'cache_c_k': nrm(6, (DEC_BATCH, DEPTH, PAST_LEN, C_KV_HEADS, HEAD_DIM), 1.0),
        'cache_c_v': nrm(7, (DEC_BATCH, DEPTH, PAST_LEN, C_KV_HEADS, HEAD_DIM), 1.0),
        'c': nrm(8, (DEC_BATCH, D), 1.0),
        'c_ctx': nrm(9, (D,), 1.0),
        'w_ada': nrm(10, (DEPTH, D, 6 * D), 0.5 * D ** -0.5),
        'b_ada': nrm(11, (DEPTH, 6 * D), 0.02),
        'g_mix': 1.0 + nrm(12, (DEPTH, D), 0.05),
        'w_in': nrm(13, (DEPTH, D, IN_COLS), D ** -0.5),
        'diff_lambda': nrm(14, (DEPTH, 4, A_QK), 0.1),
        'diff_norm_g': 1.0 + nrm(15, (DEPTH, A_V), 0.05),
        'na_bias': nrm(16, (DEPTH, B_HEADS, 2 * NA_ROWS - 1, 2 * NA_COLS - 1), 0.1),
        'swa_sink': nrm(17, (DEPTH, C_HEADS), 0.5),
        'hy_short': nrm(18, (DEPTH, 3, HY_COLS), 3 ** -0.5),
        'hy_w1': nrm(19, (DEPTH, HY_EMB, HY_FFN), HY_EMB ** -0.5),
        'hy_b1': nrm(20, (DEPTH, HY_FFN), 0.02),
        'hy_w2': nrm(21, (DEPTH, HY_FFN, HY_FFN), HY_FFN ** -0.5),
        'hy_b2': nrm(22, (DEPTH, HY_FFN), 0.02),
        'hy_w3': nrm(23, (DEPTH, HY_FFN, 2 * HY_ORDER * HY_WIDTH), 0.05 * HY_FFN ** -0.5),
        'hy_freq': 1.0 + nrm(24, (DEPTH, HY_FFN), 0.05),
        'hy_decay': decay_base * (1.0 + nrm(25, (DEPTH, 2 * HY_ORDER * HY_WIDTH), 0.05)),
        'hy_bias': nrm(26, (DEPTH, HY_ORDER, HY_WIDTH), 0.1),
        'w_branch': nrm(27, (DEPTH, N_BRANCH, BRANCH_WIDTH, D), BRANCH_WIDTH ** -0.5),
        'w_out': nrm(28, (DEPTH, D, D), D ** -0.5),
        'g_mlp': 1.0 + nrm(29, (DEPTH, D), 0.05),
        'w_up': nrm(30, (DEPTH, D, D_FF), D ** -0.5),
        'w_down': nrm(31, (DEPTH, D_FF, D), D_FF ** -0.5),
        'g_final': 1.0 + nrm(32, (D,), 0.05),
    }


def reference(x_prompt, x_sample, cache_a_k, cache_a_v, cache_b_k, cache_b_v, cache_c_k, cache_c_v,
              c, c_ctx, w_ada, b_ada, g_mix, w_in, diff_lambda, diff_norm_g, na_bias, swa_sink,
              hy_short, hy_w1, hy_b1, hy_w2, hy_b2, hy_w3, hy_freq, hy_decay, hy_bias,
              w_branch, w_out, g_mlp, w_up, w_down, g_final):
    x_ctx, x_lat = x_prompt, x_sample
    st_ak, st_av, st_bk, st_bv, st_ck, st_cv = [], [], [], [], [], []
    for li in range(DEPTH):
        p = {'diff_lambda': diff_lambda[li], 'diff_norm_g': diff_norm_g[li], 'na_bias': na_bias[li],
             'swa_sink': swa_sink[li], 'hy_short': hy_short[li], 'hy_w1': hy_w1[li], 'hy_b1': hy_b1[li],
             'hy_w2': hy_w2[li], 'hy_b2': hy_b2[li], 'hy_w3': hy_w3[li], 'hy_freq': hy_freq[li],
             'hy_decay': hy_decay[li], 'hy_bias': hy_bias[li]}
        sh1, sc1, gt1, sh2, sc2, gt2 = adaln(c_ctx[None, None, :], w_ada[li], b_ada[li])
        parts = in_projection(x_ctx, sh1, sc1, g_mix[li], w_in[li])
        branches, ctx_kv = context_mixers(parts, p, li)
        x_ctx = x_ctx + gt1 * merge_branches(branches, parts[-1], w_branch[li], w_out[li])
        x_ctx = channel_mixer(x_ctx, sh2, sc2, gt2, g_mlp[li], w_up[li], w_down[li])
        k_a, v_a, k_b, v_b, k_c, v_c = ctx_kv
        st_ak.append(k_a)
        st_av.append(v_a)
        st_bk.append(k_b)
        st_bv.append(v_b)
        st_ck.append(k_c)
        st_cv.append(v_c)
        sh1, sc1, gt1, sh2, sc2, gt2 = adaln(c[:, None, :], w_ada[li], b_ada[li])
        parts = in_projection(x_lat, sh1, sc1, g_mix[li], w_in[li])
        cache = (cache_a_k[:, li], cache_a_v[:, li], cache_b_k[:, li], cache_b_v[:, li],
                 cache_c_k[:, li], cache_c_v[:, li])
        branches = latent_mixers(parts, cache, p, li)
        x_lat = x_lat + gt1 * merge_branches(branches, parts[-1], w_branch[li], w_out[li])
        x_lat = channel_mixer(x_lat, sh2, sc2, gt2, g_mlp[li], w_up[li], w_down[li])
    y_prompt = rmsnorm(x_ctx, g_final)
    y_sample = rmsnorm(x_lat, g_final)
    new_a_k = jnp.stack(st_ak, axis=1)
    new_a_v = jnp.stack(st_av, axis=1)
    new_b_k = jnp.stack(st_bk, axis=1)
    new_b_v = jnp.stack(st_bv, axis=1)
    new_c_k = jnp.stack(st_ck, axis=1)
    new_c_v = jnp.stack(st_cv, axis=1)
    return (y_prompt, y_sample, new_a_k, new_a_v, new_b_k, new_b_v, new_c_k, new_c_v)
```

```python
import functools
import math

import numpy as np
import jax
import jax.numpy as jnp
from jax import lax
from jax.experimental import pallas as pl
from jax.experimental.pallas import tpu as pltpu

f32 = jnp.float32
bf16 = jnp.bfloat16

D_MODEL = 2048
DEPTH = 2
CTX_SEQ = 256
LAT_SEQ = 1024
GRID_W = 64
GRID_ROWS = LAT_SEQ // GRID_W
BRANCH_WIDTH = 512
N_BRANCH = 4
HEAD_DIM = 64
A_QK = 64
A_V = 128
A_HEADS = 4
B_HEADS = 8
NA_ROWS = 8
NA_COLS = 16
C_HEADS = 8
C_KV_HEADS = 2
C_GROUP = 4
SWA_WINDOW = 128
HY_WIDTH = 512
HY_ORDER = 2
HY_BANDS = 16
HY_EMB = 1 + 2 * HY_BANDS
HY_EMB_PAD = 128
HY_FFN = 64
D_FF = 4 * D_MODEL
ROPE_BASE = 10000.0
EPS = 1e-6
NEG_INF = -1e30
MIX_COLS = 5376
GATE_COLS = N_BRANCH * D_MODEL
COL_AQ, COL_AK, COL_AV, COL_BQ, COL_BK, COL_BV, COL_CQ = 0, 1, 2, 3, 4, 5, 6
COL_CKV_256 = 14
HY_COL0 = 3840
N_MOD_ROWS = 16
CTX_MOD_ROW = 8

VMEM_LIMIT = 56 * 1024 * 1024
TM_INPROJ = 1024
TN_INPROJ = 768
TM_MERGE = 512
TM_MLP = 512
TF_MLP = 1024
TQ_ATTN = 256
TN_ADA = 1024
HY_CB = 256


def _params(*sem):
    return pltpu.CompilerParams(dimension_semantics=sem, vmem_limit_bytes=VMEM_LIMIT)


def _dot(a, b):
    return jnp.dot(a, b, preferred_element_type=f32)


def _dot_nt(a, b):
    return lax.dot_general(a, b, (((1,), (1,)), ((), ())), preferred_element_type=f32)


def _dot_hi(a, b):
    return jnp.dot(a, b, preferred_element_type=f32, precision=lax.Precision.HIGHEST)


@functools.lru_cache(maxsize=None)
def _rope_tables():
    half = HEAD_DIM // 2
    nf = half // 2
    inv = ROPE_BASE ** (-np.arange(nf, dtype=np.float64) / nf)
    t = np.arange(LAT_SEQ)
    pos = np.stack([t // GRID_W, t % GRID_W], axis=1).astype(np.float64)
    lane = np.arange(HEAD_DIM)
    ang = pos[:, lane // half] * inv[lane % nf][None, :]
    first = (lane % half) < nf
    cos = np.cos(ang)
    sin = np.where(first[None, :], -np.sin(ang), np.sin(ang))
    reps = 512 // HEAD_DIM
    return (np.tile(cos, (1, reps)).astype(np.float32), np.tile(sin, (1, reps)).astype(np.float32))


@functools.lru_cache(maxsize=None)
def _dft_tables(L):
    f = np.arange(L, dtype=np.int64)
    prod = (f[:, None] * f[None, :]) % (2 * L)
    ang = np.pi * prod.astype(np.float64) / L
    fc = np.cos(ang)
    fs = np.sin(ang)
    fs[0, :] = np.where(f % 2 == 0, 1.0, -1.0)
    return fc.astype(np.float32), fs.astype(np.float32)


@functools.lru_cache(maxsize=None)
def _hyena_features(L):
    n = np.arange(L, dtype=np.float64)[:, None]
    t = n / max(L - 1, 1)
    w = 2.0 * math.pi * n / L
    bands = np.linspace(1e-4, HY_BANDS - 1, HY_BANDS, dtype=np.float64)[None, :]
    z = np.concatenate([t, np.cos(bands * w), -np.sin(bands * w)], axis=-1)
    z = np.pad(z, ((0, 0), (0, HY_EMB_PAD - HY_EMB)))
    return z.astype(np.float32)


def _modulated_norm(x, g, scale, shift):
    ms = jnp.mean(x * x, axis=-1, keepdims=True)
    y = x * lax.rsqrt(ms + EPS) * g
    return y * (1.0 + scale) + shift


def _rope(x, cos, sin_signed):
    n = x.shape[-1]
    lane = lax.broadcasted_iota(jnp.int32, x.shape, 1)
    first = (lane & (HEAD_DIM // 2 - 1)) < (HEAD_DIM // 4)
    partner = jnp.where(first, pltpu.roll(x, n - HEAD_DIM // 4, 1), pltpu.roll(x, HEAD_DIM // 4, 1))
    return x * cos + partner * sin_signed


def _softmax_parts(scores, sink=None):
    m = jnp.max(scores[0], axis=-1, keepdims=True)
    for s in scores[1:]:
        m = jnp.maximum(m, jnp.max(s, axis=-1, keepdims=True))
    if sink is not None:
        m = jnp.maximum(m, sink)
    es = [jnp.exp(s - m) for s in scores]
    d = jnp.sum(es[0], axis=-1, keepdims=True)
    for e in es[1:]:
        d = d + jnp.sum(e, axis=-1, keepdims=True)
    if sink is not None:
        d = d + jnp.exp(sink - m)
    return es, 1.0 / d


def _diff_lambda(dl_ref, lam_init):
    dl = dl_ref[...]
    a = jnp.sum(dl[0:1] * dl[1:2], axis=-1, keepdims=True)
    b = jnp.sum(dl[2:3] * dl[3:4], axis=-1, keepdims=True)
    return jnp.exp(a) - jnp.exp(b) + lam_init


def _diff_head_out(w, v, dg, lam_init):
    o = _dot(w.astype(bf16), v)
    o = o * lax.rsqrt(jnp.mean(o * o, axis=-1, keepdims=True) + EPS) * dg
    return o * (1.0 - lam_init)


def _ada_kernel(cv_ref, w_ref, b_ref, o_ref):
    cv = cv_ref[...]
    s = (cv * jax.nn.sigmoid(cv)).astype(bf16)
    o_ref[...] = _dot(s, w_ref[...].astype(bf16)) + b_ref[...]


def _adaln_all(cv, w_ada, b_ada):
    n6 = 6 * D_MODEL
    return pl.pallas_call(
        _ada_kernel,
        grid=(DEPTH, n6 // TN_ADA),
        in_specs=[pl.BlockSpec((N_MOD_ROWS, D_MODEL), lambda l, j: (0, 0)),
                  pl.BlockSpec((None, D_MODEL, TN_ADA), lambda l, j: (l, 0, j)),
                  pl.BlockSpec((None, 1, TN_ADA), lambda l, j: (l, 0, j))],
        out_specs=pl.BlockSpec((None, N_MOD_ROWS, TN_ADA), lambda l, j: (l, 0, j)),
        out_shape=jax.ShapeDtypeStruct((DEPTH, N_MOD_ROWS, n6), f32),
        compiler_params=_params("parallel", "parallel"),
        name="adaln",
    )(cv, w_ada, b_ada.reshape(DEPTH, 1, n6))


def _mod_spec(li, k, tm, n_ctx):
    n_ctx_tiles = n_ctx // tm

    def index(i, j):
        r = jnp.where(i < n_ctx_tiles, CTX_MOD_ROW, ((i - n_ctx_tiles) * tm) // LAT_SEQ)
        return ((li * N_MOD_ROWS + r) * 6 + k, 0, 0)

    return pl.BlockSpec((None, 1, D_MODEL), index)


def _inproj_kernel(x_ref, sh_ref, sc_ref, g_ref, w_ref, o_ref, h_ref):
    @pl.when(pl.program_id(1) == 0)
    def _():
        h_ref[...] = _modulated_norm(x_ref[...], g_ref[...], sc_ref[...], sh_ref[...]).astype(bf16)

    o_ref[...] = _dot(h_ref[...], w_ref[...])


def _inproj(x, mod, g, w, li, n_ctx):
    T = x.shape[0]
    tm, tn = TM_INPROJ, TN_INPROJ
    return pl.pallas_call(
        _inproj_kernel,
        grid=(T // tm, MIX_COLS // tn),
        in_specs=[pl.BlockSpec((tm, D_MODEL), lambda i, j: (i, 0)),
                  _mod_spec(li, 0, tm, n_ctx), _mod_spec(li, 1, tm, n_ctx),
                  pl.BlockSpec((1, D_MODEL), lambda i, j: (0, 0)),
                  pl.BlockSpec((D_MODEL, tn), lambda i, j: (0, j))],
        out_specs=pl.BlockSpec((tm, tn), lambda i, j: (i, j)),
        out_shape=jax.ShapeDtypeStruct((T, MIX_COLS), f32),
        scratch_shapes=[pltpu.VMEM((tm, D_MODEL), bf16)],
        compiler_params=_params("parallel", "arbitrary"),
        name="inproj",
    )(x, mod, mod, g, w)


def _merge_kernel(x_ref, sh_ref, sc_ref, gt_ref, g_ref, o_ref, w_ref, wb_ref, out_ref, h_ref, acc_ref):
    n = pl.program_id(1)

    @pl.when(n == 0)
    def _():
        h_ref[...] = _modulated_norm(x_ref[...], g_ref[...], sc_ref[...], sh_ref[...]).astype(bf16)
        acc_ref[...] = jnp.zeros_like(acc_ref)

    @pl.when(n < N_BRANCH)
    def _():
        gate = jax.nn.sigmoid(_dot(h_ref[...], w_ref[...]))
        acc_ref[...] += gate * _dot(o_ref[...], wb_ref[...])

    @pl.when(n == N_BRANCH)
    def _():
        y = _dot(acc_ref[...].astype(bf16), w_ref[...])
        out_ref[...] = x_ref[...] + gt_ref[...] * y


def _merge(x, mod, g, o_stack, w5, wb, li, n_ctx):
    T = x.shape[0]
    tm = TM_MERGE
    last = N_BRANCH - 1
    return pl.pallas_call(
        _merge_kernel,
        grid=(T // tm, N_BRANCH + 1),
        in_specs=[pl.BlockSpec((tm, D_MODEL), lambda i, n: (i, 0)),
                  _mod_spec(li, 0, tm, n_ctx), _mod_spec(li, 1, tm, n_ctx), _mod_spec(li, 2, tm, n_ctx),
                  pl.BlockSpec((1, D_MODEL), lambda i, n: (0, 0)),
                  pl.BlockSpec((None, tm, BRANCH_WIDTH), lambda i, n: (jnp.minimum(n, last), i, 0)),
                  pl.BlockSpec((None, D_MODEL, D_MODEL), lambda i, n: (n, 0, 0)),
                  pl.BlockSpec((None, BRANCH_WIDTH, D_MODEL), lambda i, n: (jnp.minimum(n, last), 0, 0))],
        out_specs=pl.BlockSpec((tm, D_MODEL), lambda i, n: (i, 0)),
        out_shape=jax.ShapeDtypeStruct((T, D_MODEL), f32),
        scratch_shapes=[pltpu.VMEM((tm, D_MODEL), bf16), pltpu.VMEM((tm, D_MODEL), f32)],
        compiler_params=_params("parallel", "arbitrary"),
        name="merge",
    )(x, mod, mod, mod, g, o_stack, w5, wb)


def _mlp_kernel(x_ref, sh_ref, sc_ref, gt_ref, g_ref, wu_ref, wd_ref, gf_ref, out_ref, h_ref, acc_ref,
                *, final_norm):
    j = pl.program_id(1)

    @pl.when(j == 0)
    def _():
        h_ref[...] = _modulated_norm(x_ref[...], g_ref[...], sc_ref[...], sh_ref[...]).astype(bf16)
        acc_ref[...] = jnp.zeros_like(acc_ref)

    a = jnp.maximum(_dot(h_ref[...], wu_ref[...]), 0.0)
    acc_ref[...] += _dot((a * a).astype(bf16), wd_ref[...])

    @pl.when(j == pl.num_programs(1) - 1)
    def _():
        y = x_ref[...] + gt_ref[...] * acc_ref[...]
        if final_norm:
            y = y * lax.rsqrt(jnp.mean(y * y, axis=-1, keepdims=True) + EPS) * gf_ref[...]
        out_ref[...] = y


def _mlp(x, mod, g, w_up, w_down, g_final, li, n_ctx, final_norm):
    T = x.shape[0]
    tm, tf = TM_MLP, TF_MLP
    return pl.pallas_call(
        functools.partial(_mlp_kernel, final_norm=final_norm),
        grid=(T // tm, D_FF // tf),
        in_specs=[pl.BlockSpec((tm, D_MODEL), lambda i, j: (i, 0)),
                  _mod_spec(li, 3, tm, n_ctx), _mod_spec(li, 4, tm, n_ctx), _mod_spec(li, 5, tm, n_ctx),
                  pl.BlockSpec((1, D_MODEL), lambda i, j: (0, 0)),
                  pl.BlockSpec((D_MODEL, tf), lambda i, j: (0, j)),
                  pl.BlockSpec((tf, D_MODEL), lambda i, j: (j, 0)),
                  pl.BlockSpec((1, D_MODEL), lambda i, j: (0, 0))],
        out_specs=pl.BlockSpec((tm, D_MODEL), lambda i, j: (i, 0)),
        out_shape=jax.ShapeDtypeStruct((T, D_MODEL), f32),
        scratch_shapes=[pltpu.VMEM((tm, D_MODEL), bf16), pltpu.VMEM((tm, D_MODEL), f32)],
        compiler_params=_params("parallel", "arbitrary"),
        name="mlp",
    )(x, mod, mod, mod, g, w_up, w_down, g_final)


def _ctx_attn_kernel(aq_ref, ak_ref, av_ref, bq_ref, bk_ref, bv_ref, cq_ref, ckv_ref,
                     dl_ref, dg_ref, sink_ref, oa_ref, ob_ref, oc_ref, *, lam_init):
    scale = HEAD_DIM ** -0.5
    lam = _diff_lambda(dl_ref, lam_init)
    q = (aq_ref[...] * scale).astype(bf16)
    k = ak_ref[...].astype(bf16)
    v = av_ref[...].astype(bf16)
    dg = dg_ref[...]
    half = A_HEADS * A_QK
    for h in range(A_HEADS):
        prs = []
        for c in range(2):
            lo = c * half + h * A_QK
            (e,), r = _softmax_parts([_dot_nt(q[:, lo:lo + A_QK], k[:, lo:lo + A_QK])])
            prs.append((e, r))
        w = prs[0][0] * prs[0][1] - prs[1][0] * (lam * prs[1][1])
        oa_ref[:, h * A_V:(h + 1) * A_V] = _diff_head_out(w, v[:, h * A_V:(h + 1) * A_V], dg, lam_init).astype(bf16)
    q = (bq_ref[...] * scale).astype(bf16)
    k = bk_ref[...].astype(bf16)
    v = bv_ref[...].astype(bf16)
    for h in range(B_HEADS):
        sl = slice(h * HEAD_DIM, (h + 1) * HEAD_DIM)
        (e,), r = _softmax_parts([_dot_nt(q[:, sl], k[:, sl])])
        ob_ref[:, sl] = (_dot(e.astype(bf16), v[:, sl]) * r).astype(bf16)
    q = (cq_ref[...] * scale).astype(bf16)
    kv = ckv_ref[...].astype(bf16)
    kw = C_KV_HEADS * HEAD_DIM
    for h in range(C_HEADS):
        g = h // C_GROUP
        sl = slice(h * HEAD_DIM, (h + 1) * HEAD_DIM)
        kg = kv[:, g * HEAD_DIM:(g + 1) * HEAD_DIM]
        vg = kv[:, kw + g * HEAD_DIM:kw + (g + 1) * HEAD_DIM]
        (e,), r = _softmax_parts([_dot_nt(q[:, sl], kg)], sink=sink_ref[h])
        oc_ref[:, sl] = (_dot(e.astype(bf16), vg) * r).astype(bf16)


def _ctx_attn(p, dl, dg, sink, li, n_ctx):
    nb = n_ctx // CTX_SEQ
    S = CTX_SEQ
    lam_init = 0.8 - 0.6 * math.exp(-0.3 * li)

    def col(c):
        return pl.BlockSpec((S, 512), lambda b, c=c: (b, c))

    o_sds = jax.ShapeDtypeStruct((n_ctx, BRANCH_WIDTH), bf16)
    o_spec = pl.BlockSpec((S, BRANCH_WIDTH), lambda b: (b, 0))
    return pl.pallas_call(
        functools.partial(_ctx_attn_kernel, lam_init=lam_init),
        grid=(nb,),
        in_specs=[col(COL_AQ), col(COL_AK), col(COL_AV), col(COL_BQ), col(COL_BK), col(COL_BV), col(COL_CQ),
                  pl.BlockSpec((S, 256), lambda b: (b, COL_CKV_256)),
                  pl.BlockSpec((4, A_QK), lambda b: (0, 0)),
                  pl.BlockSpec((1, A_V), lambda b: (0, 0)),
                  pl.BlockSpec(memory_space=pltpu.SMEM)],
        out_specs=[o_spec, o_spec, o_spec],
        out_shape=[o_sds, o_sds, o_sds],
        compiler_params=_params("parallel"),
        name="ctx_attn",
    )(p, p, p, p, p, p, p, p, dl, dg, sink)


def _lat_a_kernel(q_ref, k_ref, v_ref, ck_ref, cv_ref, cosq_ref, sinq_ref, cosk_ref, sink_ref,
                  dl_ref, dg_ref, o_ref, kk_ref, vv_ref, *, lam_init):
    L = LAT_SEQ

    @pl.when(pl.program_id(1) == 0)
    def _():
        kk_ref[0:L, :] = _rope(k_ref[...], cosk_ref[...], sink_ref[...]).astype(bf16)
        kk_ref[L:, :] = ck_ref[...].astype(bf16)
        vv_ref[0:L, :] = v_ref[...].astype(bf16)
        vv_ref[L:, :] = cv_ref[...].astype(bf16)

    lam = _diff_lambda(dl_ref, lam_init)
    q = (_rope(q_ref[...], cosq_ref[...], sinq_ref[...]) * (A_QK ** -0.5)).astype(bf16)
    dg = dg_ref[...]
    half = A_HEADS * A_QK
    for h in range(A_HEADS):
        prs = []
        for c in range(2):
            lo = c * half + h * A_QK
            (e,), r = _softmax_parts([_dot_nt(q[:, lo:lo + A_QK], kk_ref[:, lo:lo + A_QK])])
            prs.append((e, r))
        w = prs[0][0] * prs[0][1] - prs[1][0] * (lam * prs[1][1])
        o_ref[:, h * A_V:(h + 1) * A_V] = _diff_head_out(w, vv_ref[:, h * A_V:(h + 1) * A_V], dg, lam_init).astype(bf16)


def _lat_b_kernel(q_ref, k_ref, v_ref, ck_ref, cv_ref, bias_ref, o_ref, kk_ref, vv_ref):
    L = LAT_SEQ

    @pl.when(pl.program_id(1) == 0)
    def _():
        kk_ref[0:L, :] = k_ref[...].astype(bf16)
        kk_ref[L:, :] = ck_ref[...].astype(bf16)
        vv_ref[0:L, :] = v_ref[...].astype(bf16)
        vv_ref[L:, :] = cv_ref[...].astype(bf16)

    q = (q_ref[...] * (HEAD_DIM ** -0.5)).astype(bf16)
    for h in range(B_HEADS):
        sl = slice(h * HEAD_DIM, (h + 1) * HEAD_DIM)
        s_loc = _dot_nt(q[:, sl], kk_ref[0:L, sl]) + bias_ref[h].astype(f32)
        s_ctx = _dot_nt(q[:, sl], kk_ref[L:, sl])
        (e_loc, e_ctx), r = _softmax_parts([s_loc, s_ctx])
        o = _dot(e_loc.astype(bf16), vv_ref[0:L, sl]) + _dot(e_ctx.astype(bf16), vv_ref[L:, sl])
        o_ref[:, sl] = (o * r).astype(bf16)


def _lat_c_kernel(q_ref, kv_ref, ck_ref, cv_ref, cosq_ref, sinq_ref, cosk_ref, sink_ref, snk_ref,
                  o_ref, kk_ref, vv_ref):
    L = LAT_SEQ
    kw = C_KV_HEADS * HEAD_DIM
    tq = q_ref.shape[0]

    @pl.when(pl.program_id(1) == 0)
    def _():
        kk_ref[0:L, :] = _rope(kv_ref[:, 0:kw], cosk_ref[...], sink_ref[...]).astype(bf16)
        kk_ref[L:, :] = ck_ref[...].astype(bf16)
        vv_ref[0:L, :] = kv_ref[:, kw:2 * kw].astype(bf16)
        vv_ref[L:, :] = cv_ref[...].astype(bf16)

    q = (_rope(q_ref[...], cosq_ref[...], sinq_ref[...]) * (HEAD_DIM ** -0.5)).astype(bf16)
    qpos = pl.program_id(1) * tq + lax.broadcasted_iota(jnp.int32, (tq, L), 0)
    kpos = lax.broadcasted_iota(jnp.int32, (tq, L), 1)
    valid = jnp.abs(qpos - kpos) <= SWA_WINDOW
    for h in range(C_HEADS):
        g = h // C_GROUP
        sl = slice(h * HEAD_DIM, (h + 1) * HEAD_DIM)
        gl = slice(g * HEAD_DIM, (g + 1) * HEAD_DIM)
        s_loc = jnp.where(valid, _dot_nt(q[:, sl], kk_ref[0:L, gl]), NEG_INF)
        s_ctx = _dot_nt(q[:, sl], kk_ref[L:, gl])
        (e_loc, e_ctx), r = _softmax_parts([s_loc, s_ctx], sink=snk_ref[h])
        o = _dot(e_loc.astype(bf16), vv_ref[0:L, gl]) + _dot(e_ctx.astype(bf16), vv_ref[L:, gl])
        o_ref[:, sl] = (o * r).astype(bf16)


def _lat_attn(p, caches, na_dense, dl, dg, sink, cos_t, sin_t, li, n_ctx, nb_lat):
    cak, cav, cbk, cbv, cck, ccv = caches
    L, tq = LAT_SEQ, TQ_ATTN
    nq = L // tq
    lam_init = 0.8 - 0.6 * math.exp(-0.3 * li)
    row_l = n_ctx // L
    row_q = n_ctx // tq

    def qcol(c):
        return pl.BlockSpec((tq, 512), lambda b, i, c=c: (row_q + b * nq + i, c))

    def kcol(c, width=512):
        return pl.BlockSpec((L, width), lambda b, i, c=c: (row_l + b, c))

    def cache(width):
        return pl.BlockSpec((None, None, CTX_SEQ, width), lambda b, i: (b, li, 0, 0))

    tab_q = pl.BlockSpec((tq, 512), lambda b, i: (i, 0))
    tab_k = pl.BlockSpec((L, 512), lambda b, i: (0, 0))
    tab_k128 = pl.BlockSpec((L, 128), lambda b, i: (0, 0))
    o_sds = jax.ShapeDtypeStruct((nb_lat * L, BRANCH_WIDTH), bf16)
    o_spec = pl.BlockSpec((tq, BRANCH_WIDTH), lambda b, i: (b * nq + i, 0))
    kv_scratch = [pltpu.VMEM((L + CTX_SEQ, 512), bf16), pltpu.VMEM((L + CTX_SEQ, 512), bf16)]
    smem = pl.BlockSpec(memory_space=pltpu.SMEM)
    cp = _params("parallel", "arbitrary")

    oa = pl.pallas_call(
        functools.partial(_lat_a_kernel, lam_init=lam_init),
        grid=(nb_lat, nq),
        in_specs=[qcol(COL_AQ), kcol(COL_AK), kcol(COL_AV), cache(512), cache(512),
                  tab_q, tab_q, tab_k, tab_k,
                  pl.BlockSpec((4, A_QK), lambda b, i: (0, 0)), pl.BlockSpec((1, A_V), lambda b, i: (0, 0))],
        out_specs=o_spec, out_shape=o_sds, scratch_shapes=kv_scratch, compiler_params=cp, name="lat_attn_a",
    )(p, p, p, cak, cav, cos_t, sin_t, cos_t, sin_t, dl, dg)

    ob = pl.pallas_call(
        _lat_b_kernel,
        grid=(nb_lat, nq),
        in_specs=[qcol(COL_BQ), kcol(COL_BK), kcol(COL_BV), cache(512), cache(512),
                  pl.BlockSpec((B_HEADS, tq, L), lambda b, i: (0, i, 0))],
        out_specs=o_spec, out_shape=o_sds, scratch_shapes=kv_scratch, compiler_params=cp, name="lat_attn_b",
    )(p, p, p, cbk, cbv, na_dense)

    kw = C_KV_HEADS * HEAD_DIM
    oc = pl.pallas_call(
        _lat_c_kernel,
        grid=(nb_lat, nq),
        in_specs=[qcol(COL_CQ), kcol(COL_CKV_256, 256), cache(kw), cache(kw),
                  tab_q, tab_q, tab_k128, tab_k128, smem],
        out_specs=o_spec, out_shape=o_sds,
        scratch_shapes=[pltpu.VMEM((L + CTX_SEQ, kw), bf16), pltpu.VMEM((L + CTX_SEQ, kw), bf16)],
        compiler_params=cp, name="lat_attn_c",
    )(p, p, cck, ccv, cos_t, sin_t, cos_t, sin_t, sink)
    return oa, ob, oc


def _na_bias_kernel(rb_ref, o_ref):
    h = pl.program_id(0)
    W = GRID_W
    n_dc = 2 * NA_COLS - 1
    n_dr = 2 * NA_ROWS - 1
    qc = lax.broadcasted_iota(jnp.int32, (W, 2 * W), 0)
    lane = lax.broadcasted_iota(jnp.int32, (W, 2 * W), 1)
    second = lane >= W
    kc = jnp.where(second, lane - W, lane)
    dc = jnp.clip(kc - qc, -(NA_COLS - 1), NA_COLS - 1) + NA_COLS - 1
    c0 = jnp.clip(qc - NA_COLS // 2, 0, W - NA_COLS)
    col_ok = (kc >= c0) & (kc < c0 + NA_COLS)
    base = h * (n_dr * n_dc)

    def pair_tile(dr0):
        t = jnp.zeros((W, 2 * W), f32)
        for j in range(n_dc):
            lo = rb_ref[base + dr0 * n_dc + j] if 0 <= dr0 < n_dr else 0.0
            hi = rb_ref[base + (dr0 + 1) * n_dc + j] if 0 <= dr0 + 1 < n_dr else 0.0
            t = jnp.where(dc == j, jnp.where(second, hi, lo), t)
        return jnp.where(col_ok, t, NEG_INF)

    tiles = {dr0: pair_tile(dr0) for dr0 in range(-1, n_dr)}
    neg = jnp.full((W, 2 * W), NEG_INF, f32)
    for qr in range(GRID_ROWS):
        r0 = min(max(qr - NA_ROWS // 2, 0), GRID_ROWS - NA_ROWS)
        for pr in range(GRID_ROWS // 2):
            kr = 2 * pr
            ok0 = r0 <= kr < r0 + NA_ROWS
            ok1 = r0 <= kr + 1 < r0 + NA_ROWS
            if not (ok0 or ok1):
                t = neg
            else:
                t = tiles[kr - qr + NA_ROWS - 1]
                if not ok0:
                    t = jnp.where(second, t, NEG_INF)
                if not ok1:
                    t = jnp.where(second, NEG_INF, t)
            o_ref[qr * W:(qr + 1) * W, kr * W:(kr + 2) * W] = t.astype(bf16)


def _na_bias_dense(rel_bias):
    return pl.pallas_call(
        _na_bias_kernel,
        grid=(B_HEADS,),
        in_specs=[pl.BlockSpec(memory_space=pltpu.SMEM)],
        out_specs=pl.BlockSpec((None, LAT_SEQ, LAT_SEQ), lambda h: (h, 0, 0)),
        out_shape=jax.ShapeDtypeStruct((B_HEADS, LAT_SEQ, LAT_SEQ), bf16),
        compiler_params=_params("parallel"),
        name="na_bias",
    )(rel_bias.reshape(-1))


def _hy_filter_kernel(z_ref, w1_ref, b1_ref, w2_ref, b2_ref, fr_ref, w3f_ref, w3b_ref, dcf_ref, dcb_ref,
                      bias_ref, fc_ref, fs_ref, o_ref, *, L):
    z = z_ref[...]
    fr = fr_ref[...]
    h = jnp.sin(fr * (_dot_hi(z, w1_ref[...]) + b1_ref[...]))
    h = jnp.sin(fr * (_dot_hi(h, w2_ref[...]) + b2_ref[...]))
    t = z[:, 0:1]
    hf = _dot_hi(h, w3f_ref[...]) * jnp.exp(-t * jnp.abs(dcf_ref[...]))
    hb = _dot_hi(h, w3b_ref[...]) * jnp.exp(-t * jnp.abs(dcb_ref[...]))
    ssum = hf + hb
    bias = bias_ref[...]
    ga = _dot_hi(fc_ref[...], ssum) + bias
    gb = _dot_hi(fs_ref[...], hb - hf)
    g_nyq = _dot_hi(fs_ref[0:8, :], ssum)[0:1] + bias
    row0 = lax.broadcasted_iota(jnp.int32, ga.shape, 0) == 0
    inv = 1.0 / L
    o_ref[0] = jnp.where(row0, 0.5 * ga, ga) * inv
    o_ref[1] = jnp.where(row0, 0.0, gb) * inv
    o_ref[2] = jnp.where(row0, 0.5 * g_nyq, ga) * inv


def _hy_filter_tables(L, w1p, b1, w2, b2, w3, freq, decay, hy_bias):
    cb = HY_CB
    ncb = HY_WIDTH // cb
    z = jnp.asarray(_hyena_features(L))
    fc, fs = (jnp.asarray(a) for a in _dft_tables(L))
    full = lambda shape: pl.BlockSpec(shape, lambda o, c: (0,) * len(shape))
    fwd = lambda rows: pl.BlockSpec((rows, cb), lambda o, c: (0, o * 2 * ncb + c))
    bwd = lambda rows: pl.BlockSpec((rows, cb), lambda o, c: (0, o * 2 * ncb + ncb + c))
    return pl.pallas_call(
        functools.partial(_hy_filter_kernel, L=L),
        grid=(HY_ORDER, ncb),
        in_specs=[full((L, HY_EMB_PAD)), full((HY_EMB_PAD, HY_FFN)), full((1, HY_FFN)),
                  full((HY_FFN, HY_FFN)), full((1, HY_FFN)), full((1, HY_FFN)),
                  fwd(HY_FFN), bwd(HY_FFN), fwd(1), bwd(1),
                  pl.BlockSpec((None, 1, cb), lambda o, c: (o, 0, c)),
                  full((L, L)), full((L, L))],
        out_specs=pl.BlockSpec((None, 3, L, cb), lambda o, c: (o, 0, 0, c)),
        out_shape=jax.ShapeDtypeStruct((HY_ORDER, 3, L, HY_WIDTH), f32),
        compiler_params=_params("parallel", "parallel"),
        name="hy_filter",
    )(z, w1p, b1, w2, b2, freq, w3, w3, decay, decay, hy_bias.reshape(HY_ORDER, 1, HY_WIDTH), fc, fs)


def _hyena_kernel(v_ref, x1_ref, x2_ref, wv_ref, w1_ref, w2_ref, fc_ref, fs_ref, fst_ref, tab_ref, o_ref):
    L = v_ref.shape[0]
    row = lax.broadcasted_iota(jnp.int32, v_ref.shape, 0)

    def short_conv(u_ref, w_ref):
        u = u_ref[...]
        w = w_ref[...]
        prev = jnp.where(row == 0, 0.0, pltpu.roll(u, 1, 0))
        nxt = jnp.where(row == L - 1, 0.0, pltpu.roll(u, L - 1, 0))
        return prev * w[0:1] + u * w[1:2] + nxt * w[2:3]

    def long_conv(u, order):
        ub = u.astype(bf16)
        a = _dot(fc_ref[...], ub)
        b = _dot(fs_ref[...], ub)
        t0, t1, t2 = tab_ref[order, 0], tab_ref[order, 1], tab_ref[order, 2]
        pr = (a * t0 + b * t1).astype(bf16)
        qi = (b * t2 - a * t1).astype(bf16)
        return _dot(fc_ref[...], pr) + _dot(fst_ref[...], qi)

    z = short_conv(x1_ref, w1_ref) * long_conv(short_conv(v_ref, wv_ref), 0)
    o_ref[...] = (short_conv(x2_ref, w2_ref) * long_conv(z, 1)).astype(bf16)


def _hyena(p, hy_short, tabs, L, row0, nb):
    cb = HY_CB
    ncb = HY_WIDTH // cb
    c0 = HY_COL0 // cb
    fc, fs = _dft_tables(L)
    fcb, fsb, fstb = jnp.asarray(fc, bf16), jnp.asarray(fs, bf16), jnp.asarray(fs.T.copy(), bf16)
    rb0 = row0 // L

    def part(k):
        return pl.BlockSpec((L, cb), lambda c, b, k=k: (rb0 + b, c0 + k * ncb + c))

    def wpart(k):
        return pl.BlockSpec((3, cb), lambda c, b, k=k: (0, k * ncb + c))

    mat = pl.BlockSpec((L, L), lambda c, b: (0, 0))
    return pl.pallas_call(
        _hyena_kernel,
        grid=(ncb, nb),
        in_specs=[part(0), part(1), part(2), wpart(0), wpart(1), wpart(2), mat, mat, mat,
                  pl.BlockSpec((HY_ORDER, 3, L, cb), lambda c, b: (0, 0, 0, c))],
        out_specs=pl.BlockSpec((L, cb), lambda c, b: (b, c)),
        out_shape=jax.ShapeDtypeStruct((nb * L, HY_WIDTH), bf16),
        compiler_params=_params("parallel", "arbitrary"),
        name="hyena_%d" % L,
    )(p, p, p, hy_short, hy_short, hy_short, fcb, fsb, fstb, tabs)


def kernel(x_prompt, x_sample, cache_a_k, cache_a_v, cache_b_k, cache_b_v, cache_c_k, cache_c_v, c, c_ctx, w_ada, b_ada, g_mix, w_in, diff_lambda, diff_norm_g, na_bias, swa_sink, hy_short, hy_w1, hy_b1, hy_w2, hy_b2, hy_w3, hy_freq, hy_decay, hy_bias, w_branch, w_out, g_mlp, w_up, w_down, g_final):
    nb_ctx, nb_lat = x_prompt.shape[0], x_sample.shape[0]
    n_ctx = nb_ctx * CTX_SEQ
    n_lat = nb_lat * LAT_SEQ
    assert x_prompt.shape[1:] == (CTX_SEQ, D_MODEL) and x_sample.shape[1:] == (LAT_SEQ, D_MODEL)
    assert nb_lat <= CTX_MOD_ROW and n_ctx % TM_INPROJ == 0

    x = jnp.concatenate([x_prompt.reshape(n_ctx, D_MODEL), x_sample.reshape(n_lat, D_MODEL)], axis=0)
    cv = jnp.zeros((N_MOD_ROWS, D_MODEL), f32).at[:nb_lat].set(c).at[CTX_MOD_ROW].set(c_ctx)
    mod = _adaln_all(cv, w_ada, b_ada).reshape(DEPTH * N_MOD_ROWS * 6, 1, D_MODEL)
    cos_t, sin_t = (jnp.asarray(a) for a in _rope_tables())
    caches = (cache_a_k.reshape(nb_lat, DEPTH, CTX_SEQ, 512), cache_a_v.reshape(nb_lat, DEPTH, CTX_SEQ, 512),
              cache_b_k.reshape(nb_lat, DEPTH, CTX_SEQ, 512), cache_b_v.reshape(nb_lat, DEPTH, CTX_SEQ, 512),
              cache_c_k.reshape(nb_lat, DEPTH, CTX_SEQ, 128), cache_c_v.reshape(nb_lat, DEPTH, CTX_SEQ, 128))

    new_kv = [[] for _ in range(6)]
    for li in range(DEPTH):
        w_mix = w_in[li, :, :MIX_COLS].astype(bf16)
        w_gate = w_in[li, :, MIX_COLS:].reshape(D_MODEL, N_BRANCH, D_MODEL).transpose(1, 0, 2)
        w5 = jnp.concatenate([w_gate, w_out[li][None]], axis=0).astype(bf16)
        wb = w_branch[li].astype(bf16)
        wu = w_up[li].astype(bf16)
        wd = w_down[li].astype(bf16)
        g1 = g_mix[li].reshape(1, D_MODEL)
        g2 = g_mlp[li].reshape(1, D_MODEL)
        dl = diff_lambda[li]
        dg = diff_norm_g[li].reshape(1, A_V)
        sink = swa_sink[li]

        p = _inproj(x, mod, g1, w_mix, li, n_ctx)
        pc = p[:n_ctx].reshape(nb_ctx, CTX_SEQ, MIX_COLS)
        new_kv[0].append(pc[..., 512:1024].reshape(nb_ctx, CTX_SEQ, 2, A_HEADS, A_QK))
        new_kv[1].append(pc[..., 1024:1536].reshape(nb_ctx, CTX_SEQ, A_HEADS, A_V))
        new_kv[2].append(pc[..., 2048:2560].reshape(nb_ctx, CTX_SEQ, B_HEADS, HEAD_DIM))
        new_kv[3].append(pc[..., 2560:3072].reshape(nb_ctx, CTX_SEQ, B_HEADS, HEAD_DIM))
        new_kv[4].append(pc[..., 3584:3712].reshape(nb_ctx, CTX_SEQ, C_KV_HEADS, HEAD_DIM))
        new_kv[5].append(pc[..., 3712:3840].reshape(nb_ctx, CTX_SEQ, C_KV_HEADS, HEAD_DIM))

        oa_c, ob_c, oc_c = _ctx_attn(p, dl, dg, sink, li, n_ctx)
        na_dense = _na_bias_dense(na_bias[li])
        oa_l, ob_l, oc_l = _lat_attn(p, caches, na_dense, dl, dg, sink, cos_t, sin_t, li, n_ctx, nb_lat)

        w1p = jnp.pad(hy_w1[li], ((0, HY_EMB_PAD - HY_EMB), (0, 0)))
        hy_args = (w1p, hy_b1[li].reshape(1, HY_FFN), hy_w2[li], hy_b2[li].reshape(1, HY_FFN), hy_w3[li],
                   hy_freq[li].reshape(1, HY_FFN), hy_decay[li].reshape(1, -1), hy_bias[li])
        od_c = _hyena(p, hy_short[li], _hy_filter_tables(CTX_SEQ, *hy_args), CTX_SEQ, 0, nb_ctx)
        od_l = _hyena(p, hy_short[li], _hy_filter_tables(LAT_SEQ, *hy_args), LAT_SEQ, n_ctx, nb_lat)

        o_stack = jnp.stack([jnp.concatenate([oa_c, oa_l], axis=0), jnp.concatenate([ob_c, ob_l], axis=0),
                             jnp.concatenate([oc_c, oc_l], axis=0), jnp.concatenate([od_c, od_l], axis=0)], axis=0)
        x = _merge(x, mod, g1, o_stack, w5, wb, li, n_ctx)
        x = _mlp(x, mod, g2, wu, wd, g_final.reshape(1, D_MODEL), li, n_ctx, final_norm=(li == DEPTH - 1))

    y_prompt = x[:n_ctx].reshape(nb_ctx, CTX_SEQ, D_MODEL)
    y_sample = x[n_ctx:].reshape(nb_lat, LAT_SEQ, D_MODEL)
    return (y_prompt, y_sample) + tuple(jnp.stack(s, axis=1) for s in new_kv)
```

```python
import functools
import math

import numpy as np
import jax
import jax.numpy as jnp
from jax import lax
from jax.experimental import pallas as pl
from jax.experimental.pallas import tpu as pltpu

f32 = jnp.float32
bf16 = jnp.bfloat16

D_MODEL = 2048
DEPTH = 2
CTX_SEQ = 256
LAT_SEQ = 1024
GRID_W = 64
GRID_ROWS = LAT_SEQ // GRID_W
BRANCH_WIDTH = 512
N_BRANCH = 4
HEAD_DIM = 64
A_QK = 64
A_V = 128
A_HEADS = 4
B_HEADS = 8
NA_ROWS = 8
NA_COLS = 16
C_HEADS = 8
C_KV_HEADS = 2
C_GROUP = 4
SWA_WINDOW = 128
HY_WIDTH = 512
HY_ORDER = 2
HY_BANDS = 16
HY_EMB = 1 + 2 * HY_BANDS
HY_EMB_PAD = 128
HY_FFN = 64
D_FF = 4 * D_MODEL
ROPE_BASE = 10000.0
EPS = 1e-6
NEG_INF = -1e30
MIX_COLS = 5376
GATE_COLS = N_BRANCH * D_MODEL
COL_AQ, COL_BQ, COL_CQ, COL_AK, COL_AV, COL_BK, COL_BV = 0, 1, 2, 3, 4, 5, 6
COL_CKV_256 = 14
KV_COL0 = 1536
KV_COLS = 2304
HY_COL0 = 3840
N_MOD_ROWS = 16
CTX_MOD_ROW = 8

VMEM_LIMIT = 56 * 1024 * 1024
TM_INPROJ = 1024
TN_INPROJ = 768
TM_MERGE = 512
TM_MLP = 512
TF_MLP = 1024
TQ_ATTN = 256
TN_ADA = 1024
HY_CB = 256


def _params(*sem):
    return pltpu.CompilerParams(dimension_semantics=sem, vmem_limit_bytes=VMEM_LIMIT)


def _dot(a, b):
    return jnp.dot(a, b, preferred_element_type=f32)


def _dot_nt(a, b):
    return lax.dot_general(a, b, (((1,), (1,)), ((), ())), preferred_element_type=f32)


def _dot_hi(a, b):
    return jnp.dot(a, b, preferred_element_type=f32, precision=lax.Precision.HIGHEST)


@functools.lru_cache(maxsize=None)
def _rope_tables():
    half = HEAD_DIM // 2
    nf = half // 2
    inv = ROPE_BASE ** (-np.arange(nf, dtype=np.float64) / nf)
    t = np.arange(LAT_SEQ)
    pos = np.stack([t // GRID_W, t % GRID_W], axis=1).astype(np.float64)
    lane = np.arange(HEAD_DIM)
    ang = pos[:, lane // half] * inv[lane % nf][None, :]
    first = (lane % half) < nf
    cos = np.cos(ang)
    sin = np.where(first[None, :], -np.sin(ang), np.sin(ang))
    reps = 512 // HEAD_DIM
    return (np.tile(cos, (1, reps)).astype(np.float32), np.tile(sin, (1, reps)).astype(np.float32))


@functools.lru_cache(maxsize=None)
def _dft_tables(L):
    f = np.arange(L, dtype=np.int64)
    prod = (f[:, None] * f[None, :]) % (2 * L)
    ang = np.pi * prod.astype(np.float64) / L
    fc = np.cos(ang)
    fs = np.sin(ang)
    fs[0, :] = np.where(f % 2 == 0, 1.0, -1.0)
    return fc.astype(np.float32), fs.astype(np.float32)


@functools.lru_cache(maxsize=None)
def _hyena_features(L):
    n = np.arange(L, dtype=np.float64)[:, None]
    t = n / max(L - 1, 1)
    w = 2.0 * math.pi * n / L
    bands = np.linspace(1e-4, HY_BANDS - 1, HY_BANDS, dtype=np.float64)[None, :]
    z = np.concatenate([t, np.cos(bands * w), -np.sin(bands * w)], axis=-1)
    z = np.pad(z, ((0, 0), (0, HY_EMB_PAD - HY_EMB)))
    return z.astype(np.float32)


def _modulated_norm(x, g, scale, shift):
    ms = jnp.mean(x * x, axis=-1, keepdims=True)
    y = x * lax.rsqrt(ms + EPS) * g
    return y * (1.0 + scale) + shift


def _rope(x, cos, sin_signed):
    n = x.shape[-1]
    lane = lax.broadcasted_iota(jnp.int32, x.shape, 1)
    first = (lane & (HEAD_DIM // 2 - 1)) < (HEAD_DIM // 4)
    partner = jnp.where(first, pltpu.roll(x, n - HEAD_DIM // 4, 1), pltpu.roll(x, HEAD_DIM // 4, 1))
    return x * cos + partner * sin_signed


def _softmax_parts(scores, sink=None):
    m = jnp.max(scores[0], axis=-1, keepdims=True)
    for s in scores[1:]:
        m = jnp.maximum(m, jnp.max(s, axis=-1, keepdims=True))
    if sink is not None:
        m = jnp.maximum(m, sink)
    es = [jnp.exp(s - m) for s in scores]
    d = jnp.sum(es[0], axis=-1, keepdims=True)
    for e in es[1:]:
        d = d + jnp.sum(e, axis=-1, keepdims=True)
    if sink is not None:
        d = d + jnp.exp(sink - m)
    return es, 1.0 / d


def _diff_lambda(dl_ref, lam_init):
    dl = dl_ref[...]
    a = jnp.sum(dl[0:1] * dl[1:2], axis=-1, keepdims=True)
    b = jnp.sum(dl[2:3] * dl[3:4], axis=-1, keepdims=True)
    return jnp.exp(a) - jnp.exp(b) + lam_init


def _diff_head_out(w, v, dg, lam_init):
    o = _dot(w.astype(bf16), v)
    o = o * lax.rsqrt(jnp.mean(o * o, axis=-1, keepdims=True) + EPS) * dg
    return o * (1.0 - lam_init)


def _ada_kernel(cv_ref, w_ref, b_ref, o_ref):
    cv = cv_ref[...]
    s = (cv * jax.nn.sigmoid(cv)).astype(bf16)
    o_ref[...] = _dot(s, w_ref[...].astype(bf16)) + b_ref[...]


def _adaln_all(cv, w_ada, b_ada):
    n6 = 6 * D_MODEL
    return pl.pallas_call(
        _ada_kernel,
        grid=(DEPTH, n6 // TN_ADA),
        in_specs=[pl.BlockSpec((N_MOD_ROWS, D_MODEL), lambda l, j: (0, 0)),
                  pl.BlockSpec((None, D_MODEL, TN_ADA), lambda l, j: (l, 0, j)),
                  pl.BlockSpec((None, 1, TN_ADA), lambda l, j: (l, 0, j))],
        out_specs=pl.BlockSpec((None, N_MOD_ROWS, TN_ADA), lambda l, j: (l, 0, j)),
        out_shape=jax.ShapeDtypeStruct((DEPTH, N_MOD_ROWS, n6), f32),
        compiler_params=_params("parallel", "parallel"),
        name="adaln",
    )(cv, w_ada, b_ada.reshape(DEPTH, 1, n6))


def _mod_spec(li, k, tm, n_ctx):
    n_ctx_tiles = n_ctx // tm

    def index(i, j):
        r = jnp.where(i < n_ctx_tiles, CTX_MOD_ROW, ((i - n_ctx_tiles) * tm) // LAT_SEQ)
        return ((li * N_MOD_ROWS + r) * 6 + k, 0, 0)

    return pl.BlockSpec((None, 1, D_MODEL), index)


def _inproj_kernel(x_ref, sh_ref, sc_ref, g_ref, w_ref, *rest, n_ctx_tiles, kv_j0, kv_nj):
    o_ref, kv_ref, h_ref = rest[-3:]
    i, j = pl.program_id(0), pl.program_id(1)

    @pl.when(j == 0)
    def _():
        h_ref[...] = _modulated_norm(x_ref[...], g_ref[...], sc_ref[...], sh_ref[...]).astype(bf16)

    res = _dot(h_ref[...], w_ref[...])
    o_ref[...] = res

    @pl.when((i < n_ctx_tiles) & (j >= kv_j0) & (j < kv_j0 + kv_nj))
    def _():
        kv_ref[...] = res.reshape(kv_ref.shape)


def _inproj(x, mod, g, w, kv_prev, li, n_ctx):
    T = x.shape[0]
    tm, tn = TM_INPROJ, TN_INPROJ
    nct = n_ctx // tm
    kv_j0, kv_nj = KV_COL0 // tn, KV_COLS // tn
    nb_ctx = n_ctx // CTX_SEQ

    def kv_index(i, j):
        ctx = i < nct
        return (jnp.where(ctx, i, nct - 1), li, 0, jnp.where(ctx, jnp.clip(j - kv_j0, 0, kv_nj - 1), kv_nj - 1))

    in_specs = [pl.BlockSpec((tm, D_MODEL), lambda i, j: (i, 0)),
                _mod_spec(li, 0, tm, n_ctx), _mod_spec(li, 1, tm, n_ctx),
                pl.BlockSpec((1, D_MODEL), lambda i, j: (0, 0)),
                pl.BlockSpec((D_MODEL, tn), lambda i, j: (0, j))]
    args = [x, mod, mod, g, w]
    aliases = {}
    if kv_prev is not None:
        in_specs.append(pl.BlockSpec(memory_space=pl.ANY))
        args.append(kv_prev)
        aliases = {len(args) - 1: 1}
    return pl.pallas_call(
        functools.partial(_inproj_kernel, n_ctx_tiles=nct, kv_j0=kv_j0, kv_nj=kv_nj),
        grid=(T // tm, MIX_COLS // tn),
        in_specs=in_specs,
        out_specs=[pl.BlockSpec((tm, tn), lambda i, j: (i, j)),
                   pl.BlockSpec((tm // CTX_SEQ, None, CTX_SEQ, tn), kv_index)],
        out_shape=[jax.ShapeDtypeStruct((T, MIX_COLS), f32),
                   jax.ShapeDtypeStruct((nb_ctx, DEPTH, CTX_SEQ, KV_COLS), f32)],
        scratch_shapes=[pltpu.VMEM((tm, D_MODEL), bf16)],
        input_output_aliases=aliases,
        compiler_params=_params("arbitrary", "arbitrary"),
        name="inproj",
    )(*args)


def _merge_kernel(x_ref, sh_ref, sc_ref, gt_ref, g_ref, o_ref, w_ref, wb_ref, out_ref, h_ref, acc_ref):
    n = pl.program_id(1)

    @pl.when(n == 0)
    def _():
        h_ref[...] = _modulated_norm(x_ref[...], g_ref[...], sc_ref[...], sh_ref[...]).astype(bf16)
        acc_ref[...] = jnp.zeros_like(acc_ref)

    @pl.when(n < N_BRANCH)
    def _():
        gate = jax.nn.sigmoid(_dot(h_ref[...], w_ref[...]))
        acc_ref[...] += gate * _dot(o_ref[...], wb_ref[...])

    @pl.when(n == N_BRANCH)
    def _():
        y = _dot(acc_ref[...].astype(bf16), w_ref[...])
        out_ref[...] = x_ref[...] + gt_ref[...] * y


def _merge(x, mod, g, o_stack, w5, wb, li, n_ctx):
    T = x.shape[0]
    tm = TM_MERGE
    last = N_BRANCH - 1
    return pl.pallas_call(
        _merge_kernel,
        grid=(T // tm, N_BRANCH + 1),
        in_specs=[pl.BlockSpec((tm, D_MODEL), lambda i, n: (i, 0)),
                  _mod_spec(li, 0, tm, n_ctx), _mod_spec(li, 1, tm, n_ctx), _mod_spec(li, 2, tm, n_ctx),
                  pl.BlockSpec((1, D_MODEL), lambda i, n: (0, 0)),
                  pl.BlockSpec((None, tm, BRANCH_WIDTH), lambda i, n: (jnp.minimum(n, last), i, 0)),
                  pl.BlockSpec((D_MODEL, D_MODEL), lambda i, n: (0, n)),
                  pl.BlockSpec((None, BRANCH_WIDTH, D_MODEL), lambda i, n: (jnp.minimum(n, last), 0, 0))],
        out_specs=pl.BlockSpec((tm, D_MODEL), lambda i, n: (i, 0)),
        out_shape=jax.ShapeDtypeStruct((T, D_MODEL), f32),
        scratch_shapes=[pltpu.VMEM((tm, D_MODEL), bf16), pltpu.VMEM((tm, D_MODEL), f32)],
        compiler_params=_params("parallel", "arbitrary"),
        name="merge",
    )(x, mod, mod, mod, g, o_stack, w5, wb)


def _mlp_kernel(x_ref, sh_ref, sc_ref, gt_ref, g_ref, wu_ref, wd_ref, gf_ref, out_ref, h_ref, acc_ref,
                *, final_norm):
    j = pl.program_id(1)

    @pl.when(j == 0)
    def _():
        h_ref[...] = _modulated_norm(x_ref[...], g_ref[...], sc_ref[...], sh_ref[...]).astype(bf16)
        acc_ref[...] = jnp.zeros_like(acc_ref)

    a = jnp.maximum(_dot(h_ref[...], wu_ref[...]), 0.0)
    acc_ref[...] += _dot((a * a).astype(bf16), wd_ref[...])

    @pl.when(j == pl.num_programs(1) - 1)
    def _():
        y = x_ref[...] + gt_ref[...] * acc_ref[...]
        if final_norm:
            y = y * lax.rsqrt(jnp.mean(y * y, axis=-1, keepdims=True) + EPS) * gf_ref[...]
        out_ref[...] = y


def _mlp(x, mod, g, w_up, w_down, g_final, li, n_ctx, final_norm):
    T = x.shape[0]
    tm, tf = TM_MLP, TF_MLP
    return pl.pallas_call(
        functools.partial(_mlp_kernel, final_norm=final_norm),
        grid=(T // tm, D_FF // tf),
        in_specs=[pl.BlockSpec((tm, D_MODEL), lambda i, j: (i, 0)),
                  _mod_spec(li, 3, tm, n_ctx), _mod_spec(li, 4, tm, n_ctx), _mod_spec(li, 5, tm, n_ctx),
                  pl.BlockSpec((1, D_MODEL), lambda i, j: (0, 0)),
                  pl.BlockSpec((D_MODEL, tf), lambda i, j: (0, j)),
                  pl.BlockSpec((tf, D_MODEL), lambda i, j: (j, 0)),
                  pl.BlockSpec((1, D_MODEL), lambda i, j: (0, 0))],
        out_specs=pl.BlockSpec((tm, D_MODEL), lambda i, j: (i, 0)),
        out_shape=jax.ShapeDtypeStruct((T, D_MODEL), f32),
        scratch_shapes=[pltpu.VMEM((tm, D_MODEL), bf16), pltpu.VMEM((tm, D_MODEL), f32)],
        compiler_params=_params("parallel", "arbitrary"),
        name="mlp",
    )(x, mod, mod, mod, g, w_up, w_down, g_final)


def _ctx_attn_kernel(aq_ref, ak_ref, av_ref, bq_ref, bk_ref, bv_ref, cq_ref, ckv_ref,
                     dl_ref, dg_ref, sink_ref, o_ref, *, lam_init):
    oa_ref, ob_ref, oc_ref = o_ref.at[0], o_ref.at[1], o_ref.at[2]
    scale = HEAD_DIM ** -0.5
    lam = _diff_lambda(dl_ref, lam_init)
    q = (aq_ref[...] * scale).astype(bf16)
    k = ak_ref[...].astype(bf16)
    v = av_ref[...].astype(bf16)
    dg = dg_ref[...]
    half = A_HEADS * A_QK
    for h in range(A_HEADS):
        prs = []
        for c in range(2):
            lo = c * half + h * A_QK
            (e,), r = _softmax_parts([_dot_nt(q[:, lo:lo + A_QK], k[:, lo:lo + A_QK])])
            prs.append((e, r))
        w = prs[0][0] * prs[0][1] - prs[1][0] * (lam * prs[1][1])
        oa_ref[:, h * A_V:(h + 1) * A_V] = _diff_head_out(w, v[:, h * A_V:(h + 1) * A_V], dg, lam_init).astype(bf16)
    q = (bq_ref[...] * scale).astype(bf16)
    k = bk_ref[...].astype(bf16)
    v = bv_ref[...].astype(bf16)
    for h in range(B_HEADS):
        sl = slice(h * HEAD_DIM, (h + 1) * HEAD_DIM)
        (e,), r = _softmax_parts([_dot_nt(q[:, sl], k[:, sl])])
        ob_ref[:, sl] = (_dot(e.astype(bf16), v[:, sl]) * r).astype(bf16)
    q = (cq_ref[...] * scale).astype(bf16)
    kv = ckv_ref[...].astype(bf16)
    kw = C_KV_HEADS * HEAD_DIM
    for h in range(C_HEADS):
        g = h // C_GROUP
        sl = slice(h * HEAD_DIM, (h + 1) * HEAD_DIM)
        kg = kv[:, g * HEAD_DIM:(g + 1) * HEAD_DIM]
        vg = kv[:, kw + g * HEAD_DIM:kw + (g + 1) * HEAD_DIM]
        (e,), r = _softmax_parts([_dot_nt(q[:, sl], kg)], sink=sink_ref[h])
        oc_ref[:, sl] = (_dot(e.astype(bf16), vg) * r).astype(bf16)


def _ctx_attn(p, dl, dg, sink, li, n_ctx):
    nb = n_ctx // CTX_SEQ
    S = CTX_SEQ
    lam_init = 0.8 - 0.6 * math.exp(-0.3 * li)

    def col(c):
        return pl.BlockSpec((S, 512), lambda b, c=c: (b, c))

    return pl.pallas_call(
        functools.partial(_ctx_attn_kernel, lam_init=lam_init),
        grid=(nb,),
        in_specs=[col(COL_AQ), col(COL_AK), col(COL_AV), col(COL_BQ), col(COL_BK), col(COL_BV), col(COL_CQ),
                  pl.BlockSpec((S, 256), lambda b: (b, COL_CKV_256)),
                  pl.BlockSpec((4, A_QK), lambda b: (0, 0)),
                  pl.BlockSpec((1, A_V), lambda b: (0, 0)),
                  pl.BlockSpec(memory_space=pltpu.SMEM)],
        out_specs=pl.BlockSpec((N_BRANCH - 1, S, BRANCH_WIDTH), lambda b: (0, b, 0)),
        out_shape=jax.ShapeDtypeStruct((N_BRANCH, p.shape[0], BRANCH_WIDTH), bf16),
        compiler_params=_params("parallel"),
        name="ctx_attn",
    )(p, p, p, p, p, p, p, p, dl, dg, sink)


def _lat_a_kernel(q_ref, k_ref, v_ref, ck_ref, cv_ref, cosq_ref, sinq_ref, cosk_ref, sink_ref,
                  dl_ref, dg_ref, _, o_ref, kk_ref, vv_ref, *, lam_init):
    L = LAT_SEQ

    @pl.when(pl.program_id(1) == 0)
    def _():
        kk_ref[0:L, :] = _rope(k_ref[...], cosk_ref[...], sink_ref[...]).astype(bf16)
        kk_ref[L:, :] = ck_ref[...].astype(bf16)
        vv_ref[0:L, :] = v_ref[...].astype(bf16)
        vv_ref[L:, :] = cv_ref[...].astype(bf16)

    lam = _diff_lambda(dl_ref, lam_init)
    q = (_rope(q_ref[...], cosq_ref[...], sinq_ref[...]) * (A_QK ** -0.5)).astype(bf16)
    dg = dg_ref[...]
    half = A_HEADS * A_QK
    for h in range(A_HEADS):
        prs = []
        for c in range(2):
            lo = c * half + h * A_QK
            (e,), r = _softmax_parts([_dot_nt(q[:, lo:lo + A_QK], kk_ref[:, lo:lo + A_QK])])
            prs.append((e, r))
        w = prs[0][0] * prs[0][1] - prs[1][0] * (lam * prs[1][1])
        o_ref[:, h * A_V:(h + 1) * A_V] = _diff_head_out(w, vv_ref[:, h * A_V:(h + 1) * A_V], dg, lam_init).astype(bf16)


def _lat_b_kernel(q_ref, k_ref, v_ref, ck_ref, cv_ref, bias_ref, _, o_ref, kk_ref, vv_ref):
    L = LAT_SEQ

    @pl.when(pl.program_id(1) == 0)
    def _():
        kk_ref[0:L, :] = k_ref[...].astype(bf16)
        kk_ref[L:, :] = ck_ref[...].astype(bf16)
        vv_ref[0:L, :] = v_ref[...].astype(bf16)
        vv_ref[L:, :] = cv_ref[...].astype(bf16)

    q = (q_ref[...] * (HEAD_DIM ** -0.5)).astype(bf16)
    for h in range(B_HEADS):
        sl = slice(h * HEAD_DIM, (h + 1) * HEAD_DIM)
        s_loc = _dot_nt(q[:, sl], kk_ref[0:L, sl]) + bias_ref[h].astype(f32)
        s_ctx = _dot_nt(q[:, sl], kk_ref[L:, sl])
        (e_loc, e_ctx), r = _softmax_parts([s_loc, s_ctx])
        o = _dot(e_loc.astype(bf16), vv_ref[0:L, sl]) + _dot(e_ctx.astype(bf16), vv_ref[L:, sl])
        o_ref[:, sl] = (o * r).astype(bf16)


def _lat_c_kernel(q_ref, kv_ref, ck_ref, cv_ref, cosq_ref, sinq_ref, cosk_ref, sink_ref, snk_ref,
                  _, o_ref, kk_ref, vv_ref):
    L = LAT_SEQ
    kw = C_KV_HEADS * HEAD_DIM
    tq = q_ref.shape[0]

    @pl.when(pl.program_id(1) == 0)
    def _():
        kk_ref[0:L, :] = _rope(kv_ref[:, 0:kw], cosk_ref[...], sink_ref[...]).astype(bf16)
        kk_ref[L:, :] = ck_ref[...].astype(bf16)
        vv_ref[0:L, :] = kv_ref[:, kw:2 * kw].astype(bf16)
        vv_ref[L:, :] = cv_ref[...].astype(bf16)

    q = (_rope(q_ref[...], cosq_ref[...], sinq_ref[...]) * (HEAD_DIM ** -0.5)).astype(bf16)
    qpos = pl.program_id(1) * tq + lax.broadcasted_iota(jnp.int32, (tq, L), 0)
    kpos = lax.broadcasted_iota(jnp.int32, (tq, L), 1)
    valid = jnp.abs(qpos - kpos) <= SWA_WINDOW
    for h in range(C_HEADS):
        g = h // C_GROUP
        sl = slice(h * HEAD_DIM, (h + 1) * HEAD_DIM)
        gl = slice(g * HEAD_DIM, (g + 1) * HEAD_DIM)
        s_loc = jnp.where(valid, _dot_nt(q[:, sl], kk_ref[0:L, gl]), NEG_INF)
        s_ctx = _dot_nt(q[:, sl], kk_ref[L:, gl])
        (e_loc, e_ctx), r = _softmax_parts([s_loc, s_ctx], sink=snk_ref[h])
        o = _dot(e_loc.astype(bf16), vv_ref[0:L, gl]) + _dot(e_ctx.astype(bf16), vv_ref[L:, gl])
        o_ref[:, sl] = (o * r).astype(bf16)


def _lat_attn(p, o_stack, caches, na_dense, dl, dg, sink, cos_t, sin_t, li, n_ctx, nb_lat):
    cak, cav, cbk, cbv, cck, ccv = caches
    L, tq = LAT_SEQ, TQ_ATTN
    nq = L // tq
    lam_init = 0.8 - 0.6 * math.exp(-0.3 * li)
    row_l = n_ctx // L
    row_q = n_ctx // tq

    def qcol(c):
        return pl.BlockSpec((tq, 512), lambda b, i, c=c: (row_q + b * nq + i, c))

    def kcol(c, width=512):
        return pl.BlockSpec((L, width), lambda b, i, c=c: (row_l + b, c))

    def cache(width):
        return pl.BlockSpec((None, None, CTX_SEQ, width), lambda b, i: (b, li, 0, 0))

    tab_q = pl.BlockSpec((tq, 512), lambda b, i: (i, 0))
    tab_k = pl.BlockSpec((L, 512), lambda b, i: (0, 0))
    tab_k128 = pl.BlockSpec((L, 128), lambda b, i: (0, 0))
    o_sds = jax.ShapeDtypeStruct(o_stack.shape, o_stack.dtype)
    anyspec = pl.BlockSpec(memory_space=pl.ANY)

    def o_spec(branch):
        return pl.BlockSpec((None, tq, BRANCH_WIDTH), lambda b, i: (branch, row_q + b * nq + i, 0))

    kv_scratch = [pltpu.VMEM((L + CTX_SEQ, 512), bf16), pltpu.VMEM((L + CTX_SEQ, 512), bf16)]
    smem = pl.BlockSpec(memory_space=pltpu.SMEM)
    cp = _params("parallel", "arbitrary")

    o_stack = pl.pallas_call(
        functools.partial(_lat_a_kernel, lam_init=lam_init),
        grid=(nb_lat, nq),
        in_specs=[qcol(COL_AQ), kcol(COL_AK), kcol(COL_AV), cache(512), cache(512),
                  tab_q, tab_q, tab_k, tab_k,
                  pl.BlockSpec((4, A_QK), lambda b, i: (0, 0)), pl.BlockSpec((1, A_V), lambda b, i: (0, 0)), anyspec],
        out_specs=o_spec(0), out_shape=o_sds, scratch_shapes=kv_scratch, input_output_aliases={11: 0},
        compiler_params=cp, name="lat_attn_a",
    )(p, p, p, cak, cav, cos_t, sin_t, cos_t, sin_t, dl, dg, o_stack)

    o_stack = pl.pallas_call(
        _lat_b_kernel,
        grid=(nb_lat, nq),
        in_specs=[qcol(COL_BQ), kcol(COL_BK), kcol(COL_BV), cache(512), cache(512),
                  pl.BlockSpec((B_HEADS, tq, L), lambda b, i: (0, i, 0)), anyspec],
        out_specs=o_spec(1), out_shape=o_sds, scratch_shapes=kv_scratch, input_output_aliases={6: 0},
        compiler_params=cp, name="lat_attn_b",
    )(p, p, p, cbk, cbv, na_dense, o_stack)

    kw = C_KV_HEADS * HEAD_DIM
    return pl.pallas_call(
        _lat_c_kernel,
        grid=(nb_lat, nq),
        in_specs=[qcol(COL_CQ), kcol(COL_CKV_256, 256), cache(kw), cache(kw),
                  tab_q, tab_q, tab_k128, tab_k128, smem, anyspec],
        out_specs=o_spec(2), out_shape=o_sds,
        scratch_shapes=[pltpu.VMEM((L + CTX_SEQ, kw), bf16), pltpu.VMEM((L + CTX_SEQ, kw), bf16)],
        input_output_aliases={9: 0}, compiler_params=cp, name="lat_attn_c",
    )(p, p, cck, ccv, cos_t, sin_t, cos_t, sin_t, sink, o_stack)


def _na_bias_kernel(rb_ref, o_ref):
    h = pl.program_id(0)
    W = GRID_W
    n_dc = 2 * NA_COLS - 1
    n_dr = 2 * NA_ROWS - 1
    qc = lax.broadcasted_iota(jnp.int32, (W, 2 * W), 0)
    lane = lax.broadcasted_iota(jnp.int32, (W, 2 * W), 1)
    second = lane >= W
    kc = jnp.where(second, lane - W, lane)
    dc = jnp.clip(kc - qc, -(NA_COLS - 1), NA_COLS - 1) + NA_COLS - 1
    c0 = jnp.clip(qc - NA_COLS // 2, 0, W - NA_COLS)
    col_ok = (kc >= c0) & (kc < c0 + NA_COLS)
    base = h * (n_dr * n_dc)

    def pair_tile(dr0):
        t = jnp.zeros((W, 2 * W), f32)
        for j in range(n_dc):
            lo = rb_ref[base + dr0 * n_dc + j] if 0 <= dr0 < n_dr else 0.0
            hi = rb_ref[base + (dr0 + 1) * n_dc + j] if 0 <= dr0 + 1 < n_dr else 0.0
            t = jnp.where(dc == j, jnp.where(second, hi, lo), t)
        return jnp.where(col_ok, t, NEG_INF)

    tiles = {dr0: pair_tile(dr0) for dr0 in range(-1, n_dr)}
    neg = jnp.full((W, 2 * W), NEG_INF, f32)
    for qr in range(GRID_ROWS):
        r0 = min(max(qr - NA_ROWS // 2, 0), GRID_ROWS - NA_ROWS)
        for pr in range(GRID_ROWS // 2):
            kr = 2 * pr
            ok0 = r0 <= kr < r0 + NA_ROWS
            ok1 = r0 <= kr + 1 < r0 + NA_ROWS
            if not (ok0 or ok1):
                t = neg
            else:
                t = tiles[kr - qr + NA_ROWS - 1]
                if not ok0:
                    t = jnp.where(second, t, NEG_INF)
                if not ok1:
                    t = jnp.where(second, NEG_INF, t)
            o_ref[qr * W:(qr + 1) * W, kr * W:(kr + 2) * W] = t.astype(bf16)


def _na_bias_dense(rel_bias):
    return pl.pallas_call(
        _na_bias_kernel,
        grid=(B_HEADS,),
        in_specs=[pl.BlockSpec(memory_space=pltpu.SMEM)],
        out_specs=pl.BlockSpec((None, LAT_SEQ, LAT_SEQ), lambda h: (h, 0, 0)),
        out_shape=jax.ShapeDtypeStruct((B_HEADS, LAT_SEQ, LAT_SEQ), bf16),
        compiler_params=_params("parallel"),
        name="na_bias",
    )(rel_bias.reshape(-1))


def _hy_filter_kernel(z_ref, w1_ref, b1_ref, w2_ref, b2_ref, fr_ref, w3f_ref, w3b_ref, dcf_ref, dcb_ref,
                      bias_ref, fc_ref, fs_ref, o_ref, *, L):
    z = z_ref[...]
    fr = fr_ref[...]
    h = jnp.sin(fr * (_dot_hi(z, w1_ref[...]) + b1_ref[...]))
    h = jnp.sin(fr * (_dot_hi(h, w2_ref[...]) + b2_ref[...]))
    t = z[:, 0:1]
    hf = _dot_hi(h, w3f_ref[...]) * jnp.exp(-t * jnp.abs(dcf_ref[...]))
    hb = _dot_hi(h, w3b_ref[...]) * jnp.exp(-t * jnp.abs(dcb_ref[...]))
    ssum = hf + hb
    bias = bias_ref[...]
    ga = _dot_hi(fc_ref[...], ssum) + bias
    gb = _dot_hi(fs_ref[...], hb - hf)
    g_nyq = _dot_hi(fs_ref[0:8, :], ssum)[0:1] + bias
    row0 = lax.broadcasted_iota(jnp.int32, ga.shape, 0) == 0
    inv = 1.0 / L
    o_ref[0] = jnp.where(row0, 0.5 * ga, ga) * inv
    o_ref[1] = jnp.where(row0, 0.0, gb) * inv
    o_ref[2] = jnp.where(row0, 0.5 * g_nyq, ga) * inv


def _hy_filter_tables(L, w1p, b1, w2, b2, w3, freq, decay, hy_bias):
    cb = HY_CB
    ncb = HY_WIDTH // cb
    z = jnp.asarray(_hyena_features(L))
    fc, fs = (jnp.asarray(a) for a in _dft_tables(L))
    full = lambda shape: pl.BlockSpec(shape, lambda o, c: (0,) * len(shape))
    fwd = lambda rows: pl.BlockSpec((rows, cb), lambda o, c: (0, o * 2 * ncb + c))
    bwd = lambda rows: pl.BlockSpec((rows, cb), lambda o, c: (0, o * 2 * ncb + ncb + c))
    return pl.pallas_call(
        functools.partial(_hy_filter_kernel, L=L),
        grid=(HY_ORDER, ncb),
        in_specs=[full((L, HY_EMB_PAD)), full((HY_EMB_PAD, HY_FFN)), full((1, HY_FFN)),
                  full((HY_FFN, HY_FFN)), full((1, HY_FFN)), full((1, HY_FFN)),
                  fwd(HY_FFN), bwd(HY_FFN), fwd(1), bwd(1),
                  pl.BlockSpec((None, 1, cb), lambda o, c: (o, 0, c)),
                  full((L, L)), full((L, L))],
        out_specs=pl.BlockSpec((None, 3, L, cb), lambda o, c: (o, 0, 0, c)),
        out_shape=jax.ShapeDtypeStruct((HY_ORDER, 3, L, HY_WIDTH), f32),
        compiler_params=_params("parallel", "parallel"),
        name="hy_filter",
    )(z, w1p, b1, w2, b2, freq, w3, w3, decay, decay, hy_bias.reshape(HY_ORDER, 1, HY_WIDTH), fc, fs)


def _hyena_kernel(v_ref, x1_ref, x2_ref, wv_ref, w1_ref, w2_ref, fc_ref, fs_ref, fst_ref, tab_ref, _, o_ref):
    L = v_ref.shape[0]
    row = lax.broadcasted_iota(jnp.int32, v_ref.shape, 0)

    def short_conv(u_ref, w_ref):
        u = u_ref[...]
        w = w_ref[...]
        prev = jnp.where(row == 0, 0.0, pltpu.roll(u, 1, 0))
        nxt = jnp.where(row == L - 1, 0.0, pltpu.roll(u, L - 1, 0))
        return prev * w[0:1] + u * w[1:2] + nxt * w[2:3]

    def long_conv(u, order):
        ub = u.astype(bf16)
        a = _dot(fc_ref[...], ub)
        b = _dot(fs_ref[...], ub)
        t0, t1, t2 = tab_ref[order, 0], tab_ref[order, 1], tab_ref[order, 2]
        pr = (a * t0 + b * t1).astype(bf16)
        qi = (b * t2 - a * t1).astype(bf16)
        return _dot(fc_ref[...], pr) + _dot(fst_ref[...], qi)

    z = short_conv(x1_ref, w1_ref) * long_conv(short_conv(v_ref, wv_ref), 0)
    o_ref[...] = (short_conv(x2_ref, w2_ref) * long_conv(z, 1)).astype(bf16)


def _hyena(p, o_stack, hy_short, tabs, L, row0, nb):
    cb = HY_CB
    ncb = HY_WIDTH // cb
    c0 = HY_COL0 // cb
    fc, fs = _dft_tables(L)
    fcb, fsb, fstb = jnp.asarray(fc, bf16), jnp.asarray(fs, bf16), jnp.asarray(fs.T.copy(), bf16)
    rb0 = row0 // L

    def part(k):
        return pl.BlockSpec((L, cb), lambda c, b, k=k: (rb0 + b, c0 + k * ncb + c))

    def wpart(k):
        return pl.BlockSpec((3, cb), lambda c, b, k=k: (0, k * ncb + c))

    mat = pl.BlockSpec((L, L), lambda c, b: (0, 0))
    return pl.pallas_call(
        _hyena_kernel,
        grid=(ncb, nb),
        in_specs=[part(0), part(1), part(2), wpart(0), wpart(1), wpart(2), mat, mat, mat,
                  pl.BlockSpec((HY_ORDER, 3, L, cb), lambda c, b: (0, 0, 0, c)),
                  pl.BlockSpec(memory_space=pl.ANY)],
        out_specs=pl.BlockSpec((None, L, cb), lambda c, b: (N_BRANCH - 1, rb0 + b, c)),
        out_shape=jax.ShapeDtypeStruct(o_stack.shape, o_stack.dtype),
        input_output_aliases={10: 0},
        compiler_params=_params("parallel", "arbitrary"),
        name="hyena_%d" % L,
    )(p, p, p, hy_short, hy_short, hy_short, fcb, fsb, fstb, tabs, o_stack)


def kernel(x_prompt, x_sample, cache_a_k, cache_a_v, cache_b_k, cache_b_v, cache_c_k, cache_c_v, c, c_ctx, w_ada, b_ada, g_mix, w_in, diff_lambda, diff_norm_g, na_bias, swa_sink, hy_short, hy_w1, hy_b1, hy_w2, hy_b2, hy_w3, hy_freq, hy_decay, hy_bias, w_branch, w_out, g_mlp, w_up, w_down, g_final):
    nb_ctx, nb_lat = x_prompt.shape[0], x_sample.shape[0]
    n_ctx = nb_ctx * CTX_SEQ
    n_lat = nb_lat * LAT_SEQ
    assert x_prompt.shape[1:] == (CTX_SEQ, D_MODEL) and x_sample.shape[1:] == (LAT_SEQ, D_MODEL)
    assert nb_lat <= CTX_MOD_ROW and n_ctx % TM_INPROJ == 0

    x = jnp.concatenate([x_prompt.reshape(n_ctx, D_MODEL), x_sample.reshape(n_lat, D_MODEL)], axis=0)
    cv = jnp.zeros((N_MOD_ROWS, D_MODEL), f32).at[:nb_lat].set(c).at[CTX_MOD_ROW].set(c_ctx)
    mod = _adaln_all(cv, w_ada, b_ada).reshape(DEPTH * N_MOD_ROWS * 6, 1, D_MODEL)
    cos_t, sin_t = (jnp.asarray(a) for a in _rope_tables())
    caches = (cache_a_k.reshape(nb_lat, DEPTH, CTX_SEQ, 512), cache_a_v.reshape(nb_lat, DEPTH, CTX_SEQ, 512),
              cache_b_k.reshape(nb_lat, DEPTH, CTX_SEQ, 512), cache_b_v.reshape(nb_lat, DEPTH, CTX_SEQ, 512),
              cache_c_k.reshape(nb_lat, DEPTH, CTX_SEQ, 128), cache_c_v.reshape(nb_lat, DEPTH, CTX_SEQ, 128))

    kv = None
    for li in range(DEPTH):
        wl = w_in[li]
        w_mix = jnp.concatenate([wl[:, 0:512], wl[:, 1536:2048], wl[:, 3072:3584], wl[:, 512:1536],
                                 wl[:, 2048:3072], wl[:, 3584:MIX_COLS]], axis=1).astype(bf16)
        w5 = jnp.concatenate([wl[:, MIX_COLS:], w_out[li]], axis=1).astype(bf16)
        wb = w_branch[li].astype(bf16)
        wu = w_up[li].astype(bf16)
        wd = w_down[li].astype(bf16)
        g1 = g_mix[li].reshape(1, D_MODEL)
        g2 = g_mlp[li].reshape(1, D_MODEL)
        dl = diff_lambda[li]
        dg = diff_norm_g[li].reshape(1, A_V)
        sink = swa_sink[li]

        p, kv = _inproj(x, mod, g1, w_mix, kv, li, n_ctx)
        o_stack = _ctx_attn(p, dl, dg, sink, li, n_ctx)
        na_dense = _na_bias_dense(na_bias[li])
        o_stack = _lat_attn(p, o_stack, caches, na_dense, dl, dg, sink, cos_t, sin_t, li, n_ctx, nb_lat)

        w1p = jnp.pad(hy_w1[li], ((0, HY_EMB_PAD - HY_EMB), (0, 0)))
        hy_args = (w1p, hy_b1[li].reshape(1, HY_FFN), hy_w2[li], hy_b2[li].reshape(1, HY_FFN), hy_w3[li],
                   hy_freq[li].reshape(1, HY_FFN), hy_decay[li].reshape(1, -1), hy_bias[li])
        o_stack = _hyena(p, o_stack, hy_short[li], _hy_filter_tables(CTX_SEQ, *hy_args), CTX_SEQ, 0, nb_ctx)
        o_stack = _hyena(p, o_stack, hy_short[li], _hy_filter_tables(LAT_SEQ, *hy_args), LAT_SEQ, n_ctx, nb_lat)
        x = _merge(x, mod, g1, o_stack, w5, wb, li, n_ctx)
        x = _mlp(x, mod, g2, wu, wd, g_final.reshape(1, D_MODEL), li, n_ctx, final_norm=(li == DEPTH - 1))

    y_prompt = x[:n_ctx].reshape(nb_ctx, CTX_SEQ, D_MODEL)
    y_sample = x[n_ctx:].reshape(nb_lat, LAT_SEQ, D_MODEL)
    kv_shapes = ((0, 512, (2, A_HEADS, A_QK)), (512, 1024, (A_HEADS, A_V)),
                 (1024, 1536, (B_HEADS, HEAD_DIM)), (1536, 2048, (B_HEADS, HEAD_DIM)),
                 (2048, 2176, (C_KV_HEADS, HEAD_DIM)), (2176, 2304, (C_KV_HEADS, HEAD_DIM)))
    new_kv = tuple(kv[..., a:b].reshape((nb_ctx, DEPTH, CTX_SEQ) + s) for a, b, s in kv_shapes)
    return (y_prompt, y_sample) + new_kv
```

```python
import functools
import math

import numpy as np
import jax
import jax.numpy as jnp
from jax import lax
from jax.experimental import pallas as pl
from jax.experimental.pallas import tpu as pltpu

f32 = jnp.float32
bf16 = jnp.bfloat16

D_MODEL = 2048
DEPTH = 2
CTX_SEQ = 256
LAT_SEQ = 1024
GRID_W = 64
GRID_ROWS = LAT_SEQ // GRID_W
BRANCH_WIDTH = 512
N_BRANCH = 4
HEAD_DIM = 64
A_QK = 64
A_V = 128
A_HEADS = 4
B_HEADS = 8
NA_ROWS = 8
NA_COLS = 16
C_HEADS = 8
C_KV_HEADS = 2
C_GROUP = 4
SWA_WINDOW = 128
HY_WIDTH = 512
HY_ORDER = 2
HY_BANDS = 16
HY_EMB = 1 + 2 * HY_BANDS
HY_EMB_PAD = 128
HY_FFN = 64
D_FF = 4 * D_MODEL
ROPE_BASE = 10000.0
EPS = 1e-6
NEG_INF = -1e30
MIX_COLS = 5376
GATE_COLS = N_BRANCH * D_MODEL
COL_AQ, COL_BQ, COL_CQ, COL_AK, COL_AV, COL_BK, COL_BV = 0, 1, 2, 3, 4, 5, 6
COL_CKV_256 = 14
KV_COL0 = 1536
KV_COLS = 2304
HY_COL0 = 3840
N_MOD_ROWS = 16
CTX_MOD_ROW = 8

VMEM_LIMIT = 56 * 1024 * 1024
TM_INPROJ = 1024
TN_INPROJ = 768
TM_MERGE = 1024
TM_MLP = 1024
TF_MLP = 512
TQ_ATTN = 256
TN_ADA = 1024
HY_CB = 256


def _params(*sem):
    return pltpu.CompilerParams(dimension_semantics=sem, vmem_limit_bytes=VMEM_LIMIT)


def _dot(a, b):
    return jnp.dot(a, b, preferred_element_type=f32)


def _dot_nt(a, b):
    return lax.dot_general(a, b, (((1,), (1,)), ((), ())), preferred_element_type=f32)


def _dot_hi(a, b):
    return jnp.dot(a, b, preferred_element_type=f32, precision=lax.Precision.HIGHEST)


@functools.lru_cache(maxsize=None)
def _rope_tables():
    half = HEAD_DIM // 2
    nf = half // 2
    inv = ROPE_BASE ** (-np.arange(nf, dtype=np.float64) / nf)
    t = np.arange(LAT_SEQ)
    pos = np.stack([t // GRID_W, t % GRID_W], axis=1).astype(np.float64)
    lane = np.arange(HEAD_DIM)
    ang = pos[:, lane // half] * inv[lane % nf][None, :]
    first = (lane % half) < nf
    cos = np.cos(ang)
    sin = np.where(first[None, :], -np.sin(ang), np.sin(ang))
    reps = 512 // HEAD_DIM
    return (np.tile(cos, (1, reps)).astype(np.float32), np.tile(sin, (1, reps)).astype(np.float32))


@functools.lru_cache(maxsize=None)
def _dft_tables(L):
    f = np.arange(L, dtype=np.int64)
    prod = (f[:, None] * f[None, :]) % (2 * L)
    ang = np.pi * prod.astype(np.float64) / L
    fc = np.cos(ang)
    fs = np.sin(ang)
    fs[0, :] = np.where(f % 2 == 0, 1.0, -1.0)
    return fc.astype(np.float32), fs.astype(np.float32)


@functools.lru_cache(maxsize=None)
def _hyena_features(L):
    n = np.arange(L, dtype=np.float64)[:, None]
    t = n / max(L - 1, 1)
    w = 2.0 * math.pi * n / L
    bands = np.linspace(1e-4, HY_BANDS - 1, HY_BANDS, dtype=np.float64)[None, :]
    z = np.concatenate([t, np.cos(bands * w), -np.sin(bands * w)], axis=-1)
    z = np.pad(z, ((0, 0), (0, HY_EMB_PAD - HY_EMB)))
    return z.astype(np.float32)


def _modulated_norm(x, g, scale, shift):
    ms = jnp.mean(x * x, axis=-1, keepdims=True)
    y = x * lax.rsqrt(ms + EPS) * g
    return y * (1.0 + scale) + shift


def _rope(x, cos, sin_signed):
    x = x.astype(f32)
    n = x.shape[-1]
    lane = lax.broadcasted_iota(jnp.int32, x.shape, 1)
    first = (lane & (HEAD_DIM // 2 - 1)) < (HEAD_DIM // 4)
    partner = jnp.where(first, pltpu.roll(x, n - HEAD_DIM // 4, 1), pltpu.roll(x, HEAD_DIM // 4, 1))
    return x * cos + partner * sin_signed


def _softmax_parts(scores, sink=None):
    m = jnp.max(scores[0], axis=-1, keepdims=True)
    for s in scores[1:]:
        m = jnp.maximum(m, jnp.max(s, axis=-1, keepdims=True))
    if sink is not None:
        m = jnp.maximum(m, sink)
    es = [jnp.exp(s - m) for s in scores]
    d = jnp.sum(es[0], axis=-1, keepdims=True)
    for e in es[1:]:
        d = d + jnp.sum(e, axis=-1, keepdims=True)
    if sink is not None:
        d = d + jnp.exp(sink - m)
    return es, 1.0 / d


def _diff_lambda(dl_ref, lam_init):
    dl = dl_ref[...]
    a = jnp.sum(dl[0:1] * dl[1:2], axis=-1, keepdims=True)
    b = jnp.sum(dl[2:3] * dl[3:4], axis=-1, keepdims=True)
    return jnp.exp(a) - jnp.exp(b) + lam_init


def _diff_head_out(w, v, dg, lam_init):
    o = _dot(w.astype(bf16), v)
    o = o * lax.rsqrt(jnp.mean(o * o, axis=-1, keepdims=True) + EPS) * dg
    return o * (1.0 - lam_init)


def _ada_kernel(cv_ref, w_ref, b_ref, o_ref):
    cv = cv_ref[...]
    s = (cv * jax.nn.sigmoid(cv)).astype(bf16)
    o_ref[...] = _dot(s, w_ref[...].astype(bf16)) + b_ref[...]


def _adaln_all(cv, w_ada, b_ada):
    n6 = 6 * D_MODEL
    return pl.pallas_call(
        _ada_kernel,
        grid=(DEPTH, n6 // TN_ADA),
        in_specs=[pl.BlockSpec((N_MOD_ROWS, D_MODEL), lambda l, j: (0, 0)),
                  pl.BlockSpec((None, D_MODEL, TN_ADA), lambda l, j: (l, 0, j)),
                  pl.BlockSpec((None, 1, TN_ADA), lambda l, j: (l, 0, j))],
        out_specs=pl.BlockSpec((None, N_MOD_ROWS, TN_ADA), lambda l, j: (l, 0, j)),
        out_shape=jax.ShapeDtypeStruct((DEPTH, N_MOD_ROWS, n6), f32),
        compiler_params=_params("parallel", "parallel"),
        name="adaln",
    )(cv, w_ada, b_ada.reshape(DEPTH, 1, n6))


def _mod_spec(li, k, tm, n_ctx):
    n_ctx_tiles = n_ctx // tm

    def index(i, j):
        r = jnp.where(i < n_ctx_tiles, CTX_MOD_ROW, ((i - n_ctx_tiles) * tm) // LAT_SEQ)
        return ((li * N_MOD_ROWS + r) * 6 + k, 0, 0)

    return pl.BlockSpec((None, 1, D_MODEL), index)


def _inproj_kernel(x_ref, sh_ref, sc_ref, g_ref, w_ref, *rest, n_ctx_tiles, kv_j0, kv_nj):
    o_ref, kv_ref, h_ref = rest[-3:]
    i, j = pl.program_id(0), pl.program_id(1)

    @pl.when(j == 0)
    def _():
        h_ref[...] = _modulated_norm(x_ref[...], g_ref[...], sc_ref[...], sh_ref[...]).astype(bf16)

    res = _dot(h_ref[...], w_ref[...])
    o_ref[...] = res.astype(bf16)

    @pl.when((i < n_ctx_tiles) & (j >= kv_j0) & (j < kv_j0 + kv_nj))
    def _():
        kv_ref[...] = res.reshape(kv_ref.shape)


def _inproj(x, mod, g, w, kv_prev, li, n_ctx):
    T = x.shape[0]
    tm, tn = TM_INPROJ, TN_INPROJ
    nct = n_ctx // tm
    kv_j0, kv_nj = KV_COL0 // tn, KV_COLS // tn
    nb_ctx = n_ctx // CTX_SEQ

    def kv_index(i, j):
        ctx = i < nct
        return (jnp.where(ctx, i, nct - 1), li, 0, jnp.where(ctx, jnp.clip(j - kv_j0, 0, kv_nj - 1), kv_nj - 1))

    in_specs = [pl.BlockSpec((tm, D_MODEL), lambda i, j: (i, 0)),
                _mod_spec(li, 0, tm, n_ctx), _mod_spec(li, 1, tm, n_ctx),
                pl.BlockSpec((1, D_MODEL), lambda i, j: (0, 0)),
                pl.BlockSpec((None, D_MODEL, tn), lambda i, j: (j, 0, 0))]
    args = [x, mod, mod, g, w]
    aliases = {}
    if kv_prev is not None:
        in_specs.append(pl.BlockSpec(memory_space=pl.ANY))
        args.append(kv_prev)
        aliases = {len(args) - 1: 1}
    return pl.pallas_call(
        functools.partial(_inproj_kernel, n_ctx_tiles=nct, kv_j0=kv_j0, kv_nj=kv_nj),
        grid=(T // tm, MIX_COLS // tn),
        in_specs=in_specs,
        out_specs=[pl.BlockSpec((tm, tn), lambda i, j: (i, j)),
                   pl.BlockSpec((tm // CTX_SEQ, None, CTX_SEQ, tn), kv_index)],
        out_shape=[jax.ShapeDtypeStruct((T, MIX_COLS), bf16),
                   jax.ShapeDtypeStruct((nb_ctx, DEPTH, CTX_SEQ, KV_COLS), f32)],
        scratch_shapes=[pltpu.VMEM((tm, D_MODEL), bf16)],
        input_output_aliases=aliases,
        compiler_params=_params("arbitrary", "arbitrary"),
        name="inproj",
    )(*args)


def _merge_kernel(x_ref, sh_ref, sc_ref, gt_ref, g_ref, o_ref, w_ref, wb_ref, out_ref, h_ref):
    s = pl.program_id(1)
    n_gate = 2 * N_BRANCH
    H = D_MODEL // 2

    @pl.when(s == 0)
    def _():
        h_ref[...] = _modulated_norm(x_ref[...], g_ref[...], sc_ref[...], sh_ref[...]).astype(bf16)
        out_ref[...] = jnp.zeros_like(out_ref)

    def branch_step(half):
        cols = slice(half * H, (half + 1) * H)
        gate = jax.nn.sigmoid(_dot(h_ref[...], w_ref[...]))
        out_ref[:, cols] += gate * _dot(o_ref[...], wb_ref[...])

    def out_step(half):
        cols = slice(half * H, (half + 1) * H)
        out_ref[:, cols] = x_ref[:, cols] + gt_ref[:, cols] * _dot(h_ref[...], w_ref[...])

    for half in range(2):
        pl.when((s < n_gate) & (s % 2 == half))(functools.partial(branch_step, half))

    @pl.when(s == n_gate)
    def _():
        h_ref[...] = out_ref[...].astype(bf16)

    for half in range(2):
        pl.when(s == n_gate + half)(functools.partial(out_step, half))


def _merge(x, mod, g, o_stack, w10, wb8, li, n_ctx):
    T = x.shape[0]
    tm = TM_MERGE
    H = D_MODEL // 2
    n_gate = 2 * N_BRANCH
    return pl.pallas_call(
        _merge_kernel,
        grid=(T // tm, n_gate + 2),
        in_specs=[pl.BlockSpec((tm, D_MODEL), lambda i, s: (i, 0), pipeline_mode=pl.Buffered(1)),
                  _mod_spec(li, 0, tm, n_ctx), _mod_spec(li, 1, tm, n_ctx), _mod_spec(li, 2, tm, n_ctx),
                  pl.BlockSpec((1, D_MODEL), lambda i, s: (0, 0)),
                  pl.BlockSpec((None, tm, BRANCH_WIDTH), lambda i, s: (jnp.minimum(s // 2, N_BRANCH - 1), i, 0)),
                  pl.BlockSpec((None, D_MODEL, H), lambda i, s: (s, 0, 0)),
                  pl.BlockSpec((None, BRANCH_WIDTH, H), lambda i, s: (jnp.minimum(s, n_gate - 1), 0, 0))],
        out_specs=pl.BlockSpec((tm, D_MODEL), lambda i, s: (i, 0)),
        out_shape=jax.ShapeDtypeStruct((T, D_MODEL), f32),
        scratch_shapes=[pltpu.VMEM((tm, D_MODEL), bf16)],
        compiler_params=_params("parallel", "arbitrary"),
        name="merge",
    )(x, mod, mod, mod, g, o_stack, w10, wb8)


def _mlp_kernel(x_ref, sh_ref, sc_ref, gt_ref, g_ref, wu_ref, wd_ref, gf_ref, out_ref, h_ref, *, final_norm):
    j = pl.program_id(1)

    @pl.when(j == 0)
    def _():
        h_ref[...] = _modulated_norm(x_ref[...], g_ref[...], sc_ref[...], sh_ref[...]).astype(bf16)
        out_ref[...] = jnp.zeros_like(out_ref)

    a = jnp.maximum(_dot(h_ref[...], wu_ref[...]), 0.0)
    out_ref[...] += _dot((a * a).astype(bf16), wd_ref[...])

    @pl.when(j == pl.num_programs(1) - 1)
    def _():
        y = x_ref[...] + gt_ref[...] * out_ref[...]
        if final_norm:
            y = y * lax.rsqrt(jnp.mean(y * y, axis=-1, keepdims=True) + EPS) * gf_ref[...]
        out_ref[...] = y


def _mlp(x, mod, g, w_up, w_down, g_final, li, n_ctx, final_norm):
    T = x.shape[0]
    tm, tf = TM_MLP, TF_MLP
    return pl.pallas_call(
        functools.partial(_mlp_kernel, final_norm=final_norm),
        grid=(T // tm, D_FF // tf),
        in_specs=[pl.BlockSpec((tm, D_MODEL), lambda i, j: (i, 0), pipeline_mode=pl.Buffered(1)),
                  _mod_spec(li, 3, tm, n_ctx), _mod_spec(li, 4, tm, n_ctx), _mod_spec(li, 5, tm, n_ctx),
                  pl.BlockSpec((1, D_MODEL), lambda i, j: (0, 0)),
                  pl.BlockSpec((None, D_MODEL, tf), lambda i, j: (j, 0, 0)),
                  pl.BlockSpec((tf, D_MODEL), lambda i, j: (j, 0)),
                  pl.BlockSpec((1, D_MODEL), lambda i, j: (0, 0))],
        out_specs=pl.BlockSpec((tm, D_MODEL), lambda i, j: (i, 0)),
        out_shape=jax.ShapeDtypeStruct((T, D_MODEL), f32),
        scratch_shapes=[pltpu.VMEM((tm, D_MODEL), bf16)],
        compiler_params=_params("parallel", "arbitrary"),
        name="mlp",
    )(x, mod, mod, mod, g, w_up, w_down, g_final)


def _ctx_attn_kernel(aq_ref, ak_ref, av_ref, bq_ref, bk_ref, bv_ref, cq_ref, ckv_ref,
                     dl_ref, dg_ref, sink_ref, o_ref, *, lam_init):
    oa_ref, ob_ref, oc_ref = o_ref.at[0], o_ref.at[1], o_ref.at[2]
    scale = HEAD_DIM ** -0.5
    lam = _diff_lambda(dl_ref, lam_init)
    q = aq_ref[...] * scale
    k = ak_ref[...]
    v = av_ref[...]
    dg = dg_ref[...]
    half = A_HEADS * A_QK
    for h in range(A_HEADS):
        prs = []
        for c in range(2):
            lo = c * half + h * A_QK
            (e,), r = _softmax_parts([_dot_nt(q[:, lo:lo + A_QK], k[:, lo:lo + A_QK])])
            prs.append((e, r))
        w = prs[0][0] * prs[0][1] - prs[1][0] * (lam * prs[1][1])
        oa_ref[:, h * A_V:(h + 1) * A_V] = _diff_head_out(w, v[:, h * A_V:(h + 1) * A_V], dg, lam_init).astype(bf16)
    q = bq_ref[...] * scale
    k = bk_ref[...]
    v = bv_ref[...]
    for h in range(B_HEADS):
        sl = slice(h * HEAD_DIM, (h + 1) * HEAD_DIM)
        (e,), r = _softmax_parts([_dot_nt(q[:, sl], k[:, sl])])
        ob_ref[:, sl] = (_dot(e.astype(bf16), v[:, sl]) * r).astype(bf16)
    q = cq_ref[...] * scale
    kv = ckv_ref[...]
    kw = C_KV_HEADS * HEAD_DIM
    for h in range(C_HEADS):
        g = h // C_GROUP
        sl = slice(h * HEAD_DIM, (h + 1) * HEAD_DIM)
        kg = kv[:, g * HEAD_DIM:(g + 1) * HEAD_DIM]
        vg = kv[:, kw + g * HEAD_DIM:kw + (g + 1) * HEAD_DIM]
        (e,), r = _softmax_parts([_dot_nt(q[:, sl], kg)], sink=sink_ref[h])
        oc_ref[:, sl] = (_dot(e.astype(bf16), vg) * r).astype(bf16)


def _ctx_attn(p, dl, dg, sink, li, n_ctx):
    nb = n_ctx // CTX_SEQ
    S = CTX_SEQ
    lam_init = 0.8 - 0.6 * math.exp(-0.3 * li)

    def col(c):
        return pl.BlockSpec((S, 512), lambda b, c=c: (b, c))

    return pl.pallas_call(
        functools.partial(_ctx_attn_kernel, lam_init=lam_init),
        grid=(nb,),
        in_specs=[col(COL_AQ), col(COL_AK), col(COL_AV), col(COL_BQ), col(COL_BK), col(COL_BV), col(COL_CQ),
                  pl.BlockSpec((S, 256), lambda b: (b, COL_CKV_256)),
                  pl.BlockSpec((4, A_QK), lambda b: (0, 0)),
                  pl.BlockSpec((1, A_V), lambda b: (0, 0)),
                  pl.BlockSpec(memory_space=pltpu.SMEM)],
        out_specs=pl.BlockSpec((N_BRANCH - 1, S, BRANCH_WIDTH), lambda b: (0, b, 0)),
        out_shape=jax.ShapeDtypeStruct((N_BRANCH, p.shape[0], BRANCH_WIDTH), bf16),
        compiler_params=_params("parallel"),
        name="ctx_attn",
    )(p, p, p, p, p, p, p, p, dl, dg, sink)


def _lat_a_kernel(q_ref, k_ref, v_ref, ck_ref, cv_ref, cosq_ref, sinq_ref, cosk_ref, sink_ref,
                  dl_ref, dg_ref, _, o_ref, kk_ref, vv_ref, *, lam_init):
    L = LAT_SEQ

    @pl.when(pl.program_id(1) == 0)
    def _():
        kk_ref[0:L, :] = _rope(k_ref[...], cosk_ref[...], sink_ref[...]).astype(bf16)
        kk_ref[L:, :] = ck_ref[...].astype(bf16)
        vv_ref[0:L, :] = v_ref[...].astype(bf16)
        vv_ref[L:, :] = cv_ref[...].astype(bf16)

    lam = _diff_lambda(dl_ref, lam_init)
    q = (_rope(q_ref[...], cosq_ref[...], sinq_ref[...]) * (A_QK ** -0.5)).astype(bf16)
    dg = dg_ref[...]
    half = A_HEADS * A_QK
    for h in range(A_HEADS):
        prs = []
        for c in range(2):
            lo = c * half + h * A_QK
            (e,), r = _softmax_parts([_dot_nt(q[:, lo:lo + A_QK], kk_ref[:, lo:lo + A_QK])])
            prs.append((e, r))
        w = prs[0][0] * prs[0][1] - prs[1][0] * (lam * prs[1][1])
        o_ref[:, h * A_V:(h + 1) * A_V] = _diff_head_out(w, vv_ref[:, h * A_V:(h + 1) * A_V], dg, lam_init).astype(bf16)


def _lat_b_kernel(q_ref, k_ref, v_ref, ck_ref, cv_ref, bias_ref, _, o_ref, kk_ref, vv_ref):
    L = LAT_SEQ

    @pl.when(pl.program_id(1) == 0)
    def _():
        kk_ref[0:L, :] = k_ref[...].astype(bf16)
        kk_ref[L:, :] = ck_ref[...].astype(bf16)
        vv_ref[0:L, :] = v_ref[...].astype(bf16)
        vv_ref[L:, :] = cv_ref[...].astype(bf16)

    q = (q_ref[...] * (HEAD_DIM ** -0.5)).astype(bf16)
    for h in range(B_HEADS):
        sl = slice(h * HEAD_DIM, (h + 1) * HEAD_DIM)
        s_loc = _dot_nt(q[:, sl], kk_ref[0:L, sl]) + bias_ref[h].astype(f32)
        s_ctx = _dot_nt(q[:, sl], kk_ref[L:, sl])
        (e_loc, e_ctx), r = _softmax_parts([s_loc, s_ctx])
        o = _dot(e_loc.astype(bf16), vv_ref[0:L, sl]) + _dot(e_ctx.astype(bf16), vv_ref[L:, sl])
        o_ref[:, sl] = (o * r).astype(bf16)


def _lat_c_kernel(q_ref, kv_ref, ck_ref, cv_ref, cosq_ref, sinq_ref, cosk_ref, sink_ref, snk_ref,
                  _, o_ref, kk_ref, vv_ref):
    L = LAT_SEQ
    kw = C_KV_HEADS * HEAD_DIM
    tq = q_ref.shape[0]

    @pl.when(pl.program_id(1) == 0)
    def _():
        kk_ref[0:L, :] = _rope(kv_ref[:, 0:kw], cosk_ref[...], sink_ref[...]).astype(bf16)
        kk_ref[L:, :] = ck_ref[...].astype(bf16)
        vv_ref[0:L, :] = kv_ref[:, kw:2 * kw].astype(bf16)
        vv_ref[L:, :] = cv_ref[...].astype(bf16)

    q = (_rope(q_ref[...], cosq_ref[...], sinq_ref[...]) * (HEAD_DIM ** -0.5)).astype(bf16)
    qpos = pl.program_id(1) * tq + lax.broadcasted_iota(jnp.int32, (tq, L), 0)
    kpos = lax.broadcasted_iota(jnp.int32, (tq, L), 1)
    valid = jnp.abs(qpos - kpos) <= SWA_WINDOW
    for h in range(C_HEADS):
        g = h // C_GROUP
        sl = slice(h * HEAD_DIM, (h + 1) * HEAD_DIM)
        gl = slice(g * HEAD_DIM, (g + 1) * HEAD_DIM)
        s_loc = jnp.where(valid, _dot_nt(q[:, sl], kk_ref[0:L, gl]), NEG_INF)
        s_ctx = _dot_nt(q[:, sl], kk_ref[L:, gl])
        (e_loc, e_ctx), r = _softmax_parts([s_loc, s_ctx], sink=snk_ref[h])
        o = _dot(e_loc.astype(bf16), vv_ref[0:L, gl]) + _dot(e_ctx.astype(bf16), vv_ref[L:, gl])
        o_ref[:, sl] = (o * r).astype(bf16)


def _lat_attn(p, o_stack, caches, na_dense, dl, dg, sink, cos_t, sin_t, li, n_ctx, nb_lat):
    cak, cav, cbk, cbv, cck, ccv = caches
    L, tq = LAT_SEQ, TQ_ATTN
    nq = L // tq
    lam_init = 0.8 - 0.6 * math.exp(-0.3 * li)
    row_l = n_ctx // L
    row_q = n_ctx // tq

    def qcol(c):
        return pl.BlockSpec((tq, 512), lambda b, i, c=c: (row_q + b * nq + i, c))

    def kcol(c, width=512):
        return pl.BlockSpec((L, width), lambda b, i, c=c: (row_l + b, c))

    def cache(width):
        return pl.BlockSpec((None, None, CTX_SEQ, width), lambda b, i: (b, li, 0, 0))

    tab_q = pl.BlockSpec((tq, 512), lambda b, i: (i, 0))
    tab_k = pl.BlockSpec((L, 512), lambda b, i: (0, 0))
    tab_k128 = pl.BlockSpec((L, 128), lambda b, i: (0, 0))
    o_sds = jax.ShapeDtypeStruct(o_stack.shape, o_stack.dtype)
    anyspec = pl.BlockSpec(memory_space=pl.ANY)

    def o_spec(branch):
        return pl.BlockSpec((None, tq, BRANCH_WIDTH), lambda b, i: (branch, row_q + b * nq + i, 0))

    kv_scratch = [pltpu.VMEM((L + CTX_SEQ, 512), bf16), pltpu.VMEM((L + CTX_SEQ, 512), bf16)]
    smem = pl.BlockSpec(memory_space=pltpu.SMEM)
    cp = _params("parallel", "arbitrary")

    o_stack = pl.pallas_call(
        functools.partial(_lat_a_kernel, lam_init=lam_init),
        grid=(nb_lat, nq),
        in_specs=[qcol(COL_AQ), kcol(COL_AK), kcol(COL_AV), cache(512), cache(512),
                  tab_q, tab_q, tab_k, tab_k,
                  pl.BlockSpec((4, A_QK), lambda b, i: (0, 0)), pl.BlockSpec((1, A_V), lambda b, i: (0, 0)), anyspec],
        out_specs=o_spec(0), out_shape=o_sds, scratch_shapes=kv_scratch, input_output_aliases={11: 0},
        compiler_params=cp, name="lat_attn_a",
    )(p, p, p, cak, cav, cos_t, sin_t, cos_t, sin_t, dl, dg, o_stack)

    o_stack = pl.pallas_call(
        _lat_b_kernel,
        grid=(nb_lat, nq),
        in_specs=[qcol(COL_BQ), kcol(COL_BK), kcol(COL_BV), cache(512), cache(512),
                  pl.BlockSpec((B_HEADS, tq, L), lambda b, i: (0, i, 0)), anyspec],
        out_specs=o_spec(1), out_shape=o_sds, scratch_shapes=kv_scratch, input_output_aliases={6: 0},
        compiler_params=cp, name="lat_attn_b",
    )(p, p, p, cbk, cbv, na_dense, o_stack)

    kw = C_KV_HEADS * HEAD_DIM
    return pl.pallas_call(
        _lat_c_kernel,
        grid=(nb_lat, nq),
        in_specs=[qcol(COL_CQ), kcol(COL_CKV_256, 256), cache(kw), cache(kw),
                  tab_q, tab_q, tab_k128, tab_k128, smem, anyspec],
        out_specs=o_spec(2), out_shape=o_sds,
        scratch_shapes=[pltpu.VMEM((L + CTX_SEQ, kw), bf16), pltpu.VMEM((L + CTX_SEQ, kw), bf16)],
        input_output_aliases={9: 0}, compiler_params=cp, name="lat_attn_c",
    )(p, p, cck, ccv, cos_t, sin_t, cos_t, sin_t, sink, o_stack)


def _na_bias_kernel(rb_ref, o_ref):
    h = pl.program_id(0)
    W = GRID_W
    n_dc = 2 * NA_COLS - 1
    n_dr = 2 * NA_ROWS - 1
    qc = lax.broadcasted_iota(jnp.int32, (W, 2 * W), 0)
    lane = lax.broadcasted_iota(jnp.int32, (W, 2 * W), 1)
    second = lane >= W
    kc = jnp.where(second, lane - W, lane)
    dc = jnp.clip(kc - qc, -(NA_COLS - 1), NA_COLS - 1) + NA_COLS - 1
    c0 = jnp.clip(qc - NA_COLS // 2, 0, W - NA_COLS)
    col_ok = (kc >= c0) & (kc < c0 + NA_COLS)
    base = h * (n_dr * n_dc)

    def pair_tile(dr0):
        t = jnp.zeros((W, 2 * W), f32)
        for j in range(n_dc):
            lo = rb_ref[base + dr0 * n_dc + j] if 0 <= dr0 < n_dr else 0.0
            hi = rb_ref[base + (dr0 + 1) * n_dc + j] if 0 <= dr0 + 1 < n_dr else 0.0
            t = jnp.where(dc == j, jnp.where(second, hi, lo), t)
        return jnp.where(col_ok, t, NEG_INF)

    tiles = {dr0: pair_tile(dr0) for dr0 in range(-1, n_dr)}
    neg = jnp.full((W, 2 * W), NEG_INF, f32)
    for qr in range(GRID_ROWS):
        r0 = min(max(qr - NA_ROWS // 2, 0), GRID_ROWS - NA_ROWS)
        for pr in range(GRID_ROWS // 2):
            kr = 2 * pr
            ok0 = r0 <= kr < r0 + NA_ROWS
            ok1 = r0 <= kr + 1 < r0 + NA_ROWS
            if not (ok0 or ok1):
                t = neg
            else:
                t = tiles[kr - qr + NA_ROWS - 1]
                if not ok0:
                    t = jnp.where(second, t, NEG_INF)
                if not ok1:
                    t = jnp.where(second, NEG_INF, t)
            o_ref[qr * W:(qr + 1) * W, kr * W:(kr + 2) * W] = t.astype(bf16)


def _na_bias_dense(rel_bias):
    return pl.pallas_call(
        _na_bias_kernel,
        grid=(B_HEADS,),
        in_specs=[pl.BlockSpec(memory_space=pltpu.SMEM)],
        out_specs=pl.BlockSpec((None, LAT_SEQ, LAT_SEQ), lambda h: (h, 0, 0)),
        out_shape=jax.ShapeDtypeStruct((B_HEADS, LAT_SEQ, LAT_SEQ), bf16),
        compiler_params=_params("parallel"),
        name="na_bias",
    )(rel_bias.reshape(-1))


def _hy_filter_kernel(z_ref, w1_ref, b1_ref, w2_ref, b2_ref, fr_ref, w3f_ref, w3b_ref, dcf_ref, dcb_ref,
                      bias_ref, fc_ref, fs_ref, o_ref, *, L):
    z = z_ref[...]
    fr = fr_ref[...]
    h = jnp.sin(fr * (_dot_hi(z, w1_ref[...]) + b1_ref[...]))
    h = jnp.sin(fr * (_dot_hi(h, w2_ref[...]) + b2_ref[...]))
    t = z[:, 0:1]
    hf = _dot_hi(h, w3f_ref[...]) * jnp.exp(-t * jnp.abs(dcf_ref[...]))
    hb = _dot_hi(h, w3b_ref[...]) * jnp.exp(-t * jnp.abs(dcb_ref[...]))
    ssum = hf + hb
    bias = bias_ref[...]
    ga = _dot_hi(fc_ref[...], ssum) + bias
    gb = _dot_hi(fs_ref[...], hb - hf)
    g_nyq = _dot_hi(fs_ref[0:8, :], ssum)[0:1] + bias
    row0 = lax.broadcasted_iota(jnp.int32, ga.shape, 0) == 0
    inv = 1.0 / L
    o_ref[0] = jnp.where(row0, 0.5 * ga, ga) * inv
    o_ref[1] = jnp.where(row0, 0.0, gb) * inv
    o_ref[2] = jnp.where(row0, 0.5 * g_nyq, ga) * inv


def _hy_filter_tables(L, w1p, b1, w2, b2, w3, freq, decay, hy_bias):
    cb = HY_CB
    ncb = HY_WIDTH // cb
    z = jnp.asarray(_hyena_features(L))
    fc, fs = (jnp.asarray(a) for a in _dft_tables(L))
    full = lambda shape: pl.BlockSpec(shape, lambda o, c: (0,) * len(shape))
    fwd = lambda rows: pl.BlockSpec((rows, cb), lambda o, c: (0, o * 2 * ncb + c))
    bwd = lambda rows: pl.BlockSpec((rows, cb), lambda o, c: (0, o * 2 * ncb + ncb + c))
    return pl.pallas_call(
        functools.partial(_hy_filter_kernel, L=L),
        grid=(HY_ORDER, ncb),
        in_specs=[full((L, HY_EMB_PAD)), full((HY_EMB_PAD, HY_FFN)), full((1, HY_FFN)),
                  full((HY_FFN, HY_FFN)), full((1, HY_FFN)), full((1, HY_FFN)),
                  fwd(HY_FFN), bwd(HY_FFN), fwd(1), bwd(1),
                  pl.BlockSpec((None, 1, cb), lambda o, c: (o, 0, c)),
                  full((L, L)), full((L, L))],
        out_specs=pl.BlockSpec((None, 3, L, cb), lambda o, c: (o, 0, 0, c)),
        out_shape=jax.ShapeDtypeStruct((HY_ORDER, 3, L, HY_WIDTH), f32),
        compiler_params=_params("parallel", "parallel"),
        name="hy_filter",
    )(z, w1p, b1, w2, b2, freq, w3, w3, decay, decay, hy_bias.reshape(HY_ORDER, 1, HY_WIDTH), fc, fs)


def _hyena_kernel(v_ref, x1_ref, x2_ref, wv_ref, w1_ref, w2_ref, fc_ref, fs_ref, fst_ref, tab_ref, _, o_ref):
    L = v_ref.shape[0]
    row = lax.broadcasted_iota(jnp.int32, v_ref.shape, 0)

    def short_conv(u_ref, w_ref):
        u = u_ref[...].astype(f32)
        w = w_ref[...]
        prev = jnp.where(row == 0, 0.0, pltpu.roll(u, 1, 0))
        nxt = jnp.where(row == L - 1, 0.0, pltpu.roll(u, L - 1, 0))
        return prev * w[0:1] + u * w[1:2] + nxt * w[2:3]

    def long_conv(u, order):
        ub = u.astype(bf16)
        a = _dot(fc_ref[...], ub)
        b = _dot(fs_ref[...], ub)
        t0, t1, t2 = tab_ref[order, 0], tab_ref[order, 1], tab_ref[order, 2]
        pr = (a * t0 + b * t1).astype(bf16)
        qi = (b * t2 - a * t1).astype(bf16)
        return _dot(fc_ref[...], pr) + _dot(fst_ref[...], qi)

    z = short_conv(x1_ref, w1_ref) * long_conv(short_conv(v_ref, wv_ref), 0)
    o_ref[...] = (short_conv(x2_ref, w2_ref) * long_conv(z, 1)).astype(bf16)


def _hyena(p, o_stack, hy_short, tabs, L, row0, nb):
    cb = HY_CB
    ncb = HY_WIDTH // cb
    c0 = HY_COL0 // cb
    fc, fs = _dft_tables(L)
    fcb, fsb, fstb = jnp.asarray(fc, bf16), jnp.asarray(fs, bf16), jnp.asarray(fs.T.copy(), bf16)
    rb0 = row0 // L

    def part(k):
        return pl.BlockSpec((L, cb), lambda c, b, k=k: (rb0 + b, c0 + k * ncb + c))

    def wpart(k):
        return pl.BlockSpec((3, cb), lambda c, b, k=k: (0, k * ncb + c))

    mat = pl.BlockSpec((L, L), lambda c, b: (0, 0))
    return pl.pallas_call(
        _hyena_kernel,
        grid=(ncb, nb),
        in_specs=[part(0), part(1), part(2), wpart(0), wpart(1), wpart(2), mat, mat, mat,
                  pl.BlockSpec((HY_ORDER, 3, L, cb), lambda c, b: (0, 0, 0, c)),
                  pl.BlockSpec(memory_space=pl.ANY)],
        out_specs=pl.BlockSpec((None, L, cb), lambda c, b: (N_BRANCH - 1, rb0 + b, c)),
        out_shape=jax.ShapeDtypeStruct(o_stack.shape, o_stack.dtype),
        input_output_aliases={10: 0},
        compiler_params=_params("parallel", "arbitrary"),
        name="hyena_%d" % L,
    )(p, p, p, hy_short, hy_short, hy_short, fcb, fsb, fstb, tabs, o_stack)


def kernel(x_prompt, x_sample, cache_a_k, cache_a_v, cache_b_k, cache_b_v, cache_c_k, cache_c_v, c, c_ctx, w_ada, b_ada, g_mix, w_in, diff_lambda, diff_norm_g, na_bias, swa_sink, hy_short, hy_w1, hy_b1, hy_w2, hy_b2, hy_w3, hy_freq, hy_decay, hy_bias, w_branch, w_out, g_mlp, w_up, w_down, g_final):
    nb_ctx, nb_lat = x_prompt.shape[0], x_sample.shape[0]
    n_ctx = nb_ctx * CTX_SEQ
    n_lat = nb_lat * LAT_SEQ
    assert x_prompt.shape[1:] == (CTX_SEQ, D_MODEL) and x_sample.shape[1:] == (LAT_SEQ, D_MODEL)
    assert nb_lat <= CTX_MOD_ROW and n_ctx % TM_INPROJ == 0

    x = jnp.concatenate([x_prompt.reshape(n_ctx, D_MODEL), x_sample.reshape(n_lat, D_MODEL)], axis=0)
    cv = jnp.zeros((N_MOD_ROWS, D_MODEL), f32).at[:nb_lat].set(c).at[CTX_MOD_ROW].set(c_ctx)
    mod = _adaln_all(cv, w_ada, b_ada).reshape(DEPTH * N_MOD_ROWS * 6, 1, D_MODEL)
    cos_t, sin_t = (jnp.asarray(a) for a in _rope_tables())
    caches = (cache_a_k.reshape(nb_lat, DEPTH, CTX_SEQ, 512), cache_a_v.reshape(nb_lat, DEPTH, CTX_SEQ, 512),
              cache_b_k.reshape(nb_lat, DEPTH, CTX_SEQ, 512), cache_b_v.reshape(nb_lat, DEPTH, CTX_SEQ, 512),
              cache_c_k.reshape(nb_lat, DEPTH, CTX_SEQ, 128), cache_c_v.reshape(nb_lat, DEPTH, CTX_SEQ, 128))

    kv = None
    for li in range(DEPTH):
        wl = w_in[li]
        w_mix = jnp.concatenate([wl[:, 0:512], wl[:, 1536:2048], wl[:, 3072:3584], wl[:, 512:1536],
                                 wl[:, 2048:3072], wl[:, 3584:MIX_COLS]], axis=1).astype(bf16)
        H = D_MODEL // 2
        w_mix = w_mix.reshape(D_MODEL, MIX_COLS // TN_INPROJ, TN_INPROJ).transpose(1, 0, 2)
        w10 = jnp.concatenate([wl[:, MIX_COLS:], w_out[li]], axis=1).astype(bf16)
        w10 = w10.reshape(D_MODEL, 2 * N_BRANCH + 2, H).transpose(1, 0, 2)
        wb8 = w_branch[li].astype(bf16).reshape(N_BRANCH, BRANCH_WIDTH, 2, H).transpose(0, 2, 1, 3)
        wb8 = wb8.reshape(2 * N_BRANCH, BRANCH_WIDTH, H)
        wu = w_up[li].astype(bf16).reshape(D_MODEL, D_FF // TF_MLP, TF_MLP).transpose(1, 0, 2)
        wd = w_down[li].astype(bf16)
        g1 = g_mix[li].reshape(1, D_MODEL)
        g2 = g_mlp[li].reshape(1, D_MODEL)
        dl = diff_lambda[li]
        dg = diff_norm_g[li].reshape(1, A_V)
        sink = swa_sink[li]

        p, kv = _inproj(x, mod, g1, w_mix, kv, li, n_ctx)
        o_stack = _ctx_attn(p, dl, dg, sink, li, n_ctx)
        na_dense = _na_bias_dense(na_bias[li])
        o_stack = _lat_attn(p, o_stack, caches, na_dense, dl, dg, sink, cos_t, sin_t, li, n_ctx, nb_lat)

        w1p = jnp.pad(hy_w1[li], ((0, HY_EMB_PAD - HY_EMB), (0, 0)))
        hy_args = (w1p, hy_b1[li].reshape(1, HY_FFN), hy_w2[li], hy_b2[li].reshape(1, HY_FFN), hy_w3[li],
                   hy_freq[li].reshape(1, HY_FFN), hy_decay[li].reshape(1, -1), hy_bias[li])
        o_stack = _hyena(p, o_stack, hy_short[li], _hy_filter_tables(CTX_SEQ, *hy_args), CTX_SEQ, 0, nb_ctx)
        o_stack = _hyena(p, o_stack, hy_short[li], _hy_filter_tables(LAT_SEQ, *hy_args), LAT_SEQ, n_ctx, nb_lat)
        x = _merge(x, mod, g1, o_stack, w10, wb8, li, n_ctx)
        x = _mlp(x, mod, g2, wu, wd, g_final.reshape(1, D_MODEL), li, n_ctx, final_norm=(li == DEPTH - 1))

    y_prompt = x[:n_ctx].reshape(nb_ctx, CTX_SEQ, D_MODEL)
    y_sample = x[n_ctx:].reshape(nb_lat, LAT_SEQ, D_MODEL)
    kv_shapes = ((0, 512, (2, A_HEADS, A_QK)), (512, 1024, (A_HEADS, A_V)),
                 (1024, 1536, (B_HEADS, HEAD_DIM)), (1536, 2048, (B_HEADS, HEAD_DIM)),
                 (2048, 2176, (C_KV_HEADS, HEAD_DIM)), (2176, 2304, (C_KV_HEADS, HEAD_DIM)))
    new_kv = tuple(kv[..., a:b].reshape((nb_ctx, DEPTH, CTX_SEQ) + s) for a, b, s in kv_shapes)
    return (y_prompt, y_sample) + new_kv
```

```python
import functools
import math

import numpy as np
import jax
import jax.numpy as jnp
from jax import lax
from jax.experimental import pallas as pl
from jax.experimental.pallas import tpu as pltpu

f32 = jnp.float32
bf16 = jnp.bfloat16

D_MODEL = 2048
DEPTH = 2
CTX_SEQ = 256
LAT_SEQ = 1024
GRID_W = 64
GRID_ROWS = LAT_SEQ // GRID_W
BRANCH_WIDTH = 512
N_BRANCH = 4
HEAD_DIM = 64
A_QK = 64
A_V = 128
A_HEADS = 4
B_HEADS = 8
NA_ROWS = 8
NA_COLS = 16
C_HEADS = 8
C_KV_HEADS = 2
C_GROUP = 4
SWA_WINDOW = 128
HY_WIDTH = 512
HY_ORDER = 2
HY_BANDS = 16
HY_EMB = 1 + 2 * HY_BANDS
HY_EMB_PAD = 128
HY_FFN = 64
D_FF = 4 * D_MODEL
ROPE_BASE = 10000.0
EPS = 1e-6
NEG_INF = -1e30
MIX_COLS = 5376
GATE_COLS = N_BRANCH * D_MODEL
COL_AQ, COL_BQ, COL_CQ, COL_AK, COL_AV, COL_BK, COL_BV = 0, 1, 2, 3, 4, 5, 6
COL_CKV_256 = 14
KV_COL0 = 1536
KV_COLS = 2304
HY_COL0 = 3840
N_MOD_ROWS = 16
CTX_MOD_ROW = 8

VMEM_LIMIT = 56 * 1024 * 1024
TM_INPROJ = 1024
TN_INPROJ = 768
TM_MERGE = 512
TM_MLP = 512
TF_MLP = 1024
TQ_ATTN = 256
TN_ADA = 1024
HY_CB = 256


def _params(*sem):
    return pltpu.CompilerParams(dimension_semantics=sem, vmem_limit_bytes=VMEM_LIMIT)


def _dot(a, b):
    return jnp.dot(a, b, preferred_element_type=f32)


def _dot_nt(a, b):
    return lax.dot_general(a, b, (((1,), (1,)), ((), ())), preferred_element_type=f32)


def _dot_hi(a, b):
    return jnp.dot(a, b, preferred_element_type=f32, precision=lax.Precision.HIGHEST)


@functools.lru_cache(maxsize=None)
def _rope_tables():
    half = HEAD_DIM // 2
    nf = half // 2
    inv = ROPE_BASE ** (-np.arange(nf, dtype=np.float64) / nf)
    t = np.arange(LAT_SEQ)
    pos = np.stack([t // GRID_W, t % GRID_W], axis=1).astype(np.float64)
    lane = np.arange(HEAD_DIM)
    ang = pos[:, lane // half] * inv[lane % nf][None, :]
    first = (lane % half) < nf
    cos = np.cos(ang)
    sin = np.where(first[None, :], -np.sin(ang), np.sin(ang))
    reps = 512 // HEAD_DIM
    return (np.tile(cos, (1, reps)).astype(np.float32), np.tile(sin, (1, reps)).astype(np.float32))


@functools.lru_cache(maxsize=None)
def _dft_tables(L):
    f = np.arange(L, dtype=np.int64)
    prod = (f[:, None] * f[None, :]) % (2 * L)
    ang = np.pi * prod.astype(np.float64) / L
    fc = np.cos(ang)
    fs = np.sin(ang)
    fs[0, :] = np.where(f % 2 == 0, 1.0, -1.0)
    return fc.astype(np.float32), fs.astype(np.float32)


@functools.lru_cache(maxsize=None)
def _hyena_features(L):
    n = np.arange(L, dtype=np.float64)[:, None]
    t = n / max(L - 1, 1)
    w = 2.0 * math.pi * n / L
    bands = np.linspace(1e-4, HY_BANDS - 1, HY_BANDS, dtype=np.float64)[None, :]
    z = np.concatenate([t, np.cos(bands * w), -np.sin(bands * w)], axis=-1)
    z = np.pad(z, ((0, 0), (0, HY_EMB_PAD - HY_EMB)))
    return z.astype(np.float32)


def _modulated_norm(x, g, scale, shift):
    ms = jnp.mean(x * x, axis=-1, keepdims=True)
    return x * lax.rsqrt(ms + EPS) * (g * (1.0 + scale)) + shift


def _rope(x, cos, sin_signed):
    x = x.astype(f32)
    n = x.shape[-1]
    lane = lax.broadcasted_iota(jnp.int32, x.shape, 1)
    first = (lane & (HEAD_DIM // 2 - 1)) < (HEAD_DIM // 4)
    partner = jnp.where(first, pltpu.roll(x, n - HEAD_DIM // 4, 1), pltpu.roll(x, HEAD_DIM // 4, 1))
    return x * cos + partner * sin_signed


def _softmax_parts(scores, sink=None):
    m = jnp.max(scores[0], axis=-1, keepdims=True)
    for s in scores[1:]:
        m = jnp.maximum(m, jnp.max(s, axis=-1, keepdims=True))
    if sink is not None:
        m = jnp.maximum(m, sink)
    es = [jnp.exp(s - m) for s in scores]
    d = jnp.sum(es[0], axis=-1, keepdims=True)
    for e in es[1:]:
        d = d + jnp.sum(e, axis=-1, keepdims=True)
    if sink is not None:
        d = d + jnp.exp(sink - m)
    return es, 1.0 / d


def _diff_lambda(dl_ref, lam_init):
    dl = dl_ref[...]
    a = jnp.sum(dl[0:1] * dl[1:2], axis=-1, keepdims=True)
    b = jnp.sum(dl[2:3] * dl[3:4], axis=-1, keepdims=True)
    return jnp.exp(a) - jnp.exp(b) + lam_init


def _diff_head_out(w, v, dg, lam_init):
    o = _dot(w.astype(bf16), v)
    o = o * lax.rsqrt(jnp.mean(o * o, axis=-1, keepdims=True) + EPS) * dg
    return o * (1.0 - lam_init)


def _ada_kernel(cv_ref, w_ref, b_ref, o_ref):
    cv = cv_ref[...]
    s = (cv * jax.nn.sigmoid(cv)).astype(bf16)
    o_ref[...] = _dot(s, w_ref[...].astype(bf16)) + b_ref[...]


def _adaln_all(cv, w_ada, b_ada):
    n6 = 6 * D_MODEL
    return pl.pallas_call(
        _ada_kernel,
        grid=(DEPTH, n6 // TN_ADA),
        in_specs=[pl.BlockSpec((N_MOD_ROWS, D_MODEL), lambda l, j: (0, 0)),
                  pl.BlockSpec((None, D_MODEL, TN_ADA), lambda l, j: (l, 0, j)),
                  pl.BlockSpec((None, 1, TN_ADA), lambda l, j: (l, 0, j))],
        out_specs=pl.BlockSpec((None, N_MOD_ROWS, TN_ADA), lambda l, j: (l, 0, j)),
        out_shape=jax.ShapeDtypeStruct((DEPTH, N_MOD_ROWS, n6), f32),
        compiler_params=_params("parallel", "parallel"),
        name="adaln",
    )(cv, w_ada, b_ada.reshape(DEPTH, 1, n6))


def _mod_spec(li, k, tm, n_ctx):
    n_ctx_tiles = n_ctx // tm

    def index(i, j):
        r = jnp.where(i < n_ctx_tiles, CTX_MOD_ROW, ((i - n_ctx_tiles) * tm) // LAT_SEQ)
        return ((li * N_MOD_ROWS + r) * 6 + k, 0, 0)

    return pl.BlockSpec((None, 1, D_MODEL), index)


def _inproj_kernel(x_ref, sh_ref, sc_ref, g_ref, w_ref, *rest, n_ctx_tiles, kv_j0, kv_nj):
    o_ref, kv_ref, h_ref = rest[-3:]
    i, j = pl.program_id(0), pl.program_id(1)

    @pl.when(j == 0)
    def _():
        h_ref[...] = _modulated_norm(x_ref[...], g_ref[...], sc_ref[...], sh_ref[...]).astype(bf16)

    res = _dot(h_ref[...], w_ref[...])
    o_ref[...] = res.astype(bf16)

    @pl.when((i < n_ctx_tiles) & (j >= kv_j0) & (j < kv_j0 + kv_nj))
    def _():
        kv_ref[...] = res.reshape(kv_ref.shape)


def _inproj(x, mod, g, w, kv_prev, li, n_ctx):
    T = x.shape[0]
    tm, tn = TM_INPROJ, TN_INPROJ
    nct = n_ctx // tm
    kv_j0, kv_nj = KV_COL0 // tn, KV_COLS // tn
    nb_ctx = n_ctx // CTX_SEQ

    def kv_index(i, j):
        ctx = i < nct
        return (jnp.where(ctx, i, nct - 1), li, 0, jnp.where(ctx, jnp.clip(j - kv_j0, 0, kv_nj - 1), kv_nj - 1))

    in_specs = [pl.BlockSpec((tm, D_MODEL), lambda i, j: (i, 0)),
                _mod_spec(li, 0, tm, n_ctx), _mod_spec(li, 1, tm, n_ctx),
                pl.BlockSpec((1, D_MODEL), lambda i, j: (0, 0)),
                pl.BlockSpec((D_MODEL, tn), lambda i, j: (0, j))]
    args = [x, mod, mod, g, w]
    aliases = {}
    if kv_prev is not None:
        in_specs.append(pl.BlockSpec(memory_space=pl.ANY))
        args.append(kv_prev)
        aliases = {len(args) - 1: 1}
    return pl.pallas_call(
        functools.partial(_inproj_kernel, n_ctx_tiles=nct, kv_j0=kv_j0, kv_nj=kv_nj),
        grid=(T // tm, MIX_COLS // tn),
        in_specs=in_specs,
        out_specs=[pl.BlockSpec((tm, tn), lambda i, j: (i, j)),
                   pl.BlockSpec((tm // CTX_SEQ, None, CTX_SEQ, tn), kv_index),
                   pl.BlockSpec((tm, D_MODEL), lambda i, j: (i, 0))],
        out_shape=[jax.ShapeDtypeStruct((T, MIX_COLS), bf16),
                   jax.ShapeDtypeStruct((nb_ctx, DEPTH, CTX_SEQ, KV_COLS), f32),
                   jax.ShapeDtypeStruct((T, D_MODEL), bf16)],
        input_output_aliases=aliases,
        compiler_params=_params("arbitrary", "arbitrary"),
        name="inproj",
    )(*args)


def _merge_kernel(x_ref, gt_ref, h_ref, o_ref, w_ref, wb_ref, out_ref, acc_ref):
    n = pl.program_id(1)

    def contribution():
        return jax.nn.sigmoid(_dot(h_ref[...], w_ref[...])) * _dot(o_ref[...], wb_ref[...])

    @pl.when(n == 0)
    def _():
        acc_ref[...] = contribution()

    @pl.when((n > 0) & (n < N_BRANCH))
    def _():
        acc_ref[...] += contribution()

    @pl.when(n == N_BRANCH)
    def _():
        out_ref[...] = x_ref[...] + gt_ref[...] * _dot(acc_ref[...].astype(bf16), w_ref[...])


def _merge(x, mod, h, o_stack, w5, wb, li, n_ctx):
    T = x.shape[0]
    tm = TM_MERGE
    last = N_BRANCH - 1
    return pl.pallas_call(
        _merge_kernel,
        grid=(T // tm, N_BRANCH + 1),
        in_specs=[pl.BlockSpec((tm, D_MODEL), lambda i, n: (i, 0)),
                  _mod_spec(li, 2, tm, n_ctx),
                  pl.BlockSpec((tm, D_MODEL), lambda i, n: (i, 0)),
                  pl.BlockSpec((None, tm, BRANCH_WIDTH), lambda i, n: (jnp.minimum(n, last), i, 0)),
                  pl.BlockSpec((D_MODEL, D_MODEL), lambda i, n: (0, n)),
                  pl.BlockSpec((None, BRANCH_WIDTH, D_MODEL), lambda i, n: (jnp.minimum(n, last), 0, 0))],
        out_specs=pl.BlockSpec((tm, D_MODEL), lambda i, n: (i, 0)),
        out_shape=jax.ShapeDtypeStruct((T, D_MODEL), f32),
        scratch_shapes=[pltpu.VMEM((tm, D_MODEL), f32)],
        compiler_params=_params("parallel", "arbitrary"),
        name="merge",
    )(x, mod, h, o_stack, w5, wb)


def _mlp_kernel(x_ref, sh_ref, sc_ref, gt_ref, g_ref, wu_ref, wd_ref, gf_ref, out_ref, h_ref, acc_ref,
                *, final_norm):
    j = pl.program_id(1)

    def chunk():
        a = jnp.maximum(_dot(h_ref[...], wu_ref[...]), 0.0)
        return _dot((a * a).astype(bf16), wd_ref[...])

    @pl.when(j == 0)
    def _():
        h_ref[...] = _modulated_norm(x_ref[...], g_ref[...], sc_ref[...], sh_ref[...]).astype(bf16)
        acc_ref[...] = chunk()

    @pl.when(j > 0)
    def _():
        acc_ref[...] += chunk()

    @pl.when(j == pl.num_programs(1) - 1)
    def _():
        y = x_ref[...] + gt_ref[...] * acc_ref[...]
        if final_norm:
            y = y * lax.rsqrt(jnp.mean(y * y, axis=-1, keepdims=True) + EPS) * gf_ref[...]
        out_ref[...] = y


def _mlp(x, mod, g, w_up, w_down, g_final, li, n_ctx, final_norm):
    T = x.shape[0]
    tm, tf = TM_MLP, TF_MLP
    return pl.pallas_call(
        functools.partial(_mlp_kernel, final_norm=final_norm),
        grid=(T // tm, D_FF // tf),
        in_specs=[pl.BlockSpec((tm, D_MODEL), lambda i, j: (i, 0)),
                  _mod_spec(li, 3, tm, n_ctx), _mod_spec(li, 4, tm, n_ctx), _mod_spec(li, 5, tm, n_ctx),
                  pl.BlockSpec((1, D_MODEL), lambda i, j: (0, 0)),
                  pl.BlockSpec((D_MODEL, tf), lambda i, j: (0, j)),
                  pl.BlockSpec((tf, D_MODEL), lambda i, j: (j, 0)),
                  pl.BlockSpec((1, D_MODEL), lambda i, j: (0, 0))],
        out_specs=pl.BlockSpec((tm, D_MODEL), lambda i, j: (i, 0)),
        out_shape=jax.ShapeDtypeStruct((T, D_MODEL), f32),
        scratch_shapes=[pltpu.VMEM((tm, D_MODEL), bf16), pltpu.VMEM((tm, D_MODEL), f32)],
        compiler_params=_params("parallel", "arbitrary"),
        name="mlp",
    )(x, mod, mod, mod, g, w_up, w_down, g_final)


def _ctx_attn_kernel(aq_ref, ak_ref, av_ref, bq_ref, bk_ref, bv_ref, cq_ref, ckv_ref,
                     dl_ref, dg_ref, sink_ref, o_ref, *, lam_init):
    oa_ref, ob_ref, oc_ref = o_ref.at[0], o_ref.at[1], o_ref.at[2]
    scale = HEAD_DIM ** -0.5
    lam = _diff_lambda(dl_ref, lam_init)
    q = aq_ref[...] * scale
    k = ak_ref[...]
    v = av_ref[...]
    dg = dg_ref[...]
    half = A_HEADS * A_QK
    for h in range(A_HEADS):
        prs = []
        for c in range(2):
            lo = c * half + h * A_QK
            (e,), r = _softmax_parts([_dot_nt(q[:, lo:lo + A_QK], k[:, lo:lo + A_QK])])
            prs.append((e, r))
        w = prs[0][0] * prs[0][1] - prs[1][0] * (lam * prs[1][1])
        oa_ref[:, h * A_V:(h + 1) * A_V] = _diff_head_out(w, v[:, h * A_V:(h + 1) * A_V], dg, lam_init).astype(bf16)
    q = bq_ref[...] * scale
    k = bk_ref[...]
    v = bv_ref[...]
    for h in range(B_HEADS):
        sl = slice(h * HEAD_DIM, (h + 1) * HEAD_DIM)
        (e,), r = _softmax_parts([_dot_nt(q[:, sl], k[:, sl])])
        ob_ref[:, sl] = (_dot(e.astype(bf16), v[:, sl]) * r).astype(bf16)
    q = cq_ref[...] * scale
    kv = ckv_ref[...]
    kw = C_KV_HEADS * HEAD_DIM
    for h in range(C_HEADS):
        g = h // C_GROUP
        sl = slice(h * HEAD_DIM, (h + 1) * HEAD_DIM)
        kg = kv[:, g * HEAD_DIM:(g + 1) * HEAD_DIM]
        vg = kv[:, kw + g * HEAD_DIM:kw + (g + 1) * HEAD_DIM]
        (e,), r = _softmax_parts([_dot_nt(q[:, sl], kg)], sink=sink_ref[h])
        oc_ref[:, sl] = (_dot(e.astype(bf16), vg) * r).astype(bf16)


def _ctx_attn(p, dl, dg, sink, li, n_ctx):
    nb = n_ctx // CTX_SEQ
    S = CTX_SEQ
    lam_init = 0.8 - 0.6 * math.exp(-0.3 * li)

    def col(c):
        return pl.BlockSpec((S, 512), lambda b, c=c: (b, c))

    return pl.pallas_call(
        functools.partial(_ctx_attn_kernel, lam_init=lam_init),
        grid=(nb,),
        in_specs=[col(COL_AQ), col(COL_AK), col(COL_AV), col(COL_BQ), col(COL_BK), col(COL_BV), col(COL_CQ),
                  pl.BlockSpec((S, 256), lambda b: (b, COL_CKV_256)),
                  pl.BlockSpec((4, A_QK), lambda b: (0, 0)),
                  pl.BlockSpec((1, A_V), lambda b: (0, 0)),
                  pl.BlockSpec(memory_space=pltpu.SMEM)],
        out_specs=pl.BlockSpec((N_BRANCH - 1, S, BRANCH_WIDTH), lambda b: (0, b, 0)),
        out_shape=jax.ShapeDtypeStruct((N_BRANCH, p.shape[0], BRANCH_WIDTH), bf16),
        compiler_params=_params("parallel"),
        name="ctx_attn",
    )(p, p, p, p, p, p, p, p, dl, dg, sink)


def _lat_a_kernel(q_ref, k_ref, v_ref, ck_ref, cv_ref, cosq_ref, sinq_ref, cosk_ref, sink_ref,
                  dl_ref, dg_ref, _, o_ref, kk_ref, vv_ref, *, lam_init):
    L = LAT_SEQ

    @pl.when(pl.program_id(1) == 0)
    def _():
        kk_ref[0:L, :] = _rope(k_ref[...], cosk_ref[...], sink_ref[...]).astype(bf16)
        kk_ref[L:, :] = ck_ref[...].astype(bf16)
        vv_ref[0:L, :] = v_ref[...].astype(bf16)
        vv_ref[L:, :] = cv_ref[...].astype(bf16)

    lam = _diff_lambda(dl_ref, lam_init)
    q = (_rope(q_ref[...], cosq_ref[...], sinq_ref[...]) * (A_QK ** -0.5)).astype(bf16)
    dg = dg_ref[...]
    half = A_HEADS * A_QK
    for h in range(A_HEADS):
        prs = []
        for c in range(2):
            lo = c * half + h * A_QK
            (e,), r = _softmax_parts([_dot_nt(q[:, lo:lo + A_QK], kk_ref[:, lo:lo + A_QK])])
            prs.append((e, r))
        w = prs[0][0] * prs[0][1] - prs[1][0] * (lam * prs[1][1])
        o_ref[:, h * A_V:(h + 1) * A_V] = _diff_head_out(w, vv_ref[:, h * A_V:(h + 1) * A_V], dg, lam_init).astype(bf16)


def _lat_b_kernel(q_ref, k_ref, v_ref, ck_ref, cv_ref, bias_ref, _, o_ref, kk_ref, vv_ref):
    L = LAT_SEQ

    @pl.when(pl.program_id(1) == 0)
    def _():
        kk_ref[0:L, :] = k_ref[...].astype(bf16)
        kk_ref[L:, :] = ck_ref[...].astype(bf16)
        vv_ref[0:L, :] = v_ref[...].astype(bf16)
        vv_ref[L:, :] = cv_ref[...].astype(bf16)

    q = (q_ref[...] * (HEAD_DIM ** -0.5)).astype(bf16)
    for h in range(B_HEADS):
        sl = slice(h * HEAD_DIM, (h + 1) * HEAD_DIM)
        s_loc = _dot_nt(q[:, sl], kk_ref[0:L, sl]) + bias_ref[h].astype(f32)
        s_ctx = _dot_nt(q[:, sl], kk_ref[L:, sl])
        (e_loc, e_ctx), r = _softmax_parts([s_loc, s_ctx])
        o = _dot(e_loc.astype(bf16), vv_ref[0:L, sl]) + _dot(e_ctx.astype(bf16), vv_ref[L:, sl])
        o_ref[:, sl] = (o * r).astype(bf16)


def _lat_c_kernel(q_ref, kv_ref, ck_ref, cv_ref, cosq_ref, sinq_ref, cosk_ref, sink_ref, snk_ref,
                  _, o_ref, kk_ref, vv_ref):
    L = LAT_SEQ
    kw = C_KV_HEADS * HEAD_DIM
    tq = q_ref.shape[0]

    @pl.when(pl.program_id(1) == 0)
    def _():
        kk_ref[0:L, :] = _rope(kv_ref[:, 0:kw], cosk_ref[...], sink_ref[...]).astype(bf16)
        kk_ref[L:, :] = ck_ref[...].astype(bf16)
        vv_ref[0:L, :] = kv_ref[:, kw:2 * kw].astype(bf16)
        vv_ref[L:, :] = cv_ref[...].astype(bf16)

    q = (_rope(q_ref[...], cosq_ref[...], sinq_ref[...]) * (HEAD_DIM ** -0.5)).astype(bf16)
    qpos = pl.program_id(1) * tq + lax.broadcasted_iota(jnp.int32, (tq, L), 0)
    kpos = lax.broadcasted_iota(jnp.int32, (tq, L), 1)
    valid = jnp.abs(qpos - kpos) <= SWA_WINDOW
    for h in range(C_HEADS):
        g = h // C_GROUP
        sl = slice(h * HEAD_DIM, (h + 1) * HEAD_DIM)
        gl = slice(g * HEAD_DIM, (g + 1) * HEAD_DIM)
        s_loc = jnp.where(valid, _dot_nt(q[:, sl], kk_ref[0:L, gl]), NEG_INF)
        s_ctx = _dot_nt(q[:, sl], kk_ref[L:, gl])
        (e_loc, e_ctx), r = _softmax_parts([s_loc, s_ctx], sink=snk_ref[h])
        o = _dot(e_loc.astype(bf16), vv_ref[0:L, gl]) + _dot(e_ctx.astype(bf16), vv_ref[L:, gl])
        o_ref[:, sl] = (o * r).astype(bf16)


def _lat_attn(p, o_stack, caches, na_dense, dl, dg, sink, cos_t, sin_t, li, n_ctx, nb_lat):
    cak, cav, cbk, cbv, cck, ccv = caches
    L, tq = LAT_SEQ, TQ_ATTN
    nq = L // tq
    lam_init = 0.8 - 0.6 * math.exp(-0.3 * li)
    row_l = n_ctx // L
    row_q = n_ctx // tq

    def qcol(c):
        return pl.BlockSpec((tq, 512), lambda b, i, c=c: (row_q + b * nq + i, c))

    def kcol(c, width=512):
        return pl.BlockSpec((L, width), lambda b, i, c=c: (row_l + b, c))

    def cache(width):
        return pl.BlockSpec((None, None, CTX_SEQ, width), lambda b, i: (b, li, 0, 0))

    tab_q = pl.BlockSpec((tq, 512), lambda b, i: (i, 0))
    tab_k = pl.BlockSpec((L, 512), lambda b, i: (0, 0))
    tab_k128 = pl.BlockSpec((L, 128), lambda b, i: (0, 0))
    o_sds = jax.ShapeDtypeStruct(o_stack.shape, o_stack.dtype)
    anyspec = pl.BlockSpec(memory_space=pl.ANY)

    def o_spec(branch):
        return pl.BlockSpec((None, tq, BRANCH_WIDTH), lambda b, i: (branch, row_q + b * nq + i, 0))

    kv_scratch = [pltpu.VMEM((L + CTX_SEQ, 512), bf16), pltpu.VMEM((L + CTX_SEQ, 512), bf16)]
    smem = pl.BlockSpec(memory_space=pltpu.SMEM)
    cp = _params("parallel", "arbitrary")

    o_stack = pl.pallas_call(
        functools.partial(_lat_a_kernel, lam_init=lam_init),
        grid=(nb_lat, nq),
        in_specs=[qcol(COL_AQ), kcol(COL_AK), kcol(COL_AV), cache(512), cache(512),
                  tab_q, tab_q, tab_k, tab_k,
                  pl.BlockSpec((4, A_QK), lambda b, i: (0, 0)), pl.BlockSpec((1, A_V), lambda b, i: (0, 0)), anyspec],
        out_specs=o_spec(0), out_shape=o_sds, scratch_shapes=kv_scratch, input_output_aliases={11: 0},
        compiler_params=cp, name="lat_attn_a",
    )(p, p, p, cak, cav, cos_t, sin_t, cos_t, sin_t, dl, dg, o_stack)

    o_stack = pl.pallas_call(
        _lat_b_kernel,
        grid=(nb_lat, nq),
        in_specs=[qcol(COL_BQ), kcol(COL_BK), kcol(COL_BV), cache(512), cache(512),
                  pl.BlockSpec((B_HEADS, tq, L), lambda b, i: (0, i, 0)), anyspec],
        out_specs=o_spec(1), out_shape=o_sds, scratch_shapes=kv_scratch, input_output_aliases={6: 0},
        compiler_params=cp, name="lat_attn_b",
    )(p, p, p, cbk, cbv, na_dense, o_stack)

    kw = C_KV_HEADS * HEAD_DIM
    return pl.pallas_call(
        _lat_c_kernel,
        grid=(nb_lat, nq),
        in_specs=[qcol(COL_CQ), kcol(COL_CKV_256, 256), cache(kw), cache(kw),
                  tab_q, tab_q, tab_k128, tab_k128, smem, anyspec],
        out_specs=o_spec(2), out_shape=o_sds,
        scratch_shapes=[pltpu.VMEM((L + CTX_SEQ, kw), bf16), pltpu.VMEM((L + CTX_SEQ, kw), bf16)],
        input_output_aliases={9: 0}, compiler_params=cp, name="lat_attn_c",
    )(p, p, cck, ccv, cos_t, sin_t, cos_t, sin_t, sink, o_stack)


def _na_bias_kernel(rb_ref, o_ref):
    h = pl.program_id(0)
    W = GRID_W
    n_dc = 2 * NA_COLS - 1
    n_dr = 2 * NA_ROWS - 1
    qc = lax.broadcasted_iota(jnp.int32, (W, 2 * W), 0)
    lane = lax.broadcasted_iota(jnp.int32, (W, 2 * W), 1)
    second = lane >= W
    kc = jnp.where(second, lane - W, lane)
    dc = jnp.clip(kc - qc, -(NA_COLS - 1), NA_COLS - 1) + NA_COLS - 1
    c0 = jnp.clip(qc - NA_COLS // 2, 0, W - NA_COLS)
    col_ok = (kc >= c0) & (kc < c0 + NA_COLS)
    base = h * (n_dr * n_dc)

    def pair_tile(dr0):
        t = jnp.zeros((W, 2 * W), f32)
        for j in range(n_dc):
            lo = rb_ref[base + dr0 * n_dc + j] if 0 <= dr0 < n_dr else 0.0
            hi = rb_ref[base + (dr0 + 1) * n_dc + j] if 0 <= dr0 + 1 < n_dr else 0.0
            t = jnp.where(dc == j, jnp.where(second, hi, lo), t)
        return jnp.where(col_ok, t, NEG_INF)

    tiles = {dr0: pair_tile(dr0) for dr0 in range(-1, n_dr)}
    neg = jnp.full((W, 2 * W), NEG_INF, f32)
    for qr in range(GRID_ROWS):
        r0 = min(max(qr - NA_ROWS // 2, 0), GRID_ROWS - NA_ROWS)
        for pr in range(GRID_ROWS // 2):
            kr = 2 * pr
            ok0 = r0 <= kr < r0 + NA_ROWS
            ok1 = r0 <= kr + 1 < r0 + NA_ROWS
            if not (ok0 or ok1):
                t = neg
            else:
                t = tiles[kr - qr + NA_ROWS - 1]
                if not ok0:
                    t = jnp.where(second, t, NEG_INF)
                if not ok1:
                    t = jnp.where(second, NEG_INF, t)
            o_ref[qr * W:(qr + 1) * W, kr * W:(kr + 2) * W] = t.astype(bf16)


def _na_bias_dense(rel_bias):
    return pl.pallas_call(
        _na_bias_kernel,
        grid=(B_HEADS,),
        in_specs=[pl.BlockSpec(memory_space=pltpu.SMEM)],
        out_specs=pl.BlockSpec((None, LAT_SEQ, LAT_SEQ), lambda h: (h, 0, 0)),
        out_shape=jax.ShapeDtypeStruct((B_HEADS, LAT_SEQ, LAT_SEQ), bf16),
        compiler_params=_params("parallel"),
        name="na_bias",
    )(rel_bias.reshape(-1))


def _hy_filter_kernel(z_ref, w1_ref, b1_ref, w2_ref, b2_ref, fr_ref, w3f_ref, w3b_ref, dcf_ref, dcb_ref,
                      bias_ref, fc_ref, fs_ref, o_ref, *, L):
    z = z_ref[...]
    fr = fr_ref[...]
    h = jnp.sin(fr * (_dot_hi(z, w1_ref[...]) + b1_ref[...]))
    h = jnp.sin(fr * (_dot_hi(h, w2_ref[...]) + b2_ref[...]))
    t = z[:, 0:1]
    hf = _dot_hi(h, w3f_ref[...]) * jnp.exp(-t * jnp.abs(dcf_ref[...]))
    hb = _dot_hi(h, w3b_ref[...]) * jnp.exp(-t * jnp.abs(dcb_ref[...]))
    ssum = hf + hb
    bias = bias_ref[...]
    ga = _dot_hi(fc_ref[...], ssum) + bias
    gb = _dot_hi(fs_ref[...], hb - hf)
    g_nyq = _dot_hi(fs_ref[0:8, :], ssum)[0:1] + bias
    row0 = lax.broadcasted_iota(jnp.int32, ga.shape, 0) == 0
    inv = 1.0 / L
    o_ref[0] = jnp.where(row0, 0.5 * ga, ga) * inv
    o_ref[1] = jnp.where(row0, 0.0, gb) * inv
    o_ref[2] = jnp.where(row0, 0.5 * g_nyq, ga) * inv


def _hy_filter_tables(L, w1p, b1, w2, b2, w3, freq, decay, hy_bias):
    cb = HY_CB
    ncb = HY_WIDTH // cb
    z = jnp.asarray(_hyena_features(L))
    fc, fs = (jnp.asarray(a) for a in _dft_tables(L))
    full = lambda shape: pl.BlockSpec(shape, lambda o, c: (0,) * len(shape))
    fwd = lambda rows: pl.BlockSpec((rows, cb), lambda o, c: (0, o * 2 * ncb + c))
    bwd = lambda rows: pl.BlockSpec((rows, cb), lambda o, c: (0, o * 2 * ncb + ncb + c))
    return pl.pallas_call(
        functools.partial(_hy_filter_kernel, L=L),
        grid=(HY_ORDER, ncb),
        in_specs=[full((L, HY_EMB_PAD)), full((HY_EMB_PAD, HY_FFN)), full((1, HY_FFN)),
                  full((HY_FFN, HY_FFN)), full((1, HY_FFN)), full((1, HY_FFN)),
                  fwd(HY_FFN), bwd(HY_FFN), fwd(1), bwd(1),
                  pl.BlockSpec((None, 1, cb), lambda o, c: (o, 0, c)),
                  full((L, L)), full((L, L))],
        out_specs=pl.BlockSpec((None, 3, L, cb), lambda o, c: (o, 0, 0, c)),
        out_shape=jax.ShapeDtypeStruct((HY_ORDER, 3, L, HY_WIDTH), f32),
        compiler_params=_params("parallel", "parallel"),
        name="hy_filter",
    )(z, w1p, b1, w2, b2, freq, w3, w3, decay, decay, hy_bias.reshape(HY_ORDER, 1, HY_WIDTH), fc, fs)


def _hyena_kernel(v_ref, x1_ref, x2_ref, wv_ref, w1_ref, w2_ref, fc_ref, fs_ref, fst_ref, tab_ref, _, o_ref):
    L = v_ref.shape[0]
    row = lax.broadcasted_iota(jnp.int32, v_ref.shape, 0)

    def short_conv(u_ref, w_ref):
        u = u_ref[...].astype(f32)
        w = w_ref[...]
        prev = jnp.where(row == 0, 0.0, pltpu.roll(u, 1, 0))
        nxt = jnp.where(row == L - 1, 0.0, pltpu.roll(u, L - 1, 0))
        return prev * w[0:1] + u * w[1:2] + nxt * w[2:3]

    def long_conv(u, order):
        ub = u.astype(bf16)
        a = _dot(fc_ref[...], ub)
        b = _dot(fs_ref[...], ub)
        t0, t1, t2 = tab_ref[order, 0], tab_ref[order, 1], tab_ref[order, 2]
        pr = (a * t0 + b * t1).astype(bf16)
        qi = (b * t2 - a * t1).astype(bf16)
        return _dot(fc_ref[...], pr) + _dot(fst_ref[...], qi)

    z = short_conv(x1_ref, w1_ref) * long_conv(short_conv(v_ref, wv_ref), 0)
    o_ref[...] = (short_conv(x2_ref, w2_ref) * long_conv(z, 1)).astype(bf16)


def _hyena(p, o_stack, hy_short, tabs, L, row0, nb):
    cb = HY_CB
    ncb = HY_WIDTH // cb
    c0 = HY_COL0 // cb
    fc, fs = _dft_tables(L)
    fcb, fsb, fstb = jnp.asarray(fc, bf16), jnp.asarray(fs, bf16), jnp.asarray(fs.T.copy(), bf16)
    rb0 = row0 // L

    def part(k):
        return pl.BlockSpec((L, cb), lambda c, b, k=k: (rb0 + b, c0 + k * ncb + c))

    def wpart(k):
        return pl.BlockSpec((3, cb), lambda c, b, k=k: (0, k * ncb + c))

    mat = pl.BlockSpec((L, L), lambda c, b: (0, 0))
    return pl.pallas_call(
        _hyena_kernel,
        grid=(ncb, nb),
        in_specs=[part(0), part(1), part(2), wpart(0), wpart(1), wpart(2), mat, mat, mat,
                  pl.BlockSpec((HY_ORDER, 3, L, cb), lambda c, b: (0, 0, 0, c)),
                  pl.BlockSpec(memory_space=pl.ANY)],
        out_specs=pl.BlockSpec((None, L, cb), lambda c, b: (N_BRANCH - 1, rb0 + b, c)),
        out_shape=jax.ShapeDtypeStruct(o_stack.shape, o_stack.dtype),
        input_output_aliases={10: 0},
        compiler_params=_params("parallel", "arbitrary"),
        name="hyena_%d" % L,
    )(p, p, p, hy_short, hy_short, hy_short, fcb, fsb, fstb, tabs, o_stack)


def kernel(x_prompt, x_sample, cache_a_k, cache_a_v, cache_b_k, cache_b_v, cache_c_k, cache_c_v, c, c_ctx, w_ada, b_ada, g_mix, w_in, diff_lambda, diff_norm_g, na_bias, swa_sink, hy_short, hy_w1, hy_b1, hy_w2, hy_b2, hy_w3, hy_freq, hy_decay, hy_bias, w_branch, w_out, g_mlp, w_up, w_down, g_final):
    nb_ctx, nb_lat = x_prompt.shape[0], x_sample.shape[0]
    n_ctx = nb_ctx * CTX_SEQ
    n_lat = nb_lat * LAT_SEQ
    assert x_prompt.shape[1:] == (CTX_SEQ, D_MODEL) and x_sample.shape[1:] == (LAT_SEQ, D_MODEL)
    assert nb_lat <= CTX_MOD_ROW and n_ctx % TM_INPROJ == 0

    x = jnp.concatenate([x_prompt.reshape(n_ctx, D_MODEL), x_sample.reshape(n_lat, D_MODEL)], axis=0)
    cv = jnp.zeros((N_MOD_ROWS, D_MODEL), f32).at[:nb_lat].set(c).at[CTX_MOD_ROW].set(c_ctx)
    mod = _adaln_all(cv, w_ada, b_ada).reshape(DEPTH * N_MOD_ROWS * 6, 1, D_MODEL)
    cos_t, sin_t = (jnp.asarray(a) for a in _rope_tables())
    caches = (cache_a_k.reshape(nb_lat, DEPTH, CTX_SEQ, 512), cache_a_v.reshape(nb_lat, DEPTH, CTX_SEQ, 512),
              cache_b_k.reshape(nb_lat, DEPTH, CTX_SEQ, 512), cache_b_v.reshape(nb_lat, DEPTH, CTX_SEQ, 512),
              cache_c_k.reshape(nb_lat, DEPTH, CTX_SEQ, 128), cache_c_v.reshape(nb_lat, DEPTH, CTX_SEQ, 128))

    kv = None
    for li in range(DEPTH):
        wl = w_in[li]
        w_mix = jnp.concatenate([wl[:, 0:512], wl[:, 1536:2048], wl[:, 3072:3584], wl[:, 512:1536],
                                 wl[:, 2048:3072], wl[:, 3584:MIX_COLS]], axis=1).astype(bf16)
        w5 = jnp.concatenate([wl[:, MIX_COLS:], w_out[li]], axis=1).astype(bf16)
        wb = w_branch[li].astype(bf16)
        wu = w_up[li].astype(bf16)
        wd = w_down[li].astype(bf16)
        g1 = g_mix[li].reshape(1, D_MODEL)
        g2 = g_mlp[li].reshape(1, D_MODEL)
        dl = diff_lambda[li]
        dg = diff_norm_g[li].reshape(1, A_V)
        sink = swa_sink[li]

        p, kv, h = _inproj(x, mod, g1, w_mix, kv, li, n_ctx)
        o_stack = _ctx_attn(p, dl, dg, sink, li, n_ctx)
        na_dense = _na_bias_dense(na_bias[li])
        o_stack = _lat_attn(p, o_stack, caches, na_dense, dl, dg, sink, cos_t, sin_t, li, n_ctx, nb_lat)

        w1p = jnp.pad(hy_w1[li], ((0, HY_EMB_PAD - HY_EMB), (0, 0)))
        hy_args = (w1p, hy_b1[li].reshape(1, HY_FFN), hy_w2[li], hy_b2[li].reshape(1, HY_FFN), hy_w3[li],
                   hy_freq[li].reshape(1, HY_FFN), hy_decay[li].reshape(1, -1), hy_bias[li])
        o_stack = _hyena(p, o_stack, hy_short[li], _hy_filter_tables(CTX_SEQ, *hy_args), CTX_SEQ, 0, nb_ctx)
        o_stack = _hyena(p, o_stack, hy_short[li], _hy_filter_tables(LAT_SEQ, *hy_args), LAT_SEQ, n_ctx, nb_lat)
        x = _merge(x, mod, h, o_stack, w5, wb, li, n_ctx)
        x = _mlp(x, mod, g2, wu, wd, g_final.reshape(1, D_MODEL), li, n_ctx, final_norm=(li == DEPTH - 1))

    y_prompt = x[:n_ctx].reshape(nb_ctx, CTX_SEQ, D_MODEL)
    y_sample = x[n_ctx:].reshape(nb_lat, LAT_SEQ, D_MODEL)
    kv_shapes = ((0, 512, (2, A_HEADS, A_QK)), (512, 1024, (A_HEADS, A_V)),
                 (1024, 1536, (B_HEADS, HEAD_DIM)), (1536, 2048, (B_HEADS, HEAD_DIM)),
                 (2048, 2176, (C_KV_HEADS, HEAD_DIM)), (2176, 2304, (C_KV_HEADS, HEAD_DIM)))
    new_kv = tuple(kv[..., a:b].reshape((nb_ctx, DEPTH, CTX_SEQ) + s) for a, b, s in kv_shapes)
    return (y_prompt, y_sample) + new_kv
```

```python
import functools
import math

import numpy as np
import jax
import jax.numpy as jnp
from jax import lax
from jax.experimental import pallas as pl
from jax.experimental.pallas import tpu as pltpu

f32 = jnp.float32
bf16 = jnp.bfloat16

D_MODEL = 2048
DEPTH = 2
CTX_SEQ = 256
LAT_SEQ = 1024
GRID_W = 64
GRID_ROWS = LAT_SEQ // GRID_W
BRANCH_WIDTH = 512
N_BRANCH = 4
HEAD_DIM = 64
A_QK = 64
A_V = 128
A_HEADS = 4
B_HEADS = 8
NA_ROWS = 8
NA_COLS = 16
C_HEADS = 8
C_KV_HEADS = 2
C_GROUP = 4
SWA_WINDOW = 128
SWA_KEYS = 512
NA_Q_ROWS = 4
NA_KEY_ROWS = 12
HY_WIDTH = 512
HY_ORDER = 2
HY_BANDS = 16
HY_EMB = 1 + 2 * HY_BANDS
HY_EMB_PAD = 128
HY_FFN = 64
D_FF = 4 * D_MODEL
ROPE_BASE = 10000.0
EPS = 1e-6
NEG_INF = -1e30
MIX_COLS = 5376
GATE_COLS = N_BRANCH * D_MODEL
COL_AQ, COL_BQ, COL_CQ, COL_AK, COL_AV, COL_BK, COL_BV = 0, 1, 2, 3, 4, 5, 6
COL_CKV_256 = 14
KV_COL0 = 1536
KV_COLS = 2304
HY_COL0 = 3840
N_MOD_ROWS = 16
CTX_MOD_ROW = 8

VMEM_LIMIT = 56 * 1024 * 1024
TM_INPROJ = 1024
TN_INPROJ = 768
TM_MERGE = 512
TM_MLP = 512
TF_MLP = 1024
TQ_ATTN = 256
TN_ADA = 1024
HY_CB = 256


def _params(*sem):
    return pltpu.CompilerParams(dimension_semantics=sem, vmem_limit_bytes=VMEM_LIMIT)


def _dot(a, b):
    return jnp.dot(a, b, preferred_element_type=f32)


def _dot_nt(a, b):
    return lax.dot_general(a, b, (((1,), (1,)), ((), ())), preferred_element_type=f32)


def _dot_hi(a, b):
    return jnp.dot(a, b, preferred_element_type=f32, precision=lax.Precision.HIGHEST)


@functools.lru_cache(maxsize=None)
def _rope_tables():
    half = HEAD_DIM // 2
    nf = half // 2
    inv = ROPE_BASE ** (-np.arange(nf, dtype=np.float64) / nf)
    t = np.arange(LAT_SEQ)
    pos = np.stack([t // GRID_W, t % GRID_W], axis=1).astype(np.float64)
    lane = np.arange(HEAD_DIM)
    ang = pos[:, lane // half] * inv[lane % nf][None, :]
    first = (lane % half) < nf
    cos = np.cos(ang)
    sin = np.where(first[None, :], -np.sin(ang), np.sin(ang))
    reps = 512 // HEAD_DIM
    return (np.tile(cos, (1, reps)).astype(np.float32), np.tile(sin, (1, reps)).astype(np.float32))


@functools.lru_cache(maxsize=None)
def _dft_tables(L):
    f = np.arange(L, dtype=np.int64)
    prod = (f[:, None] * f[None, :]) % (2 * L)
    ang = np.pi * prod.astype(np.float64) / L
    fc = np.cos(ang)
    fs = np.sin(ang)
    fs[0, :] = np.where(f % 2 == 0, 1.0, -1.0)
    return fc.astype(np.float32), fs.astype(np.float32)


@functools.lru_cache(maxsize=None)
def _hyena_features(L):
    n = np.arange(L, dtype=np.float64)[:, None]
    t = n / max(L - 1, 1)
    w = 2.0 * math.pi * n / L
    bands = np.linspace(1e-4, HY_BANDS - 1, HY_BANDS, dtype=np.float64)[None, :]
    z = np.concatenate([t, np.cos(bands * w), -np.sin(bands * w)], axis=-1)
    z = np.pad(z, ((0, 0), (0, HY_EMB_PAD - HY_EMB)))
    return z.astype(np.float32)


def _modulated_norm(x, g, scale, shift):
    ms = jnp.mean(x * x, axis=-1, keepdims=True)
    return x * lax.rsqrt(ms + EPS) * (g * (1.0 + scale)) + shift


def _rope(x, cos, sin_signed):
    x = x.astype(f32)
    n = x.shape[-1]
    lane = lax.broadcasted_iota(jnp.int32, x.shape, 1)
    first = (lane & (HEAD_DIM // 2 - 1)) < (HEAD_DIM // 4)
    partner = jnp.where(first, pltpu.roll(x, n - HEAD_DIM // 4, 1), pltpu.roll(x, HEAD_DIM // 4, 1))
    return x * cos + partner * sin_signed


LOG2E = 1.4426950408889634
QK_SCALE2 = HEAD_DIM ** -0.5 * LOG2E
PAIR = 2 * HEAD_DIM


def _lo_half(shape):
    return lax.broadcasted_iota(jnp.int32, shape, 1) < HEAD_DIM


def _keep_half(x, half, fill):
    lo = _lo_half(x.shape)
    return jnp.where(lo if half == 0 else jnp.logical_not(lo), x, fill)


def _head_q(q_tile, half):
    return _keep_half(q_tile, half, 0.0).astype(bf16)


def _values_with_ones(v_tile, half):
    return _keep_half(v_tile.astype(f32), half, 1.0).astype(bf16)


def _exp2_parts(scores, sink2=None):
    m = jnp.max(scores[0], axis=-1, keepdims=True)
    for s in scores[1:]:
        m = jnp.maximum(m, jnp.max(s, axis=-1, keepdims=True))
    if sink2 is not None:
        m = jnp.maximum(m, sink2)
    return [jnp.exp2(s - m) for s in scores], m


def _pair_out(acc_even, acc_odd, extra_even=None, extra_odd=None):
    lo = _lo_half(acc_even.shape)
    num = jnp.where(lo, acc_even, acc_odd)
    den = pltpu.roll(jnp.where(lo, acc_odd, acc_even), HEAD_DIM, 1)
    if extra_even is not None:
        den = den + jnp.where(lo, extra_even, extra_odd)
    return num / den


def _swap_halves_variants(v_tile):
    sw = pltpu.roll(v_tile, HEAD_DIM, 1)
    lo = _lo_half(v_tile.shape)
    tiles = (jnp.where(lo, v_tile, 1.0), jnp.where(lo, 1.0, sw), jnp.where(lo, sw, 1.0), jnp.where(lo, 1.0, v_tile))
    return [t.astype(bf16) for t in tiles]


def _gqa_head_q(q_tile, q_swapped, half, g):
    return _keep_half(q_tile if half == g else q_swapped, g, 0.0).astype(bf16)


def _diff_lambda(dl_ref, lam_init):
    dl = dl_ref[...]
    a = jnp.sum(dl[0:1] * dl[1:2], axis=-1, keepdims=True)
    b = jnp.sum(dl[2:3] * dl[3:4], axis=-1, keepdims=True)
    return jnp.exp(a) - jnp.exp(b) + lam_init


def _diff_head(q, k_of, v_of, h, lam, dg, lam_init):
    res = []
    for c in range(2):
        t = c * (A_HEADS // 2) + h // 2
        s = _dot_nt(_head_q(q[:, t * PAIR:(t + 1) * PAIR], h % 2), k_of(t))
        (e,), _ = _exp2_parts([s])
        res.append((_dot(e.astype(bf16), v_of(h)), jnp.sum(e, axis=-1, keepdims=True)))
    o = res[0][0] * (1.0 / res[0][1]) - res[1][0] * (lam / res[1][1])
    o = o * lax.rsqrt(jnp.mean(o * o, axis=-1, keepdims=True) + EPS) * dg
    return o * (1.0 - lam_init)


def _ada_kernel(cv_ref, w_ref, b_ref, o_ref):
    cv = cv_ref[...]
    s = (cv * jax.nn.sigmoid(cv)).astype(bf16)
    o_ref[...] = _dot(s, w_ref[...].astype(bf16)) + b_ref[...]


def _adaln_all(cv, w_ada, b_ada):
    n6 = 6 * D_MODEL
    return pl.pallas_call(
        _ada_kernel,
        grid=(DEPTH, n6 // TN_ADA),
        in_specs=[pl.BlockSpec((N_MOD_ROWS, D_MODEL), lambda l, j: (0, 0)),
                  pl.BlockSpec((None, D_MODEL, TN_ADA), lambda l, j: (l, 0, j)),
                  pl.BlockSpec((None, 1, TN_ADA), lambda l, j: (l, 0, j))],
        out_specs=pl.BlockSpec((None, N_MOD_ROWS, TN_ADA), lambda l, j: (l, 0, j)),
        out_shape=jax.ShapeDtypeStruct((DEPTH, N_MOD_ROWS, n6), f32),
        compiler_params=_params("parallel", "parallel"),
        name="adaln",
    )(cv, w_ada, b_ada.reshape(DEPTH, 1, n6))


def _mod_spec(li, k, tm, n_ctx):
    n_ctx_tiles = n_ctx // tm

    def index(i, j):
        r = jnp.where(i < n_ctx_tiles, CTX_MOD_ROW, ((i - n_ctx_tiles) * tm) // LAT_SEQ)
        return ((li * N_MOD_ROWS + r) * 6 + k, 0, 0)

    return pl.BlockSpec((None, 1, D_MODEL), index)


def _inproj_kernel(x_ref, sh_ref, sc_ref, g_ref, w_ref, *rest, n_ctx_tiles, kv_j0, kv_nj):
    o_ref, kv_ref, h_ref = rest[-3:]
    i, j = pl.program_id(0), pl.program_id(1)

    @pl.when(j == 0)
    def _():
        h_ref[...] = _modulated_norm(x_ref[...], g_ref[...], sc_ref[...], sh_ref[...]).astype(bf16)

    res = _dot(h_ref[...], w_ref[...])
    o_ref[...] = res.astype(bf16)

    @pl.when((i < n_ctx_tiles) & (j >= kv_j0) & (j < kv_j0 + kv_nj))
    def _():
        kv_ref[...] = res.reshape(kv_ref.shape)


def _inproj(x, mod, g, w, kv_prev, li, n_ctx):
    T = x.shape[0]
    tm, tn = TM_INPROJ, TN_INPROJ
    nct = n_ctx // tm
    kv_j0, kv_nj = KV_COL0 // tn, KV_COLS // tn
    nb_ctx = n_ctx // CTX_SEQ

    def kv_index(i, j):
        ctx = i < nct
        return (jnp.where(ctx, i, nct - 1), li, 0, jnp.where(ctx, jnp.clip(j - kv_j0, 0, kv_nj - 1), kv_nj - 1))

    in_specs = [pl.BlockSpec((tm, D_MODEL), lambda i, j: (i, 0)),
                _mod_spec(li, 0, tm, n_ctx), _mod_spec(li, 1, tm, n_ctx),
                pl.BlockSpec((1, D_MODEL), lambda i, j: (0, 0)),
                pl.BlockSpec((D_MODEL, tn), lambda i, j: (0, j))]
    args = [x, mod, mod, g, w]
    aliases = {}
    if kv_prev is not None:
        in_specs.append(pl.BlockSpec(memory_space=pl.ANY))
        args.append(kv_prev)
        aliases = {len(args) - 1: 1}
    return pl.pallas_call(
        functools.partial(_inproj_kernel, n_ctx_tiles=nct, kv_j0=kv_j0, kv_nj=kv_nj),
        grid=(T // tm, MIX_COLS // tn),
        in_specs=in_specs,
        out_specs=[pl.BlockSpec((tm, tn), lambda i, j: (i, j)),
                   pl.BlockSpec((tm // CTX_SEQ, None, CTX_SEQ, tn), kv_index),
                   pl.BlockSpec((tm, D_MODEL), lambda i, j: (i, 0))],
        out_shape=[jax.ShapeDtypeStruct((T, MIX_COLS), bf16),
                   jax.ShapeDtypeStruct((nb_ctx, DEPTH, CTX_SEQ, KV_COLS), f32),
                   jax.ShapeDtypeStruct((T, D_MODEL), bf16)],
        input_output_aliases=aliases,
        compiler_params=_params("arbitrary", "arbitrary"),
        name="inproj",
    )(*args)


def _merge_kernel(x_ref, gt_ref, h_ref, o_ref, w_ref, wb_ref, out_ref, acc_ref):
    n = pl.program_id(1)

    def contribution():
        return jax.nn.sigmoid(_dot(h_ref[...], w_ref[...])) * _dot(o_ref[...], wb_ref[...])

    @pl.when(n == 0)
    def _():
        acc_ref[...] = contribution()

    @pl.when((n > 0) & (n < N_BRANCH))
    def _():
        acc_ref[...] += contribution()

    @pl.when(n == N_BRANCH)
    def _():
        out_ref[...] = x_ref[...] + gt_ref[...] * _dot(acc_ref[...].astype(bf16), w_ref[...])


def _merge(x, mod, h, o_stack, w5, wb, li, n_ctx):
    T = x.shape[0]
    tm = TM_MERGE
    last = N_BRANCH - 1
    return pl.pallas_call(
        _merge_kernel,
        grid=(T // tm, N_BRANCH + 1),
        in_specs=[pl.BlockSpec((tm, D_MODEL), lambda i, n: (i, 0)),
                  _mod_spec(li, 2, tm, n_ctx),
                  pl.BlockSpec((tm, D_MODEL), lambda i, n: (i, 0)),
                  pl.BlockSpec((None, tm, BRANCH_WIDTH), lambda i, n: (jnp.minimum(n, last), i, 0)),
                  pl.BlockSpec((D_MODEL, D_MODEL), lambda i, n: (0, n)),
                  pl.BlockSpec((None, BRANCH_WIDTH, D_MODEL), lambda i, n: (jnp.minimum(n, last), 0, 0))],
        out_specs=pl.BlockSpec((tm, D_MODEL), lambda i, n: (i, 0)),
        out_shape=jax.ShapeDtypeStruct((T, D_MODEL), f32),
        scratch_shapes=[pltpu.VMEM((tm, D_MODEL), f32)],
        compiler_params=_params("parallel", "arbitrary"),
        name="merge",
    )(x, mod, h, o_stack, w5, wb)


def _mlp_kernel(x_ref, sh_ref, sc_ref, gt_ref, g_ref, wu_ref, wd_ref, gf_ref, out_ref, h_ref, acc_ref,
                *, final_norm):
    j = pl.program_id(1)

    def chunk():
        a = jnp.maximum(_dot(h_ref[...], wu_ref[...]), 0.0)
        return _dot((a * a).astype(bf16), wd_ref[...])

    @pl.when(j == 0)
    def _():
        h_ref[...] = _modulated_norm(x_ref[...], g_ref[...], sc_ref[...], sh_ref[...]).astype(bf16)
        acc_ref[...] = chunk()

    @pl.when(j > 0)
    def _():
        acc_ref[...] += chunk()

    @pl.when(j == pl.num_programs(1) - 1)
    def _():
        y = x_ref[...] + gt_ref[...] * acc_ref[...]
        if final_norm:
            y = y * lax.rsqrt(jnp.mean(y * y, axis=-1, keepdims=True) + EPS) * gf_ref[...]
        out_ref[...] = y


def _mlp(x, mod, g, w_up, w_down, g_final, li, n_ctx, final_norm):
    T = x.shape[0]
    tm, tf = TM_MLP, TF_MLP
    return pl.pallas_call(
        functools.partial(_mlp_kernel, final_norm=final_norm),
        grid=(T // tm, D_FF // tf),
        in_specs=[pl.BlockSpec((tm, D_MODEL), lambda i, j: (i, 0)),
                  _mod_spec(li, 3, tm, n_ctx), _mod_spec(li, 4, tm, n_ctx), _mod_spec(li, 5, tm, n_ctx),
                  pl.BlockSpec((1, D_MODEL), lambda i, j: (0, 0)),
                  pl.BlockSpec((D_MODEL, tf), lambda i, j: (0, j)),
                  pl.BlockSpec((tf, D_MODEL), lambda i, j: (j, 0)),
                  pl.BlockSpec((1, D_MODEL), lambda i, j: (0, 0))],
        out_specs=pl.BlockSpec((tm, D_MODEL), lambda i, j: (i, 0)),
        out_shape=jax.ShapeDtypeStruct((T, D_MODEL), f32),
        scratch_shapes=[pltpu.VMEM((tm, D_MODEL), bf16), pltpu.VMEM((tm, D_MODEL), f32)],
        compiler_params=_params("parallel", "arbitrary"),
        name="mlp",
    )(x, mod, mod, mod, g, w_up, w_down, g_final)


def _ctx_attn_kernel(aq_ref, ak_ref, av_ref, bq_ref, bk_ref, bv_ref, cq_ref, ckv_ref,
                     dl_ref, dg_ref, sink_ref, o_ref, *, lam_init):
    lam = _diff_lambda(dl_ref, lam_init)
    q = aq_ref[...].astype(f32) * QK_SCALE2
    outs = [_diff_head(q, lambda t: ak_ref[:, t * PAIR:(t + 1) * PAIR], lambda h: av_ref[:, h * A_V:(h + 1) * A_V],
                       h, lam, dg_ref[...], lam_init) for h in range(A_HEADS)]
    o_ref[0] = jnp.concatenate(outs, axis=1).astype(bf16)
    q = bq_ref[...].astype(f32) * QK_SCALE2
    tiles = []
    for t in range(B_HEADS // 2):
        cols = slice(t * PAIR, (t + 1) * PAIR)
        accs = []
        for half in range(2):
            (e,), _ = _exp2_parts([_dot_nt(_head_q(q[:, cols], half), bk_ref[:, cols])])
            accs.append(_dot(e.astype(bf16), _values_with_ones(bv_ref[:, cols], half)))
        tiles.append(_pair_out(*accs))
    o_ref[1] = jnp.concatenate(tiles, axis=1).astype(bf16)
    q = cq_ref[...].astype(f32) * QK_SCALE2
    k_tile = ckv_ref[:, 0:PAIR]
    v_variants = _swap_halves_variants(ckv_ref[:, PAIR:2 * PAIR].astype(f32))
    tiles = []
    for t in range(C_HEADS // 2):
        q_tile = q[:, t * PAIR:(t + 1) * PAIR]
        q_swapped = pltpu.roll(q_tile, HEAD_DIM, 1)
        accs, extras = [], []
        for half in range(2):
            h = 2 * t + half
            g = h // C_GROUP
            sink2 = sink_ref[h] * LOG2E
            (e,), m = _exp2_parts([_dot_nt(_gqa_head_q(q_tile, q_swapped, half, g), k_tile)], sink2)
            accs.append(_dot(e.astype(bf16), v_variants[g * 2 + half]))
            extras.append(jnp.exp2(sink2 - m))
        tiles.append(_pair_out(accs[0], accs[1], extras[0], extras[1]))
    o_ref[2] = jnp.concatenate(tiles, axis=1).astype(bf16)


def _ctx_attn(p, dl, dg, sink, li, n_ctx):
    nb = n_ctx // CTX_SEQ
    S = CTX_SEQ
    lam_init = 0.8 - 0.6 * math.exp(-0.3 * li)

    def col(c):
        return pl.BlockSpec((S, 512), lambda b, c=c: (b, c))

    return pl.pallas_call(
        functools.partial(_ctx_attn_kernel, lam_init=lam_init),
        grid=(nb,),
        in_specs=[col(COL_AQ), col(COL_AK), col(COL_AV), col(COL_BQ), col(COL_BK), col(COL_BV), col(COL_CQ),
                  pl.BlockSpec((S, 256), lambda b: (b, COL_CKV_256)),
                  pl.BlockSpec((4, A_QK), lambda b: (0, 0)),
                  pl.BlockSpec((1, A_V), lambda b: (0, 0)),
                  pl.BlockSpec(memory_space=pltpu.SMEM)],
        out_specs=pl.BlockSpec((N_BRANCH - 1, S, BRANCH_WIDTH), lambda b: (0, b, 0)),
        out_shape=jax.ShapeDtypeStruct((N_BRANCH, p.shape[0], BRANCH_WIDTH), bf16),
        compiler_params=_params("parallel"),
        name="ctx_attn",
    )(p, p, p, p, p, p, p, p, dl, dg, sink)


def _lat_a_kernel(q_ref, k_ref, v_ref, ck_ref, cv_ref, cosq_ref, sinq_ref, cosk_ref, sink_ref,
                  dl_ref, dg_ref, _, o_ref, kk_ref, vv_ref, *, lam_init):
    L = LAT_SEQ

    @pl.when(pl.program_id(1) == 0)
    def _():
        kk_ref[0:L, :] = _rope(k_ref[...], cosk_ref[...], sink_ref[...]).astype(bf16)
        kk_ref[L:, :] = ck_ref[...].astype(bf16)
        vv_ref[0:L, :] = v_ref[...].astype(bf16)
        vv_ref[L:, :] = cv_ref[...].astype(bf16)

    lam = _diff_lambda(dl_ref, lam_init)
    q = _rope(q_ref[...], cosq_ref[...], sinq_ref[...]) * QK_SCALE2
    outs = [_diff_head(q, lambda t: kk_ref[:, t * PAIR:(t + 1) * PAIR], lambda h: vv_ref[:, h * A_V:(h + 1) * A_V],
                       h, lam, dg_ref[...], lam_init) for h in range(A_HEADS)]
    o_ref[...] = jnp.concatenate(outs, axis=1).astype(bf16)


def _lat_b_kernel(q_ref, k_ref, v_ref, ck_ref, cv_ref, bias_ref, _, o_ref, kk_ref, va_ref):
    L, W = LAT_SEQ, NA_KEY_ROWS * GRID_W
    qb = pl.program_id(1)

    @pl.when(qb == 0)
    def _():
        kk_ref[0:L, :] = k_ref[...]
        kk_ref[L:, :] = ck_ref[...].astype(bf16)
        for t in range(B_HEADS // 2):
            cols = slice(t * PAIR, (t + 1) * PAIR)
            for half in range(2):
                va_ref[2 * t + half, 0:L, :] = _values_with_ones(v_ref[:, cols], half)
                va_ref[2 * t + half, L:, :] = _values_with_ones(cv_ref[:, cols], half)

    start = pl.multiple_of(_na_key_row0(qb) * GRID_W, 256)
    q = q_ref[...].astype(f32) * QK_SCALE2
    tiles = []
    for t in range(B_HEADS // 2):
        cols = slice(t * PAIR, (t + 1) * PAIR)
        accs = []
        for half in range(2):
            h = 2 * t + half
            qh = _head_q(q[:, cols], half)
            s_loc = _dot_nt(qh, kk_ref[pl.ds(start, W), cols]) + bias_ref[h].astype(f32)
            s_ctx = _dot_nt(qh, kk_ref[L:, cols])
            (e_loc, e_ctx), _m = _exp2_parts([s_loc, s_ctx])
            accs.append(_dot(e_loc.astype(bf16), va_ref[h, pl.ds(start, W), :])
                        + _dot(e_ctx.astype(bf16), va_ref[h, L:, :]))
        tiles.append(_pair_out(*accs))
    o_ref[...] = jnp.concatenate(tiles, axis=1).astype(bf16)


def _lat_c_kernel(q_ref, kv_ref, ck_ref, cv_ref, cosq_ref, sinq_ref, cosk_ref, sink_ref, snk_ref,
                  _, o_ref, kk_ref, va_ref):
    L, W = LAT_SEQ, SWA_KEYS
    tq = q_ref.shape[0]
    qb = pl.program_id(1)

    @pl.when(qb == 0)
    def _():
        kk_ref[0:L, :] = _rope(kv_ref[:, 0:PAIR], cosk_ref[...], sink_ref[...]).astype(bf16)
        kk_ref[L:, :] = ck_ref[...].astype(bf16)
        lat = _swap_halves_variants(kv_ref[:, PAIR:2 * PAIR].astype(f32))
        ctx = _swap_halves_variants(cv_ref[...])
        for idx in range(4):
            va_ref[idx, 0:L, :] = lat[idx]
            va_ref[idx, L:, :] = ctx[idx]

    start = pl.multiple_of(jnp.clip(qb * tq - SWA_WINDOW, 0, L - W), 128)
    q = _rope(q_ref[...], cosq_ref[...], sinq_ref[...]) * QK_SCALE2
    qpos = qb * tq + lax.broadcasted_iota(jnp.int32, (tq, W), 0)
    kpos = start + lax.broadcasted_iota(jnp.int32, (tq, W), 1)
    valid = jnp.abs(qpos - kpos) <= SWA_WINDOW
    tiles = []
    for t in range(C_HEADS // 2):
        q_tile = q[:, t * PAIR:(t + 1) * PAIR]
        q_swapped = pltpu.roll(q_tile, HEAD_DIM, 1)
        accs, extras = [], []
        for half in range(2):
            h = 2 * t + half
            g = h // C_GROUP
            qh = _gqa_head_q(q_tile, q_swapped, half, g)
            s_loc = jnp.where(valid, _dot_nt(qh, kk_ref[pl.ds(start, W), :]), NEG_INF)
            s_ctx = _dot_nt(qh, kk_ref[L:, :])
            sink2 = snk_ref[h] * LOG2E
            (e_loc, e_ctx), m = _exp2_parts([s_loc, s_ctx], sink2)
            idx = g * 2 + half
            accs.append(_dot(e_loc.astype(bf16), va_ref[idx, pl.ds(start, W), :])
                        + _dot(e_ctx.astype(bf16), va_ref[idx, L:, :]))
            extras.append(jnp.exp2(sink2 - m))
        tiles.append(_pair_out(accs[0], accs[1], extras[0], extras[1]))
    o_ref[...] = jnp.concatenate(tiles, axis=1).astype(bf16)


def _lat_attn(p, o_stack, caches, na_dense, dl, dg, sink, cos_t, sin_t, li, n_ctx, nb_lat):
    cak, cav, cbk, cbv, cck, ccv = caches
    L, tq = LAT_SEQ, TQ_ATTN
    assert tq == NA_Q_ROWS * GRID_W and SWA_KEYS >= tq + 2 * SWA_WINDOW
    nq = L // tq
    lam_init = 0.8 - 0.6 * math.exp(-0.3 * li)
    row_l = n_ctx // L
    row_q = n_ctx // tq

    def qcol(c):
        return pl.BlockSpec((tq, 512), lambda b, i, c=c: (row_q + b * nq + i, c))

    def kcol(c, width=512):
        return pl.BlockSpec((L, width), lambda b, i, c=c: (row_l + b, c))

    def cache(width):
        return pl.BlockSpec((None, None, CTX_SEQ, width), lambda b, i: (b, li, 0, 0))

    tab_q = pl.BlockSpec((tq, 512), lambda b, i: (i, 0))
    tab_k = pl.BlockSpec((L, 512), lambda b, i: (0, 0))
    tab_k128 = pl.BlockSpec((L, 128), lambda b, i: (0, 0))
    o_sds = jax.ShapeDtypeStruct(o_stack.shape, o_stack.dtype)
    anyspec = pl.BlockSpec(memory_space=pl.ANY)

    def o_spec(branch):
        return pl.BlockSpec((None, tq, BRANCH_WIDTH), lambda b, i: (branch, row_q + b * nq + i, 0))

    kv_scratch = [pltpu.VMEM((L + CTX_SEQ, 512), bf16), pltpu.VMEM((L + CTX_SEQ, 512), bf16)]
    smem = pl.BlockSpec(memory_space=pltpu.SMEM)
    cp = _params("parallel", "arbitrary")

    o_stack = pl.pallas_call(
        functools.partial(_lat_a_kernel, lam_init=lam_init),
        grid=(nb_lat, nq),
        in_specs=[qcol(COL_AQ), kcol(COL_AK), kcol(COL_AV), cache(512), cache(512),
                  tab_q, tab_q, tab_k, tab_k,
                  pl.BlockSpec((4, A_QK), lambda b, i: (0, 0)), pl.BlockSpec((1, A_V), lambda b, i: (0, 0)), anyspec],
        out_specs=o_spec(0), out_shape=o_sds, scratch_shapes=kv_scratch, input_output_aliases={11: 0},
        compiler_params=cp, name="lat_attn_a",
    )(p, p, p, cak, cav, cos_t, sin_t, cos_t, sin_t, dl, dg, o_stack)

    o_stack = pl.pallas_call(
        _lat_b_kernel,
        grid=(nb_lat, nq),
        in_specs=[qcol(COL_BQ), kcol(COL_BK), kcol(COL_BV), cache(512), cache(512),
                  pl.BlockSpec((B_HEADS, tq, NA_KEY_ROWS * GRID_W), lambda b, i: (0, i, 0)), anyspec],
        out_specs=o_spec(1), out_shape=o_sds,
        scratch_shapes=[pltpu.VMEM((L + CTX_SEQ, 512), bf16), pltpu.VMEM((B_HEADS, L + CTX_SEQ, PAIR), bf16)],
        input_output_aliases={6: 0}, compiler_params=cp, name="lat_attn_b",
    )(p, p, p, cbk, cbv, na_dense, o_stack)

    return pl.pallas_call(
        _lat_c_kernel,
        grid=(nb_lat, nq),
        in_specs=[qcol(COL_CQ), kcol(COL_CKV_256, 256), cache(PAIR), cache(PAIR),
                  tab_q, tab_q, tab_k128, tab_k128, smem, anyspec],
        out_specs=o_spec(2), out_shape=o_sds,
        scratch_shapes=[pltpu.VMEM((L + CTX_SEQ, PAIR), bf16), pltpu.VMEM((4, L + CTX_SEQ, PAIR), bf16)],
        input_output_aliases={9: 0}, compiler_params=cp, name="lat_attn_c",
    )(p, p, cck, ccv, cos_t, sin_t, cos_t, sin_t, sink, o_stack)


def _na_key_row0(qb):
    lo, hi = NA_Q_ROWS * qb - NA_ROWS // 2, GRID_ROWS - NA_KEY_ROWS
    return min(max(lo, 0), hi) if isinstance(qb, int) else jnp.clip(lo, 0, hi)


def _na_bias_kernel(rb_ref, o_ref):
    h = pl.program_id(0)
    W = GRID_W
    n_dc = 2 * NA_COLS - 1
    n_dr = 2 * NA_ROWS - 1
    qc = lax.broadcasted_iota(jnp.int32, (W, 2 * W), 0)
    lane = lax.broadcasted_iota(jnp.int32, (W, 2 * W), 1)
    second = lane >= W
    kc = jnp.where(second, lane - W, lane)
    dc = jnp.clip(kc - qc, -(NA_COLS - 1), NA_COLS - 1) + NA_COLS - 1
    c0 = jnp.clip(qc - NA_COLS // 2, 0, W - NA_COLS)
    col_ok = (kc >= c0) & (kc < c0 + NA_COLS)
    base = h * (n_dr * n_dc)

    def pair_tile(dr0):
        t = jnp.zeros((W, 2 * W), f32)
        for j in range(n_dc):
            lo = rb_ref[base + dr0 * n_dc + j] if 0 <= dr0 < n_dr else 0.0
            hi = rb_ref[base + (dr0 + 1) * n_dc + j] if 0 <= dr0 + 1 < n_dr else 0.0
            t = jnp.where(dc == j, jnp.where(second, hi, lo), t)
        return jnp.where(col_ok, t * LOG2E, NEG_INF)

    tiles = {dr0: pair_tile(dr0) for dr0 in range(-1, n_dr)}
    neg = jnp.full((W, 2 * W), NEG_INF, f32)
    for qr in range(GRID_ROWS):
        r0 = min(max(qr - NA_ROWS // 2, 0), GRID_ROWS - NA_ROWS)
        k0 = _na_key_row0(qr // NA_Q_ROWS)
        assert k0 % 2 == 0 and k0 <= r0 and r0 + NA_ROWS <= k0 + NA_KEY_ROWS
        for pr in range(NA_KEY_ROWS // 2):
            kr = k0 + 2 * pr
            ok0 = r0 <= kr < r0 + NA_ROWS
            ok1 = r0 <= kr + 1 < r0 + NA_ROWS
            if not (ok0 or ok1):
                t = neg
            else:
                t = tiles[kr - qr + NA_ROWS - 1]
                if not ok0:
                    t = jnp.where(second, t, NEG_INF)
                if not ok1:
                    t = jnp.where(second, NEG_INF, t)
            o_ref[qr * W:(qr + 1) * W, 2 * pr * W:(2 * pr + 2) * W] = t.astype(bf16)


def _na_bias_dense(rel_bias):
    kw = NA_KEY_ROWS * GRID_W
    return pl.pallas_call(
        _na_bias_kernel,
        grid=(B_HEADS,),
        in_specs=[pl.BlockSpec(memory_space=pltpu.SMEM)],
        out_specs=pl.BlockSpec((None, LAT_SEQ, kw), lambda h: (h, 0, 0)),
        out_shape=jax.ShapeDtypeStruct((B_HEADS, LAT_SEQ, kw), bf16),
        compiler_params=_params("parallel"),
        name="na_bias",
    )(rel_bias.reshape(-1))


def _hy_filter_kernel(z_ref, w1_ref, b1_ref, w2_ref, b2_ref, fr_ref, w3f_ref, w3b_ref, dcf_ref, dcb_ref,
                      bias_ref, fc_ref, fs_ref, o_ref, *, L):
    z = z_ref[...]
    fr = fr_ref[...]
    h = jnp.sin(fr * (_dot_hi(z, w1_ref[...]) + b1_ref[...]))
    h = jnp.sin(fr * (_dot_hi(h, w2_ref[...]) + b2_ref[...]))
    t = z[:, 0:1]
    hf = _dot_hi(h, w3f_ref[...]) * jnp.exp(-t * jnp.abs(dcf_ref[...]))
    hb = _dot_hi(h, w3b_ref[...]) * jnp.exp(-t * jnp.abs(dcb_ref[...]))
    ssum = hf + hb
    bias = bias_ref[...]
    ga = _dot_hi(fc_ref[...], ssum) + bias
    gb = _dot_hi(fs_ref[...], hb - hf)
    g_nyq = _dot_hi(fs_ref[0:8, :], ssum)[0:1] + bias
    row0 = lax.broadcasted_iota(jnp.int32, ga.shape, 0) == 0
    inv = 1.0 / L
    o_ref[0] = jnp.where(row0, 0.5 * ga, ga) * inv
    o_ref[1] = jnp.where(row0, 0.0, gb) * inv
    o_ref[2] = jnp.where(row0, 0.5 * g_nyq, ga) * inv


def _hy_filter_tables(L, w1p, b1, w2, b2, w3, freq, decay, hy_bias):
    cb = HY_CB
    ncb = HY_WIDTH // cb
    z = jnp.asarray(_hyena_features(L))
    fc, fs = (jnp.asarray(a) for a in _dft_tables(L))
    full = lambda shape: pl.BlockSpec(shape, lambda o, c: (0,) * len(shape))
    fwd = lambda rows: pl.BlockSpec((rows, cb), lambda o, c: (0, o * 2 * ncb + c))
    bwd = lambda rows: pl.BlockSpec((rows, cb), lambda o, c: (0, o * 2 * ncb + ncb + c))
    return pl.pallas_call(
        functools.partial(_hy_filter_kernel, L=L),
        grid=(HY_ORDER, ncb),
        in_specs=[full((L, HY_EMB_PAD)), full((HY_EMB_PAD, HY_FFN)), full((1, HY_FFN)),
                  full((HY_FFN, HY_FFN)), full((1, HY_FFN)), full((1, HY_FFN)),
                  fwd(HY_FFN), bwd(HY_FFN), fwd(1), bwd(1),
                  pl.BlockSpec((None, 1, cb), lambda o, c: (o, 0, c)),
                  full((L, L)), full((L, L))],
        out_specs=pl.BlockSpec((None, 3, L, cb), lambda o, c: (o, 0, 0, c)),
        out_shape=jax.ShapeDtypeStruct((HY_ORDER, 3, L, HY_WIDTH), f32),
        compiler_params=_params("parallel", "parallel"),
        name="hy_filter",
    )(z, w1p, b1, w2, b2, freq, w3, w3, decay, decay, hy_bias.reshape(HY_ORDER, 1, HY_WIDTH), fc, fs)


def _hyena_kernel(v_ref, x1_ref, x2_ref, wv_ref, w1_ref, w2_ref, fc_ref, fs_ref, fst_ref, tab_ref, _, o_ref):
    L = v_ref.shape[0]
    row = lax.broadcasted_iota(jnp.int32, v_ref.shape, 0)

    def short_conv(u_ref, w_ref):
        u = u_ref[...].astype(f32)
        w = w_ref[...]
        prev = jnp.where(row == 0, 0.0, pltpu.roll(u, 1, 0))
        nxt = jnp.where(row == L - 1, 0.0, pltpu.roll(u, L - 1, 0))
        return prev * w[0:1] + u * w[1:2] + nxt * w[2:3]

    def long_conv(u, order):
        ub = u.astype(bf16)
        a = _dot(fc_ref[...], ub)
        b = _dot(fs_ref[...], ub)
        t0, t1, t2 = tab_ref[order, 0], tab_ref[order, 1], tab_ref[order, 2]
        pr = (a * t0 + b * t1).astype(bf16)
        qi = (b * t2 - a * t1).astype(bf16)
        return _dot(fc_ref[...], pr) + _dot(fst_ref[...], qi)

    z = short_conv(x1_ref, w1_ref) * long_conv(short_conv(v_ref, wv_ref), 0)
    o_ref[...] = (short_conv(x2_ref, w2_ref) * long_conv(z, 1)).astype(bf16)


def _hyena(p, o_stack, hy_short, tabs, L, row0, nb):
    cb = HY_CB
    ncb = HY_WIDTH // cb
    c0 = HY_COL0 // cb
    fc, fs = _dft_tables(L)
    fcb, fsb, fstb = jnp.asarray(fc, bf16), jnp.asarray(fs, bf16), jnp.asarray(fs.T.copy(), bf16)
    rb0 = row0 // L

    def part(k):
        return pl.BlockSpec((L, cb), lambda c, b, k=k: (rb0 + b, c0 + k * ncb + c))

    def wpart(k):
        return pl.BlockSpec((3, cb), lambda c, b, k=k: (0, k * ncb + c))

    mat = pl.BlockSpec((L, L), lambda c, b: (0, 0))
    return pl.pallas_call(
        _hyena_kernel,
        grid=(ncb, nb),
        in_specs=[part(0), part(1), part(2), wpart(0), wpart(1), wpart(2), mat, mat, mat,
                  pl.BlockSpec((HY_ORDER, 3, L, cb), lambda c, b: (0, 0, 0, c)),
                  pl.BlockSpec(memory_space=pl.ANY)],
        out_specs=pl.BlockSpec((None, L, cb), lambda c, b: (N_BRANCH - 1, rb0 + b, c)),
        out_shape=jax.ShapeDtypeStruct(o_stack.shape, o_stack.dtype),
        input_output_aliases={10: 0},
        compiler_params=_params("parallel", "arbitrary"),
        name="hyena_%d" % L,
    )(p, p, p, hy_short, hy_short, hy_short, fcb, fsb, fstb, tabs, o_stack)


def kernel(x_prompt, x_sample, cache_a_k, cache_a_v, cache_b_k, cache_b_v, cache_c_k, cache_c_v, c, c_ctx, w_ada, b_ada, g_mix, w_in, diff_lambda, diff_norm_g, na_bias, swa_sink, hy_short, hy_w1, hy_b1, hy_w2, hy_b2, hy_w3, hy_freq, hy_decay, hy_bias, w_branch, w_out, g_mlp, w_up, w_down, g_final):
    nb_ctx, nb_lat = x_prompt.shape[0], x_sample.shape[0]
    n_ctx = nb_ctx * CTX_SEQ
    n_lat = nb_lat * LAT_SEQ
    assert x_prompt.shape[1:] == (CTX_SEQ, D_MODEL) and x_sample.shape[1:] == (LAT_SEQ, D_MODEL)
    assert nb_lat <= CTX_MOD_ROW and n_ctx % TM_INPROJ == 0

    x = jnp.concatenate([x_prompt.reshape(n_ctx, D_MODEL), x_sample.reshape(n_lat, D_MODEL)], axis=0)
    cv = jnp.zeros((N_MOD_ROWS, D_MODEL), f32).at[:nb_lat].set(c).at[CTX_MOD_ROW].set(c_ctx)
    mod = _adaln_all(cv, w_ada, b_ada).reshape(DEPTH * N_MOD_ROWS * 6, 1, D_MODEL)
    cos_t, sin_t = (jnp.asarray(a) for a in _rope_tables())
    caches = (cache_a_k.reshape(nb_lat, DEPTH, CTX_SEQ, 512), cache_a_v.reshape(nb_lat, DEPTH, CTX_SEQ, 512),
              cache_b_k.reshape(nb_lat, DEPTH, CTX_SEQ, 512), cache_b_v.reshape(nb_lat, DEPTH, CTX_SEQ, 512),
              cache_c_k.reshape(nb_lat, DEPTH, CTX_SEQ, 128), cache_c_v.reshape(nb_lat, DEPTH, CTX_SEQ, 128))

    kv = None
    for li in range(DEPTH):
        wl = w_in[li]
        w_mix = jnp.concatenate([wl[:, 0:512], wl[:, 1536:2048], wl[:, 3072:3584], wl[:, 512:1536],
                                 wl[:, 2048:3072], wl[:, 3584:MIX_COLS]], axis=1).astype(bf16)
        w5 = jnp.concatenate([wl[:, MIX_COLS:], w_out[li]], axis=1).astype(bf16)
        wb = w_branch[li].astype(bf16)
        wu = w_up[li].astype(bf16)
        wd = w_down[li].astype(bf16)
        g1 = g_mix[li].reshape(1, D_MODEL)
        g2 = g_mlp[li].reshape(1, D_MODEL)
        dl = diff_lambda[li]
        dg = diff_norm_g[li].reshape(1, A_V)
        sink = swa_sink[li]

        p, kv, h = _inproj(x, mod, g1, w_mix, kv, li, n_ctx)
        o_stack = _ctx_attn(p, dl, dg, sink, li, n_ctx)
        na_dense = _na_bias_dense(na_bias[li])
        o_stack = _lat_attn(p, o_stack, caches, na_dense, dl, dg, sink, cos_t, sin_t, li, n_ctx, nb_lat)

        w1p = jnp.pad(hy_w1[li], ((0, HY_EMB_PAD - HY_EMB), (0, 0)))
        hy_args = (w1p, hy_b1[li].reshape(1, HY_FFN), hy_w2[li], hy_b2[li].reshape(1, HY_FFN), hy_w3[li],
                   hy_freq[li].reshape(1, HY_FFN), hy_decay[li].reshape(1, -1), hy_bias[li])
        o_stack = _hyena(p, o_stack, hy_short[li], _hy_filter_tables(CTX_SEQ, *hy_args), CTX_SEQ, 0, nb_ctx)
        o_stack = _hyena(p, o_stack, hy_short[li], _hy_filter_tables(LAT_SEQ, *hy_args), LAT_SEQ, n_ctx, nb_lat)
        x = _merge(x, mod, h, o_stack, w5, wb, li, n_ctx)
        x = _mlp(x, mod, g2, wu, wd, g_final.reshape(1, D_MODEL), li, n_ctx, final_norm=(li == DEPTH - 1))

    y_prompt = x[:n_ctx].reshape(nb_ctx, CTX_SEQ, D_MODEL)
    y_sample = x[n_ctx:].reshape(nb_lat, LAT_SEQ, D_MODEL)
    kv_shapes = ((0, 512, (2, A_HEADS, A_QK)), (512, 1024, (A_HEADS, A_V)),
                 (1024, 1536, (B_HEADS, HEAD_DIM)), (1536, 2048, (B_HEADS, HEAD_DIM)),
                 (2048, 2176, (C_KV_HEADS, HEAD_DIM)), (2176, 2304, (C_KV_HEADS, HEAD_DIM)))
    new_kv = tuple(kv[..., a:b].reshape((nb_ctx, DEPTH, CTX_SEQ) + s) for a, b, s in kv_shapes)
    return (y_prompt, y_sample) + new_kv
```

```python
import functools
import math

import numpy as np
import jax
import jax.numpy as jnp
from jax import lax
from jax.experimental import pallas as pl
from jax.experimental.pallas import tpu as pltpu

f32 = jnp.float32
bf16 = jnp.bfloat16

D_MODEL = 2048
DEPTH = 2
CTX_SEQ = 256
LAT_SEQ = 1024
GRID_W = 64
GRID_ROWS = LAT_SEQ // GRID_W
BRANCH_WIDTH = 512
N_BRANCH = 4
HEAD_DIM = 64
A_QK = 64
A_V = 128
A_HEADS = 4
B_HEADS = 8
NA_ROWS = 8
NA_COLS = 16
C_HEADS = 8
C_KV_HEADS = 2
C_GROUP = 4
SWA_WINDOW = 128
SWA_KEYS = 512
NA_Q_ROWS = 4
NA_KEY_ROWS = 12
HY_WIDTH = 512
HY_ORDER = 2
HY_BANDS = 16
HY_EMB = 1 + 2 * HY_BANDS
HY_EMB_PAD = 128
HY_FFN = 64
D_FF = 4 * D_MODEL
ROPE_BASE = 10000.0
EPS = 1e-6
NEG_INF = -1e30
MIX_COLS = 5376
GATE_COLS = N_BRANCH * D_MODEL
COL_AQ, COL_BQ, COL_CQ, COL_AK, COL_AV, COL_BK, COL_BV = 0, 1, 2, 3, 4, 5, 6
COL_CKV_256 = 14
KV_COL0 = 1536
KV_COLS = 2304
HY_COL0 = 3840
N_MOD_ROWS = 16
CTX_MOD_ROW = 8

VMEM_LIMIT = 56 * 1024 * 1024
TM_INPROJ = 1024
TN_INPROJ = 768
TM_MERGE = 512
TM_MLP = 512
TF_MLP = 1024
TQ_ATTN = 256
TN_ADA = 1024
HY_CB = 256
HY_ROWS_PER_STEP = 2048


def _params(*sem):
    return pltpu.CompilerParams(dimension_semantics=sem, vmem_limit_bytes=VMEM_LIMIT)


def _dot(a, b):
    return jnp.dot(a, b, preferred_element_type=f32)


def _dot_nt(a, b):
    return lax.dot_general(a, b, (((1,), (1,)), ((), ())), preferred_element_type=f32)


def _dot_hi(a, b):
    return jnp.dot(a, b, preferred_element_type=f32, precision=lax.Precision.HIGHEST)


def _dot_3pass(a, b):
    a_hi, b_hi = a.astype(bf16), b.astype(bf16)
    a_lo = (a - a_hi.astype(f32)).astype(bf16)
    b_lo = (b - b_hi.astype(f32)).astype(bf16)
    return _dot(a_hi, b_hi) + (_dot(a_hi, b_lo) + _dot(a_lo, b_hi))


@functools.lru_cache(maxsize=None)
def _rope_tables():
    half = HEAD_DIM // 2
    nf = half // 2
    inv = ROPE_BASE ** (-np.arange(nf, dtype=np.float64) / nf)
    t = np.arange(LAT_SEQ)
    pos = np.stack([t // GRID_W, t % GRID_W], axis=1).astype(np.float64)
    lane = np.arange(HEAD_DIM)
    ang = pos[:, lane // half] * inv[lane % nf][None, :]
    first = (lane % half) < nf
    cos = np.cos(ang)
    sin = np.where(first[None, :], -np.sin(ang), np.sin(ang))
    reps = 512 // HEAD_DIM
    return (np.tile(cos, (1, reps)).astype(np.float32), np.tile(sin, (1, reps)).astype(np.float32))


@functools.lru_cache(maxsize=None)
def _dft_tables(L):
    f = np.arange(L, dtype=np.int64)
    prod = (f[:, None] * f[None, :]) % (2 * L)
    ang = np.pi * prod.astype(np.float64) / L
    fc = np.cos(ang)
    fs = np.sin(ang)
    fs[0, :] = np.where(f % 2 == 0, 1.0, -1.0)
    return fc.astype(np.float32), fs.astype(np.float32)


@functools.lru_cache(maxsize=None)
def _hyena_features(L):
    n = np.arange(L, dtype=np.float64)[:, None]
    t = n / max(L - 1, 1)
    w = 2.0 * math.pi * n / L
    bands = np.linspace(1e-4, HY_BANDS - 1, HY_BANDS, dtype=np.float64)[None, :]
    z = np.concatenate([t, np.cos(bands * w), -np.sin(bands * w)], axis=-1)
    z = np.pad(z, ((0, 0), (0, HY_EMB_PAD - HY_EMB)))
    return z.astype(np.float32)


def _modulated_norm(x, g, scale, shift):
    ms = jnp.mean(x * x, axis=-1, keepdims=True)
    return x * lax.rsqrt(ms + EPS) * (g * (1.0 + scale)) + shift


def _rope(x, cos, sin_signed):
    x = x.astype(f32)
    n = x.shape[-1]
    lane = lax.broadcasted_iota(jnp.int32, x.shape, 1)
    first = (lane & (HEAD_DIM // 2 - 1)) < (HEAD_DIM // 4)
    partner = jnp.where(first, pltpu.roll(x, n - HEAD_DIM // 4, 1), pltpu.roll(x, HEAD_DIM // 4, 1))
    return x * cos + partner * sin_signed


LOG2E = 1.4426950408889634
QK_SCALE2 = HEAD_DIM ** -0.5 * LOG2E
PAIR = 2 * HEAD_DIM


def _lo_half(shape):
    return lax.broadcasted_iota(jnp.int32, shape, 1) < HEAD_DIM


def _keep_half(x, half, fill):
    lo = _lo_half(x.shape)
    return jnp.where(lo if half == 0 else jnp.logical_not(lo), x, fill)


def _head_q(q_tile, half):
    return _keep_half(q_tile, half, 0.0).astype(bf16)


def _values_with_ones(v_tile, half):
    return _keep_half(v_tile.astype(f32), half, 1.0).astype(bf16)


def _exp2_parts(scores, sink2=None):
    m = jnp.max(scores[0], axis=-1, keepdims=True)
    for s in scores[1:]:
        m = jnp.maximum(m, jnp.max(s, axis=-1, keepdims=True))
    if sink2 is not None:
        m = jnp.maximum(m, sink2)
    return [jnp.exp2(s - m) for s in scores], m


def _pair_out(acc_even, acc_odd, extra_even=None, extra_odd=None):
    lo = _lo_half(acc_even.shape)
    num = jnp.where(lo, acc_even, acc_odd)
    den = pltpu.roll(jnp.where(lo, acc_odd, acc_even), HEAD_DIM, 1)
    if extra_even is not None:
        den = den + jnp.where(lo, extra_even, extra_odd)
    return num / den


def _swap_halves_variants(v_tile):
    sw = pltpu.roll(v_tile, HEAD_DIM, 1)
    lo = _lo_half(v_tile.shape)
    tiles = (jnp.where(lo, v_tile, 1.0), jnp.where(lo, 1.0, sw), jnp.where(lo, sw, 1.0), jnp.where(lo, 1.0, v_tile))
    return [t.astype(bf16) for t in tiles]


def _gqa_head_q(q_tile, q_swapped, half, g):
    return _keep_half(q_tile if half == g else q_swapped, g, 0.0).astype(bf16)


def _diff_lambda(dl_ref, lam_init):
    dl = dl_ref[...]
    a = jnp.sum(dl[0:1] * dl[1:2], axis=-1, keepdims=True)
    b = jnp.sum(dl[2:3] * dl[3:4], axis=-1, keepdims=True)
    return jnp.exp(a) - jnp.exp(b) + lam_init


def _diff_head(q, k_of, v_of, h, lam, dg, lam_init):
    res = []
    for c in range(2):
        t = c * (A_HEADS // 2) + h // 2
        s = _dot_nt(_head_q(q[:, t * PAIR:(t + 1) * PAIR], h % 2), k_of(t))
        (e,), _ = _exp2_parts([s])
        res.append((_dot(e.astype(bf16), v_of(h)), jnp.sum(e, axis=-1, keepdims=True)))
    o = res[0][0] * (1.0 / res[0][1]) - res[1][0] * (lam / res[1][1])
    o = o * lax.rsqrt(jnp.mean(o * o, axis=-1, keepdims=True) + EPS) * dg
    return o * (1.0 - lam_init)


def _ada_kernel(cv_ref, w_ref, b_ref, o_ref):
    cv = cv_ref[...]
    s = (cv * jax.nn.sigmoid(cv)).astype(bf16)
    o_ref[...] = _dot(s, w_ref[...].astype(bf16)) + b_ref[...]


def _adaln_all(cv, w_ada, b_ada):
    n6 = 6 * D_MODEL
    return pl.pallas_call(
        _ada_kernel,
        grid=(DEPTH, n6 // TN_ADA),
        in_specs=[pl.BlockSpec((N_MOD_ROWS, D_MODEL), lambda l, j: (0, 0)),
                  pl.BlockSpec((None, D_MODEL, TN_ADA), lambda l, j: (l, 0, j)),
                  pl.BlockSpec((None, 1, TN_ADA), lambda l, j: (l, 0, j))],
        out_specs=pl.BlockSpec((None, N_MOD_ROWS, TN_ADA), lambda l, j: (l, 0, j)),
        out_shape=jax.ShapeDtypeStruct((DEPTH, N_MOD_ROWS, n6), f32),
        compiler_params=_params("parallel", "parallel"),
        name="adaln",
    )(cv, w_ada, b_ada.reshape(DEPTH, 1, n6))


def _mod_spec(li, k, tm, is_ctx):
    def index(i, j):
        r = CTX_MOD_ROW if is_ctx else (i * tm) // LAT_SEQ
        return ((li * N_MOD_ROWS + r) * 6 + k, 0, 0)

    return pl.BlockSpec((None, 1, D_MODEL), index)


def _inproj_kernel(x_ref, sh_ref, sc_ref, g_ref, w_ref, *rest, kv_j0, kv_nj, emit_kv):
    o_ref, h_ref = rest[-3 if emit_kv else -2], rest[-1]
    j = pl.program_id(1)

    @pl.when(j == 0)
    def _():
        h_ref[...] = _modulated_norm(x_ref[...], g_ref[...], sc_ref[...], sh_ref[...]).astype(bf16)

    res = _dot(h_ref[...], w_ref[...])
    o_ref[...] = res.astype(bf16)

    if emit_kv:
        kv_ref = rest[-2]

        @pl.when((j >= kv_j0) & (j < kv_j0 + kv_nj))
        def _():
            kv_ref[...] = res.reshape(kv_ref.shape)


def _inproj(x, mod, g, w, li, is_ctx, kv_prev=None):
    T = x.shape[0]
    tm, tn = TM_INPROJ, TN_INPROJ
    kv_j0, kv_nj = KV_COL0 // tn, KV_COLS // tn

    in_specs = [pl.BlockSpec((tm, D_MODEL), lambda i, j: (i, 0)),
                _mod_spec(li, 0, tm, is_ctx), _mod_spec(li, 1, tm, is_ctx),
                pl.BlockSpec((1, D_MODEL), lambda i, j: (0, 0)),
                pl.BlockSpec((D_MODEL, tn), lambda i, j: (0, j))]
    args = [x, mod, mod, g, w]
    out_specs = [pl.BlockSpec((tm, tn), lambda i, j: (i, j)), pl.BlockSpec((tm, D_MODEL), lambda i, j: (i, 0))]
    out_shape = [jax.ShapeDtypeStruct((T, MIX_COLS), bf16), jax.ShapeDtypeStruct((T, D_MODEL), bf16)]
    aliases = {}
    if is_ctx:
        out_specs.insert(1, pl.BlockSpec((tm // CTX_SEQ, None, CTX_SEQ, tn),
                                         lambda i, j: (i, li, 0, jnp.clip(j - kv_j0, 0, kv_nj - 1))))
        out_shape.insert(1, jax.ShapeDtypeStruct((T // CTX_SEQ, DEPTH, CTX_SEQ, KV_COLS), f32))
        if kv_prev is not None:
            in_specs.append(pl.BlockSpec(memory_space=pl.ANY))
            args.append(kv_prev)
            aliases = {len(args) - 1: 1}
    return pl.pallas_call(
        functools.partial(_inproj_kernel, kv_j0=kv_j0, kv_nj=kv_nj, emit_kv=is_ctx),
        grid=(T // tm, MIX_COLS // tn),
        in_specs=in_specs,
        out_specs=out_specs,
        out_shape=out_shape,
        input_output_aliases=aliases,
        compiler_params=_params("parallel", "arbitrary"),
        name="inproj",
    )(*args)


def _merge_kernel(x_ref, gt_ref, h_ref, o_ref, w_ref, wb_ref, out_ref, acc_ref):
    n = pl.program_id(1)

    def contribution():
        return jax.nn.sigmoid(_dot(h_ref[...], w_ref[...])) * _dot(o_ref[...], wb_ref[...])

    @pl.when(n == 0)
    def _():
        acc_ref[...] = contribution()

    @pl.when((n > 0) & (n < N_BRANCH))
    def _():
        acc_ref[...] += contribution()

    @pl.when(n == N_BRANCH)
    def _():
        out_ref[...] = x_ref[...] + gt_ref[...] * _dot(acc_ref[...].astype(bf16), w_ref[...])


def _merge(x, mod, h, o_stack, w5, wb, li, is_ctx):
    T = x.shape[0]
    tm = TM_MERGE
    last = N_BRANCH - 1
    return pl.pallas_call(
        _merge_kernel,
        grid=(T // tm, N_BRANCH + 1),
        in_specs=[pl.BlockSpec((tm, D_MODEL), lambda i, n: (i, 0)),
                  _mod_spec(li, 2, tm, is_ctx),
                  pl.BlockSpec((tm, D_MODEL), lambda i, n: (i, 0)),
                  pl.BlockSpec((None, tm, BRANCH_WIDTH), lambda i, n: (jnp.minimum(n, last), i, 0)),
                  pl.BlockSpec((D_MODEL, D_MODEL), lambda i, n: (0, n)),
                  pl.BlockSpec((None, BRANCH_WIDTH, D_MODEL), lambda i, n: (jnp.minimum(n, last), 0, 0))],
        out_specs=pl.BlockSpec((tm, D_MODEL), lambda i, n: (i, 0)),
        out_shape=jax.ShapeDtypeStruct((T, D_MODEL), f32),
        scratch_shapes=[pltpu.VMEM((tm, D_MODEL), f32)],
        compiler_params=_params("parallel", "arbitrary"),
        name="merge",
    )(x, mod, h, o_stack, w5, wb)


def _mlp_kernel(x_ref, sh_ref, sc_ref, gt_ref, g_ref, wu_ref, wd_ref, gf_ref, out_ref, h_ref, acc_ref,
                *, final_norm):
    j = pl.program_id(1)

    def chunk():
        a = jnp.maximum(_dot(h_ref[...], wu_ref[...]), 0.0)
        return _dot((a * a).astype(bf16), wd_ref[...])

    @pl.when(j == 0)
    def _():
        h_ref[...] = _modulated_norm(x_ref[...], g_ref[...], sc_ref[...], sh_ref[...]).astype(bf16)
        acc_ref[...] = chunk()

    @pl.when(j > 0)
    def _():
        acc_ref[...] += chunk()

    @pl.when(j == pl.num_programs(1) - 1)
    def _():
        y = x_ref[...] + gt_ref[...] * acc_ref[...]
        if final_norm:
            y = y * lax.rsqrt(jnp.mean(y * y, axis=-1, keepdims=True) + EPS) * gf_ref[...]
        out_ref[...] = y


def _mlp(x, mod, g, w_up, w_down, g_final, li, is_ctx, final_norm):
    T = x.shape[0]
    tm, tf = TM_MLP, TF_MLP
    return pl.pallas_call(
        functools.partial(_mlp_kernel, final_norm=final_norm),
        grid=(T // tm, D_FF // tf),
        in_specs=[pl.BlockSpec((tm, D_MODEL), lambda i, j: (i, 0)),
                  _mod_spec(li, 3, tm, is_ctx), _mod_spec(li, 4, tm, is_ctx), _mod_spec(li, 5, tm, is_ctx),
                  pl.BlockSpec((1, D_MODEL), lambda i, j: (0, 0)),
                  pl.BlockSpec((D_MODEL, tf), lambda i, j: (0, j)),
                  pl.BlockSpec((tf, D_MODEL), lambda i, j: (j, 0)),
                  pl.BlockSpec((1, D_MODEL), lambda i, j: (0, 0))],
        out_specs=pl.BlockSpec((tm, D_MODEL), lambda i, j: (i, 0)),
        out_shape=jax.ShapeDtypeStruct((T, D_MODEL), f32),
        scratch_shapes=[pltpu.VMEM((tm, D_MODEL), bf16), pltpu.VMEM((tm, D_MODEL), f32)],
        compiler_params=_params("parallel", "arbitrary"),
        name="mlp",
    )(x, mod, mod, mod, g, w_up, w_down, g_final)


def _ctx_attn_kernel(aq_ref, ak_ref, av_ref, bq_ref, bk_ref, bv_ref, cq_ref, ckv_ref,
                     dl_ref, dg_ref, sink_ref, o_ref, *, lam_init):
    lam = _diff_lambda(dl_ref, lam_init)
    q = aq_ref[...].astype(f32) * QK_SCALE2
    outs = [_diff_head(q, lambda t: ak_ref[:, t * PAIR:(t + 1) * PAIR], lambda h: av_ref[:, h * A_V:(h + 1) * A_V],
                       h, lam, dg_ref[...], lam_init) for h in range(A_HEADS)]
    o_ref[0] = jnp.concatenate(outs, axis=1).astype(bf16)
    q = bq_ref[...].astype(f32) * QK_SCALE2
    tiles = []
    for t in range(B_HEADS // 2):
        cols = slice(t * PAIR, (t + 1) * PAIR)
        accs = []
        for half in range(2):
            (e,), _ = _exp2_parts([_dot_nt(_head_q(q[:, cols], half), bk_ref[:, cols])])
            accs.append(_dot(e.astype(bf16), _values_with_ones(bv_ref[:, cols], half)))
        tiles.append(_pair_out(*accs))
    o_ref[1] = jnp.concatenate(tiles, axis=1).astype(bf16)
    q = cq_ref[...].astype(f32) * QK_SCALE2
    k_tile = ckv_ref[:, 0:PAIR]
    v_variants = _swap_halves_variants(ckv_ref[:, PAIR:2 * PAIR].astype(f32))
    tiles = []
    for t in range(C_HEADS // 2):
        q_tile = q[:, t * PAIR:(t + 1) * PAIR]
        q_swapped = pltpu.roll(q_tile, HEAD_DIM, 1)
        accs, extras = [], []
        for half in range(2):
            h = 2 * t + half
            g = h // C_GROUP
            sink2 = sink_ref[h] * LOG2E
            (e,), m = _exp2_parts([_dot_nt(_gqa_head_q(q_tile, q_swapped, half, g), k_tile)], sink2)
            accs.append(_dot(e.astype(bf16), v_variants[g * 2 + half]))
            extras.append(jnp.exp2(sink2 - m))
        tiles.append(_pair_out(accs[0], accs[1], extras[0], extras[1]))
    o_ref[2] = jnp.concatenate(tiles, axis=1).astype(bf16)


def _ctx_attn(p, dl, dg, sink, li):
    nb = p.shape[0] // CTX_SEQ
    S = CTX_SEQ
    lam_init = 0.8 - 0.6 * math.exp(-0.3 * li)

    def col(c):
        return pl.BlockSpec((S, 512), lambda b, c=c: (b, c))

    return pl.pallas_call(
        functools.partial(_ctx_attn_kernel, lam_init=lam_init),
        grid=(nb,),
        in_specs=[col(COL_AQ), col(COL_AK), col(COL_AV), col(COL_BQ), col(COL_BK), col(COL_BV), col(COL_CQ),
                  pl.BlockSpec((S, 256), lambda b: (b, COL_CKV_256)),
                  pl.BlockSpec((4, A_QK), lambda b: (0, 0)),
                  pl.BlockSpec((1, A_V), lambda b: (0, 0)),
                  pl.BlockSpec(memory_space=pltpu.SMEM)],
        out_specs=pl.BlockSpec((N_BRANCH - 1, S, BRANCH_WIDTH), lambda b: (0, b, 0)),
        out_shape=jax.ShapeDtypeStruct((N_BRANCH, p.shape[0], BRANCH_WIDTH), bf16),
        compiler_params=_params("parallel"),
        name="ctx_attn",
    )(p, p, p, p, p, p, p, p, dl, dg, sink)


def _lat_a_kernel(q_ref, k_ref, v_ref, ck_ref, cv_ref, cosq_ref, sinq_ref, cosk_ref, sink_ref,
                  dl_ref, dg_ref, o_ref, kk_ref, vv_ref, *, lam_init):
    L = LAT_SEQ

    @pl.when(pl.program_id(1) == 0)
    def _():
        kk_ref[0:L, :] = _rope(k_ref[...], cosk_ref[...], sink_ref[...]).astype(bf16)
        kk_ref[L:, :] = ck_ref[...].astype(bf16)
        vv_ref[0:L, :] = v_ref[...].astype(bf16)
        vv_ref[L:, :] = cv_ref[...].astype(bf16)

    lam = _diff_lambda(dl_ref, lam_init)
    q = _rope(q_ref[...], cosq_ref[...], sinq_ref[...]) * QK_SCALE2
    outs = [_diff_head(q, lambda t: kk_ref[:, t * PAIR:(t + 1) * PAIR], lambda h: vv_ref[:, h * A_V:(h + 1) * A_V],
                       h, lam, dg_ref[...], lam_init) for h in range(A_HEADS)]
    o_ref[...] = jnp.concatenate(outs, axis=1).astype(bf16)


def _lat_b_kernel(q_ref, k_ref, v_ref, ck_ref, cv_ref, bias_ref, _, o_ref, kk_ref, va_ref):
    L, W = LAT_SEQ, NA_KEY_ROWS * GRID_W
    qb = pl.program_id(1)

    @pl.when(qb == 0)
    def _():
        kk_ref[0:L, :] = k_ref[...]
        kk_ref[L:, :] = ck_ref[...].astype(bf16)
        for t in range(B_HEADS // 2):
            cols = slice(t * PAIR, (t + 1) * PAIR)
            for half in range(2):
                va_ref[2 * t + half, 0:L, :] = _values_with_ones(v_ref[:, cols], half)
                va_ref[2 * t + half, L:, :] = _values_with_ones(cv_ref[:, cols], half)

    start = pl.multiple_of(_na_key_row0(qb) * GRID_W, 256)
    q = q_ref[...].astype(f32) * QK_SCALE2
    tiles = []
    for t in range(B_HEADS // 2):
        cols = slice(t * PAIR, (t + 1) * PAIR)
        accs = []
        for half in range(2):
            h = 2 * t + half
            qh = _head_q(q[:, cols], half)
            s_loc = _dot_nt(qh, kk_ref[pl.ds(start, W), cols]) + bias_ref[h].astype(f32)
            s_ctx = _dot_nt(qh, kk_ref[L:, cols])
            (e_loc, e_ctx), _m = _exp2_parts([s_loc, s_ctx])
            accs.append(_dot(e_loc.astype(bf16), va_ref[h, pl.ds(start, W), :])
                        + _dot(e_ctx.astype(bf16), va_ref[h, L:, :]))
        tiles.append(_pair_out(*accs))
    o_ref[...] = jnp.concatenate(tiles, axis=1).astype(bf16)


def _lat_c_kernel(q_ref, kv_ref, ck_ref, cv_ref, cosq_ref, sinq_ref, cosk_ref, sink_ref, snk_ref,
                  _, o_ref, kk_ref, va_ref):
    L, W = LAT_SEQ, SWA_KEYS
    tq = q_ref.shape[0]
    qb = pl.program_id(1)

    @pl.when(qb == 0)
    def _():
        kk_ref[0:L, :] = _rope(kv_ref[:, 0:PAIR], cosk_ref[...], sink_ref[...]).astype(bf16)
        kk_ref[L:, :] = ck_ref[...].astype(bf16)
        lat = _swap_halves_variants(kv_ref[:, PAIR:2 * PAIR].astype(f32))
        ctx = _swap_halves_variants(cv_ref[...])
        for idx in range(4):
            va_ref[idx, 0:L, :] = lat[idx]
            va_ref[idx, L:, :] = ctx[idx]

    start = pl.multiple_of(jnp.clip(qb * tq - SWA_WINDOW, 0, L - W), 128)
    q = _rope(q_ref[...], cosq_ref[...], sinq_ref[...]) * QK_SCALE2
    qpos = qb * tq + lax.broadcasted_iota(jnp.int32, (tq, W), 0)
    kpos = start + lax.broadcasted_iota(jnp.int32, (tq, W), 1)
    valid = jnp.abs(qpos - kpos) <= SWA_WINDOW
    tiles = []
    for t in range(C_HEADS // 2):
        q_tile = q[:, t * PAIR:(t + 1) * PAIR]
        q_swapped = pltpu.roll(q_tile, HEAD_DIM, 1)
        accs, extras = [], []
        for half in range(2):
            h = 2 * t + half
            g = h // C_GROUP
            qh = _gqa_head_q(q_tile, q_swapped, half, g)
            s_loc = jnp.where(valid, _dot_nt(qh, kk_ref[pl.ds(start, W), :]), NEG_INF)
            s_ctx = _dot_nt(qh, kk_ref[L:, :])
            sink2 = snk_ref[h] * LOG2E
            (e_loc, e_ctx), m = _exp2_parts([s_loc, s_ctx], sink2)
            idx = g * 2 + half
            accs.append(_dot(e_loc.astype(bf16), va_ref[idx, pl.ds(start, W), :])
                        + _dot(e_ctx.astype(bf16), va_ref[idx, L:, :]))
            extras.append(jnp.exp2(sink2 - m))
        tiles.append(_pair_out(accs[0], accs[1], extras[0], extras[1]))
    o_ref[...] = jnp.concatenate(tiles, axis=1).astype(bf16)


def _lat_attn(p, caches, na_dense, dl, dg, sink, cos_t, sin_t, li):
    cak, cav, cbk, cbv, cck, ccv = caches
    L, tq = LAT_SEQ, TQ_ATTN
    assert tq == NA_Q_ROWS * GRID_W and SWA_KEYS >= tq + 2 * SWA_WINDOW
    nq = L // tq
    nb_lat = p.shape[0] // L
    lam_init = 0.8 - 0.6 * math.exp(-0.3 * li)

    def qcol(c):
        return pl.BlockSpec((tq, 512), lambda b, i, c=c: (b * nq + i, c))

    def kcol(c, width=512):
        return pl.BlockSpec((L, width), lambda b, i, c=c: (b, c))

    def cache(width):
        return pl.BlockSpec((None, None, CTX_SEQ, width), lambda b, i: (b, li, 0, 0))

    tab_q = pl.BlockSpec((tq, 512), lambda b, i: (i, 0))
    tab_k = pl.BlockSpec((L, 512), lambda b, i: (0, 0))
    tab_k128 = pl.BlockSpec((L, 128), lambda b, i: (0, 0))
    o_sds = jax.ShapeDtypeStruct((N_BRANCH, p.shape[0], BRANCH_WIDTH), bf16)
    anyspec = pl.BlockSpec(memory_space=pl.ANY)

    def o_spec(branch):
        return pl.BlockSpec((None, tq, BRANCH_WIDTH), lambda b, i: (branch, b * nq + i, 0))

    kv_scratch = [pltpu.VMEM((L + CTX_SEQ, 512), bf16), pltpu.VMEM((L + CTX_SEQ, 512), bf16)]
    smem = pl.BlockSpec(memory_space=pltpu.SMEM)
    cp = _params("parallel", "arbitrary")

    o_stack = pl.pallas_call(
        functools.partial(_lat_a_kernel, lam_init=lam_init),
        grid=(nb_lat, nq),
        in_specs=[qcol(COL_AQ), kcol(COL_AK), kcol(COL_AV), cache(512), cache(512),
                  tab_q, tab_q, tab_k, tab_k,
                  pl.BlockSpec((4, A_QK), lambda b, i: (0, 0)), pl.BlockSpec((1, A_V), lambda b, i: (0, 0))],
        out_specs=o_spec(0), out_shape=o_sds, scratch_shapes=kv_scratch,
        compiler_params=cp, name="lat_attn_a",
    )(p, p, p, cak, cav, cos_t, sin_t, cos_t, sin_t, dl, dg)

    o_stack = pl.pallas_call(
        _lat_b_kernel,
        grid=(nb_lat, nq),
        in_specs=[qcol(COL_BQ), kcol(COL_BK), kcol(COL_BV), cache(512), cache(512),
                  pl.BlockSpec((B_HEADS, tq, NA_KEY_ROWS * GRID_W), lambda b, i: (0, i, 0)), anyspec],
        out_specs=o_spec(1), out_shape=o_sds,
        scratch_shapes=[pltpu.VMEM((L + CTX_SEQ, 512), bf16), pltpu.VMEM((B_HEADS, L + CTX_SEQ, PAIR), bf16)],
        input_output_aliases={6: 0}, compiler_params=cp, name="lat_attn_b",
    )(p, p, p, cbk, cbv, na_dense, o_stack)

    return pl.pallas_call(
        _lat_c_kernel,
        grid=(nb_lat, nq),
        in_specs=[qcol(COL_CQ), kcol(COL_CKV_256, 256), cache(PAIR), cache(PAIR),
                  tab_q, tab_q, tab_k128, tab_k128, smem, anyspec],
        out_specs=o_spec(2), out_shape=o_sds,
        scratch_shapes=[pltpu.VMEM((L + CTX_SEQ, PAIR), bf16), pltpu.VMEM((4, L + CTX_SEQ, PAIR), bf16)],
        input_output_aliases={9: 0}, compiler_params=cp, name="lat_attn_c",
    )(p, p, cck, ccv, cos_t, sin_t, cos_t, sin_t, sink, o_stack)


def _na_key_row0(qb):
    lo, hi = NA_Q_ROWS * qb - NA_ROWS // 2, GRID_ROWS - NA_KEY_ROWS
    return min(max(lo, 0), hi) if isinstance(qb, int) else jnp.clip(lo, 0, hi)


def _na_bias_kernel(rb_ref, o_ref):
    h = pl.program_id(0)
    W = GRID_W
    n_dc = 2 * NA_COLS - 1
    n_dr = 2 * NA_ROWS - 1
    qc = lax.broadcasted_iota(jnp.int32, (W, 2 * W), 0)
    lane = lax.broadcasted_iota(jnp.int32, (W, 2 * W), 1)
    second = lane >= W
    kc = jnp.where(second, lane - W, lane)
    dc = jnp.clip(kc - qc, -(NA_COLS - 1), NA_COLS - 1) + NA_COLS - 1
    c0 = jnp.clip(qc - NA_COLS // 2, 0, W - NA_COLS)
    col_ok = (kc >= c0) & (kc < c0 + NA_COLS)
    base = h * (n_dr * n_dc)

    def pair_tile(dr0):
        t = jnp.zeros((W, 2 * W), f32)
        for j in range(n_dc):
            lo = rb_ref[base + dr0 * n_dc + j] if 0 <= dr0 < n_dr else 0.0
            hi = rb_ref[base + (dr0 + 1) * n_dc + j] if 0 <= dr0 + 1 < n_dr else 0.0
            t = jnp.where(dc == j, jnp.where(second, hi, lo), t)
        return jnp.where(col_ok, t * LOG2E, NEG_INF)

    tiles = {dr0: pair_tile(dr0) for dr0 in range(-1, n_dr)}
    neg = jnp.full((W, 2 * W), NEG_INF, f32)
    for qr in range(GRID_ROWS):
        r0 = min(max(qr - NA_ROWS // 2, 0), GRID_ROWS - NA_ROWS)
        k0 = _na_key_row0(qr // NA_Q_ROWS)
        assert k0 % 2 == 0 and k0 <= r0 and r0 + NA_ROWS <= k0 + NA_KEY_ROWS
        for pr in range(NA_KEY_ROWS // 2):
            kr = k0 + 2 * pr
            ok0 = r0 <= kr < r0 + NA_ROWS
            ok1 = r0 <= kr + 1 < r0 + NA_ROWS
            if not (ok0 or ok1):
                t = neg
            else:
                t = tiles[kr - qr + NA_ROWS - 1]
                if not ok0:
                    t = jnp.where(second, t, NEG_INF)
                if not ok1:
                    t = jnp.where(second, NEG_INF, t)
            o_ref[qr * W:(qr + 1) * W, 2 * pr * W:(2 * pr + 2) * W] = t.astype(bf16)


def _na_bias_dense(rel_bias):
    kw = NA_KEY_ROWS * GRID_W
    return pl.pallas_call(
        _na_bias_kernel,
        grid=(B_HEADS,),
        in_specs=[pl.BlockSpec(memory_space=pltpu.SMEM)],
        out_specs=pl.BlockSpec((None, LAT_SEQ, kw), lambda h: (h, 0, 0)),
        out_shape=jax.ShapeDtypeStruct((B_HEADS, LAT_SEQ, kw), bf16),
        compiler_params=_params("parallel"),
        name="na_bias",
    )(rel_bias.reshape(-1))


def _hy_filter_kernel(z_ref, w1_ref, b1_ref, w2_ref, b2_ref, fr_ref, w3f_ref, w3b_ref, dcf_ref, dcb_ref,
                      bias_ref, fc_ref, fs_ref, o_ref, *, L):
    z = z_ref[...]
    fr = fr_ref[...]
    h = jnp.sin(fr * (_dot_hi(z, w1_ref[...]) + b1_ref[...]))
    h = jnp.sin(fr * (_dot_hi(h, w2_ref[...]) + b2_ref[...]))
    t = z[:, 0:1]
    hf = _dot_hi(h, w3f_ref[...]) * jnp.exp(-t * jnp.abs(dcf_ref[...]))
    hb = _dot_hi(h, w3b_ref[...]) * jnp.exp(-t * jnp.abs(dcb_ref[...]))
    ssum = hf + hb
    bias = bias_ref[...]
    ga = _dot_3pass(fc_ref[...], ssum) + bias
    gb = _dot_3pass(fs_ref[...], hb - hf)
    g_nyq = _dot_hi(fs_ref[0:8, :], ssum)[0:1] + bias
    row0 = lax.broadcasted_iota(jnp.int32, ga.shape, 0) == 0
    inv = 1.0 / L
    o_ref[0] = jnp.where(row0, 0.5 * ga, ga) * inv
    o_ref[1] = jnp.where(row0, 0.0, gb) * inv
    o_ref[2] = jnp.where(row0, 0.5 * g_nyq, ga) * inv


def _hy_filter_tables(L, w1p, b1, w2, b2, w3, freq, decay, hy_bias):
    cb = HY_CB
    ncb = HY_WIDTH // cb
    z = jnp.asarray(_hyena_features(L))
    fc, fs = (jnp.asarray(a) for a in _dft_tables(L))
    full = lambda shape: pl.BlockSpec(shape, lambda o, c: (0,) * len(shape))
    fwd = lambda rows: pl.BlockSpec((rows, cb), lambda o, c: (0, o * 2 * ncb + c))
    bwd = lambda rows: pl.BlockSpec((rows, cb), lambda o, c: (0, o * 2 * ncb + ncb + c))
    return pl.pallas_call(
        functools.partial(_hy_filter_kernel, L=L),
        grid=(HY_ORDER, ncb),
        in_specs=[full((L, HY_EMB_PAD)), full((HY_EMB_PAD, HY_FFN)), full((1, HY_FFN)),
                  full((HY_FFN, HY_FFN)), full((1, HY_FFN)), full((1, HY_FFN)),
                  fwd(HY_FFN), bwd(HY_FFN), fwd(1), bwd(1),
                  pl.BlockSpec((None, 1, cb), lambda o, c: (o, 0, c)),
                  full((L, L)), full((L, L))],
        out_specs=pl.BlockSpec((None, 3, L, cb), lambda o, c: (o, 0, 0, c)),
        out_shape=jax.ShapeDtypeStruct((HY_ORDER, 3, L, HY_WIDTH), f32),
        compiler_params=_params("parallel", "parallel"),
        name="hy_filter",
    )(z, w1p, b1, w2, b2, freq, w3, w3, decay, decay, hy_bias.reshape(HY_ORDER, 1, HY_WIDTH), fc, fs)


def _hyena_kernel(v_ref, x1_ref, x2_ref, wv_ref, w1_ref, w2_ref, fc_ref, fs_ref, fst_ref, tab_ref, _, o_ref):
    L = fc_ref.shape[0]
    row = lax.broadcasted_iota(jnp.int32, (L, v_ref.shape[1]), 0)

    def short_conv(u_ref, w_ref, s):
        u = u_ref[s * L:(s + 1) * L, :].astype(f32)
        w = w_ref[...]
        prev = jnp.where(row == 0, 0.0, pltpu.roll(u, 1, 0))
        nxt = jnp.where(row == L - 1, 0.0, pltpu.roll(u, L - 1, 0))
        return prev * w[0:1] + u * w[1:2] + nxt * w[2:3]

    def long_conv(u, order):
        ub = u.astype(bf16)
        a = _dot(fc_ref[...], ub)
        b = _dot(fs_ref[...], ub)
        t0, t1, t2 = tab_ref[order, 0], tab_ref[order, 1], tab_ref[order, 2]
        pr = (a * t0 + b * t1).astype(bf16)
        qi = (b * t2 - a * t1).astype(bf16)
        return _dot(fc_ref[...], pr) + _dot(fst_ref[...], qi)

    outs = []
    for s in range(v_ref.shape[0] // L):
        z = short_conv(x1_ref, w1_ref, s) * long_conv(short_conv(v_ref, wv_ref, s), 0)
        outs.append(short_conv(x2_ref, w2_ref, s) * long_conv(z, 1))
    o_ref[...] = jnp.concatenate(outs, axis=0).astype(bf16)


def _hyena(p, o_stack, hy_short, tabs, L):
    cb = HY_CB
    ncb = HY_WIDTH // cb
    c0 = HY_COL0 // cb
    rows = min(HY_ROWS_PER_STEP, p.shape[0])
    nb = p.shape[0] // rows
    fc, fs = _dft_tables(L)
    fcb, fsb, fstb = jnp.asarray(fc, bf16), jnp.asarray(fs, bf16), jnp.asarray(fs.T.copy(), bf16)

    def part(k):
        return pl.BlockSpec((rows, cb), lambda c, b, k=k: (b, c0 + k * ncb + c))

    def wpart(k):
        return pl.BlockSpec((3, cb), lambda c, b, k=k: (0, k * ncb + c))

    mat = pl.BlockSpec((L, L), lambda c, b: (0, 0))
    return pl.pallas_call(
        _hyena_kernel,
        grid=(ncb, nb),
        in_specs=[part(0), part(1), part(2), wpart(0), wpart(1), wpart(2), mat, mat, mat,
                  pl.BlockSpec((HY_ORDER, 3, L, cb), lambda c, b: (0, 0, 0, c)),
                  pl.BlockSpec(memory_space=pl.ANY)],
        out_specs=pl.BlockSpec((None, rows, cb), lambda c, b: (N_BRANCH - 1, b, c)),
        out_shape=jax.ShapeDtypeStruct(o_stack.shape, o_stack.dtype),
        input_output_aliases={10: 0},
        compiler_params=_params("parallel", "arbitrary"),
        name="hyena_%d" % L,
    )(p, p, p, hy_short, hy_short, hy_short, fcb, fsb, fstb, tabs, o_stack)


def kernel(x_prompt, x_sample, cache_a_k, cache_a_v, cache_b_k, cache_b_v, cache_c_k, cache_c_v, c, c_ctx, w_ada, b_ada, g_mix, w_in, diff_lambda, diff_norm_g, na_bias, swa_sink, hy_short, hy_w1, hy_b1, hy_w2, hy_b2, hy_w3, hy_freq, hy_decay, hy_bias, w_branch, w_out, g_mlp, w_up, w_down, g_final):
    nb_ctx, nb_lat = x_prompt.shape[0], x_sample.shape[0]
    assert x_prompt.shape[1:] == (CTX_SEQ, D_MODEL) and x_sample.shape[1:] == (LAT_SEQ, D_MODEL)
    assert nb_lat <= CTX_MOD_ROW and (nb_ctx * CTX_SEQ) % TM_INPROJ == 0

    x_ctx = x_prompt.reshape(nb_ctx * CTX_SEQ, D_MODEL)
    x_lat = x_sample.reshape(nb_lat * LAT_SEQ, D_MODEL)
    cv = jnp.zeros((N_MOD_ROWS, D_MODEL), f32).at[:nb_lat].set(c).at[CTX_MOD_ROW].set(c_ctx)
    mod = _adaln_all(cv, w_ada, b_ada).reshape(DEPTH * N_MOD_ROWS * 6, 1, D_MODEL)
    cos_t, sin_t = (jnp.asarray(a) for a in _rope_tables())
    caches = (cache_a_k.reshape(nb_lat, DEPTH, CTX_SEQ, 512), cache_a_v.reshape(nb_lat, DEPTH, CTX_SEQ, 512),
              cache_b_k.reshape(nb_lat, DEPTH, CTX_SEQ, 512), cache_b_v.reshape(nb_lat, DEPTH, CTX_SEQ, 512),
              cache_c_k.reshape(nb_lat, DEPTH, CTX_SEQ, 128), cache_c_v.reshape(nb_lat, DEPTH, CTX_SEQ, 128))

    kv = None
    for li in range(DEPTH):
        wl = w_in[li]
        w_mix = jnp.concatenate([wl[:, 0:512], wl[:, 1536:2048], wl[:, 3072:3584], wl[:, 512:1536],
                                 wl[:, 2048:3072], wl[:, 3584:MIX_COLS]], axis=1).astype(bf16)
        w5 = jnp.concatenate([wl[:, MIX_COLS:], w_out[li]], axis=1).astype(bf16)
        wb = w_branch[li].astype(bf16)
        wu = w_up[li].astype(bf16)
        wd = w_down[li].astype(bf16)
        g1 = g_mix[li].reshape(1, D_MODEL)
        g2 = g_mlp[li].reshape(1, D_MODEL)
        dl = diff_lambda[li]
        dg = diff_norm_g[li].reshape(1, A_V)
        sink = swa_sink[li]

        w1p = jnp.pad(hy_w1[li], ((0, HY_EMB_PAD - HY_EMB), (0, 0)))
        hy_args = (w1p, hy_b1[li].reshape(1, HY_FFN), hy_w2[li], hy_b2[li].reshape(1, HY_FFN), hy_w3[li],
                   hy_freq[li].reshape(1, HY_FFN), hy_decay[li].reshape(1, -1), hy_bias[li])
        gf = g_final.reshape(1, D_MODEL)
        final = li == DEPTH - 1

        p, kv, h = _inproj(x_ctx, mod, g1, w_mix, li, True, kv)
        o_stack = _ctx_attn(p, dl, dg, sink, li)
        o_stack = _hyena(p, o_stack, hy_short[li], _hy_filter_tables(CTX_SEQ, *hy_args), CTX_SEQ)
        x_ctx = _merge(x_ctx, mod, h, o_stack, w5, wb, li, True)
        x_ctx = _mlp(x_ctx, mod, g2, wu, wd, gf, li, True, final_norm=final)

        p, h = _inproj(x_lat, mod, g1, w_mix, li, False)
        o_stack = _lat_attn(p, caches, _na_bias_dense(na_bias[li]), dl, dg, sink, cos_t, sin_t, li)
        o_stack = _hyena(p, o_stack, hy_short[li], _hy_filter_tables(LAT_SEQ, *hy_args), LAT_SEQ)
        x_lat = _merge(x_lat, mod, h, o_stack, w5, wb, li, False)
        x_lat = _mlp(x_lat, mod, g2, wu, wd, gf, li, False, final_norm=final)

    y_prompt = x_ctx.reshape(nb_ctx, CTX_SEQ, D_MODEL)
    y_sample = x_lat.reshape(nb_lat, LAT_SEQ, D_MODEL)
    kv_shapes = ((0, 512, (2, A_HEADS, A_QK)), (512, 1024, (A_HEADS, A_V)),
                 (1024, 1536, (B_HEADS, HEAD_DIM)), (1536, 2048, (B_HEADS, HEAD_DIM)),
                 (2048, 2176, (C_KV_HEADS, HEAD_DIM)), (2176, 2304, (C_KV_HEADS, HEAD_DIM)))
    new_kv = tuple(kv[..., a:b].reshape((nb_ctx, DEPTH, CTX_SEQ) + s) for a, b, s in kv_shapes)
    return (y_prompt, y_sample) + new_kv
```

```python
import functools
import math

import numpy as np
import jax
import jax.numpy as jnp
from jax import lax
from jax.experimental import pallas as pl
from jax.experimental.pallas import tpu as pltpu

f32 = jnp.float32
bf16 = jnp.bfloat16

D_MODEL = 2048
DEPTH = 2
CTX_SEQ = 256
LAT_SEQ = 1024
GRID_W = 64
GRID_ROWS = LAT_SEQ // GRID_W
BRANCH_WIDTH = 512
N_BRANCH = 4
HEAD_DIM = 64
A_QK = 64
A_V = 128
A_HEADS = 4
B_HEADS = 8
NA_ROWS = 8
NA_COLS = 16
C_HEADS = 8
C_KV_HEADS = 2
C_GROUP = 4
SWA_WINDOW = 128
SWA_KEYS = 512
NA_Q_ROWS = 4
NA_KEY_ROWS = 12
HY_WIDTH = 512
HY_ORDER = 2
HY_BANDS = 16
HY_EMB = 1 + 2 * HY_BANDS
HY_EMB_PAD = 128
HY_FFN = 64
D_FF = 4 * D_MODEL
ROPE_BASE = 10000.0
EPS = 1e-6
NEG_INF = -1e30
MIX_COLS = 5376
GATE_COLS = N_BRANCH * D_MODEL
COL_AQ, COL_BQ, COL_CQ, COL_AK, COL_AV, COL_BK, COL_BV = 0, 1, 2, 3, 4, 5, 6
COL_CKV_256 = 14
KV_COL0 = 1536
KV_COLS = 2304
HY_COL0 = 3840
N_MOD_ROWS = 16
CTX_MOD_ROW = 8

VMEM_LIMIT = 56 * 1024 * 1024
TM_INPROJ = 1024
TN_INPROJ = 768
TM_MERGE = 512
TM_MLP = 512
TF_MLP = 1024
TQ_ATTN = 256
TN_ADA = 1024
HY_CB = 256
HY_ROWS_PER_STEP = 2048


def _params(*sem):
    return pltpu.CompilerParams(dimension_semantics=sem, vmem_limit_bytes=VMEM_LIMIT)


def _dot(a, b):
    return jnp.dot(a, b, preferred_element_type=f32)


def _dot_nt(a, b):
    return lax.dot_general(a, b, (((1,), (1,)), ((), ())), preferred_element_type=f32)


def _dot_hi(a, b):
    return jnp.dot(a, b, preferred_element_type=f32, precision=lax.Precision.HIGHEST)


def _dot_3pass(a, b):
    a_hi, b_hi = a.astype(bf16), b.astype(bf16)
    a_lo = (a - a_hi.astype(f32)).astype(bf16)
    b_lo = (b - b_hi.astype(f32)).astype(bf16)
    return _dot(a_hi, b_hi) + (_dot(a_hi, b_lo) + _dot(a_lo, b_hi))


@functools.lru_cache(maxsize=None)
def _rope_tables():
    half = HEAD_DIM // 2
    nf = half // 2
    inv = ROPE_BASE ** (-np.arange(nf, dtype=np.float64) / nf)
    t = np.arange(LAT_SEQ)
    pos = np.stack([t // GRID_W, t % GRID_W], axis=1).astype(np.float64)
    lane = np.arange(HEAD_DIM)
    ang = pos[:, lane // half] * inv[lane % nf][None, :]
    first = (lane % half) < nf
    cos = np.cos(ang)
    sin = np.where(first[None, :], -np.sin(ang), np.sin(ang))
    reps = 512 // HEAD_DIM
    return (np.tile(cos, (1, reps)).astype(np.float32), np.tile(sin, (1, reps)).astype(np.float32))


@functools.lru_cache(maxsize=None)
def _dft_tables(L):
    f = np.arange(L, dtype=np.int64)
    prod = (f[:, None] * f[None, :]) % (2 * L)
    ang = np.pi * prod.astype(np.float64) / L
    fc = np.cos(ang)
    fs = np.sin(ang)
    fs[0, :] = np.where(f % 2 == 0, 1.0, -1.0)
    return fc.astype(np.float32), fs.astype(np.float32)


@functools.lru_cache(maxsize=None)
def _hyena_features(L):
    n = np.arange(L, dtype=np.float64)[:, None]
    t = n / max(L - 1, 1)
    w = 2.0 * math.pi * n / L
    bands = np.linspace(1e-4, HY_BANDS - 1, HY_BANDS, dtype=np.float64)[None, :]
    z = np.concatenate([t, np.cos(bands * w), -np.sin(bands * w)], axis=-1)
    z = np.pad(z, ((0, 0), (0, HY_EMB_PAD - HY_EMB)))
    return z.astype(np.float32)


def _modulated_norm(x, g, scale, shift):
    ms = jnp.mean(x * x, axis=-1, keepdims=True)
    return x * lax.rsqrt(ms + EPS) * (g * (1.0 + scale)) + shift


def _rope(x, cos, sin_signed):
    x = x.astype(f32)
    n = x.shape[-1]
    lane = lax.broadcasted_iota(jnp.int32, x.shape, 1)
    first = (lane & (HEAD_DIM // 2 - 1)) < (HEAD_DIM // 4)
    partner = jnp.where(first, pltpu.roll(x, n - HEAD_DIM // 4, 1), pltpu.roll(x, HEAD_DIM // 4, 1))
    return x * cos + partner * sin_signed


LOG2E = 1.4426950408889634
QK_SCALE2 = HEAD_DIM ** -0.5 * LOG2E
PAIR = 2 * HEAD_DIM


def _lo_half(shape):
    return lax.broadcasted_iota(jnp.int32, shape, 1) < HEAD_DIM


def _keep_half(x, half, fill):
    lo = _lo_half(x.shape)
    return jnp.where(lo if half == 0 else jnp.logical_not(lo), x, fill)


def _head_q(q_tile, half):
    return _keep_half(q_tile, half, 0.0).astype(bf16)


def _values_with_ones(v_tile, half):
    return _keep_half(v_tile.astype(f32), half, 1.0).astype(bf16)


def _exp2_parts(scores, sink2=None):
    m = jnp.max(scores[0], axis=-1, keepdims=True)
    for s in scores[1:]:
        m = jnp.maximum(m, jnp.max(s, axis=-1, keepdims=True))
    if sink2 is not None:
        m = jnp.maximum(m, sink2)
    return [jnp.exp2(s - m) for s in scores], m


def _pair_out(acc_even, acc_odd, extra_even=None, extra_odd=None):
    lo = _lo_half(acc_even.shape)
    num = jnp.where(lo, acc_even, acc_odd)
    den = pltpu.roll(jnp.where(lo, acc_odd, acc_even), HEAD_DIM, 1)
    if extra_even is not None:
        den = den + jnp.where(lo, extra_even, extra_odd)
    return num / den


def _swap_halves_variants(v_tile):
    sw = pltpu.roll(v_tile, HEAD_DIM, 1)
    lo = _lo_half(v_tile.shape)
    tiles = (jnp.where(lo, v_tile, 1.0), jnp.where(lo, 1.0, sw), jnp.where(lo, sw, 1.0), jnp.where(lo, 1.0, v_tile))
    return [t.astype(bf16) for t in tiles]


def _gqa_head_q(q_tile, q_swapped, half, g):
    return _keep_half(q_tile if half == g else q_swapped, g, 0.0).astype(bf16)


def _diff_lambda(dl_ref, lam_init):
    dl = dl_ref[...]
    a = jnp.sum(dl[0:1] * dl[1:2], axis=-1, keepdims=True)
    b = jnp.sum(dl[2:3] * dl[3:4], axis=-1, keepdims=True)
    return jnp.exp(a) - jnp.exp(b) + lam_init


def _diff_head(q, k_of, v_of, h, lam, dg, lam_init):
    res = []
    for c in range(2):
        t = c * (A_HEADS // 2) + h // 2
        s = _dot_nt(_head_q(q[:, t * PAIR:(t + 1) * PAIR], h % 2), k_of(t))
        (e,), _ = _exp2_parts([s])
        res.append((_dot(e.astype(bf16), v_of(h)), jnp.sum(e, axis=-1, keepdims=True)))
    o = res[0][0] * (1.0 / res[0][1]) - res[1][0] * (lam / res[1][1])
    o = o * lax.rsqrt(jnp.mean(o * o, axis=-1, keepdims=True) + EPS) * dg
    return o * (1.0 - lam_init)


def _ada_kernel(cv_ref, w_ref, b_ref, o_ref):
    cv = cv_ref[...]
    s = (cv * jax.nn.sigmoid(cv)).astype(bf16)
    o_ref[...] = _dot(s, w_ref[...].astype(bf16)) + b_ref[...]


def _adaln_all(cv, w_ada, b_ada):
    n6 = 6 * D_MODEL
    return pl.pallas_call(
        _ada_kernel,
        grid=(DEPTH, n6 // TN_ADA),
        in_specs=[pl.BlockSpec((N_MOD_ROWS, D_MODEL), lambda l, j: (0, 0)),
                  pl.BlockSpec((None, D_MODEL, TN_ADA), lambda l, j: (l, 0, j)),
                  pl.BlockSpec((None, 1, TN_ADA), lambda l, j: (l, 0, j))],
        out_specs=pl.BlockSpec((None, N_MOD_ROWS, TN_ADA), lambda l, j: (l, 0, j)),
        out_shape=jax.ShapeDtypeStruct((DEPTH, N_MOD_ROWS, n6), f32),
        compiler_params=_params("parallel", "parallel"),
        name="adaln",
    )(cv, w_ada, b_ada.reshape(DEPTH, 1, n6))


def _mod_spec(li, k, tm, is_ctx):
    def index(i, j):
        r = CTX_MOD_ROW if is_ctx else (i * tm) // LAT_SEQ
        return ((li * N_MOD_ROWS + r) * 6 + k, 0, 0)

    return pl.BlockSpec((None, 1, D_MODEL), index)


def _inproj_kernel(x_ref, sh_ref, sc_ref, g_ref, w_ref, *rest, kv_j0, kv_nj, emit_kv):
    o_ref, h_ref = rest[-3 if emit_kv else -2], rest[-1]
    j = pl.program_id(1)

    @pl.when(j == 0)
    def _():
        h_ref[...] = _modulated_norm(x_ref[...], g_ref[...], sc_ref[...], sh_ref[...]).astype(bf16)

    res = _dot(h_ref[...], w_ref[...])
    o_ref[...] = res.astype(bf16)

    if emit_kv:
        kv_ref = rest[-2]

        @pl.when((j >= kv_j0) & (j < kv_j0 + kv_nj))
        def _():
            kv_ref[...] = res.reshape(kv_ref.shape)


def _inproj(x, mod, g, w, li, is_ctx, kv_prev=None):
    T = x.shape[0]
    tm, tn = TM_INPROJ, TN_INPROJ
    kv_j0, kv_nj = KV_COL0 // tn, KV_COLS // tn

    in_specs = [pl.BlockSpec((tm, D_MODEL), lambda i, j: (i, 0)),
                _mod_spec(li, 0, tm, is_ctx), _mod_spec(li, 1, tm, is_ctx),
                pl.BlockSpec((1, D_MODEL), lambda i, j: (0, 0)),
                pl.BlockSpec((D_MODEL, tn), lambda i, j: (0, j))]
    args = [x, mod, mod, g, w]
    out_specs = [pl.BlockSpec((tm, tn), lambda i, j: (i, j)), pl.BlockSpec((tm, D_MODEL), lambda i, j: (i, 0))]
    out_shape = [jax.ShapeDtypeStruct((T, MIX_COLS), bf16), jax.ShapeDtypeStruct((T, D_MODEL), bf16)]
    aliases = {}
    if is_ctx:
        out_specs.insert(1, pl.BlockSpec((tm // CTX_SEQ, None, CTX_SEQ, tn),
                                         lambda i, j: (i, li, 0, jnp.clip(j - kv_j0, 0, kv_nj - 1))))
        out_shape.insert(1, jax.ShapeDtypeStruct((T // CTX_SEQ, DEPTH, CTX_SEQ, KV_COLS), f32))
        if kv_prev is not None:
            in_specs.append(pl.BlockSpec(memory_space=pl.ANY))
            args.append(kv_prev)
            aliases = {len(args) - 1: 1}
    return pl.pallas_call(
        functools.partial(_inproj_kernel, kv_j0=kv_j0, kv_nj=kv_nj, emit_kv=is_ctx),
        grid=(T // tm, MIX_COLS // tn),
        in_specs=in_specs,
        out_specs=out_specs,
        out_shape=out_shape,
        input_output_aliases=aliases,
        compiler_params=_params("parallel", "arbitrary"),
        name="inproj",
    )(*args)


def _merge_kernel(x_ref, gt_ref, h_ref, o_ref, w_ref, wb_ref, sh2_ref, sc2_ref, g2_ref, out_ref, h2_ref, acc_ref):
    n = pl.program_id(1)

    def contribution():
        return jax.nn.sigmoid(_dot(h_ref[...], w_ref[...])) * _dot(o_ref[...], wb_ref[...])

    @pl.when(n == 0)
    def _():
        acc_ref[...] = contribution()

    @pl.when((n > 0) & (n < N_BRANCH))
    def _():
        acc_ref[...] += contribution()

    @pl.when(n == N_BRANCH)
    def _():
        half = x_ref.shape[0] // 2
        for r in range(2):
            rows = slice(r * half, (r + 1) * half)
            y = x_ref[rows, :] + gt_ref[...] * _dot(acc_ref[rows, :].astype(bf16), w_ref[...])
            out_ref[rows, :] = y
            h2_ref[rows, :] = _modulated_norm(y, g2_ref[...], sc2_ref[...], sh2_ref[...]).astype(bf16)


def _merge(x, mod, h, o_stack, w5, wb, g2, li, is_ctx):
    T = x.shape[0]
    tm = TM_MERGE
    last = N_BRANCH - 1
    row = pl.BlockSpec((tm, D_MODEL), lambda i, n: (i, 0))
    return pl.pallas_call(
        _merge_kernel,
        grid=(T // tm, N_BRANCH + 1),
        in_specs=[row, _mod_spec(li, 2, tm, is_ctx), row,
                  pl.BlockSpec((None, tm, BRANCH_WIDTH), lambda i, n: (jnp.minimum(n, last), i, 0)),
                  pl.BlockSpec((D_MODEL, D_MODEL), lambda i, n: (0, n)),
                  pl.BlockSpec((None, BRANCH_WIDTH, D_MODEL), lambda i, n: (jnp.minimum(n, last), 0, 0)),
                  _mod_spec(li, 3, tm, is_ctx), _mod_spec(li, 4, tm, is_ctx),
                  pl.BlockSpec((1, D_MODEL), lambda i, n: (0, 0))],
        out_specs=[row, row],
        out_shape=[jax.ShapeDtypeStruct((T, D_MODEL), f32), jax.ShapeDtypeStruct((T, D_MODEL), bf16)],
        scratch_shapes=[pltpu.VMEM((tm, D_MODEL), f32)],
        compiler_params=_params("parallel", "arbitrary"),
        name="merge",
    )(x, mod, h, o_stack, w5, wb, mod, mod, g2)


def _mlp_kernel(x_ref, h_ref, gt_ref, wu_ref, wd_ref, gf_ref, out_ref, acc_ref, *, final_norm):
    j = pl.program_id(1)

    def chunk():
        a = jnp.maximum(_dot(h_ref[...], wu_ref[...]), 0.0)
        return _dot((a * a).astype(bf16), wd_ref[...])

    @pl.when(j == 0)
    def _():
        acc_ref[...] = chunk()

    @pl.when(j > 0)
    def _():
        acc_ref[...] += chunk()

    @pl.when(j == pl.num_programs(1) - 1)
    def _():
        y = x_ref[...] + gt_ref[...] * acc_ref[...]
        if final_norm:
            y = y * lax.rsqrt(jnp.mean(y * y, axis=-1, keepdims=True) + EPS) * gf_ref[...]
        out_ref[...] = y


def _mlp(x, h2, mod, w_up, w_down, g_final, li, is_ctx, final_norm):
    T = x.shape[0]
    tm, tf = TM_MLP, TF_MLP
    row = pl.BlockSpec((tm, D_MODEL), lambda i, j: (i, 0))
    return pl.pallas_call(
        functools.partial(_mlp_kernel, final_norm=final_norm),
        grid=(T // tm, D_FF // tf),
        in_specs=[row, row, _mod_spec(li, 5, tm, is_ctx),
                  pl.BlockSpec((D_MODEL, tf), lambda i, j: (0, j)),
                  pl.BlockSpec((tf, D_MODEL), lambda i, j: (j, 0)),
                  pl.BlockSpec((1, D_MODEL), lambda i, j: (0, 0))],
        out_specs=row,
        out_shape=jax.ShapeDtypeStruct((T, D_MODEL), f32),
        scratch_shapes=[pltpu.VMEM((tm, D_MODEL), f32)],
        compiler_params=_params("parallel", "arbitrary"),
        name="mlp",
    )(x, h2, mod, w_up, w_down, g_final)


def _ctx_attn_kernel(aq_ref, ak_ref, av_ref, bq_ref, bk_ref, bv_ref, cq_ref, ckv_ref,
                     dl_ref, dg_ref, sink_ref, o_ref, *, lam_init):
    lam = _diff_lambda(dl_ref, lam_init)
    q = aq_ref[...].astype(f32) * QK_SCALE2
    outs = [_diff_head(q, lambda t: ak_ref[:, t * PAIR:(t + 1) * PAIR], lambda h: av_ref[:, h * A_V:(h + 1) * A_V],
                       h, lam, dg_ref[...], lam_init) for h in range(A_HEADS)]
    o_ref[0] = jnp.concatenate(outs, axis=1).astype(bf16)
    q = bq_ref[...].astype(f32) * QK_SCALE2
    tiles = []
    for t in range(B_HEADS // 2):
        cols = slice(t * PAIR, (t + 1) * PAIR)
        accs = []
        for half in range(2):
            (e,), _ = _exp2_parts([_dot_nt(_head_q(q[:, cols], half), bk_ref[:, cols])])
            accs.append(_dot(e.astype(bf16), _values_with_ones(bv_ref[:, cols], half)))
        tiles.append(_pair_out(*accs))
    o_ref[1] = jnp.concatenate(tiles, axis=1).astype(bf16)
    q = cq_ref[...].astype(f32) * QK_SCALE2
    k_tile = ckv_ref[:, 0:PAIR]
    v_variants = _swap_halves_variants(ckv_ref[:, PAIR:2 * PAIR].astype(f32))
    tiles = []
    for t in range(C_HEADS // 2):
        q_tile = q[:, t * PAIR:(t + 1) * PAIR]
        q_swapped = pltpu.roll(q_tile, HEAD_DIM, 1)
        accs, extras = [], []
        for half in range(2):
            h = 2 * t + half
            g = h // C_GROUP
            sink2 = sink_ref[h] * LOG2E
            (e,), m = _exp2_parts([_dot_nt(_gqa_head_q(q_tile, q_swapped, half, g), k_tile)], sink2)
            accs.append(_dot(e.astype(bf16), v_variants[g * 2 + half]))
            extras.append(jnp.exp2(sink2 - m))
        tiles.append(_pair_out(accs[0], accs[1], extras[0], extras[1]))
    o_ref[2] = jnp.concatenate(tiles, axis=1).astype(bf16)


def _ctx_attn(p, dl, dg, sink, li):
    nb = p.shape[0] // CTX_SEQ
    S = CTX_SEQ
    lam_init = 0.8 - 0.6 * math.exp(-0.3 * li)

    def col(c):
        return pl.BlockSpec((S, 512), lambda b, c=c: (b, c))

    return pl.pallas_call(
        functools.partial(_ctx_attn_kernel, lam_init=lam_init),
        grid=(nb,),
        in_specs=[col(COL_AQ), col(COL_AK), col(COL_AV), col(COL_BQ), col(COL_BK), col(COL_BV), col(COL_CQ),
                  pl.BlockSpec((S, 256), lambda b: (b, COL_CKV_256)),
                  pl.BlockSpec((4, A_QK), lambda b: (0, 0)),
                  pl.BlockSpec((1, A_V), lambda b: (0, 0)),
                  pl.BlockSpec(memory_space=pltpu.SMEM)],
        out_specs=pl.BlockSpec((N_BRANCH - 1, S, BRANCH_WIDTH), lambda b: (0, b, 0)),
        out_shape=jax.ShapeDtypeStruct((N_BRANCH, p.shape[0], BRANCH_WIDTH), bf16),
        compiler_params=_params("parallel"),
        name="ctx_attn",
    )(p, p, p, p, p, p, p, p, dl, dg, sink)


def _lat_a_kernel(q_ref, k_ref, v_ref, ck_ref, cv_ref, cosq_ref, sinq_ref, cosk_ref, sink_ref,
                  dl_ref, dg_ref, o_ref, kk_ref, vv_ref, *, lam_init):
    L = LAT_SEQ

    @pl.when(pl.program_id(1) == 0)
    def _():
        kk_ref[0:L, :] = _rope(k_ref[...], cosk_ref[...], sink_ref[...]).astype(bf16)
        kk_ref[L:, :] = ck_ref[...].astype(bf16)
        vv_ref[0:L, :] = v_ref[...].astype(bf16)
        vv_ref[L:, :] = cv_ref[...].astype(bf16)

    lam = _diff_lambda(dl_ref, lam_init)
    q = _rope(q_ref[...], cosq_ref[...], sinq_ref[...]) * QK_SCALE2
    outs = [_diff_head(q, lambda t: kk_ref[:, t * PAIR:(t + 1) * PAIR], lambda h: vv_ref[:, h * A_V:(h + 1) * A_V],
                       h, lam, dg_ref[...], lam_init) for h in range(A_HEADS)]
    o_ref[...] = jnp.concatenate(outs, axis=1).astype(bf16)


def _lat_b_kernel(q_ref, k_ref, v_ref, ck_ref, cv_ref, bias_ref, _, o_ref, kk_ref, va_ref):
    L, W = LAT_SEQ, NA_KEY_ROWS * GRID_W
    qb = pl.program_id(1)

    @pl.when(qb == 0)
    def _():
        kk_ref[0:L, :] = k_ref[...]
        kk_ref[L:, :] = ck_ref[...].astype(bf16)
        for t in range(B_HEADS // 2):
            cols = slice(t * PAIR, (t + 1) * PAIR)
            for half in range(2):
                va_ref[2 * t + half, 0:L, :] = _values_with_ones(v_ref[:, cols], half)
                va_ref[2 * t + half, L:, :] = _values_with_ones(cv_ref[:, cols], half)

    start = pl.multiple_of(_na_key_row0(qb) * GRID_W, 256)
    q = q_ref[...].astype(f32) * QK_SCALE2
    tiles = []
    for t in range(B_HEADS // 2):
        cols = slice(t * PAIR, (t + 1) * PAIR)
        accs = []
        for half in range(2):
            h = 2 * t + half
            qh = _head_q(q[:, cols], half)
            s_loc = _dot_nt(qh, kk_ref[pl.ds(start, W), cols]) + bias_ref[h].astype(f32)
            s_ctx = _dot_nt(qh, kk_ref[L:, cols])
            (e_loc, e_ctx), _m = _exp2_parts([s_loc, s_ctx])
            accs.append(_dot(e_loc.astype(bf16), va_ref[h, pl.ds(start, W), :])
                        + _dot(e_ctx.astype(bf16), va_ref[h, L:, :]))
        tiles.append(_pair_out(*accs))
    o_ref[...] = jnp.concatenate(tiles, axis=1).astype(bf16)


def _lat_c_kernel(q_ref, kv_ref, ck_ref, cv_ref, cosq_ref, sinq_ref, cosk_ref, sink_ref, snk_ref,
                  _, o_ref, kk_ref, va_ref):
    L, W = LAT_SEQ, SWA_KEYS
    tq = q_ref.shape[0]
    qb = pl.program_id(1)

    @pl.when(qb == 0)
    def _():
        kk_ref[0:L, :] = _rope(kv_ref[:, 0:PAIR], cosk_ref[...], sink_ref[...]).astype(bf16)
        kk_ref[L:, :] = ck_ref[...].astype(bf16)
        lat = _swap_halves_variants(kv_ref[:, PAIR:2 * PAIR].astype(f32))
        ctx = _swap_halves_variants(cv_ref[...])
        for idx in range(4):
            va_ref[idx, 0:L, :] = lat[idx]
            va_ref[idx, L:, :] = ctx[idx]

    start = pl.multiple_of(jnp.clip(qb * tq - SWA_WINDOW, 0, L - W), 128)
    q = _rope(q_ref[...], cosq_ref[...], sinq_ref[...]) * QK_SCALE2
    qpos = qb * tq + lax.broadcasted_iota(jnp.int32, (tq, W), 0)
    kpos = start + lax.broadcasted_iota(jnp.int32, (tq, W), 1)
    valid = jnp.abs(qpos - kpos) <= SWA_WINDOW
    tiles = []
    for t in range(C_HEADS // 2):
        q_tile = q[:, t * PAIR:(t + 1) * PAIR]
        q_swapped = pltpu.roll(q_tile, HEAD_DIM, 1)
        accs, extras = [], []
        for half in range(2):
            h = 2 * t + half
            g = h // C_GROUP
            qh = _gqa_head_q(q_tile, q_swapped, half, g)
            s_loc = jnp.where(valid, _dot_nt(qh, kk_ref[pl.ds(start, W), :]), NEG_INF)
            s_ctx = _dot_nt(qh, kk_ref[L:, :])
            sink2 = snk_ref[h] * LOG2E
            (e_loc, e_ctx), m = _exp2_parts([s_loc, s_ctx], sink2)
            idx = g * 2 + half
            accs.append(_dot(e_loc.astype(bf16), va_ref[idx, pl.ds(start, W), :])
                        + _dot(e_ctx.astype(bf16), va_ref[idx, L:, :]))
            extras.append(jnp.exp2(sink2 - m))
        tiles.append(_pair_out(accs[0], accs[1], extras[0], extras[1]))
    o_ref[...] = jnp.concatenate(tiles, axis=1).astype(bf16)


def _lat_attn(p, caches, na_dense, dl, dg, sink, cos_t, sin_t, li):
    cak, cav, cbk, cbv, cck, ccv = caches
    L, tq = LAT_SEQ, TQ_ATTN
    assert tq == NA_Q_ROWS * GRID_W and SWA_KEYS >= tq + 2 * SWA_WINDOW
    nq = L // tq
    nb_lat = p.shape[0] // L
    lam_init = 0.8 - 0.6 * math.exp(-0.3 * li)

    def qcol(c):
        return pl.BlockSpec((tq, 512), lambda b, i, c=c: (b * nq + i, c))

    def kcol(c, width=512):
        return pl.BlockSpec((L, width), lambda b, i, c=c: (b, c))

    def cache(width):
        return pl.BlockSpec((None, None, CTX_SEQ, width), lambda b, i: (b, li, 0, 0))

    tab_q = pl.BlockSpec((tq, 512), lambda b, i: (i, 0))
    tab_k = pl.BlockSpec((L, 512), lambda b, i: (0, 0))
    tab_k128 = pl.BlockSpec((L, 128), lambda b, i: (0, 0))
    o_sds = jax.ShapeDtypeStruct((N_BRANCH, p.shape[0], BRANCH_WIDTH), bf16)
    anyspec = pl.BlockSpec(memory_space=pl.ANY)

    def o_spec(branch):
        return pl.BlockSpec((None, tq, BRANCH_WIDTH), lambda b, i: (branch, b * nq + i, 0))

    kv_scratch = [pltpu.VMEM((L + CTX_SEQ, 512), bf16), pltpu.VMEM((L + CTX_SEQ, 512), bf16)]
    smem = pl.BlockSpec(memory_space=pltpu.SMEM)
    cp = _params("parallel", "arbitrary")

    o_stack = pl.pallas_call(
        functools.partial(_lat_a_kernel, lam_init=lam_init),
        grid=(nb_lat, nq),
        in_specs=[qcol(COL_AQ), kcol(COL_AK), kcol(COL_AV), cache(512), cache(512),
                  tab_q, tab_q, tab_k, tab_k,
                  pl.BlockSpec((4, A_QK), lambda b, i: (0, 0)), pl.BlockSpec((1, A_V), lambda b, i: (0, 0))],
        out_specs=o_spec(0), out_shape=o_sds, scratch_shapes=kv_scratch,
        compiler_params=cp, name="lat_attn_a",
    )(p, p, p, cak, cav, cos_t, sin_t, cos_t, sin_t, dl, dg)

    o_stack = pl.pallas_call(
        _lat_b_kernel,
        grid=(nb_lat, nq),
        in_specs=[qcol(COL_BQ), kcol(COL_BK), kcol(COL_BV), cache(512), cache(512),
                  pl.BlockSpec((B_HEADS, tq, NA_KEY_ROWS * GRID_W), lambda b, i: (0, i, 0)), anyspec],
        out_specs=o_spec(1), out_shape=o_sds,
        scratch_shapes=[pltpu.VMEM((L + CTX_SEQ, 512), bf16), pltpu.VMEM((B_HEADS, L + CTX_SEQ, PAIR), bf16)],
        input_output_aliases={6: 0}, compiler_params=cp, name="lat_attn_b",
    )(p, p, p, cbk, cbv, na_dense, o_stack)

    return pl.pallas_call(
        _lat_c_kernel,
        grid=(nb_lat, nq),
        in_specs=[qcol(COL_CQ), kcol(COL_CKV_256, 256), cache(PAIR), cache(PAIR),
                  tab_q, tab_q, tab_k128, tab_k128, smem, anyspec],
        out_specs=o_spec(2), out_shape=o_sds,
        scratch_shapes=[pltpu.VMEM((L + CTX_SEQ, PAIR), bf16), pltpu.VMEM((4, L + CTX_SEQ, PAIR), bf16)],
        input_output_aliases={9: 0}, compiler_params=cp, name="lat_attn_c",
    )(p, p, cck, ccv, cos_t, sin_t, cos_t, sin_t, sink, o_stack)


def _na_key_row0(qb):
    lo, hi = NA_Q_ROWS * qb - NA_ROWS // 2, GRID_ROWS - NA_KEY_ROWS
    return min(max(lo, 0), hi) if isinstance(qb, int) else jnp.clip(lo, 0, hi)


def _na_bias_kernel(rb_ref, o_ref):
    h = pl.program_id(0)
    W = GRID_W
    n_dc = 2 * NA_COLS - 1
    n_dr = 2 * NA_ROWS - 1
    qc = lax.broadcasted_iota(jnp.int32, (W, 2 * W), 0)
    lane = lax.broadcasted_iota(jnp.int32, (W, 2 * W), 1)
    second = lane >= W
    kc = jnp.where(second, lane - W, lane)
    dc = jnp.clip(kc - qc, -(NA_COLS - 1), NA_COLS - 1) + NA_COLS - 1
    c0 = jnp.clip(qc - NA_COLS // 2, 0, W - NA_COLS)
    col_ok = (kc >= c0) & (kc < c0 + NA_COLS)
    base = h * (n_dr * n_dc)

    def pair_tile(dr0):
        t = jnp.zeros((W, 2 * W), f32)
        for j in range(n_dc):
            lo = rb_ref[base + dr0 * n_dc + j] if 0 <= dr0 < n_dr else 0.0
            hi = rb_ref[base + (dr0 + 1) * n_dc + j] if 0 <= dr0 + 1 < n_dr else 0.0
            t = jnp.where(dc == j, jnp.where(second, hi, lo), t)
        return jnp.where(col_ok, t * LOG2E, NEG_INF)

    tiles = {dr0: pair_tile(dr0) for dr0 in range(-1, n_dr)}
    neg = jnp.full((W, 2 * W), NEG_INF, f32)
    for qr in range(GRID_ROWS):
        r0 = min(max(qr - NA_ROWS // 2, 0), GRID_ROWS - NA_ROWS)
        k0 = _na_key_row0(qr // NA_Q_ROWS)
        assert k0 % 2 == 0 and k0 <= r0 and r0 + NA_ROWS <= k0 + NA_KEY_ROWS
        for pr in range(NA_KEY_ROWS // 2):
            kr = k0 + 2 * pr
            ok0 = r0 <= kr < r0 + NA_ROWS
            ok1 = r0 <= kr + 1 < r0 + NA_ROWS
            if not (ok0 or ok1):
                t = neg
            else:
                t = tiles[kr - qr + NA_ROWS - 1]
                if not ok0:
                    t = jnp.where(second, t, NEG_INF)
                if not ok1:
                    t = jnp.where(second, NEG_INF, t)
            o_ref[qr * W:(qr + 1) * W, 2 * pr * W:(2 * pr + 2) * W] = t.astype(bf16)


def _na_bias_dense(rel_bias):
    kw = NA_KEY_ROWS * GRID_W
    return pl.pallas_call(
        _na_bias_kernel,
        grid=(B_HEADS,),
        in_specs=[pl.BlockSpec(memory_space=pltpu.SMEM)],
        out_specs=pl.BlockSpec((None, LAT_SEQ, kw), lambda h: (h, 0, 0)),
        out_shape=jax.ShapeDtypeStruct((B_HEADS, LAT_SEQ, kw), bf16),
        compiler_params=_params("parallel"),
        name="na_bias",
    )(rel_bias.reshape(-1))


def _hy_filter_kernel(z_ref, w1_ref, b1_ref, w2_ref, b2_ref, fr_ref, w3f_ref, w3b_ref, dcf_ref, dcb_ref,
                      bias_ref, fc_ref, fs_ref, o_ref, *, L):
    z = z_ref[...]
    fr = fr_ref[...]
    h = jnp.sin(fr * (_dot_hi(z, w1_ref[...]) + b1_ref[...]))
    h = jnp.sin(fr * (_dot_hi(h, w2_ref[...]) + b2_ref[...]))
    t = z[:, 0:1]
    hf = _dot_hi(h, w3f_ref[...]) * jnp.exp(-t * jnp.abs(dcf_ref[...]))
    hb = _dot_hi(h, w3b_ref[...]) * jnp.exp(-t * jnp.abs(dcb_ref[...]))
    ssum = hf + hb
    bias = bias_ref[...]
    ga = _dot_3pass(fc_ref[...], ssum) + bias
    gb = _dot_3pass(fs_ref[...], hb - hf)
    g_nyq = _dot_hi(fs_ref[0:8, :], ssum)[0:1] + bias
    row0 = lax.broadcasted_iota(jnp.int32, ga.shape, 0) == 0
    inv = 1.0 / L
    o_ref[0] = jnp.where(row0, 0.5 * ga, ga) * inv
    o_ref[1] = jnp.where(row0, 0.0, gb) * inv
    o_ref[2] = jnp.where(row0, 0.5 * g_nyq, ga) * inv


def _hy_filter_tables(L, w1p, b1, w2, b2, w3, freq, decay, hy_bias):
    cb = HY_CB
    ncb = HY_WIDTH // cb
    z = jnp.asarray(_hyena_features(L))
    fc, fs = (jnp.asarray(a) for a in _dft_tables(L))
    full = lambda shape: pl.BlockSpec(shape, lambda o, c: (0,) * len(shape))
    fwd = lambda rows: pl.BlockSpec((rows, cb), lambda o, c: (0, o * 2 * ncb + c))
    bwd = lambda rows: pl.BlockSpec((rows, cb), lambda o, c: (0, o * 2 * ncb + ncb + c))
    return pl.pallas_call(
        functools.partial(_hy_filter_kernel, L=L),
        grid=(HY_ORDER, ncb),
        in_specs=[full((L, HY_EMB_PAD)), full((HY_EMB_PAD, HY_FFN)), full((1, HY_FFN)),
                  full((HY_FFN, HY_FFN)), full((1, HY_FFN)), full((1, HY_FFN)),
                  fwd(HY_FFN), bwd(HY_FFN), fwd(1), bwd(1),
                  pl.BlockSpec((None, 1, cb), lambda o, c: (o, 0, c)),
                  full((L, L)), full((L, L))],
        out_specs=pl.BlockSpec((None, 3, L, cb), lambda o, c: (o, 0, 0, c)),
        out_shape=jax.ShapeDtypeStruct((HY_ORDER, 3, L, HY_WIDTH), f32),
        compiler_params=_params("parallel", "parallel"),
        name="hy_filter",
    )(z, w1p, b1, w2, b2, freq, w3, w3, decay, decay, hy_bias.reshape(HY_ORDER, 1, HY_WIDTH), fc, fs)


def _hyena_kernel(v_ref, x1_ref, x2_ref, wv_ref, w1_ref, w2_ref, fwd_ref, inv_ref, tab_ref, _, o_ref):
    L = inv_ref.shape[0]
    row = lax.broadcasted_iota(jnp.int32, (L, v_ref.shape[1]), 0)

    def short_conv(u_ref, w_ref, s):
        u = u_ref[s * L:(s + 1) * L, :].astype(f32)
        w = w_ref[...]
        prev = jnp.where(row == 0, 0.0, pltpu.roll(u, 1, 0))
        nxt = jnp.where(row == L - 1, 0.0, pltpu.roll(u, L - 1, 0))
        return prev * w[0:1] + u * w[1:2] + nxt * w[2:3]

    def long_conv(u, order):
        ab = _dot(fwd_ref[...], u.astype(bf16))
        a, b = ab[:L], ab[L:]
        t0, t1, t2 = tab_ref[order, 0], tab_ref[order, 1], tab_ref[order, 2]
        pr = (a * t0 + b * t1).astype(bf16)
        qi = (b * t2 - a * t1).astype(bf16)
        return _dot(inv_ref[...], jnp.concatenate([pr, qi], axis=0))

    outs = []
    for s in range(v_ref.shape[0] // L):
        z = short_conv(x1_ref, w1_ref, s) * long_conv(short_conv(v_ref, wv_ref, s), 0)
        outs.append(short_conv(x2_ref, w2_ref, s) * long_conv(z, 1))
    o_ref[...] = jnp.concatenate(outs, axis=0).astype(bf16)


def _hyena(p, o_stack, hy_short, tabs, L):
    cb = HY_CB
    ncb = HY_WIDTH // cb
    c0 = HY_COL0 // cb
    rows = min(HY_ROWS_PER_STEP, p.shape[0])
    nb = p.shape[0] // rows
    fc, fs = _dft_tables(L)
    fwd = jnp.asarray(np.concatenate([fc, fs], axis=0), bf16)
    inv = jnp.asarray(np.concatenate([fc, fs.T], axis=1), bf16)

    def part(k):
        return pl.BlockSpec((rows, cb), lambda c, b, k=k: (b, c0 + k * ncb + c))

    def wpart(k):
        return pl.BlockSpec((3, cb), lambda c, b, k=k: (0, k * ncb + c))

    return pl.pallas_call(
        _hyena_kernel,
        grid=(ncb, nb),
        in_specs=[part(0), part(1), part(2), wpart(0), wpart(1), wpart(2),
                  pl.BlockSpec((2 * L, L), lambda c, b: (0, 0)), pl.BlockSpec((L, 2 * L), lambda c, b: (0, 0)),
                  pl.BlockSpec((HY_ORDER, 3, L, cb), lambda c, b: (0, 0, 0, c)),
                  pl.BlockSpec(memory_space=pl.ANY)],
        out_specs=pl.BlockSpec((None, rows, cb), lambda c, b: (N_BRANCH - 1, b, c)),
        out_shape=jax.ShapeDtypeStruct(o_stack.shape, o_stack.dtype),
        input_output_aliases={9: 0},
        compiler_params=_params("parallel", "arbitrary"),
        name="hyena_%d" % L,
    )(p, p, p, hy_short, hy_short, hy_short, fwd, inv, tabs, o_stack)


def kernel(x_prompt, x_sample, cache_a_k, cache_a_v, cache_b_k, cache_b_v, cache_c_k, cache_c_v, c, c_ctx, w_ada, b_ada, g_mix, w_in, diff_lambda, diff_norm_g, na_bias, swa_sink, hy_short, hy_w1, hy_b1, hy_w2, hy_b2, hy_w3, hy_freq, hy_decay, hy_bias, w_branch, w_out, g_mlp, w_up, w_down, g_final):
    nb_ctx, nb_lat = x_prompt.shape[0], x_sample.shape[0]
    assert x_prompt.shape[1:] == (CTX_SEQ, D_MODEL) and x_sample.shape[1:] == (LAT_SEQ, D_MODEL)
    assert nb_lat <= CTX_MOD_ROW and (nb_ctx * CTX_SEQ) % TM_INPROJ == 0

    x_ctx = x_prompt.reshape(nb_ctx * CTX_SEQ, D_MODEL)
    x_lat = x_sample.reshape(nb_lat * LAT_SEQ, D_MODEL)
    cv = jnp.zeros((N_MOD_ROWS, D_MODEL), f32).at[:nb_lat].set(c).at[CTX_MOD_ROW].set(c_ctx)
    mod = _adaln_all(cv, w_ada, b_ada).reshape(DEPTH * N_MOD_ROWS * 6, 1, D_MODEL)
    cos_t, sin_t = (jnp.asarray(a) for a in _rope_tables())
    caches = (cache_a_k.reshape(nb_lat, DEPTH, CTX_SEQ, 512), cache_a_v.reshape(nb_lat, DEPTH, CTX_SEQ, 512),
              cache_b_k.reshape(nb_lat, DEPTH, CTX_SEQ, 512), cache_b_v.reshape(nb_lat, DEPTH, CTX_SEQ, 512),
              cache_c_k.reshape(nb_lat, DEPTH, CTX_SEQ, 128), cache_c_v.reshape(nb_lat, DEPTH, CTX_SEQ, 128))

    kv = None
    for li in range(DEPTH):
        wl = w_in[li]
        w_mix = jnp.concatenate([wl[:, 0:512], wl[:, 1536:2048], wl[:, 3072:3584], wl[:, 512:1536],
                                 wl[:, 2048:3072], wl[:, 3584:MIX_COLS]], axis=1).astype(bf16)
        w5 = jnp.concatenate([wl[:, MIX_COLS:], w_out[li]], axis=1).astype(bf16)
        wb = w_branch[li].astype(bf16)
        wu = w_up[li].astype(bf16)
        wd = w_down[li].astype(bf16)
        g1 = g_mix[li].reshape(1, D_MODEL)
        g2 = g_mlp[li].reshape(1, D_MODEL)
        dl = diff_lambda[li]
        dg = diff_norm_g[li].reshape(1, A_V)
        sink = swa_sink[li]

        w1p = jnp.pad(hy_w1[li], ((0, HY_EMB_PAD - HY_EMB), (0, 0)))
        hy_args = (w1p, hy_b1[li].reshape(1, HY_FFN), hy_w2[li], hy_b2[li].reshape(1, HY_FFN), hy_w3[li],
                   hy_freq[li].reshape(1, HY_FFN), hy_decay[li].reshape(1, -1), hy_bias[li])
        gf = g_final.reshape(1, D_MODEL)
        final = li == DEPTH - 1

        p, kv, h = _inproj(x_ctx, mod, g1, w_mix, li, True, kv)
        o_stack = _ctx_attn(p, dl, dg, sink, li)
        o_stack = _hyena(p, o_stack, hy_short[li], _hy_filter_tables(CTX_SEQ, *hy_args), CTX_SEQ)
        x_ctx, h2 = _merge(x_ctx, mod, h, o_stack, w5, wb, g2, li, True)
        x_ctx = _mlp(x_ctx, h2, mod, wu, wd, gf, li, True, final_norm=final)

        p, h = _inproj(x_lat, mod, g1, w_mix, li, False)
        o_stack = _lat_attn(p, caches, _na_bias_dense(na_bias[li]), dl, dg, sink, cos_t, sin_t, li)
        o_stack = _hyena(p, o_stack, hy_short[li], _hy_filter_tables(LAT_SEQ, *hy_args), LAT_SEQ)
        x_lat, h2 = _merge(x_lat, mod, h, o_stack, w5, wb, g2, li, False)
        x_lat = _mlp(x_lat, h2, mod, wu, wd, gf, li, False, final_norm=final)

    y_prompt = x_ctx.reshape(nb_ctx, CTX_SEQ, D_MODEL)
    y_sample = x_lat.reshape(nb_lat, LAT_SEQ, D_MODEL)
    kv_shapes = ((0, 512, (2, A_HEADS, A_QK)), (512, 1024, (A_HEADS, A_V)),
                 (1024, 1536, (B_HEADS, HEAD_DIM)), (1536, 2048, (B_HEADS, HEAD_DIM)),
                 (2048, 2176, (C_KV_HEADS, HEAD_DIM)), (2176, 2304, (C_KV_HEADS, HEAD_DIM)))
    new_kv = tuple(kv[..., a:b].reshape((nb_ctx, DEPTH, CTX_SEQ) + s) for a, b, s in kv_shapes)
    return (y_prompt, y_sample) + new_kv
```

```python
import functools
import math

import numpy as np
import jax
import jax.numpy as jnp
from jax import lax
from jax.experimental import pallas as pl
from jax.experimental.pallas import tpu as pltpu

f32 = jnp.float32
bf16 = jnp.bfloat16

D_MODEL = 2048
DEPTH = 2
CTX_SEQ = 256
LAT_SEQ = 1024
GRID_W = 64
GRID_ROWS = LAT_SEQ // GRID_W
BRANCH_WIDTH = 512
N_BRANCH = 4
HEAD_DIM = 64
A_QK = 64
A_V = 128
A_HEADS = 4
B_HEADS = 8
NA_ROWS = 8
NA_COLS = 16
C_HEADS = 8
C_KV_HEADS = 2
C_GROUP = 4
SWA_WINDOW = 128
SWA_KEYS = 512
NA_Q_ROWS = 4
NA_KEY_ROWS = 12
HY_WIDTH = 512
HY_ORDER = 2
HY_BANDS = 16
HY_EMB = 1 + 2 * HY_BANDS
HY_EMB_PAD = 128
HY_FFN = 64
D_FF = 4 * D_MODEL
ROPE_BASE = 10000.0
EPS = 1e-6
NEG_INF = -1e30
MIX_COLS = 5376
GATE_COLS = N_BRANCH * D_MODEL
COL_AQ, COL_BQ, COL_CQ, COL_AK, COL_AV, COL_BK, COL_BV = 0, 1, 2, 3, 4, 5, 6
COL_CKV_256 = 14
KV_COL0 = 1536
KV_COLS = 2304
HY_COL0 = 3840
N_MOD_ROWS = 16
CTX_MOD_ROW = 8

VMEM_LIMIT = 56 * 1024 * 1024
TM_INPROJ = 1024
TN_INPROJ = 768
TM_MERGE = 512
TM_MLP = 512
TF_MLP = 1024
TQ_ATTN = 256
TN_ADA = 1024
HY_CB = 256
HY_ROWS_PER_STEP = 2048
CAST_ROWS = 512


def _params(*sem):
    return pltpu.CompilerParams(dimension_semantics=sem, vmem_limit_bytes=VMEM_LIMIT)


def _dot(a, b):
    return jnp.dot(a, b, preferred_element_type=f32)


def _dot_nt(a, b):
    return lax.dot_general(a, b, (((1,), (1,)), ((), ())), preferred_element_type=f32)


def _dot_hi(a, b):
    return jnp.dot(a, b, preferred_element_type=f32, precision=lax.Precision.HIGHEST)


def _dot_3pass(a, b):
    a_hi, b_hi = a.astype(bf16), b.astype(bf16)
    a_lo = (a - a_hi.astype(f32)).astype(bf16)
    b_lo = (b - b_hi.astype(f32)).astype(bf16)
    return _dot(a_hi, b_hi) + (_dot(a_hi, b_lo) + _dot(a_lo, b_hi))


@functools.lru_cache(maxsize=None)
def _rope_tables():
    half = HEAD_DIM // 2
    nf = half // 2
    inv = ROPE_BASE ** (-np.arange(nf, dtype=np.float64) / nf)
    t = np.arange(LAT_SEQ)
    pos = np.stack([t // GRID_W, t % GRID_W], axis=1).astype(np.float64)
    lane = np.arange(HEAD_DIM)
    ang = pos[:, lane // half] * inv[lane % nf][None, :]
    first = (lane % half) < nf
    cos = np.cos(ang)
    sin = np.where(first[None, :], -np.sin(ang), np.sin(ang))
    reps = 512 // HEAD_DIM
    return (np.tile(cos, (1, reps)).astype(np.float32), np.tile(sin, (1, reps)).astype(np.float32))


@functools.lru_cache(maxsize=None)
def _dft_tables(L):
    f = np.arange(L, dtype=np.int64)
    prod = (f[:, None] * f[None, :]) % (2 * L)
    ang = np.pi * prod.astype(np.float64) / L
    fc = np.cos(ang)
    fs = np.sin(ang)
    fs[0, :] = np.where(f % 2 == 0, 1.0, -1.0)
    return fc.astype(np.float32), fs.astype(np.float32)


@functools.lru_cache(maxsize=None)
def _hyena_features(L):
    n = np.arange(L, dtype=np.float64)[:, None]
    t = n / max(L - 1, 1)
    w = 2.0 * math.pi * n / L
    bands = np.linspace(1e-4, HY_BANDS - 1, HY_BANDS, dtype=np.float64)[None, :]
    z = np.concatenate([t, np.cos(bands * w), -np.sin(bands * w)], axis=-1)
    z = np.pad(z, ((0, 0), (0, HY_EMB_PAD - HY_EMB)))
    return z.astype(np.float32)


def _modulated_norm(x, g, scale, shift):
    ms = jnp.mean(x * x, axis=-1, keepdims=True)
    return x * lax.rsqrt(ms + EPS) * (g * (1.0 + scale)) + shift


def _rope(x, cos, sin_signed):
    x = x.astype(f32)
    n = x.shape[-1]
    lane = lax.broadcasted_iota(jnp.int32, x.shape, 1)
    first = (lane & (HEAD_DIM // 2 - 1)) < (HEAD_DIM // 4)
    partner = jnp.where(first, pltpu.roll(x, n - HEAD_DIM // 4, 1), pltpu.roll(x, HEAD_DIM // 4, 1))
    return x * cos + partner * sin_signed


LOG2E = 1.4426950408889634
QK_SCALE2 = HEAD_DIM ** -0.5 * LOG2E
PAIR = 2 * HEAD_DIM


def _lo_half(shape):
    return lax.broadcasted_iota(jnp.int32, shape, 1) < HEAD_DIM


def _keep_half(x, half, fill):
    lo = _lo_half(x.shape)
    return jnp.where(lo if half == 0 else jnp.logical_not(lo), x, fill)


def _head_q(q_tile, half):
    return _keep_half(q_tile, half, 0.0).astype(bf16)


def _values_with_ones(v_tile, half):
    return _keep_half(v_tile.astype(f32), half, 1.0).astype(bf16)


def _exp2_parts(scores, sink2=None):
    m = jnp.max(scores[0], axis=-1, keepdims=True)
    for s in scores[1:]:
        m = jnp.maximum(m, jnp.max(s, axis=-1, keepdims=True))
    if sink2 is not None:
        m = jnp.maximum(m, sink2)
    return [jnp.exp2(s - m) for s in scores], m


def _pair_out(acc_even, acc_odd, extra_even=None, extra_odd=None):
    lo = _lo_half(acc_even.shape)
    num = jnp.where(lo, acc_even, acc_odd)
    den = pltpu.roll(jnp.where(lo, acc_odd, acc_even), HEAD_DIM, 1)
    if extra_even is not None:
        den = den + jnp.where(lo, extra_even, extra_odd)
    return num / den


def _swap_halves_variants(v_tile):
    sw = pltpu.roll(v_tile, HEAD_DIM, 1)
    lo = _lo_half(v_tile.shape)
    tiles = (jnp.where(lo, v_tile, 1.0), jnp.where(lo, 1.0, sw), jnp.where(lo, sw, 1.0), jnp.where(lo, 1.0, v_tile))
    return [t.astype(bf16) for t in tiles]


def _gqa_head_q(q_tile, q_swapped, half, g):
    return _keep_half(q_tile if half == g else q_swapped, g, 0.0).astype(bf16)


def _diff_lambda(dl_ref, lam_init):
    dl = dl_ref[...]
    a = jnp.sum(dl[0:1] * dl[1:2], axis=-1, keepdims=True)
    b = jnp.sum(dl[2:3] * dl[3:4], axis=-1, keepdims=True)
    return jnp.exp(a) - jnp.exp(b) + lam_init


def _diff_head(q, k_of, v_of, h, lam, dg, lam_init):
    res = []
    for c in range(2):
        t = c * (A_HEADS // 2) + h // 2
        s = _dot_nt(_head_q(q[:, t * PAIR:(t + 1) * PAIR], h % 2), k_of(t))
        (e,), _ = _exp2_parts([s])
        res.append((_dot(e.astype(bf16), v_of(h)), jnp.sum(e, axis=-1, keepdims=True)))
    o = res[0][0] * (1.0 / res[0][1]) - res[1][0] * (lam / res[1][1])
    o = o * lax.rsqrt(jnp.mean(o * o, axis=-1, keepdims=True) + EPS) * dg
    return o * (1.0 - lam_init)


def _cast_kernel(src_ref, *rest):
    o_ref = rest[-1]
    x = src_ref[...]
    o_ref[...] = x.reshape(o_ref.shape).astype(bf16)


def _cast_call(name, grid, src, in_spec, out_spec, out_shape, dst=None):
    in_specs, args, aliases = [in_spec], [src], {}
    if dst is not None:
        in_specs.append(pl.BlockSpec(memory_space=pl.ANY))
        args.append(dst)
        aliases = {1: 0}
    return pl.pallas_call(
        _cast_kernel, grid=grid, in_specs=in_specs, out_specs=out_spec,
        out_shape=jax.ShapeDtypeStruct(out_shape, bf16), input_output_aliases=aliases,
        compiler_params=_params(*(("parallel",) * len(grid))), name=name,
    )(*args)


def _cast_layer_weights(w_in, w_out, w_up, w_down, li):
    D, R, tf = D_MODEL, CAST_ROWS, TF_MLP
    nr = D // R
    w5 = _cast_call(
        "cast_gate", (N_BRANCH, nr), w_in,
        pl.BlockSpec((pl.Element(1), pl.Element(R), pl.Element(D)),
                     lambda n, r: (li, pl.multiple_of(r * R, R), pl.multiple_of(MIX_COLS + n * D, 128))),
        pl.BlockSpec((None, R, D), lambda n, r: (n, r, 0)), (N_BRANCH + 1, D, D))
    w5 = _cast_call(
        "cast_wout", (nr,), w_out,
        pl.BlockSpec((None, R, D), lambda r: (li, r, 0)),
        pl.BlockSpec((None, R, D), lambda r: (N_BRANCH, r, 0)), (N_BRANCH + 1, D, D), dst=w5)
    wu = _cast_call(
        "cast_wu", (D_FF // tf, nr), w_up,
        pl.BlockSpec((None, R, tf), lambda j, r: (li, r, j)),
        pl.BlockSpec((None, R, tf), lambda j, r: (j, r, 0)), (D_FF // tf, D, tf))
    wd = _cast_call(
        "cast_wd", (D_FF // R,), w_down,
        pl.BlockSpec((None, R, D), lambda r: (li, r, 0)),
        pl.BlockSpec((R, D), lambda r: (r, 0)), (D_FF, D))
    return w5, wu, wd


def _ada_kernel(cv_ref, w_ref, b_ref, o_ref):
    cv = cv_ref[...]
    s = (cv * jax.nn.sigmoid(cv)).astype(bf16)
    o_ref[...] = _dot(s, w_ref[...].astype(bf16)) + b_ref[...]


def _adaln_all(cv, w_ada, b_ada):
    n6 = 6 * D_MODEL
    return pl.pallas_call(
        _ada_kernel,
        grid=(DEPTH, n6 // TN_ADA),
        in_specs=[pl.BlockSpec((N_MOD_ROWS, D_MODEL), lambda l, j: (0, 0)),
                  pl.BlockSpec((None, D_MODEL, TN_ADA), lambda l, j: (l, 0, j)),
                  pl.BlockSpec((None, 1, TN_ADA), lambda l, j: (l, 0, j))],
        out_specs=pl.BlockSpec((None, N_MOD_ROWS, TN_ADA), lambda l, j: (l, 0, j)),
        out_shape=jax.ShapeDtypeStruct((DEPTH, N_MOD_ROWS, n6), f32),
        compiler_params=_params("parallel", "parallel"),
        name="adaln",
    )(cv, w_ada, b_ada.reshape(DEPTH, 1, n6))


def _mod_spec(li, k, tm, is_ctx):
    def index(i, j):
        r = CTX_MOD_ROW if is_ctx else (i * tm) // LAT_SEQ
        return ((li * N_MOD_ROWS + r) * 6 + k, 0, 0)

    return pl.BlockSpec((None, 1, D_MODEL), index)


def _inproj_kernel(x_ref, sh_ref, sc_ref, g_ref, w_ref, *rest, kv_j0, kv_nj, emit_kv):
    o_ref, h_ref = rest[-3 if emit_kv else -2], rest[-1]
    j = pl.program_id(1)

    @pl.when(j == 0)
    def _():
        h_ref[...] = _modulated_norm(x_ref[...], g_ref[...], sc_ref[...], sh_ref[...]).astype(bf16)

    res = _dot(h_ref[...], w_ref[...])
    o_ref[...] = res.astype(bf16)

    if emit_kv:
        kv_ref = rest[-2]

        @pl.when((j >= kv_j0) & (j < kv_j0 + kv_nj))
        def _():
            kv_ref[...] = res.reshape(kv_ref.shape)


def _inproj(x, mod, g, w, li, is_ctx, kv_prev=None):
    T = x.shape[0]
    tm, tn = TM_INPROJ, TN_INPROJ
    kv_j0, kv_nj = KV_COL0 // tn, KV_COLS // tn

    in_specs = [pl.BlockSpec((tm, D_MODEL), lambda i, j: (i, 0)),
                _mod_spec(li, 0, tm, is_ctx), _mod_spec(li, 1, tm, is_ctx),
                pl.BlockSpec((1, D_MODEL), lambda i, j: (0, 0)),
                pl.BlockSpec((D_MODEL, tn), lambda i, j: (0, j))]
    args = [x, mod, mod, g, w]
    out_specs = [pl.BlockSpec((tm, tn), lambda i, j: (i, j)), pl.BlockSpec((tm, D_MODEL), lambda i, j: (i, 0))]
    out_shape = [jax.ShapeDtypeStruct((T, MIX_COLS), bf16), jax.ShapeDtypeStruct((T, D_MODEL), bf16)]
    aliases = {}
    if is_ctx:
        out_specs.insert(1, pl.BlockSpec((tm // CTX_SEQ, None, CTX_SEQ, tn),
                                         lambda i, j: (i, li, 0, jnp.clip(j - kv_j0, 0, kv_nj - 1))))
        out_shape.insert(1, jax.ShapeDtypeStruct((T // CTX_SEQ, DEPTH, CTX_SEQ, KV_COLS), f32))
        if kv_prev is not None:
            in_specs.append(pl.BlockSpec(memory_space=pl.ANY))
            args.append(kv_prev)
            aliases = {len(args) - 1: 1}
    return pl.pallas_call(
        functools.partial(_inproj_kernel, kv_j0=kv_j0, kv_nj=kv_nj, emit_kv=is_ctx),
        grid=(T // tm, MIX_COLS // tn),
        in_specs=in_specs,
        out_specs=out_specs,
        out_shape=out_shape,
        input_output_aliases=aliases,
        compiler_params=_params("parallel", "arbitrary"),
        name="inproj",
    )(*args)


def _merge_kernel(x_ref, gt_ref, h_ref, o_ref, w_ref, wb_ref, sh2_ref, sc2_ref, g2_ref, out_ref, h2_ref, acc_ref):
    n = pl.program_id(1)

    def contribution():
        return jax.nn.sigmoid(_dot(h_ref[...], w_ref[...])) * _dot(o_ref[...], wb_ref[...])

    @pl.when(n == 0)
    def _():
        acc_ref[...] = contribution()

    @pl.when((n > 0) & (n < N_BRANCH))
    def _():
        acc_ref[...] += contribution()

    @pl.when(n == N_BRANCH)
    def _():
        half = x_ref.shape[0] // 2
        for r in range(2):
            rows = slice(r * half, (r + 1) * half)
            y = x_ref[rows, :] + gt_ref[...] * _dot(acc_ref[rows, :].astype(bf16), w_ref[...])
            out_ref[rows, :] = y
            h2_ref[rows, :] = _modulated_norm(y, g2_ref[...], sc2_ref[...], sh2_ref[...]).astype(bf16)


def _merge(x, mod, h, o_stack, w5, wb, g2, li, is_ctx):
    T = x.shape[0]
    tm = TM_MERGE
    last = N_BRANCH - 1
    row = pl.BlockSpec((tm, D_MODEL), lambda i, n: (i, 0))
    return pl.pallas_call(
        _merge_kernel,
        grid=(T // tm, N_BRANCH + 1),
        in_specs=[row, _mod_spec(li, 2, tm, is_ctx), row,
                  pl.BlockSpec((None, tm, BRANCH_WIDTH), lambda i, n: (jnp.minimum(n, last), i, 0)),
                  pl.BlockSpec((None, D_MODEL, D_MODEL), lambda i, n: (n, 0, 0)),
                  pl.BlockSpec((None, BRANCH_WIDTH, D_MODEL), lambda i, n: (jnp.minimum(n, last), 0, 0)),
                  _mod_spec(li, 3, tm, is_ctx), _mod_spec(li, 4, tm, is_ctx),
                  pl.BlockSpec((1, D_MODEL), lambda i, n: (0, 0))],
        out_specs=[row, row],
        out_shape=[jax.ShapeDtypeStruct((T, D_MODEL), f32), jax.ShapeDtypeStruct((T, D_MODEL), bf16)],
        scratch_shapes=[pltpu.VMEM((tm, D_MODEL), f32)],
        compiler_params=_params("parallel", "arbitrary"),
        name="merge",
    )(x, mod, h, o_stack, w5, wb, mod, mod, g2)


def _mlp_kernel(x_ref, h_ref, gt_ref, wu_ref, wd_ref, gf_ref, out_ref, acc_ref, *, final_norm):
    j = pl.program_id(1)

    def chunk():
        a = jnp.maximum(_dot(h_ref[...], wu_ref[...]), 0.0)
        return _dot((a * a).astype(bf16), wd_ref[...])

    @pl.when(j == 0)
    def _():
        acc_ref[...] = chunk()

    @pl.when(j > 0)
    def _():
        acc_ref[...] += chunk()

    @pl.when(j == pl.num_programs(1) - 1)
    def _():
        y = x_ref[...] + gt_ref[...] * acc_ref[...]
        if final_norm:
            y = y * lax.rsqrt(jnp.mean(y * y, axis=-1, keepdims=True) + EPS) * gf_ref[...]
        out_ref[...] = y


def _mlp(x, h2, mod, w_up, w_down, g_final, li, is_ctx, final_norm):
    T = x.shape[0]
    tm, tf = TM_MLP, TF_MLP
    row = pl.BlockSpec((tm, D_MODEL), lambda i, j: (i, 0))
    return pl.pallas_call(
        functools.partial(_mlp_kernel, final_norm=final_norm),
        grid=(T // tm, D_FF // tf),
        in_specs=[row, row, _mod_spec(li, 5, tm, is_ctx),
                  pl.BlockSpec((None, D_MODEL, tf), lambda i, j: (j, 0, 0)),
                  pl.BlockSpec((tf, D_MODEL), lambda i, j: (j, 0)),
                  pl.BlockSpec((1, D_MODEL), lambda i, j: (0, 0))],
        out_specs=row,
        out_shape=jax.ShapeDtypeStruct((T, D_MODEL), f32),
        scratch_shapes=[pltpu.VMEM((tm, D_MODEL), f32)],
        compiler_params=_params("parallel", "arbitrary"),
        name="mlp",
    )(x, h2, mod, w_up, w_down, g_final)


def _ctx_attn_kernel(aq_ref, ak_ref, av_ref, bq_ref, bk_ref, bv_ref, cq_ref, ckv_ref,
                     dl_ref, dg_ref, sink_ref, o_ref, *, lam_init):
    lam = _diff_lambda(dl_ref, lam_init)
    q = aq_ref[...].astype(f32) * QK_SCALE2
    outs = [_diff_head(q, lambda t: ak_ref[:, t * PAIR:(t + 1) * PAIR], lambda h: av_ref[:, h * A_V:(h + 1) * A_V],
                       h, lam, dg_ref[...], lam_init) for h in range(A_HEADS)]
    o_ref[0] = jnp.concatenate(outs, axis=1).astype(bf16)
    q = bq_ref[...].astype(f32) * QK_SCALE2
    tiles = []
    for t in range(B_HEADS // 2):
        cols = slice(t * PAIR, (t + 1) * PAIR)
        accs = []
        for half in range(2):
            (e,), _ = _exp2_parts([_dot_nt(_head_q(q[:, cols], half), bk_ref[:, cols])])
            accs.append(_dot(e.astype(bf16), _values_with_ones(bv_ref[:, cols], half)))
        tiles.append(_pair_out(*accs))
    o_ref[1] = jnp.concatenate(tiles, axis=1).astype(bf16)
    q = cq_ref[...].astype(f32) * QK_SCALE2
    k_tile = ckv_ref[:, 0:PAIR]
    v_variants = _swap_halves_variants(ckv_ref[:, PAIR:2 * PAIR].astype(f32))
    tiles = []
    for t in range(C_HEADS // 2):
        q_tile = q[:, t * PAIR:(t + 1) * PAIR]
        q_swapped = pltpu.roll(q_tile, HEAD_DIM, 1)
        accs, extras = [], []
        for half in range(2):
            h = 2 * t + half
            g = h // C_GROUP
            sink2 = sink_ref[h] * LOG2E
            (e,), m = _exp2_parts([_dot_nt(_gqa_head_q(q_tile, q_swapped, half, g), k_tile)], sink2)
            accs.append(_dot(e.astype(bf16), v_variants[g * 2 + half]))
            extras.append(jnp.exp2(sink2 - m))
        tiles.append(_pair_out(accs[0], accs[1], extras[0], extras[1]))
    o_ref[2] = jnp.concatenate(tiles, axis=1).astype(bf16)


def _ctx_attn(p, dl, dg, sink, li):
    nb = p.shape[0] // CTX_SEQ
    S = CTX_SEQ
    lam_init = 0.8 - 0.6 * math.exp(-0.3 * li)

    def col(c):
        return pl.BlockSpec((S, 512), lambda b, c=c: (b, c))

    return pl.pallas_call(
        functools.partial(_ctx_attn_kernel, lam_init=lam_init),
        grid=(nb,),
        in_specs=[col(COL_AQ), col(COL_AK), col(COL_AV), col(COL_BQ), col(COL_BK), col(COL_BV), col(COL_CQ),
                  pl.BlockSpec((S, 256), lambda b: (b, COL_CKV_256)),
                  pl.BlockSpec((4, A_QK), lambda b: (0, 0)),
                  pl.BlockSpec((1, A_V), lambda b: (0, 0)),
                  pl.BlockSpec(memory_space=pltpu.SMEM)],
        out_specs=pl.BlockSpec((N_BRANCH - 1, S, BRANCH_WIDTH), lambda b: (0, b, 0)),
        out_shape=jax.ShapeDtypeStruct((N_BRANCH, p.shape[0], BRANCH_WIDTH), bf16),
        compiler_params=_params("parallel"),
        name="ctx_attn",
    )(p, p, p, p, p, p, p, p, dl, dg, sink)


def _lat_a_kernel(q_ref, k_ref, v_ref, ck_ref, cv_ref, cosq_ref, sinq_ref, cosk_ref, sink_ref,
                  dl_ref, dg_ref, o_ref, kk_ref, vv_ref, *, lam_init):
    L = LAT_SEQ

    @pl.when(pl.program_id(1) == 0)
    def _():
        kk_ref[0:L, :] = _rope(k_ref[...], cosk_ref[...], sink_ref[...]).astype(bf16)
        kk_ref[L:, :] = ck_ref[...].astype(bf16)
        vv_ref[0:L, :] = v_ref[...].astype(bf16)
        vv_ref[L:, :] = cv_ref[...].astype(bf16)

    lam = _diff_lambda(dl_ref, lam_init)
    q = _rope(q_ref[...], cosq_ref[...], sinq_ref[...]) * QK_SCALE2
    outs = [_diff_head(q, lambda t: kk_ref[:, t * PAIR:(t + 1) * PAIR], lambda h: vv_ref[:, h * A_V:(h + 1) * A_V],
                       h, lam, dg_ref[...], lam_init) for h in range(A_HEADS)]
    o_ref[...] = jnp.concatenate(outs, axis=1).astype(bf16)


def _lat_b_kernel(q_ref, k_ref, v_ref, ck_ref, cv_ref, bias_ref, _, o_ref, kk_ref, va_ref):
    L, W = LAT_SEQ, NA_KEY_ROWS * GRID_W
    qb = pl.program_id(1)

    @pl.when(qb == 0)
    def _():
        kk_ref[0:L, :] = k_ref[...]
        kk_ref[L:, :] = ck_ref[...].astype(bf16)
        for t in range(B_HEADS // 2):
            cols = slice(t * PAIR, (t + 1) * PAIR)
            for half in range(2):
                va_ref[2 * t + half, 0:L, :] = _values_with_ones(v_ref[:, cols], half)
                va_ref[2 * t + half, L:, :] = _values_with_ones(cv_ref[:, cols], half)

    start = pl.multiple_of(_na_key_row0(qb) * GRID_W, 256)
    q = q_ref[...].astype(f32) * QK_SCALE2
    tiles = []
    for t in range(B_HEADS // 2):
        cols = slice(t * PAIR, (t + 1) * PAIR)
        accs = []
        for half in range(2):
            h = 2 * t + half
            qh = _head_q(q[:, cols], half)
            s_loc = _dot_nt(qh, kk_ref[pl.ds(start, W), cols]) + bias_ref[h].astype(f32)
            s_ctx = _dot_nt(qh, kk_ref[L:, cols])
            (e_loc, e_ctx), _m = _exp2_parts([s_loc, s_ctx])
            accs.append(_dot(e_loc.astype(bf16), va_ref[h, pl.ds(start, W), :])
                        + _dot(e_ctx.astype(bf16), va_ref[h, L:, :]))
        tiles.append(_pair_out(*accs))
    o_ref[...] = jnp.concatenate(tiles, axis=1).astype(bf16)


def _lat_c_kernel(q_ref, kv_ref, ck_ref, cv_ref, cosq_ref, sinq_ref, cosk_ref, sink_ref, snk_ref,
                  _, o_ref, kk_ref, va_ref):
    L, W = LAT_SEQ, SWA_KEYS
    tq = q_ref.shape[0]
    qb = pl.program_id(1)

    @pl.when(qb == 0)
    def _():
        kk_ref[0:L, :] = _rope(kv_ref[:, 0:PAIR], cosk_ref[...], sink_ref[...]).astype(bf16)
        kk_ref[L:, :] = ck_ref[...].astype(bf16)
        lat = _swap_halves_variants(kv_ref[:, PAIR:2 * PAIR].astype(f32))
        ctx = _swap_halves_variants(cv_ref[...])
        for idx in range(4):
            va_ref[idx, 0:L, :] = lat[idx]
            va_ref[idx, L:, :] = ctx[idx]

    start = pl.multiple_of(jnp.clip(qb * tq - SWA_WINDOW, 0, L - W), 128)
    q = _rope(q_ref[...], cosq_ref[...], sinq_ref[...]) * QK_SCALE2
    qpos = qb * tq + lax.broadcasted_iota(jnp.int32, (tq, W), 0)
    kpos = start + lax.broadcasted_iota(jnp.int32, (tq, W), 1)
    valid = jnp.abs(qpos - kpos) <= SWA_WINDOW
    tiles = []
    for t in range(C_HEADS // 2):
        q_tile = q[:, t * PAIR:(t + 1) * PAIR]
        q_swapped = pltpu.roll(q_tile, HEAD_DIM, 1)
        accs, extras = [], []
        for half in range(2):
            h = 2 * t + half
            g = h // C_GROUP
            qh = _gqa_head_q(q_tile, q_swapped, half, g)
            s_loc = jnp.where(valid, _dot_nt(qh, kk_ref[pl.ds(start, W), :]), NEG_INF)
            s_ctx = _dot_nt(qh, kk_ref[L:, :])
            sink2 = snk_ref[h] * LOG2E
            (e_loc, e_ctx), m = _exp2_parts([s_loc, s_ctx], sink2)
            idx = g * 2 + half
            accs.append(_dot(e_loc.astype(bf16), va_ref[idx, pl.ds(start, W), :])
                        + _dot(e_ctx.astype(bf16), va_ref[idx, L:, :]))
            extras.append(jnp.exp2(sink2 - m))
        tiles.append(_pair_out(accs[0], accs[1], extras[0], extras[1]))
    o_ref[...] = jnp.concatenate(tiles, axis=1).astype(bf16)


def _lat_attn(p, caches, na_dense, dl, dg, sink, cos_t, sin_t, li):
    cak, cav, cbk, cbv, cck, ccv = caches
    L, tq = LAT_SEQ, TQ_ATTN
    assert tq == NA_Q_ROWS * GRID_W and SWA_KEYS >= tq + 2 * SWA_WINDOW
    nq = L // tq
    nb_lat = p.shape[0] // L
    lam_init = 0.8 - 0.6 * math.exp(-0.3 * li)

    def qcol(c):
        return pl.BlockSpec((tq, 512), lambda b, i, c=c: (b * nq + i, c))

    def kcol(c, width=512):
        return pl.BlockSpec((L, width), lambda b, i, c=c: (b, c))

    def cache(width):
        return pl.BlockSpec((None, None, CTX_SEQ, width), lambda b, i: (b, li, 0, 0))

    tab_q = pl.BlockSpec((tq, 512), lambda b, i: (i, 0))
    tab_k = pl.BlockSpec((L, 512), lambda b, i: (0, 0))
    tab_k128 = pl.BlockSpec((L, 128), lambda b, i: (0, 0))
    o_sds = jax.ShapeDtypeStruct((N_BRANCH, p.shape[0], BRANCH_WIDTH), bf16)
    anyspec = pl.BlockSpec(memory_space=pl.ANY)

    def o_spec(branch):
        return pl.BlockSpec((None, tq, BRANCH_WIDTH), lambda b, i: (branch, b * nq + i, 0))

    kv_scratch = [pltpu.VMEM((L + CTX_SEQ, 512), bf16), pltpu.VMEM((L + CTX_SEQ, 512), bf16)]
    smem = pl.BlockSpec(memory_space=pltpu.SMEM)
    cp = _params("parallel", "arbitrary")

    o_stack = pl.pallas_call(
        functools.partial(_lat_a_kernel, lam_init=lam_init),
        grid=(nb_lat, nq),
        in_specs=[qcol(COL_AQ), kcol(COL_AK), kcol(COL_AV), cache(512), cache(512),
                  tab_q, tab_q, tab_k, tab_k,
                  pl.BlockSpec((4, A_QK), lambda b, i: (0, 0)), pl.BlockSpec((1, A_V), lambda b, i: (0, 0))],
        out_specs=o_spec(0), out_shape=o_sds, scratch_shapes=kv_scratch,
        compiler_params=cp, name="lat_attn_a",
    )(p, p, p, cak, cav, cos_t, sin_t, cos_t, sin_t, dl, dg)

    o_stack = pl.pallas_call(
        _lat_b_kernel,
        grid=(nb_lat, nq),
        in_specs=[qcol(COL_BQ), kcol(COL_BK), kcol(COL_BV), cache(512), cache(512),
                  pl.BlockSpec((B_HEADS, tq, NA_KEY_ROWS * GRID_W), lambda b, i: (0, i, 0)), anyspec],
        out_specs=o_spec(1), out_shape=o_sds,
        scratch_shapes=[pltpu.VMEM((L + CTX_SEQ, 512), bf16), pltpu.VMEM((B_HEADS, L + CTX_SEQ, PAIR), bf16)],
        input_output_aliases={6: 0}, compiler_params=cp, name="lat_attn_b",
    )(p, p, p, cbk, cbv, na_dense, o_stack)

    return pl.pallas_call(
        _lat_c_kernel,
        grid=(nb_lat, nq),
        in_specs=[qcol(COL_CQ), kcol(COL_CKV_256, 256), cache(PAIR), cache(PAIR),
                  tab_q, tab_q, tab_k128, tab_k128, smem, anyspec],
        out_specs=o_spec(2), out_shape=o_sds,
        scratch_shapes=[pltpu.VMEM((L + CTX_SEQ, PAIR), bf16), pltpu.VMEM((4, L + CTX_SEQ, PAIR), bf16)],
        input_output_aliases={9: 0}, compiler_params=cp, name="lat_attn_c",
    )(p, p, cck, ccv, cos_t, sin_t, cos_t, sin_t, sink, o_stack)


def _na_key_row0(qb):
    lo, hi = NA_Q_ROWS * qb - NA_ROWS // 2, GRID_ROWS - NA_KEY_ROWS
    return min(max(lo, 0), hi) if isinstance(qb, int) else jnp.clip(lo, 0, hi)


def _na_bias_kernel(rb_ref, o_ref):
    h = pl.program_id(0)
    W = GRID_W
    n_dc = 2 * NA_COLS - 1
    n_dr = 2 * NA_ROWS - 1
    qc = lax.broadcasted_iota(jnp.int32, (W, 2 * W), 0)
    lane = lax.broadcasted_iota(jnp.int32, (W, 2 * W), 1)
    second = lane >= W
    kc = jnp.where(second, lane - W, lane)
    dc = jnp.clip(kc - qc, -(NA_COLS - 1), NA_COLS - 1) + NA_COLS - 1
    c0 = jnp.clip(qc - NA_COLS // 2, 0, W - NA_COLS)
    col_ok = (kc >= c0) & (kc < c0 + NA_COLS)
    base = h * (n_dr * n_dc)

    def pair_tile(dr0):
        t = jnp.zeros((W, 2 * W), f32)
        for j in range(n_dc):
            lo = rb_ref[base + dr0 * n_dc + j] if 0 <= dr0 < n_dr else 0.0
            hi = rb_ref[base + (dr0 + 1) * n_dc + j] if 0 <= dr0 + 1 < n_dr else 0.0
            t = jnp.where(dc == j, jnp.where(second, hi, lo), t)
        return jnp.where(col_ok, t * LOG2E, NEG_INF)

    tiles = {dr0: pair_tile(dr0) for dr0 in range(-1, n_dr)}
    neg = jnp.full((W, 2 * W), NEG_INF, f32)
    for qr in range(GRID_ROWS):
        r0 = min(max(qr - NA_ROWS // 2, 0), GRID_ROWS - NA_ROWS)
        k0 = _na_key_row0(qr // NA_Q_ROWS)
        assert k0 % 2 == 0 and k0 <= r0 and r0 + NA_ROWS <= k0 + NA_KEY_ROWS
        for pr in range(NA_KEY_ROWS // 2):
            kr = k0 + 2 * pr
            ok0 = r0 <= kr < r0 + NA_ROWS
            ok1 = r0 <= kr + 1 < r0 + NA_ROWS
            if not (ok0 or ok1):
                t = neg
            else:
                t = tiles[kr - qr + NA_ROWS - 1]
                if not ok0:
                    t = jnp.where(second, t, NEG_INF)
                if not ok1:
                    t = jnp.where(second, NEG_INF, t)
            o_ref[qr * W:(qr + 1) * W, 2 * pr * W:(2 * pr + 2) * W] = t.astype(bf16)


def _na_bias_dense(rel_bias):
    kw = NA_KEY_ROWS * GRID_W
    return pl.pallas_call(
        _na_bias_kernel,
        grid=(B_HEADS,),
        in_specs=[pl.BlockSpec(memory_space=pltpu.SMEM)],
        out_specs=pl.BlockSpec((None, LAT_SEQ, kw), lambda h: (h, 0, 0)),
        out_shape=jax.ShapeDtypeStruct((B_HEADS, LAT_SEQ, kw), bf16),
        compiler_params=_params("parallel"),
        name="na_bias",
    )(rel_bias.reshape(-1))


def _hy_filter_kernel(z_ref, w1_ref, b1_ref, w2_ref, b2_ref, fr_ref, w3f_ref, w3b_ref, dcf_ref, dcb_ref,
                      bias_ref, fc_ref, fs_ref, o_ref, *, L):
    z = z_ref[...]
    fr = fr_ref[...]
    h = jnp.sin(fr * (_dot_hi(z, w1_ref[...]) + b1_ref[...]))
    h = jnp.sin(fr * (_dot_hi(h, w2_ref[...]) + b2_ref[...]))
    t = z[:, 0:1]
    hf = _dot_hi(h, w3f_ref[...]) * jnp.exp(-t * jnp.abs(dcf_ref[...]))
    hb = _dot_hi(h, w3b_ref[...]) * jnp.exp(-t * jnp.abs(dcb_ref[...]))
    ssum = hf + hb
    bias = bias_ref[...]
    ga = _dot_3pass(fc_ref[...], ssum) + bias
    gb = _dot_3pass(fs_ref[...], hb - hf)
    g_nyq = _dot_hi(fs_ref[0:8, :], ssum)[0:1] + bias
    row0 = lax.broadcasted_iota(jnp.int32, ga.shape, 0) == 0
    inv = 1.0 / L
    o_ref[0] = jnp.where(row0, 0.5 * ga, ga) * inv
    o_ref[1] = jnp.where(row0, 0.0, gb) * inv
    o_ref[2] = jnp.where(row0, 0.5 * g_nyq, ga) * inv


def _hy_filter_tables(L, w1p, b1, w2, b2, w3, freq, decay, hy_bias):
    cb = HY_CB
    ncb = HY_WIDTH // cb
    z = jnp.asarray(_hyena_features(L))
    fc, fs = (jnp.asarray(a) for a in _dft_tables(L))
    full = lambda shape: pl.BlockSpec(shape, lambda o, c: (0,) * len(shape))
    fwd = lambda rows: pl.BlockSpec((rows, cb), lambda o, c: (0, o * 2 * ncb + c))
    bwd = lambda rows: pl.BlockSpec((rows, cb), lambda o, c: (0, o * 2 * ncb + ncb + c))
    return pl.pallas_call(
        functools.partial(_hy_filter_kernel, L=L),
        grid=(HY_ORDER, ncb),
        in_specs=[full((L, HY_EMB_PAD)), full((HY_EMB_PAD, HY_FFN)), full((1, HY_FFN)),
                  full((HY_FFN, HY_FFN)), full((1, HY_FFN)), full((1, HY_FFN)),
                  fwd(HY_FFN), bwd(HY_FFN), fwd(1), bwd(1),
                  pl.BlockSpec((None, 1, cb), lambda o, c: (o, 0, c)),
                  full((L, L)), full((L, L))],
        out_specs=pl.BlockSpec((None, 3, L, cb), lambda o, c: (o, 0, 0, c)),
        out_shape=jax.ShapeDtypeStruct((HY_ORDER, 3, L, HY_WIDTH), f32),
        compiler_params=_params("parallel", "parallel"),
        name="hy_filter",
    )(z, w1p, b1, w2, b2, freq, w3, w3, decay, decay, hy_bias.reshape(HY_ORDER, 1, HY_WIDTH), fc, fs)


def _hyena_kernel(v_ref, x1_ref, x2_ref, wv_ref, w1_ref, w2_ref, fwd_ref, inv_ref, tab_ref, _, o_ref):
    L = inv_ref.shape[0]
    row = lax.broadcasted_iota(jnp.int32, (L, v_ref.shape[1]), 0)

    def short_conv(u_ref, w_ref, s):
        u = u_ref[s * L:(s + 1) * L, :].astype(f32)
        w = w_ref[...]
        prev = jnp.where(row == 0, 0.0, pltpu.roll(u, 1, 0))
        nxt = jnp.where(row == L - 1, 0.0, pltpu.roll(u, L - 1, 0))
        return prev * w[0:1] + u * w[1:2] + nxt * w[2:3]

    def long_conv(u, order):
        ab = _dot(fwd_ref[...], u.astype(bf16))
        a, b = ab[:L], ab[L:]
        t0, t1, t2 = tab_ref[order, 0], tab_ref[order, 1], tab_ref[order, 2]
        pr = (a * t0 + b * t1).astype(bf16)
        qi = (b * t2 - a * t1).astype(bf16)
        return _dot(inv_ref[...], jnp.concatenate([pr, qi], axis=0))

    outs = []
    for s in range(v_ref.shape[0] // L):
        z = short_conv(x1_ref, w1_ref, s) * long_conv(short_conv(v_ref, wv_ref, s), 0)
        outs.append(short_conv(x2_ref, w2_ref, s) * long_conv(z, 1))
    o_ref[...] = jnp.concatenate(outs, axis=0).astype(bf16)


def _hyena(p, o_stack, hy_short, tabs, L):
    cb = HY_CB
    ncb = HY_WIDTH // cb
    c0 = HY_COL0 // cb
    rows = min(HY_ROWS_PER_STEP, p.shape[0])
    nb = p.shape[0] // rows
    fc, fs = _dft_tables(L)
    fwd = jnp.asarray(np.concatenate([fc, fs], axis=0), bf16)
    inv = jnp.asarray(np.concatenate([fc, fs.T], axis=1), bf16)

    def part(k):
        return pl.BlockSpec((rows, cb), lambda c, b, k=k: (b, c0 + k * ncb + c))

    def wpart(k):
        return pl.BlockSpec((3, cb), lambda c, b, k=k: (0, k * ncb + c))

    return pl.pallas_call(
        _hyena_kernel,
        grid=(ncb, nb),
        in_specs=[part(0), part(1), part(2), wpart(0), wpart(1), wpart(2),
                  pl.BlockSpec((2 * L, L), lambda c, b: (0, 0)), pl.BlockSpec((L, 2 * L), lambda c, b: (0, 0)),
                  pl.BlockSpec((HY_ORDER, 3, L, cb), lambda c, b: (0, 0, 0, c)),
                  pl.BlockSpec(memory_space=pl.ANY)],
        out_specs=pl.BlockSpec((None, rows, cb), lambda c, b: (N_BRANCH - 1, b, c)),
        out_shape=jax.ShapeDtypeStruct(o_stack.shape, o_stack.dtype),
        input_output_aliases={9: 0},
        compiler_params=_params("parallel", "arbitrary"),
        name="hyena_%d" % L,
    )(p, p, p, hy_short, hy_short, hy_short, fwd, inv, tabs, o_stack)


def kernel(x_prompt, x_sample, cache_a_k, cache_a_v, cache_b_k, cache_b_v, cache_c_k, cache_c_v, c, c_ctx, w_ada, b_ada, g_mix, w_in, diff_lambda, diff_norm_g, na_bias, swa_sink, hy_short, hy_w1, hy_b1, hy_w2, hy_b2, hy_w3, hy_freq, hy_decay, hy_bias, w_branch, w_out, g_mlp, w_up, w_down, g_final):
    nb_ctx, nb_lat = x_prompt.shape[0], x_sample.shape[0]
    assert x_prompt.shape[1:] == (CTX_SEQ, D_MODEL) and x_sample.shape[1:] == (LAT_SEQ, D_MODEL)
    assert nb_lat <= CTX_MOD_ROW and (nb_ctx * CTX_SEQ) % TM_INPROJ == 0

    x_ctx = x_prompt.reshape(nb_ctx * CTX_SEQ, D_MODEL)
    x_lat = x_sample.reshape(nb_lat * LAT_SEQ, D_MODEL)
    cv = jnp.zeros((N_MOD_ROWS, D_MODEL), f32).at[:nb_lat].set(c).at[CTX_MOD_ROW].set(c_ctx)
    mod = _adaln_all(cv, w_ada, b_ada).reshape(DEPTH * N_MOD_ROWS * 6, 1, D_MODEL)
    cos_t, sin_t = (jnp.asarray(a) for a in _rope_tables())
    caches = (cache_a_k.reshape(nb_lat, DEPTH, CTX_SEQ, 512), cache_a_v.reshape(nb_lat, DEPTH, CTX_SEQ, 512),
              cache_b_k.reshape(nb_lat, DEPTH, CTX_SEQ, 512), cache_b_v.reshape(nb_lat, DEPTH, CTX_SEQ, 512),
              cache_c_k.reshape(nb_lat, DEPTH, CTX_SEQ, 128), cache_c_v.reshape(nb_lat, DEPTH, CTX_SEQ, 128))

    kv = None
    for li in range(DEPTH):
        wl = w_in[li]
        w_mix = jnp.concatenate([wl[:, 0:512], wl[:, 1536:2048], wl[:, 3072:3584], wl[:, 512:1536],
                                 wl[:, 2048:3072], wl[:, 3584:MIX_COLS]], axis=1).astype(bf16)
        w5, wu, wd = _cast_layer_weights(w_in, w_out, w_up, w_down, li)
        wb = w_branch[li].astype(bf16)
        g1 = g_mix[li].reshape(1, D_MODEL)
        g2 = g_mlp[li].reshape(1, D_MODEL)
        dl = diff_lambda[li]
        dg = diff_norm_g[li].reshape(1, A_V)
        sink = swa_sink[li]

        w1p = jnp.pad(hy_w1[li], ((0, HY_EMB_PAD - HY_EMB), (0, 0)))
        hy_args = (w1p, hy_b1[li].reshape(1, HY_FFN), hy_w2[li], hy_b2[li].reshape(1, HY_FFN), hy_w3[li],
                   hy_freq[li].reshape(1, HY_FFN), hy_decay[li].reshape(1, -1), hy_bias[li])
        gf = g_final.reshape(1, D_MODEL)
        final = li == DEPTH - 1

        p, kv, h = _inproj(x_ctx, mod, g1, w_mix, li, True, kv)
        o_stack = _ctx_attn(p, dl, dg, sink, li)
        o_stack = _hyena(p, o_stack, hy_short[li], _hy_filter_tables(CTX_SEQ, *hy_args), CTX_SEQ)
        x_ctx, h2 = _merge(x_ctx, mod, h, o_stack, w5, wb, g2, li, True)
        x_ctx = _mlp(x_ctx, h2, mod, wu, wd, gf, li, True, final_norm=final)

        p, h = _inproj(x_lat, mod, g1, w_mix, li, False)
        o_stack = _lat_attn(p, caches, _na_bias_dense(na_bias[li]), dl, dg, sink, cos_t, sin_t, li)
        o_stack = _hyena(p, o_stack, hy_short[li], _hy_filter_tables(LAT_SEQ, *hy_args), LAT_SEQ)
        x_lat, h2 = _merge(x_lat, mod, h, o_stack, w5, wb, g2, li, False)
        x_lat = _mlp(x_lat, h2, mod, wu, wd, gf, li, False, final_norm=final)

    y_prompt = x_ctx.reshape(nb_ctx, CTX_SEQ, D_MODEL)
    y_sample = x_lat.reshape(nb_lat, LAT_SEQ, D_MODEL)
    kv_shapes = ((0, 512, (2, A_HEADS, A_QK)), (512, 1024, (A_HEADS, A_V)),
                 (1024, 1536, (B_HEADS, HEAD_DIM)), (1536, 2048, (B_HEADS, HEAD_DIM)),
                 (2048, 2176, (C_KV_HEADS, HEAD_DIM)), (2176, 2304, (C_KV_HEADS, HEAD_DIM)))
    new_kv = tuple(kv[..., a:b].reshape((nb_ctx, DEPTH, CTX_SEQ) + s) for a, b, s in kv_shapes)
    return (y_prompt, y_sample) + new_kv
```

```python
import functools
import math

import numpy as np
import jax
import jax.numpy as jnp
from jax import lax
from jax.experimental import pallas as pl
from jax.experimental.pallas import tpu as pltpu

f32 = jnp.float32
bf16 = jnp.bfloat16

D_MODEL = 2048
DEPTH = 2
CTX_SEQ = 256
LAT_SEQ = 1024
GRID_W = 64
GRID_ROWS = LAT_SEQ // GRID_W
BRANCH_WIDTH = 512
N_BRANCH = 4
HEAD_DIM = 64
A_QK = 64
A_V = 128
A_HEADS = 4
B_HEADS = 8
NA_ROWS = 8
NA_COLS = 16
C_HEADS = 8
C_KV_HEADS = 2
C_GROUP = 4
SWA_WINDOW = 128
SWA_KEYS = 512
NA_Q_ROWS = 4
NA_KEY_ROWS = 12
HY_WIDTH = 512
HY_ORDER = 2
HY_BANDS = 16
HY_EMB = 1 + 2 * HY_BANDS
HY_EMB_PAD = 128
HY_FFN = 64
D_FF = 4 * D_MODEL
ROPE_BASE = 10000.0
EPS = 1e-6
NEG_INF = -1e30
MIX_COLS = 5376
GATE_COLS = N_BRANCH * D_MODEL
COL_AQ, COL_BQ, COL_CQ, COL_AK, COL_AV, COL_BK, COL_BV = 0, 1, 2, 3, 4, 5, 6
COL_CKV_256 = 14
KV_COL0 = 1536
KV_COLS = 2304
HY_COL0 = 3840
N_MOD_ROWS = 16
CTX_MOD_ROW = 8

VMEM_LIMIT = 56 * 1024 * 1024
TM_INPROJ = 1024
TN_INPROJ = 768
TM_MERGE = 512
TM_MLP = 512
TF_MLP = 1024
TQ_ATTN = 256
TN_ADA = 1024
HY_CB = 256
HY_SEQS_PER_STEP = {CTX_SEQ: 8, LAT_SEQ: 2}
CAST_ROWS = 512


def _params(*sem):
    return pltpu.CompilerParams(dimension_semantics=sem, vmem_limit_bytes=VMEM_LIMIT)


def _dot(a, b):
    return jnp.dot(a, b, preferred_element_type=f32)


def _dot_nt(a, b):
    return lax.dot_general(a, b, (((1,), (1,)), ((), ())), preferred_element_type=f32)


def _dot_hi(a, b):
    return jnp.dot(a, b, preferred_element_type=f32, precision=lax.Precision.HIGHEST)


def _dot_3pass(a, b):
    a_hi, b_hi = a.astype(bf16), b.astype(bf16)
    a_lo = (a - a_hi.astype(f32)).astype(bf16)
    b_lo = (b - b_hi.astype(f32)).astype(bf16)
    return _dot(a_hi, b_hi) + (_dot(a_hi, b_lo) + _dot(a_lo, b_hi))


@functools.lru_cache(maxsize=None)
def _rope_tables():
    half = HEAD_DIM // 2
    nf = half // 2
    inv = ROPE_BASE ** (-np.arange(nf, dtype=np.float64) / nf)
    t = np.arange(LAT_SEQ)
    pos = np.stack([t // GRID_W, t % GRID_W], axis=1).astype(np.float64)
    lane = np.arange(HEAD_DIM)
    ang = pos[:, lane // half] * inv[lane % nf][None, :]
    first = (lane % half) < nf
    cos = np.cos(ang)
    sin = np.where(first[None, :], -np.sin(ang), np.sin(ang))
    reps = 512 // HEAD_DIM
    return (np.tile(cos, (1, reps)).astype(np.float32), np.tile(sin, (1, reps)).astype(np.float32))


@functools.lru_cache(maxsize=None)
def _dft_tables(L):
    f = np.arange(L, dtype=np.int64)
    prod = (f[:, None] * f[None, :]) % (2 * L)
    ang = np.pi * prod.astype(np.float64) / L
    fc = np.cos(ang)
    fs = np.sin(ang)
    fs[0, :] = np.where(f % 2 == 0, 1.0, -1.0)
    return fc.astype(np.float32), fs.astype(np.float32)


@functools.lru_cache(maxsize=None)
def _hyena_features(L):
    n = np.arange(L, dtype=np.float64)[:, None]
    t = n / max(L - 1, 1)
    w = 2.0 * math.pi * n / L
    bands = np.linspace(1e-4, HY_BANDS - 1, HY_BANDS, dtype=np.float64)[None, :]
    z = np.concatenate([t, np.cos(bands * w), -np.sin(bands * w)], axis=-1)
    z = np.pad(z, ((0, 0), (0, HY_EMB_PAD - HY_EMB)))
    return z.astype(np.float32)


def _modulated_norm(x, g, scale, shift):
    ms = jnp.mean(x * x, axis=-1, keepdims=True)
    return x * lax.rsqrt(ms + EPS) * (g * (1.0 + scale)) + shift


def _rope(x, cos, sin_signed):
    x = x.astype(f32)
    n = x.shape[-1]
    lane = lax.broadcasted_iota(jnp.int32, x.shape, 1)
    first = (lane & (HEAD_DIM // 2 - 1)) < (HEAD_DIM // 4)
    partner = jnp.where(first, pltpu.roll(x, n - HEAD_DIM // 4, 1), pltpu.roll(x, HEAD_DIM // 4, 1))
    return x * cos + partner * sin_signed


LOG2E = 1.4426950408889634
QK_SCALE2 = HEAD_DIM ** -0.5 * LOG2E
PAIR = 2 * HEAD_DIM


def _lo_half(shape):
    return lax.broadcasted_iota(jnp.int32, shape, 1) < HEAD_DIM


def _keep_half(x, half, fill):
    lo = _lo_half(x.shape)
    return jnp.where(lo if half == 0 else jnp.logical_not(lo), x, fill)


def _head_q(q_tile, half):
    return _keep_half(q_tile, half, 0.0).astype(bf16)


def _values_with_ones(v_tile, half):
    return _keep_half(v_tile.astype(f32), half, 1.0).astype(bf16)


def _exp2_parts(scores, sink2=None):
    m = jnp.max(scores[0], axis=-1, keepdims=True)
    for s in scores[1:]:
        m = jnp.maximum(m, jnp.max(s, axis=-1, keepdims=True))
    if sink2 is not None:
        m = jnp.maximum(m, sink2)
    return [jnp.exp2(s - m) for s in scores], m


def _pair_out(acc_even, acc_odd, extra_even=None, extra_odd=None):
    lo = _lo_half(acc_even.shape)
    num = jnp.where(lo, acc_even, acc_odd)
    den = pltpu.roll(jnp.where(lo, acc_odd, acc_even), HEAD_DIM, 1)
    if extra_even is not None:
        den = den + jnp.where(lo, extra_even, extra_odd)
    return num / den


def _swap_halves_variants(v_tile):
    sw = pltpu.roll(v_tile, HEAD_DIM, 1)
    lo = _lo_half(v_tile.shape)
    tiles = (jnp.where(lo, v_tile, 1.0), jnp.where(lo, 1.0, sw), jnp.where(lo, sw, 1.0), jnp.where(lo, 1.0, v_tile))
    return [t.astype(bf16) for t in tiles]


def _gqa_head_q(q_tile, q_swapped, half, g):
    return _keep_half(q_tile if half == g else q_swapped, g, 0.0).astype(bf16)


def _diff_lambda(dl_ref, lam_init):
    dl = dl_ref[...]
    a = jnp.sum(dl[0:1] * dl[1:2], axis=-1, keepdims=True)
    b = jnp.sum(dl[2:3] * dl[3:4], axis=-1, keepdims=True)
    return jnp.exp(a) - jnp.exp(b) + lam_init


def _diff_head(q, k_of, v_of, h, lam, dg, lam_init):
    res = []
    for c in range(2):
        t = c * (A_HEADS // 2) + h // 2
        s = _dot_nt(_head_q(q[:, t * PAIR:(t + 1) * PAIR], h % 2), k_of(t))
        (e,), _ = _exp2_parts([s])
        res.append((_dot(e.astype(bf16), v_of(h)), jnp.sum(e, axis=-1, keepdims=True)))
    o = res[0][0] * (1.0 / res[0][1]) - res[1][0] * (lam / res[1][1])
    o = o * lax.rsqrt(jnp.mean(o * o, axis=-1, keepdims=True) + EPS) * dg
    return o * (1.0 - lam_init)


def _cast_kernel(src_ref, *rest):
    o_ref = rest[-1]
    x = src_ref[...]
    o_ref[...] = x.reshape(o_ref.shape).astype(bf16)


def _cast_call(name, grid, src, in_spec, out_spec, out_shape, dst=None):
    in_specs, args, aliases = [in_spec], [src], {}
    if dst is not None:
        in_specs.append(pl.BlockSpec(memory_space=pl.ANY))
        args.append(dst)
        aliases = {1: 0}
    return pl.pallas_call(
        _cast_kernel, grid=grid, in_specs=in_specs, out_specs=out_spec,
        out_shape=jax.ShapeDtypeStruct(out_shape, bf16), input_output_aliases=aliases,
        compiler_params=_params(*(("parallel",) * len(grid))), name=name,
    )(*args)


def _cast_mix_kernel(unit_ref, src_ref, o_ref):
    o_ref[...] = src_ref[0].astype(bf16)


def _cast_mix_weights(w_in, li):
    U = 256
    src_unit = (0, 1, 6, 7, 12, 13, 2, 3, 4, 5, 8, 9, 10, 11) + tuple(range(14, MIX_COLS // U))
    return pl.pallas_call(
        _cast_mix_kernel,
        grid_spec=pltpu.PrefetchScalarGridSpec(
            num_scalar_prefetch=1, grid=(len(src_unit),),
            in_specs=[pl.BlockSpec((pl.Element(1), pl.Element(D_MODEL), pl.Element(U)),
                                   lambda u, unit: (li, 0, pl.multiple_of(unit[u] * U, U)))],
            out_specs=pl.BlockSpec((D_MODEL, U), lambda u, unit: (0, u))),
        out_shape=jax.ShapeDtypeStruct((D_MODEL, MIX_COLS), bf16),
        compiler_params=_params("parallel"), name="cast_mix",
    )(jnp.asarray(src_unit, jnp.int32), w_in)


def _cast_layer_weights(w_in, w_out, w_up, w_down, li):
    D, R, tf = D_MODEL, CAST_ROWS, TF_MLP
    nr = D // R
    w5 = _cast_call(
        "cast_gate", (N_BRANCH, nr), w_in,
        pl.BlockSpec((pl.Element(1), pl.Element(R), pl.Element(D)),
                     lambda n, r: (li, pl.multiple_of(r * R, R), pl.multiple_of(MIX_COLS + n * D, 128))),
        pl.BlockSpec((None, R, D), lambda n, r: (n, r, 0)), (N_BRANCH + 1, D, D))
    w5 = _cast_call(
        "cast_wout", (nr,), w_out,
        pl.BlockSpec((None, R, D), lambda r: (li, r, 0)),
        pl.BlockSpec((None, R, D), lambda r: (N_BRANCH, r, 0)), (N_BRANCH + 1, D, D), dst=w5)
    wu = _cast_call(
        "cast_wu", (D_FF // tf, nr), w_up,
        pl.BlockSpec((None, R, tf), lambda j, r: (li, r, j)),
        pl.BlockSpec((None, R, tf), lambda j, r: (j, r, 0)), (D_FF // tf, D, tf))
    wd = _cast_call(
        "cast_wd", (D_FF // R,), w_down,
        pl.BlockSpec((None, R, D), lambda r: (li, r, 0)),
        pl.BlockSpec((R, D), lambda r: (r, 0)), (D_FF, D))
    return w5, wu, wd


def _ada_kernel(cv_ref, w_ref, b_ref, o_ref):
    cv = cv_ref[...]
    s = (cv * jax.nn.sigmoid(cv)).astype(bf16)
    o_ref[...] = _dot(s, w_ref[...].astype(bf16)) + b_ref[...]


def _adaln_all(cv, w_ada, b_ada):
    n6 = 6 * D_MODEL
    return pl.pallas_call(
        _ada_kernel,
        grid=(DEPTH, n6 // TN_ADA),
        in_specs=[pl.BlockSpec((N_MOD_ROWS, D_MODEL), lambda l, j: (0, 0)),
                  pl.BlockSpec((None, D_MODEL, TN_ADA), lambda l, j: (l, 0, j)),
                  pl.BlockSpec((None, 1, TN_ADA), lambda l, j: (l, 0, j))],
        out_specs=pl.BlockSpec((None, N_MOD_ROWS, TN_ADA), lambda l, j: (l, 0, j)),
        out_shape=jax.ShapeDtypeStruct((DEPTH, N_MOD_ROWS, n6), f32),
        compiler_params=_params("parallel", "parallel"),
        name="adaln",
    )(cv, w_ada, b_ada.reshape(DEPTH, 1, n6))


def _mod_spec(li, k, tm, is_ctx):
    def index(i, j):
        r = CTX_MOD_ROW if is_ctx else (i * tm) // LAT_SEQ
        return ((li * N_MOD_ROWS + r) * 6 + k, 0, 0)

    return pl.BlockSpec((None, 1, D_MODEL), index)


def _inproj_kernel(x_ref, sh_ref, sc_ref, g_ref, w_ref, *rest, kv_j0, kv_nj, emit_kv):
    o_ref, h_ref = rest[-3 if emit_kv else -2], rest[-1]
    j = pl.program_id(1)

    @pl.when(j == 0)
    def _():
        h_ref[...] = _modulated_norm(x_ref[...], g_ref[...], sc_ref[...], sh_ref[...]).astype(bf16)

    res = _dot(h_ref[...], w_ref[...])
    o_ref[...] = res.astype(bf16)

    if emit_kv:
        kv_ref = rest[-2]

        @pl.when((j >= kv_j0) & (j < kv_j0 + kv_nj))
        def _():
            kv_ref[...] = res.reshape(kv_ref.shape)


def _inproj(x, mod, g, w, li, is_ctx, kv_prev=None):
    T = x.shape[0]
    tm, tn = TM_INPROJ, TN_INPROJ
    kv_j0, kv_nj = KV_COL0 // tn, KV_COLS // tn

    in_specs = [pl.BlockSpec((tm, D_MODEL), lambda i, j: (i, 0)),
                _mod_spec(li, 0, tm, is_ctx), _mod_spec(li, 1, tm, is_ctx),
                pl.BlockSpec((1, D_MODEL), lambda i, j: (0, 0)),
                pl.BlockSpec((D_MODEL, tn), lambda i, j: (0, j))]
    args = [x, mod, mod, g, w]
    out_specs = [pl.BlockSpec((tm, tn), lambda i, j: (i, j)), pl.BlockSpec((tm, D_MODEL), lambda i, j: (i, 0))]
    out_shape = [jax.ShapeDtypeStruct((T, MIX_COLS), bf16), jax.ShapeDtypeStruct((T, D_MODEL), bf16)]
    aliases = {}
    if is_ctx:
        out_specs.insert(1, pl.BlockSpec((tm // CTX_SEQ, None, CTX_SEQ, tn),
                                         lambda i, j: (i, li, 0, jnp.clip(j - kv_j0, 0, kv_nj - 1))))
        out_shape.insert(1, jax.ShapeDtypeStruct((T // CTX_SEQ, DEPTH, CTX_SEQ, KV_COLS), f32))
        if kv_prev is not None:
            in_specs.append(pl.BlockSpec(memory_space=pl.ANY))
            args.append(kv_prev)
            aliases = {len(args) - 1: 1}
    return pl.pallas_call(
        functools.partial(_inproj_kernel, kv_j0=kv_j0, kv_nj=kv_nj, emit_kv=is_ctx),
        grid=(T // tm, MIX_COLS // tn),
        in_specs=in_specs,
        out_specs=out_specs,
        out_shape=out_shape,
        input_output_aliases=aliases,
        compiler_params=_params("parallel", "arbitrary"),
        name="inproj",
    )(*args)


def _merge_kernel(x_ref, gt_ref, h_ref, o_ref, w_ref, wb_ref, sh2_ref, sc2_ref, g2_ref, out_ref, h2_ref, acc_ref):
    n = pl.program_id(1)

    def contribution():
        return jax.nn.sigmoid(_dot(h_ref[...], w_ref[...])) * _dot(o_ref[...], wb_ref[...])

    @pl.when(n == 0)
    def _():
        acc_ref[...] = contribution()

    @pl.when((n > 0) & (n < N_BRANCH))
    def _():
        acc_ref[...] += contribution()

    @pl.when(n == N_BRANCH)
    def _():
        half = x_ref.shape[0] // 2
        for r in range(2):
            rows = slice(r * half, (r + 1) * half)
            y = x_ref[rows, :] + gt_ref[...] * _dot(acc_ref[rows, :].astype(bf16), w_ref[...])
            out_ref[rows, :] = y
            h2_ref[rows, :] = _modulated_norm(y, g2_ref[...], sc2_ref[...], sh2_ref[...]).astype(bf16)


def _merge(x, mod, h, o_stack, w5, wb, g2, li, is_ctx):
    T = x.shape[0]
    tm = TM_MERGE
    last = N_BRANCH - 1
    row = pl.BlockSpec((tm, D_MODEL), lambda i, n: (i, 0))
    return pl.pallas_call(
        _merge_kernel,
        grid=(T // tm, N_BRANCH + 1),
        in_specs=[row, _mod_spec(li, 2, tm, is_ctx), row,
                  pl.BlockSpec((None, tm, BRANCH_WIDTH), lambda i, n: (jnp.minimum(n, last), i, 0)),
                  pl.BlockSpec((None, D_MODEL, D_MODEL), lambda i, n: (n, 0, 0)),
                  pl.BlockSpec((None, BRANCH_WIDTH, D_MODEL), lambda i, n: (jnp.minimum(n, last), 0, 0)),
                  _mod_spec(li, 3, tm, is_ctx), _mod_spec(li, 4, tm, is_ctx),
                  pl.BlockSpec((1, D_MODEL), lambda i, n: (0, 0))],
        out_specs=[row, row],
        out_shape=[jax.ShapeDtypeStruct((T, D_MODEL), f32), jax.ShapeDtypeStruct((T, D_MODEL), bf16)],
        scratch_shapes=[pltpu.VMEM((tm, D_MODEL), f32)],
        compiler_params=_params("parallel", "arbitrary"),
        name="merge",
    )(x, mod, h, o_stack, w5, wb, mod, mod, g2)


def _mlp_kernel(x_ref, h_ref, gt_ref, wu_ref, wd_ref, gf_ref, out_ref, acc_ref, *, final_norm):
    j = pl.program_id(1)

    def chunk():
        a = jnp.maximum(_dot(h_ref[...], wu_ref[...]), 0.0)
        return _dot((a * a).astype(bf16), wd_ref[...])

    @pl.when(j == 0)
    def _():
        acc_ref[...] = chunk()

    @pl.when(j > 0)
    def _():
        acc_ref[...] += chunk()

    @pl.when(j == pl.num_programs(1) - 1)
    def _():
        y = x_ref[...] + gt_ref[...] * acc_ref[...]
        if final_norm:
            y = y * lax.rsqrt(jnp.mean(y * y, axis=-1, keepdims=True) + EPS) * gf_ref[...]
        out_ref[...] = y


def _mlp(x, h2, mod, w_up, w_down, g_final, li, is_ctx, final_norm):
    T = x.shape[0]
    tm, tf = TM_MLP, TF_MLP
    row = pl.BlockSpec((tm, D_MODEL), lambda i, j: (i, 0))
    return pl.pallas_call(
        functools.partial(_mlp_kernel, final_norm=final_norm),
        grid=(T // tm, D_FF // tf),
        in_specs=[row, row, _mod_spec(li, 5, tm, is_ctx),
                  pl.BlockSpec((None, D_MODEL, tf), lambda i, j: (j, 0, 0)),
                  pl.BlockSpec((tf, D_MODEL), lambda i, j: (j, 0)),
                  pl.BlockSpec((1, D_MODEL), lambda i, j: (0, 0))],
        out_specs=row,
        out_shape=jax.ShapeDtypeStruct((T, D_MODEL), f32),
        scratch_shapes=[pltpu.VMEM((tm, D_MODEL), f32)],
        compiler_params=_params("parallel", "arbitrary"),
        name="mlp",
    )(x, h2, mod, w_up, w_down, g_final)


def _ctx_attn_kernel(aq_ref, ak_ref, av_ref, bq_ref, bk_ref, bv_ref, cq_ref, ckv_ref,
                     dl_ref, dg_ref, sink_ref, o_ref, *, lam_init):
    lam = _diff_lambda(dl_ref, lam_init)
    q = aq_ref[...].astype(f32) * QK_SCALE2
    outs = [_diff_head(q, lambda t: ak_ref[:, t * PAIR:(t + 1) * PAIR], lambda h: av_ref[:, h * A_V:(h + 1) * A_V],
                       h, lam, dg_ref[...], lam_init) for h in range(A_HEADS)]
    o_ref[0] = jnp.concatenate(outs, axis=1).astype(bf16)
    q = bq_ref[...].astype(f32) * QK_SCALE2
    tiles = []
    for t in range(B_HEADS // 2):
        cols = slice(t * PAIR, (t + 1) * PAIR)
        accs = []
        for half in range(2):
            (e,), _ = _exp2_parts([_dot_nt(_head_q(q[:, cols], half), bk_ref[:, cols])])
            accs.append(_dot(e.astype(bf16), _values_with_ones(bv_ref[:, cols], half)))
        tiles.append(_pair_out(*accs))
    o_ref[1] = jnp.concatenate(tiles, axis=1).astype(bf16)
    q = cq_ref[...].astype(f32) * QK_SCALE2
    k_tile = ckv_ref[:, 0:PAIR]
    v_variants = _swap_halves_variants(ckv_ref[:, PAIR:2 * PAIR].astype(f32))
    tiles = []
    for t in range(C_HEADS // 2):
        q_tile = q[:, t * PAIR:(t + 1) * PAIR]
        q_swapped = pltpu.roll(q_tile, HEAD_DIM, 1)
        accs, extras = [], []
        for half in range(2):
            h = 2 * t + half
            g = h // C_GROUP
            sink2 = sink_ref[h] * LOG2E
            (e,), m = _exp2_parts([_dot_nt(_gqa_head_q(q_tile, q_swapped, half, g), k_tile)], sink2)
            accs.append(_dot(e.astype(bf16), v_variants[g * 2 + half]))
            extras.append(jnp.exp2(sink2 - m))
        tiles.append(_pair_out(accs[0], accs[1], extras[0], extras[1]))
    o_ref[2] = jnp.concatenate(tiles, axis=1).astype(bf16)


def _ctx_attn(p, dl, dg, sink, li):
    nb = p.shape[0] // CTX_SEQ
    S = CTX_SEQ
    lam_init = 0.8 - 0.6 * math.exp(-0.3 * li)

    def col(c):
        return pl.BlockSpec((S, 512), lambda b, c=c: (b, c))

    return pl.pallas_call(
        functools.partial(_ctx_attn_kernel, lam_init=lam_init),
        grid=(nb,),
        in_specs=[col(COL_AQ), col(COL_AK), col(COL_AV), col(COL_BQ), col(COL_BK), col(COL_BV), col(COL_CQ),
                  pl.BlockSpec((S, 256), lambda b: (b, COL_CKV_256)),
                  pl.BlockSpec((4, A_QK), lambda b: (0, 0)),
                  pl.BlockSpec((1, A_V), lambda b: (0, 0)),
                  pl.BlockSpec(memory_space=pltpu.SMEM)],
        out_specs=pl.BlockSpec((N_BRANCH - 1, S, BRANCH_WIDTH), lambda b: (0, b, 0)),
        out_shape=jax.ShapeDtypeStruct((N_BRANCH, p.shape[0], BRANCH_WIDTH), bf16),
        compiler_params=_params("parallel"),
        name="ctx_attn",
    )(p, p, p, p, p, p, p, p, dl, dg, sink)


def _lat_a_kernel(q_ref, k_ref, v_ref, ck_ref, cv_ref, cosq_ref, sinq_ref, cosk_ref, sink_ref,
                  dl_ref, dg_ref, o_ref, kk_ref, vv_ref, *, lam_init):
    L = LAT_SEQ

    @pl.when(pl.program_id(1) == 0)
    def _():
        kk_ref[0:L, :] = _rope(k_ref[...], cosk_ref[...], sink_ref[...]).astype(bf16)
        kk_ref[L:, :] = ck_ref[...].astype(bf16)
        vv_ref[0:L, :] = v_ref[...].astype(bf16)
        vv_ref[L:, :] = cv_ref[...].astype(bf16)

    lam = _diff_lambda(dl_ref, lam_init)
    q = _rope(q_ref[...], cosq_ref[...], sinq_ref[...]) * QK_SCALE2
    outs = [_diff_head(q, lambda t: kk_ref[:, t * PAIR:(t + 1) * PAIR], lambda h: vv_ref[:, h * A_V:(h + 1) * A_V],
                       h, lam, dg_ref[...], lam_init) for h in range(A_HEADS)]
    o_ref[...] = jnp.concatenate(outs, axis=1).astype(bf16)


def _lat_b_kernel(q_ref, k_ref, v_ref, ck_ref, cv_ref, bias_ref, _, o_ref, kk_ref, va_ref):
    L, W = LAT_SEQ, NA_KEY_ROWS * GRID_W
    qb = pl.program_id(1)

    @pl.when(qb == 0)
    def _():
        kk_ref[0:L, :] = k_ref[...]
        kk_ref[L:, :] = ck_ref[...].astype(bf16)
        for t in range(B_HEADS // 2):
            cols = slice(t * PAIR, (t + 1) * PAIR)
            for half in range(2):
                va_ref[2 * t + half, 0:L, :] = _values_with_ones(v_ref[:, cols], half)
                va_ref[2 * t + half, L:, :] = _values_with_ones(cv_ref[:, cols], half)

    start = pl.multiple_of(_na_key_row0(qb) * GRID_W, 256)
    q = q_ref[...].astype(f32) * QK_SCALE2
    tiles = []
    for t in range(B_HEADS // 2):
        cols = slice(t * PAIR, (t + 1) * PAIR)
        accs = []
        for half in range(2):
            h = 2 * t + half
            qh = _head_q(q[:, cols], half)
            s_loc = _dot_nt(qh, kk_ref[pl.ds(start, W), cols]) + bias_ref[h].astype(f32)
            s_ctx = _dot_nt(qh, kk_ref[L:, cols])
            (e_loc, e_ctx), _m = _exp2_parts([s_loc, s_ctx])
            accs.append(_dot(e_loc.astype(bf16), va_ref[h, pl.ds(start, W), :])
                        + _dot(e_ctx.astype(bf16), va_ref[h, L:, :]))
        tiles.append(_pair_out(*accs))
    o_ref[...] = jnp.concatenate(tiles, axis=1).astype(bf16)


def _lat_c_kernel(q_ref, kv_ref, ck_ref, cv_ref, cosq_ref, sinq_ref, cosk_ref, sink_ref, snk_ref,
                  _, o_ref, kk_ref, va_ref):
    L, W = LAT_SEQ, SWA_KEYS
    tq = q_ref.shape[0]
    qb = pl.program_id(1)

    @pl.when(qb == 0)
    def _():
        kk_ref[0:L, :] = _rope(kv_ref[:, 0:PAIR], cosk_ref[...], sink_ref[...]).astype(bf16)
        kk_ref[L:, :] = ck_ref[...].astype(bf16)
        lat = _swap_halves_variants(kv_ref[:, PAIR:2 * PAIR].astype(f32))
        ctx = _swap_halves_variants(cv_ref[...])
        for idx in range(4):
            va_ref[idx, 0:L, :] = lat[idx]
            va_ref[idx, L:, :] = ctx[idx]

    start = pl.multiple_of(jnp.clip(qb * tq - SWA_WINDOW, 0, L - W), 128)
    q = _rope(q_ref[...], cosq_ref[...], sinq_ref[...]) * QK_SCALE2
    qpos = qb * tq + lax.broadcasted_iota(jnp.int32, (tq, W), 0)
    kpos = start + lax.broadcasted_iota(jnp.int32, (tq, W), 1)
    valid = jnp.abs(qpos - kpos) <= SWA_WINDOW
    tiles = []
    for t in range(C_HEADS // 2):
        q_tile = q[:, t * PAIR:(t + 1) * PAIR]
        q_swapped = pltpu.roll(q_tile, HEAD_DIM, 1)
        accs, extras = [], []
        for half in range(2):
            h = 2 * t + half
            g = h // C_GROUP
            qh = _gqa_head_q(q_tile, q_swapped, half, g)
            s_loc = jnp.where(valid, _dot_nt(qh, kk_ref[pl.ds(start, W), :]), NEG_INF)
            s_ctx = _dot_nt(qh, kk_ref[L:, :])
            sink2 = snk_ref[h] * LOG2E
            (e_loc, e_ctx), m = _exp2_parts([s_loc, s_ctx], sink2)
            idx = g * 2 + half
            accs.append(_dot(e_loc.astype(bf16), va_ref[idx, pl.ds(start, W), :])
                        + _dot(e_ctx.astype(bf16), va_ref[idx, L:, :]))
            extras.append(jnp.exp2(sink2 - m))
        tiles.append(_pair_out(accs[0], accs[1], extras[0], extras[1]))
    o_ref[...] = jnp.concatenate(tiles, axis=1).astype(bf16)


def _lat_attn(p, caches, na_dense, dl, dg, sink, cos_t, sin_t, li):
    cak, cav, cbk, cbv, cck, ccv = caches
    L, tq = LAT_SEQ, TQ_ATTN
    assert tq == NA_Q_ROWS * GRID_W and SWA_KEYS >= tq + 2 * SWA_WINDOW
    nq = L // tq
    nb_lat = p.shape[0] // L
    lam_init = 0.8 - 0.6 * math.exp(-0.3 * li)

    def qcol(c):
        return pl.BlockSpec((tq, 512), lambda b, i, c=c: (b * nq + i, c))

    def kcol(c, width=512):
        return pl.BlockSpec((L, width), lambda b, i, c=c: (b, c))

    def cache(width):
        return pl.BlockSpec((None, None, CTX_SEQ, width), lambda b, i: (b, li, 0, 0))

    tab_q = pl.BlockSpec((tq, 512), lambda b, i: (i, 0))
    tab_k = pl.BlockSpec((L, 512), lambda b, i: (0, 0))
    tab_k128 = pl.BlockSpec((L, 128), lambda b, i: (0, 0))
    o_sds = jax.ShapeDtypeStruct((N_BRANCH, p.shape[0], BRANCH_WIDTH), bf16)
    anyspec = pl.BlockSpec(memory_space=pl.ANY)

    def o_spec(branch):
        return pl.BlockSpec((None, tq, BRANCH_WIDTH), lambda b, i: (branch, b * nq + i, 0))

    kv_scratch = [pltpu.VMEM((L + CTX_SEQ, 512), bf16), pltpu.VMEM((L + CTX_SEQ, 512), bf16)]
    smem = pl.BlockSpec(memory_space=pltpu.SMEM)
    cp = _params("parallel", "arbitrary")

    o_stack = pl.pallas_call(
        functools.partial(_lat_a_kernel, lam_init=lam_init),
        grid=(nb_lat, nq),
        in_specs=[qcol(COL_AQ), kcol(COL_AK), kcol(COL_AV), cache(512), cache(512),
                  tab_q, tab_q, tab_k, tab_k,
                  pl.BlockSpec((4, A_QK), lambda b, i: (0, 0)), pl.BlockSpec((1, A_V), lambda b, i: (0, 0))],
        out_specs=o_spec(0), out_shape=o_sds, scratch_shapes=kv_scratch,
        compiler_params=cp, name="lat_attn_a",
    )(p, p, p, cak, cav, cos_t, sin_t, cos_t, sin_t, dl, dg)

    o_stack = pl.pallas_call(
        _lat_b_kernel,
        grid=(nb_lat, nq),
        in_specs=[qcol(COL_BQ), kcol(COL_BK), kcol(COL_BV), cache(512), cache(512),
                  pl.BlockSpec((B_HEADS, tq, NA_KEY_ROWS * GRID_W), lambda b, i: (0, i, 0)), anyspec],
        out_specs=o_spec(1), out_shape=o_sds,
        scratch_shapes=[pltpu.VMEM((L + CTX_SEQ, 512), bf16), pltpu.VMEM((B_HEADS, L + CTX_SEQ, PAIR), bf16)],
        input_output_aliases={6: 0}, compiler_params=cp, name="lat_attn_b",
    )(p, p, p, cbk, cbv, na_dense, o_stack)

    return pl.pallas_call(
        _lat_c_kernel,
        grid=(nb_lat, nq),
        in_specs=[qcol(COL_CQ), kcol(COL_CKV_256, 256), cache(PAIR), cache(PAIR),
                  tab_q, tab_q, tab_k128, tab_k128, smem, anyspec],
        out_specs=o_spec(2), out_shape=o_sds,
        scratch_shapes=[pltpu.VMEM((L + CTX_SEQ, PAIR), bf16), pltpu.VMEM((4, L + CTX_SEQ, PAIR), bf16)],
        input_output_aliases={9: 0}, compiler_params=cp, name="lat_attn_c",
    )(p, p, cck, ccv, cos_t, sin_t, cos_t, sin_t, sink, o_stack)


def _na_key_row0(qb):
    lo, hi = NA_Q_ROWS * qb - NA_ROWS // 2, GRID_ROWS - NA_KEY_ROWS
    return min(max(lo, 0), hi) if isinstance(qb, int) else jnp.clip(lo, 0, hi)


def _na_bias_kernel(rb_ref, o_ref):
    h = pl.program_id(0)
    W = GRID_W
    n_dc = 2 * NA_COLS - 1
    n_dr = 2 * NA_ROWS - 1
    qc = lax.broadcasted_iota(jnp.int32, (W, 2 * W), 0)
    lane = lax.broadcasted_iota(jnp.int32, (W, 2 * W), 1)
    second = lane >= W
    kc = jnp.where(second, lane - W, lane)
    dc = jnp.clip(kc - qc, -(NA_COLS - 1), NA_COLS - 1) + NA_COLS - 1
    c0 = jnp.clip(qc - NA_COLS // 2, 0, W - NA_COLS)
    col_ok = (kc >= c0) & (kc < c0 + NA_COLS)
    base = h * (n_dr * n_dc)

    def pair_tile(dr0):
        t = jnp.zeros((W, 2 * W), f32)
        for j in range(n_dc):
            lo = rb_ref[base + dr0 * n_dc + j] if 0 <= dr0 < n_dr else 0.0
            hi = rb_ref[base + (dr0 + 1) * n_dc + j] if 0 <= dr0 + 1 < n_dr else 0.0
            t = jnp.where(dc == j, jnp.where(second, hi, lo), t)
        return jnp.where(col_ok, t * LOG2E, NEG_INF)

    tiles = {dr0: pair_tile(dr0) for dr0 in range(-1, n_dr)}
    neg = jnp.full((W, 2 * W), NEG_INF, f32)
    for qr in range(GRID_ROWS):
        r0 = min(max(qr - NA_ROWS // 2, 0), GRID_ROWS - NA_ROWS)
        k0 = _na_key_row0(qr // NA_Q_ROWS)
        assert k0 % 2 == 0 and k0 <= r0 and r0 + NA_ROWS <= k0 + NA_KEY_ROWS
        for pr in range(NA_KEY_ROWS // 2):
            kr = k0 + 2 * pr
            ok0 = r0 <= kr < r0 + NA_ROWS
            ok1 = r0 <= kr + 1 < r0 + NA_ROWS
            if not (ok0 or ok1):
                t = neg
            else:
                t = tiles[kr - qr + NA_ROWS - 1]
                if not ok0:
                    t = jnp.where(second, t, NEG_INF)
                if not ok1:
                    t = jnp.where(second, NEG_INF, t)
            o_ref[qr * W:(qr + 1) * W, 2 * pr * W:(2 * pr + 2) * W] = t.astype(bf16)


def _na_bias_dense(rel_bias):
    kw = NA_KEY_ROWS * GRID_W
    return pl.pallas_call(
        _na_bias_kernel,
        grid=(B_HEADS,),
        in_specs=[pl.BlockSpec(memory_space=pltpu.SMEM)],
        out_specs=pl.BlockSpec((None, LAT_SEQ, kw), lambda h: (h, 0, 0)),
        out_shape=jax.ShapeDtypeStruct((B_HEADS, LAT_SEQ, kw), bf16),
        compiler_params=_params("parallel"),
        name="na_bias",
    )(rel_bias.reshape(-1))


def _hy_filter_kernel(z_ref, w1_ref, b1_ref, w2_ref, b2_ref, fr_ref, w3f_ref, w3b_ref, dcf_ref, dcb_ref,
                      bias_ref, fc_ref, fs_ref, o_ref, *, L):
    z = z_ref[...]
    fr = fr_ref[...]
    h = jnp.sin(fr * (_dot_hi(z, w1_ref[...]) + b1_ref[...]))
    h = jnp.sin(fr * (_dot_hi(h, w2_ref[...]) + b2_ref[...]))
    t = z[:, 0:1]
    hf = _dot_hi(h, w3f_ref[...]) * jnp.exp(-t * jnp.abs(dcf_ref[...]))
    hb = _dot_hi(h, w3b_ref[...]) * jnp.exp(-t * jnp.abs(dcb_ref[...]))
    ssum = hf + hb
    bias = bias_ref[...]
    ga = _dot_3pass(fc_ref[...], ssum) + bias
    gb = _dot_3pass(fs_ref[...], hb - hf)
    g_nyq = _dot_hi(fs_ref[0:8, :], ssum)[0:1] + bias
    row0 = lax.broadcasted_iota(jnp.int32, ga.shape, 0) == 0
    inv = 1.0 / L
    o_ref[0] = jnp.where(row0, 0.5 * ga, ga) * inv
    o_ref[1] = jnp.where(row0, 0.0, gb) * inv
    o_ref[2] = jnp.where(row0, 0.5 * g_nyq, ga) * inv


def _hy_filter_tables(L, w1p, b1, w2, b2, w3, freq, decay, hy_bias):
    cb = HY_CB
    ncb = HY_WIDTH // cb
    z = jnp.asarray(_hyena_features(L))
    fc, fs = (jnp.asarray(a) for a in _dft_tables(L))
    full = lambda shape: pl.BlockSpec(shape, lambda o, c: (0,) * len(shape))
    fwd = lambda rows: pl.BlockSpec((rows, cb), lambda o, c: (0, o * 2 * ncb + c))
    bwd = lambda rows: pl.BlockSpec((rows, cb), lambda o, c: (0, o * 2 * ncb + ncb + c))
    return pl.pallas_call(
        functools.partial(_hy_filter_kernel, L=L),
        grid=(HY_ORDER, ncb),
        in_specs=[full((L, HY_EMB_PAD)), full((HY_EMB_PAD, HY_FFN)), full((1, HY_FFN)),
                  full((HY_FFN, HY_FFN)), full((1, HY_FFN)), full((1, HY_FFN)),
                  fwd(HY_FFN), bwd(HY_FFN), fwd(1), bwd(1),
                  pl.BlockSpec((None, 1, cb), lambda o, c: (o, 0, c)),
                  full((L, L)), full((L, L))],
        out_specs=pl.BlockSpec((None, 3, L, cb), lambda o, c: (o, 0, 0, c)),
        out_shape=jax.ShapeDtypeStruct((HY_ORDER, 3, L, HY_WIDTH), f32),
        compiler_params=_params("parallel", "parallel"),
        name="hy_filter",
    )(z, w1p, b1, w2, b2, freq, w3, w3, decay, decay, hy_bias.reshape(HY_ORDER, 1, HY_WIDTH), fc, fs)


def _hyena_kernel(v_ref, x1_ref, x2_ref, wv_ref, w1_ref, w2_ref, fwd_ref, inv_ref, tab_ref, _, o_ref):
    L = inv_ref.shape[0]
    cb = v_ref.shape[1]
    n_seq = v_ref.shape[0] // L
    row = lax.broadcasted_iota(jnp.int32, (L, cb), 0)

    def short_conv(u_ref, w_ref):
        w = w_ref[...]
        outs = []
        for s in range(n_seq):
            u = u_ref[s * L:(s + 1) * L, :].astype(f32)
            prev = jnp.where(row == 0, 0.0, pltpu.roll(u, 1, 0))
            nxt = jnp.where(row == L - 1, 0.0, pltpu.roll(u, L - 1, 0))
            outs.append(prev * w[0:1] + u * w[1:2] + nxt * w[2:3])
        return jnp.concatenate(outs, axis=1)

    def long_conv(u, order):
        ab = _dot(fwd_ref[...], u.astype(bf16))
        t0, t1, t2 = tab_ref[order, 0], tab_ref[order, 1], tab_ref[order, 2]
        pq = []
        for s in range(n_seq):
            a, b = ab[:L, s * cb:(s + 1) * cb], ab[L:, s * cb:(s + 1) * cb]
            pq.append(jnp.concatenate([(a * t0 + b * t1).astype(bf16), (b * t2 - a * t1).astype(bf16)], axis=0))
        return _dot(inv_ref[...], jnp.concatenate(pq, axis=1))

    z = short_conv(x1_ref, w1_ref) * long_conv(short_conv(v_ref, wv_ref), 0)
    y = (short_conv(x2_ref, w2_ref) * long_conv(z, 1)).astype(bf16)
    o_ref[...] = jnp.concatenate([y[:, s * cb:(s + 1) * cb] for s in range(n_seq)], axis=0)


def _hyena(p, o_stack, hy_short, tabs, L):
    cb = HY_CB
    ncb = HY_WIDTH // cb
    c0 = HY_COL0 // cb
    rows = min(HY_SEQS_PER_STEP[L] * L, p.shape[0])
    nb = p.shape[0] // rows
    fc, fs = _dft_tables(L)
    fwd = jnp.asarray(np.concatenate([fc, fs], axis=0), bf16)
    inv = jnp.asarray(np.concatenate([fc, fs.T], axis=1), bf16)

    def part(k):
        return pl.BlockSpec((rows, cb), lambda c, b, k=k: (b, c0 + k * ncb + c))

    def wpart(k):
        return pl.BlockSpec((3, cb), lambda c, b, k=k: (0, k * ncb + c))

    return pl.pallas_call(
        _hyena_kernel,
        grid=(ncb, nb),
        in_specs=[part(0), part(1), part(2), wpart(0), wpart(1), wpart(2),
                  pl.BlockSpec((2 * L, L), lambda c, b: (0, 0)), pl.BlockSpec((L, 2 * L), lambda c, b: (0, 0)),
                  pl.BlockSpec((HY_ORDER, 3, L, cb), lambda c, b: (0, 0, 0, c)),
                  pl.BlockSpec(memory_space=pl.ANY)],
        out_specs=pl.BlockSpec((None, rows, cb), lambda c, b: (N_BRANCH - 1, b, c)),
        out_shape=jax.ShapeDtypeStruct(o_stack.shape, o_stack.dtype),
        input_output_aliases={9: 0},
        compiler_params=_params("parallel", "arbitrary"),
        name="hyena_%d" % L,
    )(p, p, p, hy_short, hy_short, hy_short, fwd, inv, tabs, o_stack)


def kernel(x_prompt, x_sample, cache_a_k, cache_a_v, cache_b_k, cache_b_v, cache_c_k, cache_c_v, c, c_ctx, w_ada, b_ada, g_mix, w_in, diff_lambda, diff_norm_g, na_bias, swa_sink, hy_short, hy_w1, hy_b1, hy_w2, hy_b2, hy_w3, hy_freq, hy_decay, hy_bias, w_branch, w_out, g_mlp, w_up, w_down, g_final):
    nb_ctx, nb_lat = x_prompt.shape[0], x_sample.shape[0]
    assert x_prompt.shape[1:] == (CTX_SEQ, D_MODEL) and x_sample.shape[1:] == (LAT_SEQ, D_MODEL)
    assert nb_lat <= CTX_MOD_ROW and (nb_ctx * CTX_SEQ) % TM_INPROJ == 0

    x_ctx = x_prompt.reshape(nb_ctx * CTX_SEQ, D_MODEL)
    x_lat = x_sample.reshape(nb_lat * LAT_SEQ, D_MODEL)
    cv = jnp.zeros((N_MOD_ROWS, D_MODEL), f32).at[:nb_lat].set(c).at[CTX_MOD_ROW].set(c_ctx)
    mod = _adaln_all(cv, w_ada, b_ada).reshape(DEPTH * N_MOD_ROWS * 6, 1, D_MODEL)
    cos_t, sin_t = (jnp.asarray(a) for a in _rope_tables())
    caches = (cache_a_k.reshape(nb_lat, DEPTH, CTX_SEQ, 512), cache_a_v.reshape(nb_lat, DEPTH, CTX_SEQ, 512),
              cache_b_k.reshape(nb_lat, DEPTH, CTX_SEQ, 512), cache_b_v.reshape(nb_lat, DEPTH, CTX_SEQ, 512),
              cache_c_k.reshape(nb_lat, DEPTH, CTX_SEQ, 128), cache_c_v.reshape(nb_lat, DEPTH, CTX_SEQ, 128))

    kv = None
    for li in range(DEPTH):
        w_mix = _cast_mix_weights(w_in, li)
        w5, wu, wd = _cast_layer_weights(w_in, w_out, w_up, w_down, li)
        wb = w_branch[li].astype(bf16)
        g1 = g_mix[li].reshape(1, D_MODEL)
        g2 = g_mlp[li].reshape(1, D_MODEL)
        dl = diff_lambda[li]
        dg = diff_norm_g[li].reshape(1, A_V)
        sink = swa_sink[li]

        w1p = jnp.pad(hy_w1[li], ((0, HY_EMB_PAD - HY_EMB), (0, 0)))
        hy_args = (w1p, hy_b1[li].reshape(1, HY_FFN), hy_w2[li], hy_b2[li].reshape(1, HY_FFN), hy_w3[li],
                   hy_freq[li].reshape(1, HY_FFN), hy_decay[li].reshape(1, -1), hy_bias[li])
        gf = g_final.reshape(1, D_MODEL)
        final = li == DEPTH - 1

        p, kv, h = _inproj(x_ctx, mod, g1, w_mix, li, True, kv)
        o_stack = _ctx_attn(p, dl, dg, sink, li)
        o_stack = _hyena(p, o_stack, hy_short[li], _hy_filter_tables(CTX_SEQ, *hy_args), CTX_SEQ)
        x_ctx, h2 = _merge(x_ctx, mod, h, o_stack, w5, wb, g2, li, True)
        x_ctx = _mlp(x_ctx, h2, mod, wu, wd, gf, li, True, final_norm=final)

        p, h = _inproj(x_lat, mod, g1, w_mix, li, False)
        o_stack = _lat_attn(p, caches, _na_bias_dense(na_bias[li]), dl, dg, sink, cos_t, sin_t, li)
        o_stack = _hyena(p, o_stack, hy_short[li], _hy_filter_tables(LAT_SEQ, *hy_args), LAT_SEQ)
        x_lat, h2 = _merge(x_lat, mod, h, o_stack, w5, wb, g2, li, False)
        x_lat = _mlp(x_lat, h2, mod, wu, wd, gf, li, False, final_norm=final)

    y_prompt = x_ctx.reshape(nb_ctx, CTX_SEQ, D_MODEL)
    y_sample = x_lat.reshape(nb_lat, LAT_SEQ, D_MODEL)
    kv_shapes = ((0, 512, (2, A_HEADS, A_QK)), (512, 1024, (A_HEADS, A_V)),
                 (1024, 1536, (B_HEADS, HEAD_DIM)), (1536, 2048, (B_HEADS, HEAD_DIM)),
                 (2048, 2176, (C_KV_HEADS, HEAD_DIM)), (2176, 2304, (C_KV_HEADS, HEAD_DIM)))
    new_kv = tuple(kv[..., a:b].reshape((nb_ctx, DEPTH, CTX_SEQ) + s) for a, b, s in kv_shapes)
    return (y_prompt, y_sample) + new_kv
```

```python
import functools
import math

import numpy as np
import jax
import jax.numpy as jnp
from jax import lax
from jax.experimental import pallas as pl
from jax.experimental.pallas import tpu as pltpu

f32 = jnp.float32
bf16 = jnp.bfloat16

D_MODEL = 2048
DEPTH = 2
CTX_SEQ = 256
LAT_SEQ = 1024
GRID_W = 64
GRID_ROWS = LAT_SEQ // GRID_W
BRANCH_WIDTH = 512
N_BRANCH = 4
HEAD_DIM = 64
A_QK = 64
A_V = 128
A_HEADS = 4
B_HEADS = 8
NA_ROWS = 8
NA_COLS = 16
C_HEADS = 8
C_KV_HEADS = 2
C_GROUP = 4
SWA_WINDOW = 128
SWA_KEYS = 512
NA_Q_ROWS = 4
NA_KEY_ROWS = 12
HY_WIDTH = 512
HY_ORDER = 2
HY_BANDS = 16
HY_EMB = 1 + 2 * HY_BANDS
HY_EMB_PAD = 128
HY_FFN = 64
D_FF = 4 * D_MODEL
ROPE_BASE = 10000.0
EPS = 1e-6
NEG_INF = -1e30
MIX_COLS = 5376
GATE_COLS = N_BRANCH * D_MODEL
COL_AQ, COL_BQ, COL_CQ, COL_AK, COL_AV, COL_BK, COL_BV = 0, 1, 2, 3, 4, 5, 6
COL_CKV_256 = 14
KV_PARTS = ((1536, 2048), (2048, 2560), (2560, 3072), (3072, 3584), (3584, 3712), (3712, 3840))
HY_COL0 = 3840
N_MOD_ROWS = 16
CTX_MOD_ROW = 8

VMEM_LIMIT = 56 * 1024 * 1024
TM_INPROJ = 256
TM_MERGE = 512
TM_MLP = 512
TF_MLP = 1024
TQ_ATTN = 256
TN_ADA = 1024
HY_CB = 256
HY_SEQS_PER_STEP = {CTX_SEQ: 8, LAT_SEQ: 2}
CAST_ROWS = 512


def _params(*sem):
    return pltpu.CompilerParams(dimension_semantics=sem, vmem_limit_bytes=VMEM_LIMIT)


def _dot(a, b):
    return jnp.dot(a, b, preferred_element_type=f32)


def _dot_nt(a, b):
    return lax.dot_general(a, b, (((1,), (1,)), ((), ())), preferred_element_type=f32)


def _dot_hi(a, b):
    return jnp.dot(a, b, preferred_element_type=f32, precision=lax.Precision.HIGHEST)


def _dot_3pass(a, b):
    a_hi, b_hi = a.astype(bf16), b.astype(bf16)
    a_lo = (a - a_hi.astype(f32)).astype(bf16)
    b_lo = (b - b_hi.astype(f32)).astype(bf16)
    return _dot(a_hi, b_hi) + (_dot(a_hi, b_lo) + _dot(a_lo, b_hi))


@functools.lru_cache(maxsize=None)
def _rope_tables():
    half = HEAD_DIM // 2
    nf = half // 2
    inv = ROPE_BASE ** (-np.arange(nf, dtype=np.float64) / nf)
    t = np.arange(LAT_SEQ)
    pos = np.stack([t // GRID_W, t % GRID_W], axis=1).astype(np.float64)
    lane = np.arange(HEAD_DIM)
    ang = pos[:, lane // half] * inv[lane % nf][None, :]
    first = (lane % half) < nf
    cos = np.cos(ang)
    sin = np.where(first[None, :], -np.sin(ang), np.sin(ang))
    reps = 512 // HEAD_DIM
    return (np.tile(cos, (1, reps)).astype(np.float32), np.tile(sin, (1, reps)).astype(np.float32))


@functools.lru_cache(maxsize=None)
def _dft_tables(L):
    f = np.arange(L, dtype=np.int64)
    prod = (f[:, None] * f[None, :]) % (2 * L)
    ang = np.pi * prod.astype(np.float64) / L
    fc = np.cos(ang)
    fs = np.sin(ang)
    fs[0, :] = np.where(f % 2 == 0, 1.0, -1.0)
    return fc.astype(np.float32), fs.astype(np.float32)


@functools.lru_cache(maxsize=None)
def _hyena_features(L):
    n = np.arange(L, dtype=np.float64)[:, None]
    t = n / max(L - 1, 1)
    w = 2.0 * math.pi * n / L
    bands = np.linspace(1e-4, HY_BANDS - 1, HY_BANDS, dtype=np.float64)[None, :]
    z = np.concatenate([t, np.cos(bands * w), -np.sin(bands * w)], axis=-1)
    z = np.pad(z, ((0, 0), (0, HY_EMB_PAD - HY_EMB)))
    return z.astype(np.float32)


def _modulated_norm(x, g, scale, shift):
    ms = jnp.mean(x * x, axis=-1, keepdims=True)
    return x * lax.rsqrt(ms + EPS) * (g * (1.0 + scale)) + shift


def _rope(x, cos, sin_signed):
    x = x.astype(f32)
    n = x.shape[-1]
    lane = lax.broadcasted_iota(jnp.int32, x.shape, 1)
    first = (lane & (HEAD_DIM // 2 - 1)) < (HEAD_DIM // 4)
    partner = jnp.where(first, pltpu.roll(x, n - HEAD_DIM // 4, 1), pltpu.roll(x, HEAD_DIM // 4, 1))
    return x * cos + partner * sin_signed


LOG2E = 1.4426950408889634
QK_SCALE2 = HEAD_DIM ** -0.5 * LOG2E
PAIR = 2 * HEAD_DIM


def _lo_half(shape):
    return lax.broadcasted_iota(jnp.int32, shape, 1) < HEAD_DIM


def _keep_half(x, half, fill):
    lo = _lo_half(x.shape)
    return jnp.where(lo if half == 0 else jnp.logical_not(lo), x, fill)


def _head_q(q_tile, half):
    return _keep_half(q_tile, half, 0.0).astype(bf16)


def _values_with_ones(v_tile, half):
    return _keep_half(v_tile.astype(f32), half, 1.0).astype(bf16)


def _exp2_parts(scores, sink2=None):
    m = jnp.max(scores[0], axis=-1, keepdims=True)
    for s in scores[1:]:
        m = jnp.maximum(m, jnp.max(s, axis=-1, keepdims=True))
    if sink2 is not None:
        m = jnp.maximum(m, sink2)
    return [jnp.exp2(s - m) for s in scores], m


def _pair_out(acc_even, acc_odd, extra_even=None, extra_odd=None):
    lo = _lo_half(acc_even.shape)
    num = jnp.where(lo, acc_even, acc_odd)
    den = pltpu.roll(jnp.where(lo, acc_odd, acc_even), HEAD_DIM, 1)
    if extra_even is not None:
        den = den + jnp.where(lo, extra_even, extra_odd)
    return num / den


def _swap_halves_variants(v_tile):
    sw = pltpu.roll(v_tile, HEAD_DIM, 1)
    lo = _lo_half(v_tile.shape)
    tiles = (jnp.where(lo, v_tile, 1.0), jnp.where(lo, 1.0, sw), jnp.where(lo, sw, 1.0), jnp.where(lo, 1.0, v_tile))
    return [t.astype(bf16) for t in tiles]


def _gqa_head_q(q_tile, q_swapped, half, g):
    return _keep_half(q_tile if half == g else q_swapped, g, 0.0).astype(bf16)


def _diff_lambda(dl_ref, lam_init):
    dl = dl_ref[...]
    a = jnp.sum(dl[0:1] * dl[1:2], axis=-1, keepdims=True)
    b = jnp.sum(dl[2:3] * dl[3:4], axis=-1, keepdims=True)
    return jnp.exp(a) - jnp.exp(b) + lam_init


def _diff_head(q, k_of, v_of, h, lam, dg, lam_init):
    res = []
    for c in range(2):
        t = c * (A_HEADS // 2) + h // 2
        s = _dot_nt(_head_q(q[:, t * PAIR:(t + 1) * PAIR], h % 2), k_of(t))
        (e,), _ = _exp2_parts([s])
        res.append((_dot(e.astype(bf16), v_of(h)), jnp.sum(e, axis=-1, keepdims=True)))
    o = res[0][0] * (1.0 / res[0][1]) - res[1][0] * (lam / res[1][1])
    o = o * lax.rsqrt(jnp.mean(o * o, axis=-1, keepdims=True) + EPS) * dg
    return o * (1.0 - lam_init)


def _cast_kernel(src_ref, *rest):
    o_ref = rest[-1]
    x = src_ref[...]
    o_ref[...] = x.reshape(o_ref.shape).astype(bf16)


def _cast_call(name, grid, src, in_spec, out_spec, out_shape, dst=None):
    in_specs, args, aliases = [in_spec], [src], {}
    if dst is not None:
        in_specs.append(pl.BlockSpec(memory_space=pl.ANY))
        args.append(dst)
        aliases = {1: 0}
    return pl.pallas_call(
        _cast_kernel, grid=grid, in_specs=in_specs, out_specs=out_spec,
        out_shape=jax.ShapeDtypeStruct(out_shape, bf16), input_output_aliases=aliases,
        compiler_params=_params(*(("parallel",) * len(grid))), name=name,
    )(*args)


def _cast_mix_kernel(unit_ref, src_ref, o_ref):
    o_ref[...] = src_ref[0].astype(bf16)


def _cast_mix_weights(w_in, li):
    U = 256
    src_unit = (0, 1, 6, 7, 12, 13, 2, 3, 4, 5, 8, 9, 10, 11) + tuple(range(14, MIX_COLS // U))
    return pl.pallas_call(
        _cast_mix_kernel,
        grid_spec=pltpu.PrefetchScalarGridSpec(
            num_scalar_prefetch=1, grid=(len(src_unit),),
            in_specs=[pl.BlockSpec((pl.Element(1), pl.Element(D_MODEL), pl.Element(U)),
                                   lambda u, unit: (li, 0, pl.multiple_of(unit[u] * U, U)))],
            out_specs=pl.BlockSpec((D_MODEL, U), lambda u, unit: (0, u))),
        out_shape=jax.ShapeDtypeStruct((D_MODEL, MIX_COLS), bf16),
        compiler_params=_params("parallel"), name="cast_mix",
    )(jnp.asarray(src_unit, jnp.int32), w_in)


def _cast_layer_weights(w_in, w_out, w_up, w_down, li):
    D, R, tf = D_MODEL, CAST_ROWS, TF_MLP
    nr = D // R
    w5 = _cast_call(
        "cast_gate", (N_BRANCH, nr), w_in,
        pl.BlockSpec((pl.Element(1), pl.Element(R), pl.Element(D)),
                     lambda n, r: (li, pl.multiple_of(r * R, R), pl.multiple_of(MIX_COLS + n * D, 128))),
        pl.BlockSpec((None, R, D), lambda n, r: (n, r, 0)), (N_BRANCH + 1, D, D))
    w5 = _cast_call(
        "cast_wout", (nr,), w_out,
        pl.BlockSpec((None, R, D), lambda r: (li, r, 0)),
        pl.BlockSpec((None, R, D), lambda r: (N_BRANCH, r, 0)), (N_BRANCH + 1, D, D), dst=w5)
    wu = _cast_call(
        "cast_wu", (D_FF // tf, nr), w_up,
        pl.BlockSpec((None, R, tf), lambda j, r: (li, r, j)),
        pl.BlockSpec((None, R, tf), lambda j, r: (j, r, 0)), (D_FF // tf, D, tf))
    wd = _cast_call(
        "cast_wd", (D_FF // R,), w_down,
        pl.BlockSpec((None, R, D), lambda r: (li, r, 0)),
        pl.BlockSpec((R, D), lambda r: (r, 0)), (D_FF, D))
    return w5, wu, wd


def _ada_kernel(cv_ref, w_ref, b_ref, o_ref):
    cv = cv_ref[...]
    s = (cv * jax.nn.sigmoid(cv)).astype(bf16)
    o_ref[...] = _dot(s, w_ref[...].astype(bf16)) + b_ref[...]


def _adaln_all(cv, w_ada, b_ada):
    n6 = 6 * D_MODEL
    return pl.pallas_call(
        _ada_kernel,
        grid=(DEPTH, n6 // TN_ADA),
        in_specs=[pl.BlockSpec((N_MOD_ROWS, D_MODEL), lambda l, j: (0, 0)),
                  pl.BlockSpec((None, D_MODEL, TN_ADA), lambda l, j: (l, 0, j)),
                  pl.BlockSpec((None, 1, TN_ADA), lambda l, j: (l, 0, j))],
        out_specs=pl.BlockSpec((None, N_MOD_ROWS, TN_ADA), lambda l, j: (l, 0, j)),
        out_shape=jax.ShapeDtypeStruct((DEPTH, N_MOD_ROWS, n6), f32),
        compiler_params=_params("parallel", "parallel"),
        name="adaln",
    )(cv, w_ada, b_ada.reshape(DEPTH, 1, n6))


def _mod_spec(li, k, tm, is_ctx):
    def index(i, *_):
        r = CTX_MOD_ROW if is_ctx else (i * tm) // LAT_SEQ
        return ((li * N_MOD_ROWS + r) * 6 + k, 0, 0)

    return pl.BlockSpec((None, 1, D_MODEL), index)


def _inproj_kernel(x_ref, sh_ref, sc_ref, g_ref, w_ref, *rest, emit_kv):
    outs = rest[-(2 + len(KV_PARTS)) if emit_kv else -2:]
    h = _modulated_norm(x_ref[...], g_ref[...], sc_ref[...], sh_ref[...]).astype(bf16)
    outs[-1][...] = h
    res = _dot(h, w_ref[...])
    outs[0][...] = res.astype(bf16)
    if emit_kv:
        for ref, (lo, hi) in zip(outs[1:-1], KV_PARTS):
            ref[...] = res[:, lo:hi]


def _inproj(x, mod, g, w, li, is_ctx, kv_prev=None):
    T = x.shape[0]
    tm = TM_INPROJ
    assert tm == CTX_SEQ
    row = lambda width: pl.BlockSpec((tm, width), lambda i: (i, 0))
    in_specs = [row(D_MODEL), _mod_spec(li, 0, tm, is_ctx), _mod_spec(li, 1, tm, is_ctx),
                pl.BlockSpec((1, D_MODEL), lambda i: (0, 0)),
                pl.BlockSpec((D_MODEL, MIX_COLS), lambda i: (0, 0), pipeline_mode=pl.Buffered(1))]
    args = [x, mod, mod, g, w]
    out_specs = [row(MIX_COLS), row(D_MODEL)]
    out_shape = [jax.ShapeDtypeStruct((T, MIX_COLS), bf16), jax.ShapeDtypeStruct((T, D_MODEL), bf16)]
    aliases = {}
    if is_ctx:
        for k, (lo, hi) in enumerate(KV_PARTS):
            out_specs.insert(1 + k, pl.BlockSpec((None, None, CTX_SEQ, hi - lo), lambda i: (i, li, 0, 0)))
            out_shape.insert(1 + k, jax.ShapeDtypeStruct((T // CTX_SEQ, DEPTH, CTX_SEQ, hi - lo), f32))
        if kv_prev is not None:
            for k, a in enumerate(kv_prev):
                in_specs.append(pl.BlockSpec(memory_space=pl.ANY))
                aliases[len(args)] = 1 + k
                args.append(a)
    outs = pl.pallas_call(
        functools.partial(_inproj_kernel, emit_kv=is_ctx),
        grid=(T // tm,),
        in_specs=in_specs,
        out_specs=out_specs,
        out_shape=out_shape,
        input_output_aliases=aliases,
        compiler_params=_params("parallel"),
        name="inproj",
    )(*args)
    return (outs[0], tuple(outs[1:-1]), outs[-1]) if is_ctx else (outs[0], outs[-1])


def _merge_kernel(x_ref, gt_ref, h_ref, o_ref, w_ref, wb_ref, sh2_ref, sc2_ref, g2_ref, out_ref, h2_ref, acc_ref):
    n = pl.program_id(1)

    def contribution():
        return jax.nn.sigmoid(_dot(h_ref[...], w_ref[...])) * _dot(o_ref[...], wb_ref[...])

    @pl.when(n == 0)
    def _():
        acc_ref[...] = contribution()

    @pl.when((n > 0) & (n < N_BRANCH))
    def _():
        acc_ref[...] += contribution()

    @pl.when(n == N_BRANCH)
    def _():
        half = x_ref.shape[0] // 2
        for r in range(2):
            rows = slice(r * half, (r + 1) * half)
            y = x_ref[rows, :] + gt_ref[...] * _dot(acc_ref[rows, :].astype(bf16), w_ref[...])
            out_ref[rows, :] = y
            h2_ref[rows, :] = _modulated_norm(y, g2_ref[...], sc2_ref[...], sh2_ref[...]).astype(bf16)


def _merge(x, mod, h, o_stack, w5, wb, g2, li, is_ctx):
    T = x.shape[0]
    tm = TM_MERGE
    last = N_BRANCH - 1
    row = pl.BlockSpec((tm, D_MODEL), lambda i, n: (i, 0))
    return pl.pallas_call(
        _merge_kernel,
        grid=(T // tm, N_BRANCH + 1),
        in_specs=[row, _mod_spec(li, 2, tm, is_ctx), row,
                  pl.BlockSpec((None, tm, BRANCH_WIDTH), lambda i, n: (jnp.minimum(n, last), i, 0)),
                  pl.BlockSpec((None, D_MODEL, D_MODEL), lambda i, n: (n, 0, 0)),
                  pl.BlockSpec((None, BRANCH_WIDTH, D_MODEL), lambda i, n: (jnp.minimum(n, last), 0, 0)),
                  _mod_spec(li, 3, tm, is_ctx), _mod_spec(li, 4, tm, is_ctx),
                  pl.BlockSpec((1, D_MODEL), lambda i, n: (0, 0))],
        out_specs=[row, row],
        out_shape=[jax.ShapeDtypeStruct((T, D_MODEL), f32), jax.ShapeDtypeStruct((T, D_MODEL), bf16)],
        scratch_shapes=[pltpu.VMEM((tm, D_MODEL), f32)],
        compiler_params=_params("parallel", "arbitrary"),
        name="merge",
    )(x, mod, h, o_stack, w5, wb, mod, mod, g2)


def _mlp_kernel(x_ref, h_ref, gt_ref, wu_ref, wd_ref, gf_ref, out_ref, acc_ref, *, final_norm):
    j = pl.program_id(1)

    def chunk():
        a = jnp.maximum(_dot(h_ref[...], wu_ref[...]), 0.0)
        return _dot((a * a).astype(bf16), wd_ref[...])

    @pl.when(j == 0)
    def _():
        acc_ref[...] = chunk()

    @pl.when(j > 0)
    def _():
        acc_ref[...] += chunk()

    @pl.when(j == pl.num_programs(1) - 1)
    def _():
        y = x_ref[...] + gt_ref[...] * acc_ref[...]
        if final_norm:
            y = y * lax.rsqrt(jnp.mean(y * y, axis=-1, keepdims=True) + EPS) * gf_ref[...]
        out_ref[...] = y


def _mlp(x, h2, mod, w_up, w_down, g_final, li, is_ctx, final_norm):
    T = x.shape[0]
    tm, tf = TM_MLP, TF_MLP
    row = pl.BlockSpec((tm, D_MODEL), lambda i, j: (i, 0))
    return pl.pallas_call(
        functools.partial(_mlp_kernel, final_norm=final_norm),
        grid=(T // tm, D_FF // tf),
        in_specs=[row, row, _mod_spec(li, 5, tm, is_ctx),
                  pl.BlockSpec((None, D_MODEL, tf), lambda i, j: (j, 0, 0)),
                  pl.BlockSpec((tf, D_MODEL), lambda i, j: (j, 0)),
                  pl.BlockSpec((1, D_MODEL), lambda i, j: (0, 0))],
        out_specs=row,
        out_shape=jax.ShapeDtypeStruct((T, D_MODEL), f32),
        scratch_shapes=[pltpu.VMEM((tm, D_MODEL), f32)],
        compiler_params=_params("parallel", "arbitrary"),
        name="mlp",
    )(x, h2, mod, w_up, w_down, g_final)


def _ctx_attn_kernel(aq_ref, ak_ref, av_ref, bq_ref, bk_ref, bv_ref, cq_ref, ckv_ref,
                     dl_ref, dg_ref, sink_ref, o_ref, *, lam_init):
    lam = _diff_lambda(dl_ref, lam_init)
    q = aq_ref[...].astype(f32) * QK_SCALE2
    outs = [_diff_head(q, lambda t: ak_ref[:, t * PAIR:(t + 1) * PAIR], lambda h: av_ref[:, h * A_V:(h + 1) * A_V],
                       h, lam, dg_ref[...], lam_init) for h in range(A_HEADS)]
    o_ref[0] = jnp.concatenate(outs, axis=1).astype(bf16)
    q = bq_ref[...].astype(f32) * QK_SCALE2
    tiles = []
    for t in range(B_HEADS // 2):
        cols = slice(t * PAIR, (t + 1) * PAIR)
        accs = []
        for half in range(2):
            (e,), _ = _exp2_parts([_dot_nt(_head_q(q[:, cols], half), bk_ref[:, cols])])
            accs.append(_dot(e.astype(bf16), _values_with_ones(bv_ref[:, cols], half)))
        tiles.append(_pair_out(*accs))
    o_ref[1] = jnp.concatenate(tiles, axis=1).astype(bf16)
    q = cq_ref[...].astype(f32) * QK_SCALE2
    k_tile = ckv_ref[:, 0:PAIR]
    v_variants = _swap_halves_variants(ckv_ref[:, PAIR:2 * PAIR].astype(f32))
    tiles = []
    for t in range(C_HEADS // 2):
        q_tile = q[:, t * PAIR:(t + 1) * PAIR]
        q_swapped = pltpu.roll(q_tile, HEAD_DIM, 1)
        accs, extras = [], []
        for half in range(2):
            h = 2 * t + half
            g = h // C_GROUP
            sink2 = sink_ref[h] * LOG2E
            (e,), m = _exp2_parts([_dot_nt(_gqa_head_q(q_tile, q_swapped, half, g), k_tile)], sink2)
            accs.append(_dot(e.astype(bf16), v_variants[g * 2 + half]))
            extras.append(jnp.exp2(sink2 - m))
        tiles.append(_pair_out(accs[0], accs[1], extras[0], extras[1]))
    o_ref[2] = jnp.concatenate(tiles, axis=1).astype(bf16)


def _ctx_attn(p, dl, dg, sink, li):
    nb = p.shape[0] // CTX_SEQ
    S = CTX_SEQ
    lam_init = 0.8 - 0.6 * math.exp(-0.3 * li)

    def col(c):
        return pl.BlockSpec((S, 512), lambda b, c=c: (b, c))

    return pl.pallas_call(
        functools.partial(_ctx_attn_kernel, lam_init=lam_init),
        grid=(nb,),
        in_specs=[col(COL_AQ), col(COL_AK), col(COL_AV), col(COL_BQ), col(COL_BK), col(COL_BV), col(COL_CQ),
                  pl.BlockSpec((S, 256), lambda b: (b, COL_CKV_256)),
                  pl.BlockSpec((4, A_QK), lambda b: (0, 0)),
                  pl.BlockSpec((1, A_V), lambda b: (0, 0)),
                  pl.BlockSpec(memory_space=pltpu.SMEM)],
        out_specs=pl.BlockSpec((N_BRANCH - 1, S, BRANCH_WIDTH), lambda b: (0, b, 0)),
        out_shape=jax.ShapeDtypeStruct((N_BRANCH, p.shape[0], BRANCH_WIDTH), bf16),
        compiler_params=_params("parallel"),
        name="ctx_attn",
    )(p, p, p, p, p, p, p, p, dl, dg, sink)


def _lat_a_kernel(q_ref, k_ref, v_ref, ck_ref, cv_ref, cosq_ref, sinq_ref, cosk_ref, sink_ref,
                  dl_ref, dg_ref, o_ref, kk_ref, vv_ref, *, lam_init):
    L = LAT_SEQ

    @pl.when(pl.program_id(1) == 0)
    def _():
        kk_ref[0:L, :] = _rope(k_ref[...], cosk_ref[...], sink_ref[...]).astype(bf16)
        kk_ref[L:, :] = ck_ref[...].astype(bf16)
        vv_ref[0:L, :] = v_ref[...].astype(bf16)
        vv_ref[L:, :] = cv_ref[...].astype(bf16)

    lam = _diff_lambda(dl_ref, lam_init)
    q = _rope(q_ref[...], cosq_ref[...], sinq_ref[...]) * QK_SCALE2
    outs = [_diff_head(q, lambda t: kk_ref[:, t * PAIR:(t + 1) * PAIR], lambda h: vv_ref[:, h * A_V:(h + 1) * A_V],
                       h, lam, dg_ref[...], lam_init) for h in range(A_HEADS)]
    o_ref[...] = jnp.concatenate(outs, axis=1).astype(bf16)


def _lat_b_kernel(q_ref, k_ref, v_ref, ck_ref, cv_ref, bias_ref, _, o_ref, kk_ref, va_ref):
    L, W = LAT_SEQ, NA_KEY_ROWS * GRID_W
    qb = pl.program_id(1)

    @pl.when(qb == 0)
    def _():
        kk_ref[0:L, :] = k_ref[...]
        kk_ref[L:, :] = ck_ref[...].astype(bf16)
        for t in range(B_HEADS // 2):
            cols = slice(t * PAIR, (t + 1) * PAIR)
            for half in range(2):
                va_ref[2 * t + half, 0:L, :] = _values_with_ones(v_ref[:, cols], half)
                va_ref[2 * t + half, L:, :] = _values_with_ones(cv_ref[:, cols], half)

    start = pl.multiple_of(_na_key_row0(qb) * GRID_W, 256)
    q = q_ref[...].astype(f32) * QK_SCALE2
    tiles = []
    for t in range(B_HEADS // 2):
        cols = slice(t * PAIR, (t + 1) * PAIR)
        accs = []
        for half in range(2):
            h = 2 * t + half
            qh = _head_q(q[:, cols], half)
            s_loc = _dot_nt(qh, kk_ref[pl.ds(start, W), cols]) + bias_ref[h].astype(f32)
            s_ctx = _dot_nt(qh, kk_ref[L:, cols])
            (e_loc, e_ctx), _m = _exp2_parts([s_loc, s_ctx])
            accs.append(_dot(e_loc.astype(bf16), va_ref[h, pl.ds(start, W), :])
                        + _dot(e_ctx.astype(bf16), va_ref[h, L:, :]))
        tiles.append(_pair_out(*accs))
    o_ref[...] = jnp.concatenate(tiles, axis=1).astype(bf16)


def _lat_c_kernel(q_ref, kv_ref, ck_ref, cv_ref, cosq_ref, sinq_ref, cosk_ref, sink_ref, snk_ref,
                  _, o_ref, kk_ref, va_ref):
    L, W = LAT_SEQ, SWA_KEYS
    tq = q_ref.shape[0]
    qb = pl.program_id(1)

    @pl.when(qb == 0)
    def _():
        kk_ref[0:L, :] = _rope(kv_ref[:, 0:PAIR], cosk_ref[...], sink_ref[...]).astype(bf16)
        kk_ref[L:, :] = ck_ref[...].astype(bf16)
        lat = _swap_halves_variants(kv_ref[:, PAIR:2 * PAIR].astype(f32))
        ctx = _swap_halves_variants(cv_ref[...])
        for idx in range(4):
            va_ref[idx, 0:L, :] = lat[idx]
            va_ref[idx, L:, :] = ctx[idx]

    start = pl.multiple_of(jnp.clip(qb * tq - SWA_WINDOW, 0, L - W), 128)
    q = _rope(q_ref[...], cosq_ref[...], sinq_ref[...]) * QK_SCALE2
    qpos = qb * tq + lax.broadcasted_iota(jnp.int32, (tq, W), 0)
    kpos = start + lax.broadcasted_iota(jnp.int32, (tq, W), 1)
    valid = jnp.abs(qpos - kpos) <= SWA_WINDOW
    tiles = []
    for t in range(C_HEADS // 2):
        q_tile = q[:, t * PAIR:(t + 1) * PAIR]
        q_swapped = pltpu.roll(q_tile, HEAD_DIM, 1)
        accs, extras = [], []
        for half in range(2):
            h = 2 * t + half
            g = h // C_GROUP
            qh = _gqa_head_q(q_tile, q_swapped, half, g)
            s_loc = jnp.where(valid, _dot_nt(qh, kk_ref[pl.ds(start, W), :]), NEG_INF)
            s_ctx = _dot_nt(qh, kk_ref[L:, :])
            sink2 = snk_ref[h] * LOG2E
            (e_loc, e_ctx), m = _exp2_parts([s_loc, s_ctx], sink2)
            idx = g * 2 + half
            accs.append(_dot(e_loc.astype(bf16), va_ref[idx, pl.ds(start, W), :])
                        + _dot(e_ctx.astype(bf16), va_ref[idx, L:, :]))
            extras.append(jnp.exp2(sink2 - m))
        tiles.append(_pair_out(accs[0], accs[1], extras[0], extras[1]))
    o_ref[...] = jnp.concatenate(tiles, axis=1).astype(bf16)


def _lat_attn(p, caches, na_dense, dl, dg, sink, cos_t, sin_t, li):
    cak, cav, cbk, cbv, cck, ccv = caches
    L, tq = LAT_SEQ, TQ_ATTN
    assert tq == NA_Q_ROWS * GRID_W and SWA_KEYS >= tq + 2 * SWA_WINDOW
    nq = L // tq
    nb_lat = p.shape[0] // L
    lam_init = 0.8 - 0.6 * math.exp(-0.3 * li)

    def qcol(c):
        return pl.BlockSpec((tq, 512), lambda b, i, c=c: (b * nq + i, c))

    def kcol(c, width=512):
        return pl.BlockSpec((L, width), lambda b, i, c=c: (b, c))

    def cache(width):
        return pl.BlockSpec((None, None, CTX_SEQ, width), lambda b, i: (b, li, 0, 0))

    tab_q = pl.BlockSpec((tq, 512), lambda b, i: (i, 0))
    tab_k = pl.BlockSpec((L, 512), lambda b, i: (0, 0))
    tab_k128 = pl.BlockSpec((L, 128), lambda b, i: (0, 0))
    o_sds = jax.ShapeDtypeStruct((N_BRANCH, p.shape[0], BRANCH_WIDTH), bf16)
    anyspec = pl.BlockSpec(memory_space=pl.ANY)

    def o_spec(branch):
        return pl.BlockSpec((None, tq, BRANCH_WIDTH), lambda b, i: (branch, b * nq + i, 0))

    kv_scratch = [pltpu.VMEM((L + CTX_SEQ, 512), bf16), pltpu.VMEM((L + CTX_SEQ, 512), bf16)]
    smem = pl.BlockSpec(memory_space=pltpu.SMEM)
    cp = _params("parallel", "arbitrary")

    o_stack = pl.pallas_call(
        functools.partial(_lat_a_kernel, lam_init=lam_init),
        grid=(nb_lat, nq),
        in_specs=[qcol(COL_AQ), kcol(COL_AK), kcol(COL_AV), cache(512), cache(512),
                  tab_q, tab_q, tab_k, tab_k,
                  pl.BlockSpec((4, A_QK), lambda b, i: (0, 0)), pl.BlockSpec((1, A_V), lambda b, i: (0, 0))],
        out_specs=o_spec(0), out_shape=o_sds, scratch_shapes=kv_scratch,
        compiler_params=cp, name="lat_attn_a",
    )(p, p, p, cak, cav, cos_t, sin_t, cos_t, sin_t, dl, dg)

    o_stack = pl.pallas_call(
        _lat_b_kernel,
        grid=(nb_lat, nq),
        in_specs=[qcol(COL_BQ), kcol(COL_BK), kcol(COL_BV), cache(512), cache(512),
                  pl.BlockSpec((B_HEADS, tq, NA_KEY_ROWS * GRID_W), lambda b, i: (0, i, 0)), anyspec],
        out_specs=o_spec(1), out_shape=o_sds,
        scratch_shapes=[pltpu.VMEM((L + CTX_SEQ, 512), bf16), pltpu.VMEM((B_HEADS, L + CTX_SEQ, PAIR), bf16)],
        input_output_aliases={6: 0}, compiler_params=cp, name="lat_attn_b",
    )(p, p, p, cbk, cbv, na_dense, o_stack)

    return pl.pallas_call(
        _lat_c_kernel,
        grid=(nb_lat, nq),
        in_specs=[qcol(COL_CQ), kcol(COL_CKV_256, 256), cache(PAIR), cache(PAIR),
                  tab_q, tab_q, tab_k128, tab_k128, smem, anyspec],
        out_specs=o_spec(2), out_shape=o_sds,
        scratch_shapes=[pltpu.VMEM((L + CTX_SEQ, PAIR), bf16), pltpu.VMEM((4, L + CTX_SEQ, PAIR), bf16)],
        input_output_aliases={9: 0}, compiler_params=cp, name="lat_attn_c",
    )(p, p, cck, ccv, cos_t, sin_t, cos_t, sin_t, sink, o_stack)


def _na_key_row0(qb):
    lo, hi = NA_Q_ROWS * qb - NA_ROWS // 2, GRID_ROWS - NA_KEY_ROWS
    return min(max(lo, 0), hi) if isinstance(qb, int) else jnp.clip(lo, 0, hi)


def _na_bias_kernel(rb_ref, o_ref):
    h = pl.program_id(0)
    W = GRID_W
    n_dc = 2 * NA_COLS - 1
    n_dr = 2 * NA_ROWS - 1
    qc = lax.broadcasted_iota(jnp.int32, (W, 2 * W), 0)
    lane = lax.broadcasted_iota(jnp.int32, (W, 2 * W), 1)
    second = lane >= W
    kc = jnp.where(second, lane - W, lane)
    dc = jnp.clip(kc - qc, -(NA_COLS - 1), NA_COLS - 1) + NA_COLS - 1
    c0 = jnp.clip(qc - NA_COLS // 2, 0, W - NA_COLS)
    col_ok = (kc >= c0) & (kc < c0 + NA_COLS)
    base = h * (n_dr * n_dc)

    def pair_tile(dr0):
        t = jnp.zeros((W, 2 * W), f32)
        for j in range(n_dc):
            lo = rb_ref[base + dr0 * n_dc + j] if 0 <= dr0 < n_dr else 0.0
            hi = rb_ref[base + (dr0 + 1) * n_dc + j] if 0 <= dr0 + 1 < n_dr else 0.0
            t = jnp.where(dc == j, jnp.where(second, hi, lo), t)
        return jnp.where(col_ok, t * LOG2E, NEG_INF)

    tiles = {dr0: pair_tile(dr0) for dr0 in range(-1, n_dr)}
    neg = jnp.full((W, 2 * W), NEG_INF, f32)
    for qr in range(GRID_ROWS):
        r0 = min(max(qr - NA_ROWS // 2, 0), GRID_ROWS - NA_ROWS)
        k0 = _na_key_row0(qr // NA_Q_ROWS)
        assert k0 % 2 == 0 and k0 <= r0 and r0 + NA_ROWS <= k0 + NA_KEY_ROWS
        for pr in range(NA_KEY_ROWS // 2):
            kr = k0 + 2 * pr
            ok0 = r0 <= kr < r0 + NA_ROWS
            ok1 = r0 <= kr + 1 < r0 + NA_ROWS
            if not (ok0 or ok1):
                t = neg
            else:
                t = tiles[kr - qr + NA_ROWS - 1]
                if not ok0:
                    t = jnp.where(second, t, NEG_INF)
                if not ok1:
                    t = jnp.where(second, NEG_INF, t)
            o_ref[qr * W:(qr + 1) * W, 2 * pr * W:(2 * pr + 2) * W] = t.astype(bf16)


def _na_bias_dense(rel_bias):
    kw = NA_KEY_ROWS * GRID_W
    return pl.pallas_call(
        _na_bias_kernel,
        grid=(B_HEADS,),
        in_specs=[pl.BlockSpec(memory_space=pltpu.SMEM)],
        out_specs=pl.BlockSpec((None, LAT_SEQ, kw), lambda h: (h, 0, 0)),
        out_shape=jax.ShapeDtypeStruct((B_HEADS, LAT_SEQ, kw), bf16),
        compiler_params=_params("parallel"),
        name="na_bias",
    )(rel_bias.reshape(-1))


def _hy_filter_kernel(z_ref, w1_ref, b1_ref, w2_ref, b2_ref, fr_ref, w3f_ref, w3b_ref, dcf_ref, dcb_ref,
                      bias_ref, fc_ref, fs_ref, o_ref, *, L):
    z = z_ref[...]
    fr = fr_ref[...]
    h = jnp.sin(fr * (_dot_hi(z, w1_ref[...]) + b1_ref[...]))
    h = jnp.sin(fr * (_dot_hi(h, w2_ref[...]) + b2_ref[...]))
    t = z[:, 0:1]
    hf = _dot_hi(h, w3f_ref[...]) * jnp.exp(-t * jnp.abs(dcf_ref[...]))
    hb = _dot_hi(h, w3b_ref[...]) * jnp.exp(-t * jnp.abs(dcb_ref[...]))
    ssum = hf + hb
    bias = bias_ref[...]
    ga = _dot_3pass(fc_ref[...], ssum) + bias
    gb = _dot_3pass(fs_ref[...], hb - hf)
    g_nyq = _dot_hi(fs_ref[0:8, :], ssum)[0:1] + bias
    row0 = lax.broadcasted_iota(jnp.int32, ga.shape, 0) == 0
    inv = 1.0 / L
    o_ref[0] = jnp.where(row0, 0.5 * ga, ga) * inv
    o_ref[1] = jnp.where(row0, 0.0, gb) * inv
    o_ref[2] = jnp.where(row0, 0.5 * g_nyq, ga) * inv


def _hy_filter_tables(L, w1p, b1, w2, b2, w3, freq, decay, hy_bias):
    cb = HY_CB
    ncb = HY_WIDTH // cb
    z = jnp.asarray(_hyena_features(L))
    fc, fs = (jnp.asarray(a) for a in _dft_tables(L))
    full = lambda shape: pl.BlockSpec(shape, lambda o, c: (0,) * len(shape))
    fwd = lambda rows: pl.BlockSpec((rows, cb), lambda o, c: (0, o * 2 * ncb + c))
    bwd = lambda rows: pl.BlockSpec((rows, cb), lambda o, c: (0, o * 2 * ncb + ncb + c))
    return pl.pallas_call(
        functools.partial(_hy_filter_kernel, L=L),
        grid=(HY_ORDER, ncb),
        in_specs=[full((L, HY_EMB_PAD)), full((HY_EMB_PAD, HY_FFN)), full((1, HY_FFN)),
                  full((HY_FFN, HY_FFN)), full((1, HY_FFN)), full((1, HY_FFN)),
                  fwd(HY_FFN), bwd(HY_FFN), fwd(1), bwd(1),
                  pl.BlockSpec((None, 1, cb), lambda o, c: (o, 0, c)),
                  full((L, L)), full((L, L))],
        out_specs=pl.BlockSpec((None, 3, L, cb), lambda o, c: (o, 0, 0, c)),
        out_shape=jax.ShapeDtypeStruct((HY_ORDER, 3, L, HY_WIDTH), f32),
        compiler_params=_params("parallel", "parallel"),
        name="hy_filter",
    )(z, w1p, b1, w2, b2, freq, w3, w3, decay, decay, hy_bias.reshape(HY_ORDER, 1, HY_WIDTH), fc, fs)


def _hyena_kernel(v_ref, x1_ref, x2_ref, wv_ref, w1_ref, w2_ref, fwd_ref, inv_ref, tab_ref, _, o_ref):
    L = inv_ref.shape[0]
    cb = v_ref.shape[1]
    n_seq = v_ref.shape[0] // L
    row = lax.broadcasted_iota(jnp.int32, (L, cb), 0)

    def short_conv(u_ref, w_ref):
        w = w_ref[...]
        outs = []
        for s in range(n_seq):
            u = u_ref[s * L:(s + 1) * L, :].astype(f32)
            prev = jnp.where(row == 0, 0.0, pltpu.roll(u, 1, 0))
            nxt = jnp.where(row == L - 1, 0.0, pltpu.roll(u, L - 1, 0))
            outs.append(prev * w[0:1] + u * w[1:2] + nxt * w[2:3])
        return jnp.concatenate(outs, axis=1)

    def long_conv(u, order):
        ab = _dot(fwd_ref[...], u.astype(bf16))
        t0, t1, t2 = tab_ref[order, 0], tab_ref[order, 1], tab_ref[order, 2]
        pq = []
        for s in range(n_seq):
            a, b = ab[:L, s * cb:(s + 1) * cb], ab[L:, s * cb:(s + 1) * cb]
            pq.append(jnp.concatenate([(a * t0 + b * t1).astype(bf16), (b * t2 - a * t1).astype(bf16)], axis=0))
        return _dot(inv_ref[...], jnp.concatenate(pq, axis=1))

    z = short_conv(x1_ref, w1_ref) * long_conv(short_conv(v_ref, wv_ref), 0)
    y = (short_conv(x2_ref, w2_ref) * long_conv(z, 1)).astype(bf16)
    o_ref[...] = jnp.concatenate([y[:, s * cb:(s + 1) * cb] for s in range(n_seq)], axis=0)


def _hyena(p, o_stack, hy_short, tabs, L):
    cb = HY_CB
    ncb = HY_WIDTH // cb
    c0 = HY_COL0 // cb
    rows = min(HY_SEQS_PER_STEP[L] * L, p.shape[0])
    nb = p.shape[0] // rows
    fc, fs = _dft_tables(L)
    fwd = jnp.asarray(np.concatenate([fc, fs], axis=0), bf16)
    inv = jnp.asarray(np.concatenate([fc, fs.T], axis=1), bf16)

    def part(k):
        return pl.BlockSpec((rows, cb), lambda c, b, k=k: (b, c0 + k * ncb + c))

    def wpart(k):
        return pl.BlockSpec((3, cb), lambda c, b, k=k: (0, k * ncb + c))

    return pl.pallas_call(
        _hyena_kernel,
        grid=(ncb, nb),
        in_specs=[part(0), part(1), part(2), wpart(0), wpart(1), wpart(2),
                  pl.BlockSpec((2 * L, L), lambda c, b: (0, 0)), pl.BlockSpec((L, 2 * L), lambda c, b: (0, 0)),
                  pl.BlockSpec((HY_ORDER, 3, L, cb), lambda c, b: (0, 0, 0, c)),
                  pl.BlockSpec(memory_space=pl.ANY)],
        out_specs=pl.BlockSpec((None, rows, cb), lambda c, b: (N_BRANCH - 1, b, c)),
        out_shape=jax.ShapeDtypeStruct(o_stack.shape, o_stack.dtype),
        input_output_aliases={9: 0},
        compiler_params=_params("parallel", "arbitrary"),
        name="hyena_%d" % L,
    )(p, p, p, hy_short, hy_short, hy_short, fwd, inv, tabs, o_stack)


def kernel(x_prompt, x_sample, cache_a_k, cache_a_v, cache_b_k, cache_b_v, cache_c_k, cache_c_v, c, c_ctx, w_ada, b_ada, g_mix, w_in, diff_lambda, diff_norm_g, na_bias, swa_sink, hy_short, hy_w1, hy_b1, hy_w2, hy_b2, hy_w3, hy_freq, hy_decay, hy_bias, w_branch, w_out, g_mlp, w_up, w_down, g_final):
    nb_ctx, nb_lat = x_prompt.shape[0], x_sample.shape[0]
    assert x_prompt.shape[1:] == (CTX_SEQ, D_MODEL) and x_sample.shape[1:] == (LAT_SEQ, D_MODEL)
    assert nb_lat <= CTX_MOD_ROW and (nb_ctx * CTX_SEQ) % TM_INPROJ == 0

    x_ctx = x_prompt.reshape(nb_ctx * CTX_SEQ, D_MODEL)
    x_lat = x_sample.reshape(nb_lat * LAT_SEQ, D_MODEL)
    cv = jnp.zeros((N_MOD_ROWS, D_MODEL), f32).at[:nb_lat].set(c).at[CTX_MOD_ROW].set(c_ctx)
    mod = _adaln_all(cv, w_ada, b_ada).reshape(DEPTH * N_MOD_ROWS * 6, 1, D_MODEL)
    cos_t, sin_t = (jnp.asarray(a) for a in _rope_tables())
    caches = (cache_a_k.reshape(nb_lat, DEPTH, CTX_SEQ, 512), cache_a_v.reshape(nb_lat, DEPTH, CTX_SEQ, 512),
              cache_b_k.reshape(nb_lat, DEPTH, CTX_SEQ, 512), cache_b_v.reshape(nb_lat, DEPTH, CTX_SEQ, 512),
              cache_c_k.reshape(nb_lat, DEPTH, CTX_SEQ, 128), cache_c_v.reshape(nb_lat, DEPTH, CTX_SEQ, 128))

    kv = None
    for li in range(DEPTH):
        w_mix = _cast_mix_weights(w_in, li)
        w5, wu, wd = _cast_layer_weights(w_in, w_out, w_up, w_down, li)
        wb = w_branch[li].astype(bf16)
        g1 = g_mix[li].reshape(1, D_MODEL)
        g2 = g_mlp[li].reshape(1, D_MODEL)
        dl = diff_lambda[li]
        dg = diff_norm_g[li].reshape(1, A_V)
        sink = swa_sink[li]

        w1p = jnp.pad(hy_w1[li], ((0, HY_EMB_PAD - HY_EMB), (0, 0)))
        hy_args = (w1p, hy_b1[li].reshape(1, HY_FFN), hy_w2[li], hy_b2[li].reshape(1, HY_FFN), hy_w3[li],
                   hy_freq[li].reshape(1, HY_FFN), hy_decay[li].reshape(1, -1), hy_bias[li])
        gf = g_final.reshape(1, D_MODEL)
        final = li == DEPTH - 1

        p, kv, h = _inproj(x_ctx, mod, g1, w_mix, li, True, kv)
        o_stack = _ctx_attn(p, dl, dg, sink, li)
        o_stack = _hyena(p, o_stack, hy_short[li], _hy_filter_tables(CTX_SEQ, *hy_args), CTX_SEQ)
        x_ctx, h2 = _merge(x_ctx, mod, h, o_stack, w5, wb, g2, li, True)
        x_ctx = _mlp(x_ctx, h2, mod, wu, wd, gf, li, True, final_norm=final)

        p, h = _inproj(x_lat, mod, g1, w_mix, li, False)
        o_stack = _lat_attn(p, caches, _na_bias_dense(na_bias[li]), dl, dg, sink, cos_t, sin_t, li)
        o_stack = _hyena(p, o_stack, hy_short[li], _hy_filter_tables(LAT_SEQ, *hy_args), LAT_SEQ)
        x_lat, h2 = _merge(x_lat, mod, h, o_stack, w5, wb, g2, li, False)
        x_lat = _mlp(x_lat, h2, mod, wu, wd, gf, li, False, final_norm=final)

    y_prompt = x_ctx.reshape(nb_ctx, CTX_SEQ, D_MODEL)
    y_sample = x_lat.reshape(nb_lat, LAT_SEQ, D_MODEL)
    kv_shapes = ((2, A_HEADS, A_QK), (A_HEADS, A_V), (B_HEADS, HEAD_DIM), (B_HEADS, HEAD_DIM),
                 (C_KV_HEADS, HEAD_DIM), (C_KV_HEADS, HEAD_DIM))
    new_kv = tuple(a.reshape((nb_ctx, DEPTH, CTX_SEQ) + s) for a, s in zip(kv, kv_shapes))
    return (y_prompt, y_sample) + new_kv
```

```python
import functools
import math

import numpy as np
import jax
import jax.numpy as jnp
from jax import lax
from jax.experimental import pallas as pl
from jax.experimental.pallas import tpu as pltpu

f32 = jnp.float32
bf16 = jnp.bfloat16

D_MODEL = 2048
DEPTH = 2
CTX_SEQ = 256
LAT_SEQ = 1024
GRID_W = 64
GRID_ROWS = LAT_SEQ // GRID_W
BRANCH_WIDTH = 512
N_BRANCH = 4
HEAD_DIM = 64
A_QK = 64
A_V = 128
A_HEADS = 4
B_HEADS = 8
NA_ROWS = 8
NA_COLS = 16
C_HEADS = 8
C_KV_HEADS = 2
C_GROUP = 4
SWA_WINDOW = 128
SWA_KEYS = 512
NA_Q_ROWS = 4
NA_KEY_ROWS = 12
HY_WIDTH = 512
HY_ORDER = 2
HY_BANDS = 16
HY_EMB = 1 + 2 * HY_BANDS
HY_EMB_PAD = 128
HY_FFN = 64
D_FF = 4 * D_MODEL
ROPE_BASE = 10000.0
EPS = 1e-6
NEG_INF = -1e30
MIX_COLS = 5376
GATE_COLS = N_BRANCH * D_MODEL
COL_AQ, COL_BQ, COL_CQ, COL_AK, COL_AV, COL_BK, COL_BV = 0, 1, 2, 3, 4, 5, 6
COL_CKV_256 = 14
KV_PARTS = ((1536, 2048), (2048, 2560), (2560, 3072), (3072, 3584), (3584, 3712), (3712, 3840))
HY_COL0 = 3840
N_MOD_ROWS = 16
CTX_MOD_ROW = 8

VMEM_LIMIT = 56 * 1024 * 1024
TM_INPROJ = 256
TM_MERGE = 512
TM_MLP = 512
TF_MLP = 1024
TQ_ATTN = 256
TQ_ATTN_A = 512
CTX_SEQS_PER_STEP = 1
TN_ADA = 1024
HY_CB = 256
HY_SEQS_PER_STEP = {CTX_SEQ: 8, LAT_SEQ: 2}
CAST_ROWS = 512


def _params(*sem):
    return pltpu.CompilerParams(dimension_semantics=sem, vmem_limit_bytes=VMEM_LIMIT)


def _dot(a, b):
    return jnp.dot(a, b, preferred_element_type=f32)


def _dot_nt(a, b):
    return lax.dot_general(a, b, (((1,), (1,)), ((), ())), preferred_element_type=f32)


def _dot_hi(a, b):
    return jnp.dot(a, b, preferred_element_type=f32, precision=lax.Precision.HIGHEST)


def _dot_3pass(a, b):
    a_hi, b_hi = a.astype(bf16), b.astype(bf16)
    a_lo = (a - a_hi.astype(f32)).astype(bf16)
    b_lo = (b - b_hi.astype(f32)).astype(bf16)
    return _dot(a_hi, b_hi) + (_dot(a_hi, b_lo) + _dot(a_lo, b_hi))


@functools.lru_cache(maxsize=None)
def _rope_tables():
    half = HEAD_DIM // 2
    nf = half // 2
    inv = ROPE_BASE ** (-np.arange(nf, dtype=np.float64) / nf)
    t = np.arange(LAT_SEQ)
    pos = np.stack([t // GRID_W, t % GRID_W], axis=1).astype(np.float64)
    lane = np.arange(HEAD_DIM)
    ang = pos[:, lane // half] * inv[lane % nf][None, :]
    first = (lane % half) < nf
    cos = np.cos(ang)
    sin = np.where(first[None, :], -np.sin(ang), np.sin(ang))
    reps = 512 // HEAD_DIM
    return (np.tile(cos, (1, reps)).astype(np.float32), np.tile(sin, (1, reps)).astype(np.float32))


@functools.lru_cache(maxsize=None)
def _dft_tables(L):
    f = np.arange(L, dtype=np.int64)
    prod = (f[:, None] * f[None, :]) % (2 * L)
    ang = np.pi * prod.astype(np.float64) / L
    fc = np.cos(ang)
    fs = np.sin(ang)
    fs[0, :] = np.where(f % 2 == 0, 1.0, -1.0)
    return fc.astype(np.float32), fs.astype(np.float32)


@functools.lru_cache(maxsize=None)
def _hyena_features(L):
    n = np.arange(L, dtype=np.float64)[:, None]
    t = n / max(L - 1, 1)
    w = 2.0 * math.pi * n / L
    bands = np.linspace(1e-4, HY_BANDS - 1, HY_BANDS, dtype=np.float64)[None, :]
    z = np.concatenate([t, np.cos(bands * w), -np.sin(bands * w)], axis=-1)
    z = np.pad(z, ((0, 0), (0, HY_EMB_PAD - HY_EMB)))
    return z.astype(np.float32)


def _modulated_norm(x, g, scale, shift):
    ms = jnp.mean(x * x, axis=-1, keepdims=True)
    return x * lax.rsqrt(ms + EPS) * (g * (1.0 + scale)) + shift


def _rope(x, cos, sin_signed):
    x = x.astype(f32)
    n = x.shape[-1]
    lane = lax.broadcasted_iota(jnp.int32, x.shape, 1)
    first = (lane & (HEAD_DIM // 2 - 1)) < (HEAD_DIM // 4)
    partner = jnp.where(first, pltpu.roll(x, n - HEAD_DIM // 4, 1), pltpu.roll(x, HEAD_DIM // 4, 1))
    return x * cos + partner * sin_signed


LOG2E = 1.4426950408889634
QK_SCALE2 = HEAD_DIM ** -0.5 * LOG2E
PAIR = 2 * HEAD_DIM


def _lo_half(shape):
    return lax.broadcasted_iota(jnp.int32, shape, 1) < HEAD_DIM


def _keep_half(x, half, fill):
    lo = _lo_half(x.shape)
    return jnp.where(lo if half == 0 else jnp.logical_not(lo), x, fill)


def _head_q(q_tile, half):
    return _keep_half(q_tile, half, 0.0).astype(bf16)


def _values_with_ones(v_tile, half):
    return _keep_half(v_tile.astype(f32), half, 1.0).astype(bf16)


def _exp2_parts(scores, sink2=None):
    m = jnp.max(scores[0], axis=-1, keepdims=True)
    for s in scores[1:]:
        m = jnp.maximum(m, jnp.max(s, axis=-1, keepdims=True))
    if sink2 is not None:
        m = jnp.maximum(m, sink2)
    return [jnp.exp2(s - m) for s in scores], m


def _pair_out(acc_even, acc_odd, extra_even=None, extra_odd=None):
    lo = _lo_half(acc_even.shape)
    num = jnp.where(lo, acc_even, acc_odd)
    den = pltpu.roll(jnp.where(lo, acc_odd, acc_even), HEAD_DIM, 1)
    if extra_even is not None:
        den = den + jnp.where(lo, extra_even, extra_odd)
    return num / den


def _swap_halves_variants(v_tile):
    sw = pltpu.roll(v_tile, HEAD_DIM, 1)
    lo = _lo_half(v_tile.shape)
    tiles = (jnp.where(lo, v_tile, 1.0), jnp.where(lo, 1.0, sw), jnp.where(lo, sw, 1.0), jnp.where(lo, 1.0, v_tile))
    return [t.astype(bf16) for t in tiles]


def _gqa_head_q(q_tile, q_swapped, half, g):
    return _keep_half(q_tile if half == g else q_swapped, g, 0.0).astype(bf16)


def _diff_lambda(dl_ref, lam_init):
    dl = dl_ref[...]
    a = jnp.sum(dl[0:1] * dl[1:2], axis=-1, keepdims=True)
    b = jnp.sum(dl[2:3] * dl[3:4], axis=-1, keepdims=True)
    return jnp.exp(a) - jnp.exp(b) + lam_init


def _diff_head(q, k_of, v_of, h, lam, dg, lam_init):
    res = []
    for c in range(2):
        t = c * (A_HEADS // 2) + h // 2
        s = _dot_nt(_head_q(q[:, t * PAIR:(t + 1) * PAIR], h % 2), k_of(t))
        (e,), _ = _exp2_parts([s])
        res.append((_dot(e.astype(bf16), v_of(h)), jnp.sum(e, axis=-1, keepdims=True)))
    o = res[0][0] * (1.0 / res[0][1]) - res[1][0] * (lam / res[1][1])
    o = o * lax.rsqrt(jnp.mean(o * o, axis=-1, keepdims=True) + EPS) * dg
    return o * (1.0 - lam_init)


def _cast_kernel(src_ref, *rest):
    o_ref = rest[-1]
    x = src_ref[...]
    o_ref[...] = x.reshape(o_ref.shape).astype(bf16)


def _cast_call(name, grid, src, in_spec, out_spec, out_shape, dst=None):
    in_specs, args, aliases = [in_spec], [src], {}
    if dst is not None:
        in_specs.append(pl.BlockSpec(memory_space=pl.ANY))
        args.append(dst)
        aliases = {1: 0}
    return pl.pallas_call(
        _cast_kernel, grid=grid, in_specs=in_specs, out_specs=out_spec,
        out_shape=jax.ShapeDtypeStruct(out_shape, bf16), input_output_aliases=aliases,
        compiler_params=_params(*(("parallel",) * len(grid))), name=name,
    )(*args)


def _cast_mix_kernel(unit_ref, src_ref, o_ref):
    o_ref[...] = src_ref[0].astype(bf16)


def _cast_mix_weights(w_in, li):
    U = 256
    src_unit = (0, 1, 6, 7, 12, 13, 2, 3, 4, 5, 8, 9, 10, 11) + tuple(range(14, MIX_COLS // U))
    return pl.pallas_call(
        _cast_mix_kernel,
        grid_spec=pltpu.PrefetchScalarGridSpec(
            num_scalar_prefetch=1, grid=(len(src_unit),),
            in_specs=[pl.BlockSpec((pl.Element(1), pl.Element(D_MODEL), pl.Element(U)),
                                   lambda u, unit: (li, 0, pl.multiple_of(unit[u] * U, U)))],
            out_specs=pl.BlockSpec((D_MODEL, U), lambda u, unit: (0, u))),
        out_shape=jax.ShapeDtypeStruct((D_MODEL, MIX_COLS), bf16),
        compiler_params=_params("parallel"), name="cast_mix",
    )(jnp.asarray(src_unit, jnp.int32), w_in)


def _cast_layer_weights(w_in, w_out, w_up, w_down, li):
    D, R, tf = D_MODEL, CAST_ROWS, TF_MLP
    nr = D // R
    w5 = _cast_call(
        "cast_gate", (N_BRANCH, nr), w_in,
        pl.BlockSpec((pl.Element(1), pl.Element(R), pl.Element(D)),
                     lambda n, r: (li, pl.multiple_of(r * R, R), pl.multiple_of(MIX_COLS + n * D, 128))),
        pl.BlockSpec((None, R, D), lambda n, r: (n, r, 0)), (N_BRANCH + 1, D, D))
    w5 = _cast_call(
        "cast_wout", (nr,), w_out,
        pl.BlockSpec((None, R, D), lambda r: (li, r, 0)),
        pl.BlockSpec((None, R, D), lambda r: (N_BRANCH, r, 0)), (N_BRANCH + 1, D, D), dst=w5)
    wu = _cast_call(
        "cast_wu", (D_FF // tf, nr), w_up,
        pl.BlockSpec((None, R, tf), lambda j, r: (li, r, j)),
        pl.BlockSpec((None, R, tf), lambda j, r: (j, r, 0)), (D_FF // tf, D, tf))
    wd = _cast_call(
        "cast_wd", (D_FF // R,), w_down,
        pl.BlockSpec((None, R, D), lambda r: (li, r, 0)),
        pl.BlockSpec((R, D), lambda r: (r, 0)), (D_FF, D))
    return w5, wu, wd


def _ada_kernel(cv_ref, w_ref, b_ref, o_ref):
    cv = cv_ref[...]
    s = (cv * jax.nn.sigmoid(cv)).astype(bf16)
    o_ref[...] = _dot(s, w_ref[...].astype(bf16)) + b_ref[...]


def _adaln_all(cv, w_ada, b_ada):
    n6 = 6 * D_MODEL
    return pl.pallas_call(
        _ada_kernel,
        grid=(DEPTH, n6 // TN_ADA),
        in_specs=[pl.BlockSpec((N_MOD_ROWS, D_MODEL), lambda l, j: (0, 0)),
                  pl.BlockSpec((None, D_MODEL, TN_ADA), lambda l, j: (l, 0, j)),
                  pl.BlockSpec((None, 1, TN_ADA), lambda l, j: (l, 0, j))],
        out_specs=pl.BlockSpec((None, N_MOD_ROWS, TN_ADA), lambda l, j: (l, 0, j)),
        out_shape=jax.ShapeDtypeStruct((DEPTH, N_MOD_ROWS, n6), f32),
        compiler_params=_params("parallel", "parallel"),
        name="adaln",
    )(cv, w_ada, b_ada.reshape(DEPTH, 1, n6))


def _mod_spec(li, k, tm, is_ctx):
    def index(i, *_):
        r = CTX_MOD_ROW if is_ctx else (i * tm) // LAT_SEQ
        return ((li * N_MOD_ROWS + r) * 6 + k, 0, 0)

    return pl.BlockSpec((None, 1, D_MODEL), index)


def _inproj_kernel(x_ref, sh_ref, sc_ref, g_ref, w_ref, *rest, emit_kv):
    outs = rest[-(2 + len(KV_PARTS)) if emit_kv else -2:]
    h = _modulated_norm(x_ref[...], g_ref[...], sc_ref[...], sh_ref[...]).astype(bf16)
    outs[-1][...] = h
    res = _dot(h, w_ref[...])
    outs[0][...] = res.astype(bf16)
    if emit_kv:
        for ref, (lo, hi) in zip(outs[1:-1], KV_PARTS):
            ref[...] = res[:, lo:hi]


def _inproj(x, mod, g, w, li, is_ctx, kv_prev=None):
    T = x.shape[0]
    tm = TM_INPROJ
    assert tm == CTX_SEQ
    row = lambda width: pl.BlockSpec((tm, width), lambda i: (i, 0))
    in_specs = [row(D_MODEL), _mod_spec(li, 0, tm, is_ctx), _mod_spec(li, 1, tm, is_ctx),
                pl.BlockSpec((1, D_MODEL), lambda i: (0, 0)),
                pl.BlockSpec((D_MODEL, MIX_COLS), lambda i: (0, 0), pipeline_mode=pl.Buffered(1))]
    args = [x, mod, mod, g, w]
    out_specs = [row(MIX_COLS), row(D_MODEL)]
    out_shape = [jax.ShapeDtypeStruct((T, MIX_COLS), bf16), jax.ShapeDtypeStruct((T, D_MODEL), bf16)]
    aliases = {}
    if is_ctx:
        for k, (lo, hi) in enumerate(KV_PARTS):
            out_specs.insert(1 + k, pl.BlockSpec((None, None, CTX_SEQ, hi - lo), lambda i: (i, li, 0, 0)))
            out_shape.insert(1 + k, jax.ShapeDtypeStruct((T // CTX_SEQ, DEPTH, CTX_SEQ, hi - lo), f32))
        if kv_prev is not None:
            for k, a in enumerate(kv_prev):
                in_specs.append(pl.BlockSpec(memory_space=pl.ANY))
                aliases[len(args)] = 1 + k
                args.append(a)
    outs = pl.pallas_call(
        functools.partial(_inproj_kernel, emit_kv=is_ctx),
        grid=(T // tm,),
        in_specs=in_specs,
        out_specs=out_specs,
        out_shape=out_shape,
        input_output_aliases=aliases,
        compiler_params=_params("parallel"),
        name="inproj",
    )(*args)
    return (outs[0], tuple(outs[1:-1]), outs[-1]) if is_ctx else (outs[0], outs[-1])


def _merge_kernel(x_ref, gt_ref, h_ref, o_ref, w_ref, wb_ref, sh2_ref, sc2_ref, g2_ref, out_ref, h2_ref, acc_ref):
    n = pl.program_id(1)

    def contribution():
        return jax.nn.sigmoid(_dot(h_ref[...], w_ref[...])) * _dot(o_ref[...], wb_ref[...])

    @pl.when(n == 0)
    def _():
        acc_ref[...] = contribution()

    @pl.when((n > 0) & (n < N_BRANCH))
    def _():
        acc_ref[...] += contribution()

    @pl.when(n == N_BRANCH)
    def _():
        half = x_ref.shape[0] // 2
        for r in range(2):
            rows = slice(r * half, (r + 1) * half)
            y = x_ref[rows, :] + gt_ref[...] * _dot(acc_ref[rows, :].astype(bf16), w_ref[...])
            out_ref[rows, :] = y
            h2_ref[rows, :] = _modulated_norm(y, g2_ref[...], sc2_ref[...], sh2_ref[...]).astype(bf16)


def _merge(x, mod, h, o_stack, w5, wb, g2, li, is_ctx):
    T = x.shape[0]
    tm = TM_MERGE
    last = N_BRANCH - 1
    row = pl.BlockSpec((tm, D_MODEL), lambda i, n: (i, 0))
    return pl.pallas_call(
        _merge_kernel,
        grid=(T // tm, N_BRANCH + 1),
        in_specs=[row, _mod_spec(li, 2, tm, is_ctx), row,
                  pl.BlockSpec((None, tm, BRANCH_WIDTH), lambda i, n: (jnp.minimum(n, last), i, 0)),
                  pl.BlockSpec((None, D_MODEL, D_MODEL), lambda i, n: (n, 0, 0)),
                  pl.BlockSpec((None, BRANCH_WIDTH, D_MODEL), lambda i, n: (jnp.minimum(n, last), 0, 0)),
                  _mod_spec(li, 3, tm, is_ctx), _mod_spec(li, 4, tm, is_ctx),
                  pl.BlockSpec((1, D_MODEL), lambda i, n: (0, 0))],
        out_specs=[row, row],
        out_shape=[jax.ShapeDtypeStruct((T, D_MODEL), f32), jax.ShapeDtypeStruct((T, D_MODEL), bf16)],
        scratch_shapes=[pltpu.VMEM((tm, D_MODEL), f32)],
        compiler_params=_params("parallel", "arbitrary"),
        name="merge",
    )(x, mod, h, o_stack, w5, wb, mod, mod, g2)


def _mlp_kernel(x_ref, h_ref, gt_ref, wu_ref, wd_ref, gf_ref, out_ref, acc_ref, *, final_norm):
    j = pl.program_id(1)

    def chunk():
        a = jnp.maximum(_dot(h_ref[...], wu_ref[...]), 0.0)
        return _dot((a * a).astype(bf16), wd_ref[...])

    @pl.when(j == 0)
    def _():
        acc_ref[...] = chunk()

    @pl.when(j > 0)
    def _():
        acc_ref[...] += chunk()

    @pl.when(j == pl.num_programs(1) - 1)
    def _():
        y = x_ref[...] + gt_ref[...] * acc_ref[...]
        if final_norm:
            y = y * lax.rsqrt(jnp.mean(y * y, axis=-1, keepdims=True) + EPS) * gf_ref[...]
        out_ref[...] = y


def _mlp(x, h2, mod, w_up, w_down, g_final, li, is_ctx, final_norm):
    T = x.shape[0]
    tm, tf = TM_MLP, TF_MLP
    row = pl.BlockSpec((tm, D_MODEL), lambda i, j: (i, 0))
    return pl.pallas_call(
        functools.partial(_mlp_kernel, final_norm=final_norm),
        grid=(T // tm, D_FF // tf),
        in_specs=[row, row, _mod_spec(li, 5, tm, is_ctx),
                  pl.BlockSpec((None, D_MODEL, tf), lambda i, j: (j, 0, 0)),
                  pl.BlockSpec((tf, D_MODEL), lambda i, j: (j, 0)),
                  pl.BlockSpec((1, D_MODEL), lambda i, j: (0, 0))],
        out_specs=row,
        out_shape=jax.ShapeDtypeStruct((T, D_MODEL), f32),
        scratch_shapes=[pltpu.VMEM((tm, D_MODEL), f32)],
        compiler_params=_params("parallel", "arbitrary"),
        name="mlp",
    )(x, h2, mod, w_up, w_down, g_final)


def _ctx_attn_kernel(aq_ref, ak_ref, av_ref, bq_ref, bk_ref, bv_ref, cq_ref, ckv_ref,
                     dl_ref, dg_ref, sink_ref, o_ref, *, lam_init):
    S = CTX_SEQ
    lam = _diff_lambda(dl_ref, lam_init)
    out_a, out_b, out_c = [], [], []
    for s in range(aq_ref.shape[0] // S):
        rows = slice(s * S, (s + 1) * S)
        q = aq_ref[rows, :].astype(f32) * QK_SCALE2
        outs = [_diff_head(q, lambda t: ak_ref[rows, t * PAIR:(t + 1) * PAIR],
                           lambda h: av_ref[rows, h * A_V:(h + 1) * A_V], h, lam, dg_ref[...], lam_init)
                for h in range(A_HEADS)]
        out_a.append(jnp.concatenate(outs, axis=1))
        q = bq_ref[rows, :].astype(f32) * QK_SCALE2
        tiles = []
        for t in range(B_HEADS // 2):
            cols = slice(t * PAIR, (t + 1) * PAIR)
            accs = []
            for half in range(2):
                (e,), _ = _exp2_parts([_dot_nt(_head_q(q[:, cols], half), bk_ref[rows, cols])])
                accs.append(_dot(e.astype(bf16), _values_with_ones(bv_ref[rows, cols], half)))
            tiles.append(_pair_out(*accs))
        out_b.append(jnp.concatenate(tiles, axis=1))
        q = cq_ref[rows, :].astype(f32) * QK_SCALE2
        k_tile = ckv_ref[rows, 0:PAIR]
        v_variants = _swap_halves_variants(ckv_ref[rows, PAIR:2 * PAIR].astype(f32))
        tiles = []
        for t in range(C_HEADS // 2):
            q_tile = q[:, t * PAIR:(t + 1) * PAIR]
            q_swapped = pltpu.roll(q_tile, HEAD_DIM, 1)
            accs, extras = [], []
            for half in range(2):
                h = 2 * t + half
                g = h // C_GROUP
                sink2 = sink_ref[h] * LOG2E
                (e,), m = _exp2_parts([_dot_nt(_gqa_head_q(q_tile, q_swapped, half, g), k_tile)], sink2)
                accs.append(_dot(e.astype(bf16), v_variants[g * 2 + half]))
                extras.append(jnp.exp2(sink2 - m))
            tiles.append(_pair_out(accs[0], accs[1], extras[0], extras[1]))
        out_c.append(jnp.concatenate(tiles, axis=1))
    for k, outs in enumerate((out_a, out_b, out_c)):
        o_ref[k] = jnp.concatenate(outs, axis=0).astype(bf16)


def _ctx_attn(p, dl, dg, sink, li):
    S = CTX_SEQS_PER_STEP * CTX_SEQ
    nb = p.shape[0] // S
    lam_init = 0.8 - 0.6 * math.exp(-0.3 * li)

    def col(c):
        return pl.BlockSpec((S, 512), lambda b, c=c: (b, c))

    return pl.pallas_call(
        functools.partial(_ctx_attn_kernel, lam_init=lam_init),
        grid=(nb,),
        in_specs=[col(COL_AQ), col(COL_AK), col(COL_AV), col(COL_BQ), col(COL_BK), col(COL_BV), col(COL_CQ),
                  pl.BlockSpec((S, 256), lambda b: (b, COL_CKV_256)),
                  pl.BlockSpec((4, A_QK), lambda b: (0, 0)),
                  pl.BlockSpec((1, A_V), lambda b: (0, 0)),
                  pl.BlockSpec(memory_space=pltpu.SMEM)],
        out_specs=pl.BlockSpec((N_BRANCH - 1, S, BRANCH_WIDTH), lambda b: (0, b, 0)),
        out_shape=jax.ShapeDtypeStruct((N_BRANCH, p.shape[0], BRANCH_WIDTH), bf16),
        compiler_params=_params("parallel"),
        name="ctx_attn",
    )(p, p, p, p, p, p, p, p, dl, dg, sink)


def _lat_a_kernel(q_ref, k_ref, v_ref, ck_ref, cv_ref, cosq_ref, sinq_ref, cosk_ref, sink_ref,
                  dl_ref, dg_ref, o_ref, kk_ref, vv_ref, *, lam_init):
    L = LAT_SEQ

    @pl.when(pl.program_id(1) == 0)
    def _():
        kk_ref[0:L, :] = _rope(k_ref[...], cosk_ref[...], sink_ref[...]).astype(bf16)
        kk_ref[L:, :] = ck_ref[...].astype(bf16)
        vv_ref[0:L, :] = v_ref[...].astype(bf16)
        vv_ref[L:, :] = cv_ref[...].astype(bf16)

    lam = _diff_lambda(dl_ref, lam_init)
    q = _rope(q_ref[...], cosq_ref[...], sinq_ref[...]) * QK_SCALE2
    outs = [_diff_head(q, lambda t: kk_ref[:, t * PAIR:(t + 1) * PAIR], lambda h: vv_ref[:, h * A_V:(h + 1) * A_V],
                       h, lam, dg_ref[...], lam_init) for h in range(A_HEADS)]
    o_ref[...] = jnp.concatenate(outs, axis=1).astype(bf16)


def _lat_b_kernel(q_ref, k_ref, v_ref, ck_ref, cv_ref, bias_ref, _, o_ref, kk_ref, va_ref):
    L, W = LAT_SEQ, NA_KEY_ROWS * GRID_W
    qb = pl.program_id(1)

    @pl.when(qb == 0)
    def _():
        kk_ref[0:L, :] = k_ref[...]
        kk_ref[L:, :] = ck_ref[...].astype(bf16)
        for t in range(B_HEADS // 2):
            cols = slice(t * PAIR, (t + 1) * PAIR)
            for half in range(2):
                va_ref[2 * t + half, 0:L, :] = _values_with_ones(v_ref[:, cols], half)
                va_ref[2 * t + half, L:, :] = _values_with_ones(cv_ref[:, cols], half)

    start = pl.multiple_of(_na_key_row0(qb) * GRID_W, 256)
    q = q_ref[...].astype(f32) * QK_SCALE2
    tiles = []
    for t in range(B_HEADS // 2):
        cols = slice(t * PAIR, (t + 1) * PAIR)
        accs = []
        for half in range(2):
            h = 2 * t + half
            qh = _head_q(q[:, cols], half)
            s_loc = _dot_nt(qh, kk_ref[pl.ds(start, W), cols]) + bias_ref[h].astype(f32)
            s_ctx = _dot_nt(qh, kk_ref[L:, cols])
            (e_loc, e_ctx), _m = _exp2_parts([s_loc, s_ctx])
            accs.append(_dot(e_loc.astype(bf16), va_ref[h, pl.ds(start, W), :])
                        + _dot(e_ctx.astype(bf16), va_ref[h, L:, :]))
        tiles.append(_pair_out(*accs))
    o_ref[...] = jnp.concatenate(tiles, axis=1).astype(bf16)


def _lat_c_kernel(q_ref, kv_ref, ck_ref, cv_ref, cosq_ref, sinq_ref, cosk_ref, sink_ref, snk_ref,
                  _, o_ref, kk_ref, va_ref):
    L, W = LAT_SEQ, SWA_KEYS
    tq = q_ref.shape[0]
    qb = pl.program_id(1)

    @pl.when(qb == 0)
    def _():
        kk_ref[0:L, :] = _rope(kv_ref[:, 0:PAIR], cosk_ref[...], sink_ref[...]).astype(bf16)
        kk_ref[L:, :] = ck_ref[...].astype(bf16)
        lat = _swap_halves_variants(kv_ref[:, PAIR:2 * PAIR].astype(f32))
        ctx = _swap_halves_variants(cv_ref[...])
        for idx in range(4):
            va_ref[idx, 0:L, :] = lat[idx]
            va_ref[idx, L:, :] = ctx[idx]

    start = pl.multiple_of(jnp.clip(qb * tq - SWA_WINDOW, 0, L - W), 128)
    q = _rope(q_ref[...], cosq_ref[...], sinq_ref[...]) * QK_SCALE2
    qpos = qb * tq + lax.broadcasted_iota(jnp.int32, (tq, W), 0)
    kpos = start + lax.broadcasted_iota(jnp.int32, (tq, W), 1)
    valid = jnp.abs(qpos - kpos) <= SWA_WINDOW
    tiles = []
    for t in range(C_HEADS // 2):
        q_tile = q[:, t * PAIR:(t + 1) * PAIR]
        q_swapped = pltpu.roll(q_tile, HEAD_DIM, 1)
        accs, extras = [], []
        for half in range(2):
            h = 2 * t + half
            g = h // C_GROUP
            qh = _gqa_head_q(q_tile, q_swapped, half, g)
            s_loc = jnp.where(valid, _dot_nt(qh, kk_ref[pl.ds(start, W), :]), NEG_INF)
            s_ctx = _dot_nt(qh, kk_ref[L:, :])
            sink2 = snk_ref[h] * LOG2E
            (e_loc, e_ctx), m = _exp2_parts([s_loc, s_ctx], sink2)
            idx = g * 2 + half
            accs.append(_dot(e_loc.astype(bf16), va_ref[idx, pl.ds(start, W), :])
                        + _dot(e_ctx.astype(bf16), va_ref[idx, L:, :]))
            extras.append(jnp.exp2(sink2 - m))
        tiles.append(_pair_out(accs[0], accs[1], extras[0], extras[1]))
    o_ref[...] = jnp.concatenate(tiles, axis=1).astype(bf16)


def _lat_attn(p, caches, na_dense, dl, dg, sink, cos_t, sin_t, li):
    cak, cav, cbk, cbv, cck, ccv = caches
    L, tq = LAT_SEQ, TQ_ATTN
    assert tq == NA_Q_ROWS * GRID_W and SWA_KEYS >= tq + 2 * SWA_WINDOW
    nq = L // tq
    nb_lat = p.shape[0] // L
    lam_init = 0.8 - 0.6 * math.exp(-0.3 * li)

    def qcol(c, tq=tq):
        return pl.BlockSpec((tq, 512), lambda b, i, c=c: (b * (L // tq) + i, c))

    def kcol(c, width=512):
        return pl.BlockSpec((L, width), lambda b, i, c=c: (b, c))

    def cache(width):
        return pl.BlockSpec((None, None, CTX_SEQ, width), lambda b, i: (b, li, 0, 0))

    tab_q = pl.BlockSpec((tq, 512), lambda b, i: (i, 0))
    tab_qa = pl.BlockSpec((TQ_ATTN_A, 512), lambda b, i: (i, 0))
    tab_k = pl.BlockSpec((L, 512), lambda b, i: (0, 0))
    tab_k128 = pl.BlockSpec((L, 128), lambda b, i: (0, 0))
    o_sds = jax.ShapeDtypeStruct((N_BRANCH, p.shape[0], BRANCH_WIDTH), bf16)
    anyspec = pl.BlockSpec(memory_space=pl.ANY)

    def o_spec(branch, tq=tq):
        return pl.BlockSpec((None, tq, BRANCH_WIDTH), lambda b, i: (branch, b * (L // tq) + i, 0))

    kv_scratch = [pltpu.VMEM((L + CTX_SEQ, 512), bf16), pltpu.VMEM((L + CTX_SEQ, 512), bf16)]
    smem = pl.BlockSpec(memory_space=pltpu.SMEM)
    cp = _params("parallel", "arbitrary")

    o_stack = pl.pallas_call(
        functools.partial(_lat_a_kernel, lam_init=lam_init),
        grid=(nb_lat, L // TQ_ATTN_A),
        in_specs=[qcol(COL_AQ, TQ_ATTN_A), kcol(COL_AK), kcol(COL_AV), cache(512), cache(512),
                  tab_qa, tab_qa, tab_k, tab_k,
                  pl.BlockSpec((4, A_QK), lambda b, i: (0, 0)), pl.BlockSpec((1, A_V), lambda b, i: (0, 0))],
        out_specs=o_spec(0, TQ_ATTN_A), out_shape=o_sds, scratch_shapes=kv_scratch,
        compiler_params=cp, name="lat_attn_a",
    )(p, p, p, cak, cav, cos_t, sin_t, cos_t, sin_t, dl, dg)

    o_stack = pl.pallas_call(
        _lat_b_kernel,
        grid=(nb_lat, nq),
        in_specs=[qcol(COL_BQ), kcol(COL_BK), kcol(COL_BV), cache(512), cache(512),
                  pl.BlockSpec((B_HEADS, tq, NA_KEY_ROWS * GRID_W), lambda b, i: (0, i, 0)), anyspec],
        out_specs=o_spec(1), out_shape=o_sds,
        scratch_shapes=[pltpu.VMEM((L + CTX_SEQ, 512), bf16), pltpu.VMEM((B_HEADS, L + CTX_SEQ, PAIR), bf16)],
        input_output_aliases={6: 0}, compiler_params=cp, name="lat_attn_b",
    )(p, p, p, cbk, cbv, na_dense, o_stack)

    return pl.pallas_call(
        _lat_c_kernel,
        grid=(nb_lat, nq),
        in_specs=[qcol(COL_CQ), kcol(COL_CKV_256, 256), cache(PAIR), cache(PAIR),
                  tab_q, tab_q, tab_k128, tab_k128, smem, anyspec],
        out_specs=o_spec(2), out_shape=o_sds,
        scratch_shapes=[pltpu.VMEM((L + CTX_SEQ, PAIR), bf16), pltpu.VMEM((4, L + CTX_SEQ, PAIR), bf16)],
        input_output_aliases={9: 0}, compiler_params=cp, name="lat_attn_c",
    )(p, p, cck, ccv, cos_t, sin_t, cos_t, sin_t, sink, o_stack)


def _na_key_row0(qb):
    lo, hi = NA_Q_ROWS * qb - NA_ROWS // 2, GRID_ROWS - NA_KEY_ROWS
    return min(max(lo, 0), hi) if isinstance(qb, int) else jnp.clip(lo, 0, hi)


def _na_bias_kernel(rb_ref, o_ref):
    h = pl.program_id(0)
    W = GRID_W
    n_dc = 2 * NA_COLS - 1
    n_dr = 2 * NA_ROWS - 1
    qc = lax.broadcasted_iota(jnp.int32, (W, 2 * W), 0)
    lane = lax.broadcasted_iota(jnp.int32, (W, 2 * W), 1)
    second = lane >= W
    kc = jnp.where(second, lane - W, lane)
    dc = jnp.clip(kc - qc, -(NA_COLS - 1), NA_COLS - 1) + NA_COLS - 1
    c0 = jnp.clip(qc - NA_COLS // 2, 0, W - NA_COLS)
    col_ok = (kc >= c0) & (kc < c0 + NA_COLS)
    base = h * (n_dr * n_dc)

    def pair_tile(dr0):
        t = jnp.zeros((W, 2 * W), f32)
        for j in range(n_dc):
            lo = rb_ref[base + dr0 * n_dc + j] if 0 <= dr0 < n_dr else 0.0
            hi = rb_ref[base + (dr0 + 1) * n_dc + j] if 0 <= dr0 + 1 < n_dr else 0.0
            t = jnp.where(dc == j, jnp.where(second, hi, lo), t)
        return jnp.where(col_ok, t * LOG2E, NEG_INF)

    tiles = {dr0: pair_tile(dr0) for dr0 in range(-1, n_dr)}
    neg = jnp.full((W, 2 * W), NEG_INF, f32)
    for qr in range(GRID_ROWS):
        r0 = min(max(qr - NA_ROWS // 2, 0), GRID_ROWS - NA_ROWS)
        k0 = _na_key_row0(qr // NA_Q_ROWS)
        assert k0 % 2 == 0 and k0 <= r0 and r0 + NA_ROWS <= k0 + NA_KEY_ROWS
        for pr in range(NA_KEY_ROWS // 2):
            kr = k0 + 2 * pr
            ok0 = r0 <= kr < r0 + NA_ROWS
            ok1 = r0 <= kr + 1 < r0 + NA_ROWS
            if not (ok0 or ok1):
                t = neg
            else:
                t = tiles[kr - qr + NA_ROWS - 1]
                if not ok0:
                    t = jnp.where(second, t, NEG_INF)
                if not ok1:
                    t = jnp.where(second, NEG_INF, t)
            o_ref[qr * W:(qr + 1) * W, 2 * pr * W:(2 * pr + 2) * W] = t.astype(bf16)


def _na_bias_dense(rel_bias):
    kw = NA_KEY_ROWS * GRID_W
    return pl.pallas_call(
        _na_bias_kernel,
        grid=(B_HEADS,),
        in_specs=[pl.BlockSpec(memory_space=pltpu.SMEM)],
        out_specs=pl.BlockSpec((None, LAT_SEQ, kw), lambda h: (h, 0, 0)),
        out_shape=jax.ShapeDtypeStruct((B_HEADS, LAT_SEQ, kw), bf16),
        compiler_params=_params("parallel"),
        name="na_bias",
    )(rel_bias.reshape(-1))


def _hy_filter_kernel(z_ref, w1_ref, b1_ref, w2_ref, b2_ref, fr_ref, w3f_ref, w3b_ref, dcf_ref, dcb_ref,
                      bias_ref, fc_ref, fs_ref, o_ref, *, L):
    z = z_ref[...]
    fr = fr_ref[...]
    h = jnp.sin(fr * (_dot_hi(z, w1_ref[...]) + b1_ref[...]))
    h = jnp.sin(fr * (_dot_hi(h, w2_ref[...]) + b2_ref[...]))
    t = z[:, 0:1]
    hf = _dot_hi(h, w3f_ref[...]) * jnp.exp(-t * jnp.abs(dcf_ref[...]))
    hb = _dot_hi(h, w3b_ref[...]) * jnp.exp(-t * jnp.abs(dcb_ref[...]))
    ssum = hf + hb
    bias = bias_ref[...]
    ga = _dot_3pass(fc_ref[...], ssum) + bias
    gb = _dot_3pass(fs_ref[...], hb - hf)
    g_nyq = _dot_hi(fs_ref[0:8, :], ssum)[0:1] + bias
    row0 = lax.broadcasted_iota(jnp.int32, ga.shape, 0) == 0
    inv = 1.0 / L
    o_ref[0] = jnp.where(row0, 0.5 * ga, ga) * inv
    o_ref[1] = jnp.where(row0, 0.0, gb) * inv
    o_ref[2] = jnp.where(row0, 0.5 * g_nyq, ga) * inv


def _hy_filter_tables(L, w1p, b1, w2, b2, w3, freq, decay, hy_bias):
    cb = HY_CB
    ncb = HY_WIDTH // cb
    z = jnp.asarray(_hyena_features(L))
    fc, fs = (jnp.asarray(a) for a in _dft_tables(L))
    full = lambda shape: pl.BlockSpec(shape, lambda o, c: (0,) * len(shape))
    fwd = lambda rows: pl.BlockSpec((rows, cb), lambda o, c: (0, o * 2 * ncb + c))
    bwd = lambda rows: pl.BlockSpec((rows, cb), lambda o, c: (0, o * 2 * ncb + ncb + c))
    return pl.pallas_call(
        functools.partial(_hy_filter_kernel, L=L),
        grid=(HY_ORDER, ncb),
        in_specs=[full((L, HY_EMB_PAD)), full((HY_EMB_PAD, HY_FFN)), full((1, HY_FFN)),
                  full((HY_FFN, HY_FFN)), full((1, HY_FFN)), full((1, HY_FFN)),
                  fwd(HY_FFN), bwd(HY_FFN), fwd(1), bwd(1),
                  pl.BlockSpec((None, 1, cb), lambda o, c: (o, 0, c)),
                  full((L, L)), full((L, L))],
        out_specs=pl.BlockSpec((None, 3, L, cb), lambda o, c: (o, 0, 0, c)),
        out_shape=jax.ShapeDtypeStruct((HY_ORDER, 3, L, HY_WIDTH), f32),
        compiler_params=_params("parallel", "parallel"),
        name="hy_filter",
    )(z, w1p, b1, w2, b2, freq, w3, w3, decay, decay, hy_bias.reshape(HY_ORDER, 1, HY_WIDTH), fc, fs)


def _hyena_kernel(v_ref, x1_ref, x2_ref, wv_ref, w1_ref, w2_ref, fwd_ref, inv_ref, tab_ref, _, o_ref):
    L = inv_ref.shape[0]
    cb = v_ref.shape[1]
    n_seq = v_ref.shape[0] // L
    row = lax.broadcasted_iota(jnp.int32, (L, cb), 0)

    def short_conv(u_ref, w_ref):
        w = w_ref[...]
        outs = []
        for s in range(n_seq):
            u = u_ref[s * L:(s + 1) * L, :].astype(f32)
            prev = jnp.where(row == 0, 0.0, pltpu.roll(u, 1, 0))
            nxt = jnp.where(row == L - 1, 0.0, pltpu.roll(u, L - 1, 0))
            outs.append(prev * w[0:1] + u * w[1:2] + nxt * w[2:3])
        return jnp.concatenate(outs, axis=1)

    def long_conv(u, order):
        ab = _dot(fwd_ref[...], u.astype(bf16))
        t0, t1, t2 = tab_ref[order, 0], tab_ref[order, 1], tab_ref[order, 2]
        pq = []
        for s in range(n_seq):
            a, b = ab[:L, s * cb:(s + 1) * cb], ab[L:, s * cb:(s + 1) * cb]
            pq.append(jnp.concatenate([(a * t0 + b * t1).astype(bf16), (b * t2 - a * t1).astype(bf16)], axis=0))
        return _dot(inv_ref[...], jnp.concatenate(pq, axis=1))

    z = short_conv(x1_ref, w1_ref) * long_conv(short_conv(v_ref, wv_ref), 0)
    y = (short_conv(x2_ref, w2_ref) * long_conv(z, 1)).astype(bf16)
    o_ref[...] = jnp.concatenate([y[:, s * cb:(s + 1) * cb] for s in range(n_seq)], axis=0)


def _hyena(p, o_stack, hy_short, tabs, L):
    cb = HY_CB
    ncb = HY_WIDTH // cb
    c0 = HY_COL0 // cb
    rows = min(HY_SEQS_PER_STEP[L] * L, p.shape[0])
    nb = p.shape[0] // rows
    fc, fs = _dft_tables(L)
    fwd = jnp.asarray(np.concatenate([fc, fs], axis=0), bf16)
    inv = jnp.asarray(np.concatenate([fc, fs.T], axis=1), bf16)

    def part(k):
        return pl.BlockSpec((rows, cb), lambda c, b, k=k: (b, c0 + k * ncb + c))

    def wpart(k):
        return pl.BlockSpec((3, cb), lambda c, b, k=k: (0, k * ncb + c))

    return pl.pallas_call(
        _hyena_kernel,
        grid=(ncb, nb),
        in_specs=[part(0), part(1), part(2), wpart(0), wpart(1), wpart(2),
                  pl.BlockSpec((2 * L, L), lambda c, b: (0, 0)), pl.BlockSpec((L, 2 * L), lambda c, b: (0, 0)),
                  pl.BlockSpec((HY_ORDER, 3, L, cb), lambda c, b: (0, 0, 0, c)),
                  pl.BlockSpec(memory_space=pl.ANY)],
        out_specs=pl.BlockSpec((None, rows, cb), lambda c, b: (N_BRANCH - 1, b, c)),
        out_shape=jax.ShapeDtypeStruct(o_stack.shape, o_stack.dtype),
        input_output_aliases={9: 0},
        compiler_params=_params("parallel", "arbitrary"),
        name="hyena_%d" % L,
    )(p, p, p, hy_short, hy_short, hy_short, fwd, inv, tabs, o_stack)


def kernel(x_prompt, x_sample, cache_a_k, cache_a_v, cache_b_k, cache_b_v, cache_c_k, cache_c_v, c, c_ctx, w_ada, b_ada, g_mix, w_in, diff_lambda, diff_norm_g, na_bias, swa_sink, hy_short, hy_w1, hy_b1, hy_w2, hy_b2, hy_w3, hy_freq, hy_decay, hy_bias, w_branch, w_out, g_mlp, w_up, w_down, g_final):
    nb_ctx, nb_lat = x_prompt.shape[0], x_sample.shape[0]
    assert x_prompt.shape[1:] == (CTX_SEQ, D_MODEL) and x_sample.shape[1:] == (LAT_SEQ, D_MODEL)
    assert nb_lat <= CTX_MOD_ROW and (nb_ctx * CTX_SEQ) % TM_INPROJ == 0

    x_ctx = x_prompt.reshape(nb_ctx * CTX_SEQ, D_MODEL)
    x_lat = x_sample.reshape(nb_lat * LAT_SEQ, D_MODEL)
    cv = jnp.zeros((N_MOD_ROWS, D_MODEL), f32).at[:nb_lat].set(c).at[CTX_MOD_ROW].set(c_ctx)
    mod = _adaln_all(cv, w_ada, b_ada).reshape(DEPTH * N_MOD_ROWS * 6, 1, D_MODEL)
    cos_t, sin_t = (jnp.asarray(a) for a in _rope_tables())
    caches = (cache_a_k.reshape(nb_lat, DEPTH, CTX_SEQ, 512), cache_a_v.reshape(nb_lat, DEPTH, CTX_SEQ, 512),
              cache_b_k.reshape(nb_lat, DEPTH, CTX_SEQ, 512), cache_b_v.reshape(nb_lat, DEPTH, CTX_SEQ, 512),
              cache_c_k.reshape(nb_lat, DEPTH, CTX_SEQ, 128), cache_c_v.reshape(nb_lat, DEPTH, CTX_SEQ, 128))

    kv = None
    for li in range(DEPTH):
        w_mix = _cast_mix_weights(w_in, li)
        w5, wu, wd = _cast_layer_weights(w_in, w_out, w_up, w_down, li)
        wb = w_branch[li].astype(bf16)
        g1 = g_mix[li].reshape(1, D_MODEL)
        g2 = g_mlp[li].reshape(1, D_MODEL)
        dl = diff_lambda[li]
        dg = diff_norm_g[li].reshape(1, A_V)
        sink = swa_sink[li]

        w1p = jnp.pad(hy_w1[li], ((0, HY_EMB_PAD - HY_EMB), (0, 0)))
        hy_args = (w1p, hy_b1[li].reshape(1, HY_FFN), hy_w2[li], hy_b2[li].reshape(1, HY_FFN), hy_w3[li],
                   hy_freq[li].reshape(1, HY_FFN), hy_decay[li].reshape(1, -1), hy_bias[li])
        gf = g_final.reshape(1, D_MODEL)
        final = li == DEPTH - 1

        p, kv, h = _inproj(x_ctx, mod, g1, w_mix, li, True, kv)
        o_stack = _ctx_attn(p, dl, dg, sink, li)
        o_stack = _hyena(p, o_stack, hy_short[li], _hy_filter_tables(CTX_SEQ, *hy_args), CTX_SEQ)
        x_ctx, h2 = _merge(x_ctx, mod, h, o_stack, w5, wb, g2, li, True)
        x_ctx = _mlp(x_ctx, h2, mod, wu, wd, gf, li, True, final_norm=final)

        p, h = _inproj(x_lat, mod, g1, w_mix, li, False)
        o_stack = _lat_attn(p, caches, _na_bias_dense(na_bias[li]), dl, dg, sink, cos_t, sin_t, li)
        o_stack = _hyena(p, o_stack, hy_short[li], _hy_filter_tables(LAT_SEQ, *hy_args), LAT_SEQ)
        x_lat, h2 = _merge(x_lat, mod, h, o_stack, w5, wb, g2, li, False)
        x_lat = _mlp(x_lat, h2, mod, wu, wd, gf, li, False, final_norm=final)

    y_prompt = x_ctx.reshape(nb_ctx, CTX_SEQ, D_MODEL)
    y_sample = x_lat.reshape(nb_lat, LAT_SEQ, D_MODEL)
    kv_shapes = ((2, A_HEADS, A_QK), (A_HEADS, A_V), (B_HEADS, HEAD_DIM), (B_HEADS, HEAD_DIM),
                 (C_KV_HEADS, HEAD_DIM), (C_KV_HEADS, HEAD_DIM))
    new_kv = tuple(a.reshape((nb_ctx, DEPTH, CTX_SEQ) + s) for a, s in zip(kv, kv_shapes))
    return (y_prompt, y_sample) + new_kv
```

```python
import functools
import math

import numpy as np
import jax
import jax.numpy as jnp
from jax import lax
from jax.experimental import pallas as pl
from jax.experimental.pallas import tpu as pltpu

f32 = jnp.float32
bf16 = jnp.bfloat16

D_MODEL = 2048
DEPTH = 2
CTX_SEQ = 256
LAT_SEQ = 1024
GRID_W = 64
GRID_ROWS = LAT_SEQ // GRID_W
BRANCH_WIDTH = 512
N_BRANCH = 4
HEAD_DIM = 64
A_QK = 64
A_V = 128
A_HEADS = 4
B_HEADS = 8
NA_ROWS = 8
NA_COLS = 16
C_HEADS = 8
C_KV_HEADS = 2
C_GROUP = 4
SWA_WINDOW = 128
SWA_KEYS = 512
NA_Q_ROWS = 8
NA_KEY_ROWS = 12
HY_WIDTH = 512
HY_ORDER = 2
HY_BANDS = 16
HY_EMB = 1 + 2 * HY_BANDS
HY_EMB_PAD = 128
HY_FFN = 64
D_FF = 4 * D_MODEL
ROPE_BASE = 10000.0
EPS = 1e-6
NEG_INF = -1e30
MIX_COLS = 5376
GATE_COLS = N_BRANCH * D_MODEL
COL_AQ, COL_BQ, COL_CQ, COL_AK, COL_AV, COL_BK, COL_BV = 0, 1, 2, 3, 4, 5, 6
COL_CKV_256 = 14
KV_PARTS = ((1536, 2048), (2048, 2560), (2560, 3072), (3072, 3584), (3584, 3712), (3712, 3840))
HY_COL0 = 3840
N_MOD_ROWS = 16
CTX_MOD_ROW = 8

VMEM_LIMIT = 56 * 1024 * 1024
TM_INPROJ = 256
TM_MERGE = 512
TM_MLP = 512
TF_MLP = 1024
TQ_ATTN = 512
TQ_ATTN_C = 256
CTX_SEQS_PER_STEP = 1
TN_ADA = 1024
HY_CB = 256
HY_SEQS_PER_STEP = {CTX_SEQ: 8, LAT_SEQ: 2}
CAST_ROWS = 512


def _params(*sem):
    return pltpu.CompilerParams(dimension_semantics=sem, vmem_limit_bytes=VMEM_LIMIT)


def _dot(a, b):
    return jnp.dot(a, b, preferred_element_type=f32)


def _dot_nt(a, b):
    return lax.dot_general(a, b, (((1,), (1,)), ((), ())), preferred_element_type=f32)


def _dot_hi(a, b):
    return jnp.dot(a, b, preferred_element_type=f32, precision=lax.Precision.HIGHEST)


def _dot_3pass(a, b):
    a_hi, b_hi = a.astype(bf16), b.astype(bf16)
    a_lo = (a - a_hi.astype(f32)).astype(bf16)
    b_lo = (b - b_hi.astype(f32)).astype(bf16)
    return _dot(a_hi, b_hi) + (_dot(a_hi, b_lo) + _dot(a_lo, b_hi))


@functools.lru_cache(maxsize=None)
def _rope_tables():
    half = HEAD_DIM // 2
    nf = half // 2
    inv = ROPE_BASE ** (-np.arange(nf, dtype=np.float64) / nf)
    t = np.arange(LAT_SEQ)
    pos = np.stack([t // GRID_W, t % GRID_W], axis=1).astype(np.float64)
    lane = np.arange(HEAD_DIM)
    ang = pos[:, lane // half] * inv[lane % nf][None, :]
    first = (lane % half) < nf
    cos = np.cos(ang)
    sin = np.where(first[None, :], -np.sin(ang), np.sin(ang))
    reps = 512 // HEAD_DIM
    return (np.tile(cos, (1, reps)).astype(np.float32), np.tile(sin, (1, reps)).astype(np.float32))


@functools.lru_cache(maxsize=None)
def _dft_tables(L):
    f = np.arange(L, dtype=np.int64)
    prod = (f[:, None] * f[None, :]) % (2 * L)
    ang = np.pi * prod.astype(np.float64) / L
    fc = np.cos(ang)
    fs = np.sin(ang)
    fs[0, :] = np.where(f % 2 == 0, 1.0, -1.0)
    return fc.astype(np.float32), fs.astype(np.float32)


@functools.lru_cache(maxsize=None)
def _hyena_features(L):
    n = np.arange(L, dtype=np.float64)[:, None]
    t = n / max(L - 1, 1)
    w = 2.0 * math.pi * n / L
    bands = np.linspace(1e-4, HY_BANDS - 1, HY_BANDS, dtype=np.float64)[None, :]
    z = np.concatenate([t, np.cos(bands * w), -np.sin(bands * w)], axis=-1)
    z = np.pad(z, ((0, 0), (0, HY_EMB_PAD - HY_EMB)))
    return z.astype(np.float32)


def _modulated_norm(x, g, scale, shift):
    ms = jnp.mean(x * x, axis=-1, keepdims=True)
    return x * lax.rsqrt(ms + EPS) * (g * (1.0 + scale)) + shift


def _rope(x, cos, sin_signed):
    x = x.astype(f32)
    n = x.shape[-1]
    lane = lax.broadcasted_iota(jnp.int32, x.shape, 1)
    first = (lane & (HEAD_DIM // 2 - 1)) < (HEAD_DIM // 4)
    partner = jnp.where(first, pltpu.roll(x, n - HEAD_DIM // 4, 1), pltpu.roll(x, HEAD_DIM // 4, 1))
    return x * cos + partner * sin_signed


LOG2E = 1.4426950408889634
QK_SCALE2 = HEAD_DIM ** -0.5 * LOG2E
PAIR = 2 * HEAD_DIM


def _lo_half(shape):
    return lax.broadcasted_iota(jnp.int32, shape, 1) < HEAD_DIM


def _keep_half(x, half, fill):
    lo = _lo_half(x.shape)
    return jnp.where(lo if half == 0 else jnp.logical_not(lo), x, fill)


def _head_q(q_tile, half):
    return _keep_half(q_tile, half, 0.0).astype(bf16)


def _values_with_ones(v_tile, half):
    return _keep_half(v_tile.astype(f32), half, 1.0).astype(bf16)


def _exp2_parts(scores, sink2=None):
    m = jnp.max(scores[0], axis=-1, keepdims=True)
    for s in scores[1:]:
        m = jnp.maximum(m, jnp.max(s, axis=-1, keepdims=True))
    if sink2 is not None:
        m = jnp.maximum(m, sink2)
    return [jnp.exp2(s - m) for s in scores], m


def _pair_out(acc_even, acc_odd, extra_even=None, extra_odd=None):
    lo = _lo_half(acc_even.shape)
    num = jnp.where(lo, acc_even, acc_odd)
    den = pltpu.roll(jnp.where(lo, acc_odd, acc_even), HEAD_DIM, 1)
    if extra_even is not None:
        den = den + jnp.where(lo, extra_even, extra_odd)
    return num / den


def _swap_halves_variants(v_tile):
    sw = pltpu.roll(v_tile, HEAD_DIM, 1)
    lo = _lo_half(v_tile.shape)
    tiles = (jnp.where(lo, v_tile, 1.0), jnp.where(lo, 1.0, sw), jnp.where(lo, sw, 1.0), jnp.where(lo, 1.0, v_tile))
    return [t.astype(bf16) for t in tiles]


def _gqa_head_q(q_tile, q_swapped, half, g):
    return _keep_half(q_tile if half == g else q_swapped, g, 0.0).astype(bf16)


def _diff_lambda(dl_ref, lam_init):
    dl = dl_ref[...]
    a = jnp.sum(dl[0:1] * dl[1:2], axis=-1, keepdims=True)
    b = jnp.sum(dl[2:3] * dl[3:4], axis=-1, keepdims=True)
    return jnp.exp(a) - jnp.exp(b) + lam_init


def _diff_head(q, k_of, v_of, h, lam, dg, lam_init):
    res = []
    for c in range(2):
        t = c * (A_HEADS // 2) + h // 2
        s = _dot_nt(_head_q(q[:, t * PAIR:(t + 1) * PAIR], h % 2), k_of(t))
        (e,), _ = _exp2_parts([s])
        res.append((_dot(e.astype(bf16), v_of(h)), jnp.sum(e, axis=-1, keepdims=True)))
    o = res[0][0] * (1.0 / res[0][1]) - res[1][0] * (lam / res[1][1])
    o = o * lax.rsqrt(jnp.mean(o * o, axis=-1, keepdims=True) + EPS) * dg
    return o * (1.0 - lam_init)


def _cast_kernel(src_ref, *rest):
    o_ref = rest[-1]
    x = src_ref[...]
    o_ref[...] = x.reshape(o_ref.shape).astype(bf16)


def _cast_call(name, grid, src, in_spec, out_spec, out_shape, dst=None):
    in_specs, args, aliases = [in_spec], [src], {}
    if dst is not None:
        in_specs.append(pl.BlockSpec(memory_space=pl.ANY))
        args.append(dst)
        aliases = {1: 0}
    return pl.pallas_call(
        _cast_kernel, grid=grid, in_specs=in_specs, out_specs=out_spec,
        out_shape=jax.ShapeDtypeStruct(out_shape, bf16), input_output_aliases=aliases,
        compiler_params=_params(*(("parallel",) * len(grid))), name=name,
    )(*args)


def _cast_mix_kernel(unit_ref, src_ref, o_ref):
    o_ref[...] = src_ref[0].astype(bf16)


def _cast_mix_weights(w_in, li):
    U = 256
    src_unit = (0, 1, 6, 7, 12, 13, 2, 3, 4, 5, 8, 9, 10, 11) + tuple(range(14, MIX_COLS // U))
    return pl.pallas_call(
        _cast_mix_kernel,
        grid_spec=pltpu.PrefetchScalarGridSpec(
            num_scalar_prefetch=1, grid=(len(src_unit),),
            in_specs=[pl.BlockSpec((pl.Element(1), pl.Element(D_MODEL), pl.Element(U)),
                                   lambda u, unit: (li, 0, pl.multiple_of(unit[u] * U, U)))],
            out_specs=pl.BlockSpec((D_MODEL, U), lambda u, unit: (0, u))),
        out_shape=jax.ShapeDtypeStruct((D_MODEL, MIX_COLS), bf16),
        compiler_params=_params("parallel"), name="cast_mix",
    )(jnp.asarray(src_unit, jnp.int32), w_in)


def _cast_layer_weights(w_in, w_out, w_up, w_down, li):
    D, R, tf = D_MODEL, CAST_ROWS, TF_MLP
    nr = D // R
    w5 = _cast_call(
        "cast_gate", (N_BRANCH, nr), w_in,
        pl.BlockSpec((pl.Element(1), pl.Element(R), pl.Element(D)),
                     lambda n, r: (li, pl.multiple_of(r * R, R), pl.multiple_of(MIX_COLS + n * D, 128))),
        pl.BlockSpec((None, R, D), lambda n, r: (n, r, 0)), (N_BRANCH + 1, D, D))
    w5 = _cast_call(
        "cast_wout", (nr,), w_out,
        pl.BlockSpec((None, R, D), lambda r: (li, r, 0)),
        pl.BlockSpec((None, R, D), lambda r: (N_BRANCH, r, 0)), (N_BRANCH + 1, D, D), dst=w5)
    wu = _cast_call(
        "cast_wu", (D_FF // tf, nr), w_up,
        pl.BlockSpec((None, R, tf), lambda j, r: (li, r, j)),
        pl.BlockSpec((None, R, tf), lambda j, r: (j, r, 0)), (D_FF // tf, D, tf))
    wd = _cast_call(
        "cast_wd", (D_FF // R,), w_down,
        pl.BlockSpec((None, R, D), lambda r: (li, r, 0)),
        pl.BlockSpec((R, D), lambda r: (r, 0)), (D_FF, D))
    return w5, wu, wd


def _ada_kernel(cv_ref, w_ref, b_ref, o_ref):
    cv = cv_ref[...]
    s = (cv * jax.nn.sigmoid(cv)).astype(bf16)
    o_ref[...] = _dot(s, w_ref[...].astype(bf16)) + b_ref[...]


def _adaln_all(cv, w_ada, b_ada):
    n6 = 6 * D_MODEL
    return pl.pallas_call(
        _ada_kernel,
        grid=(DEPTH, n6 // TN_ADA),
        in_specs=[pl.BlockSpec((N_MOD_ROWS, D_MODEL), lambda l, j: (0, 0)),
                  pl.BlockSpec((None, D_MODEL, TN_ADA), lambda l, j: (l, 0, j)),
                  pl.BlockSpec((None, 1, TN_ADA), lambda l, j: (l, 0, j))],
        out_specs=pl.BlockSpec((None, N_MOD_ROWS, TN_ADA), lambda l, j: (l, 0, j)),
        out_shape=jax.ShapeDtypeStruct((DEPTH, N_MOD_ROWS, n6), f32),
        compiler_params=_params("parallel", "parallel"),
        name="adaln",
    )(cv, w_ada, b_ada.reshape(DEPTH, 1, n6))


def _mod_spec(li, k, tm, is_ctx):
    def index(i, *_):
        r = CTX_MOD_ROW if is_ctx else (i * tm) // LAT_SEQ
        return ((li * N_MOD_ROWS + r) * 6 + k, 0, 0)

    return pl.BlockSpec((None, 1, D_MODEL), index)


def _inproj_kernel(x_ref, sh_ref, sc_ref, g_ref, w_ref, *rest, emit_kv):
    outs = rest[-(2 + len(KV_PARTS)) if emit_kv else -2:]
    h = _modulated_norm(x_ref[...], g_ref[...], sc_ref[...], sh_ref[...]).astype(bf16)
    outs[-1][...] = h
    res = _dot(h, w_ref[...])
    outs[0][...] = res.astype(bf16)
    if emit_kv:
        for ref, (lo, hi) in zip(outs[1:-1], KV_PARTS):
            ref[...] = res[:, lo:hi]


def _inproj(x, mod, g, w, li, is_ctx, kv_prev=None):
    T = x.shape[0]
    tm = TM_INPROJ
    assert tm == CTX_SEQ
    row = lambda width: pl.BlockSpec((tm, width), lambda i: (i, 0))
    in_specs = [row(D_MODEL), _mod_spec(li, 0, tm, is_ctx), _mod_spec(li, 1, tm, is_ctx),
                pl.BlockSpec((1, D_MODEL), lambda i: (0, 0)),
                pl.BlockSpec((D_MODEL, MIX_COLS), lambda i: (0, 0), pipeline_mode=pl.Buffered(1))]
    args = [x, mod, mod, g, w]
    out_specs = [row(MIX_COLS), row(D_MODEL)]
    out_shape = [jax.ShapeDtypeStruct((T, MIX_COLS), bf16), jax.ShapeDtypeStruct((T, D_MODEL), bf16)]
    aliases = {}
    if is_ctx:
        for k, (lo, hi) in enumerate(KV_PARTS):
            out_specs.insert(1 + k, pl.BlockSpec((None, None, CTX_SEQ, hi - lo), lambda i: (i, li, 0, 0)))
            out_shape.insert(1 + k, jax.ShapeDtypeStruct((T // CTX_SEQ, DEPTH, CTX_SEQ, hi - lo), f32))
        if kv_prev is not None:
            for k, a in enumerate(kv_prev):
                in_specs.append(pl.BlockSpec(memory_space=pl.ANY))
                aliases[len(args)] = 1 + k
                args.append(a)
    outs = pl.pallas_call(
        functools.partial(_inproj_kernel, emit_kv=is_ctx),
        grid=(T // tm,),
        in_specs=in_specs,
        out_specs=out_specs,
        out_shape=out_shape,
        input_output_aliases=aliases,
        compiler_params=_params("parallel"),
        name="inproj",
    )(*args)
    return (outs[0], tuple(outs[1:-1]), outs[-1]) if is_ctx else (outs[0], outs[-1])


def _merge_kernel(x_ref, gt_ref, h_ref, o_ref, w_ref, wb_ref, sh2_ref, sc2_ref, g2_ref, out_ref, h2_ref, acc_ref):
    n = pl.program_id(1)

    def contribution():
        return jax.nn.sigmoid(_dot(h_ref[...], w_ref[...])) * _dot(o_ref[...], wb_ref[...])

    @pl.when(n == 0)
    def _():
        acc_ref[...] = contribution()

    @pl.when((n > 0) & (n < N_BRANCH))
    def _():
        acc_ref[...] += contribution()

    @pl.when(n == N_BRANCH)
    def _():
        half = x_ref.shape[0] // 2
        for r in range(2):
            rows = slice(r * half, (r + 1) * half)
            y = x_ref[rows, :] + gt_ref[...] * _dot(acc_ref[rows, :].astype(bf16), w_ref[...])
            out_ref[rows, :] = y
            h2_ref[rows, :] = _modulated_norm(y, g2_ref[...], sc2_ref[...], sh2_ref[...]).astype(bf16)


def _merge(x, mod, h, o_stack, w5, wb, g2, li, is_ctx):
    T = x.shape[0]
    tm = TM_MERGE
    last = N_BRANCH - 1
    row = pl.BlockSpec((tm, D_MODEL), lambda i, n: (i, 0))
    return pl.pallas_call(
        _merge_kernel,
        grid=(T // tm, N_BRANCH + 1),
        in_specs=[row, _mod_spec(li, 2, tm, is_ctx), row,
                  pl.BlockSpec((None, tm, BRANCH_WIDTH), lambda i, n: (jnp.minimum(n, last), i, 0)),
                  pl.BlockSpec((None, D_MODEL, D_MODEL), lambda i, n: (n, 0, 0)),
                  pl.BlockSpec((None, BRANCH_WIDTH, D_MODEL), lambda i, n: (jnp.minimum(n, last), 0, 0)),
                  _mod_spec(li, 3, tm, is_ctx), _mod_spec(li, 4, tm, is_ctx),
                  pl.BlockSpec((1, D_MODEL), lambda i, n: (0, 0))],
        out_specs=[row, row],
        out_shape=[jax.ShapeDtypeStruct((T, D_MODEL), f32), jax.ShapeDtypeStruct((T, D_MODEL), bf16)],
        scratch_shapes=[pltpu.VMEM((tm, D_MODEL), f32)],
        compiler_params=_params("parallel", "arbitrary"),
        name="merge",
    )(x, mod, h, o_stack, w5, wb, mod, mod, g2)


def _mlp_kernel(x_ref, h_ref, gt_ref, wu_ref, wd_ref, gf_ref, out_ref, acc_ref, *, final_norm):
    j = pl.program_id(1)

    def chunk():
        a = jnp.maximum(_dot(h_ref[...], wu_ref[...]), 0.0)
        return _dot((a * a).astype(bf16), wd_ref[...])

    @pl.when(j == 0)
    def _():
        acc_ref[...] = chunk()

    @pl.when(j > 0)
    def _():
        acc_ref[...] += chunk()

    @pl.when(j == pl.num_programs(1) - 1)
    def _():
        y = x_ref[...] + gt_ref[...] * acc_ref[...]
        if final_norm:
            y = y * lax.rsqrt(jnp.mean(y * y, axis=-1, keepdims=True) + EPS) * gf_ref[...]
        out_ref[...] = y


def _mlp(x, h2, mod, w_up, w_down, g_final, li, is_ctx, final_norm):
    T = x.shape[0]
    tm, tf = TM_MLP, TF_MLP
    row = pl.BlockSpec((tm, D_MODEL), lambda i, j: (i, 0))
    return pl.pallas_call(
        functools.partial(_mlp_kernel, final_norm=final_norm),
        grid=(T // tm, D_FF // tf),
        in_specs=[row, row, _mod_spec(li, 5, tm, is_ctx),
                  pl.BlockSpec((None, D_MODEL, tf), lambda i, j: (j, 0, 0)),
                  pl.BlockSpec((tf, D_MODEL), lambda i, j: (j, 0)),
                  pl.BlockSpec((1, D_MODEL), lambda i, j: (0, 0))],
        out_specs=row,
        out_shape=jax.ShapeDtypeStruct((T, D_MODEL), f32),
        scratch_shapes=[pltpu.VMEM((tm, D_MODEL), f32)],
        compiler_params=_params("parallel", "arbitrary"),
        name="mlp",
    )(x, h2, mod, w_up, w_down, g_final)


def _ctx_attn_kernel(aq_ref, ak_ref, av_ref, bq_ref, bk_ref, bv_ref, cq_ref, ckv_ref,
                     dl_ref, dg_ref, sink_ref, o_ref, *, lam_init):
    S = CTX_SEQ
    lam = _diff_lambda(dl_ref, lam_init)
    out_a, out_b, out_c = [], [], []
    for s in range(aq_ref.shape[0] // S):
        rows = slice(s * S, (s + 1) * S)
        q = aq_ref[rows, :].astype(f32) * QK_SCALE2
        outs = [_diff_head(q, lambda t: ak_ref[rows, t * PAIR:(t + 1) * PAIR],
                           lambda h: av_ref[rows, h * A_V:(h + 1) * A_V], h, lam, dg_ref[...], lam_init)
                for h in range(A_HEADS)]
        out_a.append(jnp.concatenate(outs, axis=1))
        q = bq_ref[rows, :].astype(f32) * QK_SCALE2
        tiles = []
        for t in range(B_HEADS // 2):
            cols = slice(t * PAIR, (t + 1) * PAIR)
            accs = []
            for half in range(2):
                (e,), _ = _exp2_parts([_dot_nt(_head_q(q[:, cols], half), bk_ref[rows, cols])])
                accs.append(_dot(e.astype(bf16), _values_with_ones(bv_ref[rows, cols], half)))
            tiles.append(_pair_out(*accs))
        out_b.append(jnp.concatenate(tiles, axis=1))
        q = cq_ref[rows, :].astype(f32) * QK_SCALE2
        k_tile = ckv_ref[rows, 0:PAIR]
        v_variants = _swap_halves_variants(ckv_ref[rows, PAIR:2 * PAIR].astype(f32))
        tiles = []
        for t in range(C_HEADS // 2):
            q_tile = q[:, t * PAIR:(t + 1) * PAIR]
            q_swapped = pltpu.roll(q_tile, HEAD_DIM, 1)
            accs, extras = [], []
            for half in range(2):
                h = 2 * t + half
                g = h // C_GROUP
                sink2 = sink_ref[h] * LOG2E
                (e,), m = _exp2_parts([_dot_nt(_gqa_head_q(q_tile, q_swapped, half, g), k_tile)], sink2)
                accs.append(_dot(e.astype(bf16), v_variants[g * 2 + half]))
                extras.append(jnp.exp2(sink2 - m))
            tiles.append(_pair_out(accs[0], accs[1], extras[0], extras[1]))
        out_c.append(jnp.concatenate(tiles, axis=1))
    for k, outs in enumerate((out_a, out_b, out_c)):
        o_ref[k] = jnp.concatenate(outs, axis=0).astype(bf16)


def _ctx_attn(p, dl, dg, sink, li):
    S = CTX_SEQS_PER_STEP * CTX_SEQ
    nb = p.shape[0] // S
    lam_init = 0.8 - 0.6 * math.exp(-0.3 * li)

    def col(c):
        return pl.BlockSpec((S, 512), lambda b, c=c: (b, c))

    return pl.pallas_call(
        functools.partial(_ctx_attn_kernel, lam_init=lam_init),
        grid=(nb,),
        in_specs=[col(COL_AQ), col(COL_AK), col(COL_AV), col(COL_BQ), col(COL_BK), col(COL_BV), col(COL_CQ),
                  pl.BlockSpec((S, 256), lambda b: (b, COL_CKV_256)),
                  pl.BlockSpec((4, A_QK), lambda b: (0, 0)),
                  pl.BlockSpec((1, A_V), lambda b: (0, 0)),
                  pl.BlockSpec(memory_space=pltpu.SMEM)],
        out_specs=pl.BlockSpec((N_BRANCH - 1, S, BRANCH_WIDTH), lambda b: (0, b, 0)),
        out_shape=jax.ShapeDtypeStruct((N_BRANCH, p.shape[0], BRANCH_WIDTH), bf16),
        compiler_params=_params("parallel"),
        name="ctx_attn",
    )(p, p, p, p, p, p, p, p, dl, dg, sink)


def _lat_a_kernel(q_ref, k_ref, v_ref, ck_ref, cv_ref, cosq_ref, sinq_ref, cosk_ref, sink_ref,
                  dl_ref, dg_ref, o_ref, kk_ref, vv_ref, *, lam_init):
    L = LAT_SEQ

    @pl.when(pl.program_id(1) == 0)
    def _():
        kk_ref[0:L, :] = _rope(k_ref[...], cosk_ref[...], sink_ref[...]).astype(bf16)
        kk_ref[L:, :] = ck_ref[...].astype(bf16)
        vv_ref[0:L, :] = v_ref[...].astype(bf16)
        vv_ref[L:, :] = cv_ref[...].astype(bf16)

    lam = _diff_lambda(dl_ref, lam_init)
    q = _rope(q_ref[...], cosq_ref[...], sinq_ref[...]) * QK_SCALE2
    outs = [_diff_head(q, lambda t: kk_ref[:, t * PAIR:(t + 1) * PAIR], lambda h: vv_ref[:, h * A_V:(h + 1) * A_V],
                       h, lam, dg_ref[...], lam_init) for h in range(A_HEADS)]
    o_ref[...] = jnp.concatenate(outs, axis=1).astype(bf16)


def _lat_b_kernel(q_ref, k_ref, v_ref, ck_ref, cv_ref, bias_ref, _, o_ref, kk_ref, va_ref):
    L, W = LAT_SEQ, NA_KEY_ROWS * GRID_W
    qb = pl.program_id(1)

    @pl.when(qb == 0)
    def _():
        kk_ref[0:L, :] = k_ref[...]
        kk_ref[L:, :] = ck_ref[...].astype(bf16)
        for t in range(B_HEADS // 2):
            cols = slice(t * PAIR, (t + 1) * PAIR)
            for half in range(2):
                va_ref[2 * t + half, 0:L, :] = _values_with_ones(v_ref[:, cols], half)
                va_ref[2 * t + half, L:, :] = _values_with_ones(cv_ref[:, cols], half)

    start = pl.multiple_of(_na_key_row0(qb) * GRID_W, 256)
    q = q_ref[...].astype(f32) * QK_SCALE2
    tiles = []
    for t in range(B_HEADS // 2):
        cols = slice(t * PAIR, (t + 1) * PAIR)
        accs = []
        for half in range(2):
            h = 2 * t + half
            qh = _head_q(q[:, cols], half)
            s_loc = _dot_nt(qh, kk_ref[pl.ds(start, W), cols]) + bias_ref[h].astype(f32)
            s_ctx = _dot_nt(qh, kk_ref[L:, cols])
            (e_loc, e_ctx), _m = _exp2_parts([s_loc, s_ctx])
            accs.append(_dot(e_loc.astype(bf16), va_ref[h, pl.ds(start, W), :])
                        + _dot(e_ctx.astype(bf16), va_ref[h, L:, :]))
        tiles.append(_pair_out(*accs))
    o_ref[...] = jnp.concatenate(tiles, axis=1).astype(bf16)


def _lat_c_kernel(q_ref, kv_ref, ck_ref, cv_ref, cosq_ref, sinq_ref, cosk_ref, sink_ref, snk_ref,
                  _, o_ref, kk_ref, va_ref):
    L, W = LAT_SEQ, SWA_KEYS
    tq = q_ref.shape[0]
    qb = pl.program_id(1)

    @pl.when(qb == 0)
    def _():
        kk_ref[0:L, :] = _rope(kv_ref[:, 0:PAIR], cosk_ref[...], sink_ref[...]).astype(bf16)
        kk_ref[L:, :] = ck_ref[...].astype(bf16)
        lat = _swap_halves_variants(kv_ref[:, PAIR:2 * PAIR].astype(f32))
        ctx = _swap_halves_variants(cv_ref[...])
        for idx in range(4):
            va_ref[idx, 0:L, :] = lat[idx]
            va_ref[idx, L:, :] = ctx[idx]

    start = pl.multiple_of(jnp.clip(qb * tq - SWA_WINDOW, 0, L - W), 128)
    q = _rope(q_ref[...], cosq_ref[...], sinq_ref[...]) * QK_SCALE2
    qpos = qb * tq + lax.broadcasted_iota(jnp.int32, (tq, W), 0)
    kpos = start + lax.broadcasted_iota(jnp.int32, (tq, W), 1)
    valid = jnp.abs(qpos - kpos) <= SWA_WINDOW
    tiles = []
    for t in range(C_HEADS // 2):
        q_tile = q[:, t * PAIR:(t + 1) * PAIR]
        q_swapped = pltpu.roll(q_tile, HEAD_DIM, 1)
        accs, extras = [], []
        for half in range(2):
            h = 2 * t + half
            g = h // C_GROUP
            qh = _gqa_head_q(q_tile, q_swapped, half, g)
            s_loc = jnp.where(valid, _dot_nt(qh, kk_ref[pl.ds(start, W), :]), NEG_INF)
            s_ctx = _dot_nt(qh, kk_ref[L:, :])
            sink2 = snk_ref[h] * LOG2E
            (e_loc, e_ctx), m = _exp2_parts([s_loc, s_ctx], sink2)
            idx = g * 2 + half
            accs.append(_dot(e_loc.astype(bf16), va_ref[idx, pl.ds(start, W), :])
                        + _dot(e_ctx.astype(bf16), va_ref[idx, L:, :]))
            extras.append(jnp.exp2(sink2 - m))
        tiles.append(_pair_out(accs[0], accs[1], extras[0], extras[1]))
    o_ref[...] = jnp.concatenate(tiles, axis=1).astype(bf16)


def _lat_attn(p, caches, na_dense, dl, dg, sink, cos_t, sin_t, li):
    cak, cav, cbk, cbv, cck, ccv = caches
    L = LAT_SEQ
    assert TQ_ATTN == NA_Q_ROWS * GRID_W and SWA_KEYS >= TQ_ATTN_C + 2 * SWA_WINDOW
    nb_lat = p.shape[0] // L
    lam_init = 0.8 - 0.6 * math.exp(-0.3 * li)

    def qcol(c, tq=TQ_ATTN):
        return pl.BlockSpec((tq, 512), lambda b, i, c=c: (b * (L // tq) + i, c))

    def kcol(c, width=512):
        return pl.BlockSpec((L, width), lambda b, i, c=c: (b, c))

    def cache(width):
        return pl.BlockSpec((None, None, CTX_SEQ, width), lambda b, i: (b, li, 0, 0))

    tab_q = pl.BlockSpec((TQ_ATTN, 512), lambda b, i: (i, 0))
    tab_qc = pl.BlockSpec((TQ_ATTN_C, 512), lambda b, i: (i, 0))
    tab_k = pl.BlockSpec((L, 512), lambda b, i: (0, 0))
    tab_k128 = pl.BlockSpec((L, 128), lambda b, i: (0, 0))
    o_sds = jax.ShapeDtypeStruct((N_BRANCH, p.shape[0], BRANCH_WIDTH), bf16)
    anyspec = pl.BlockSpec(memory_space=pl.ANY)

    def o_spec(branch, tq=TQ_ATTN):
        return pl.BlockSpec((None, tq, BRANCH_WIDTH), lambda b, i: (branch, b * (L // tq) + i, 0))

    kv_scratch = [pltpu.VMEM((L + CTX_SEQ, 512), bf16), pltpu.VMEM((L + CTX_SEQ, 512), bf16)]
    smem = pl.BlockSpec(memory_space=pltpu.SMEM)
    cp = _params("parallel", "arbitrary")

    o_stack = pl.pallas_call(
        functools.partial(_lat_a_kernel, lam_init=lam_init),
        grid=(nb_lat, L // TQ_ATTN),
        in_specs=[qcol(COL_AQ), kcol(COL_AK), kcol(COL_AV), cache(512), cache(512),
                  tab_q, tab_q, tab_k, tab_k,
                  pl.BlockSpec((4, A_QK), lambda b, i: (0, 0)), pl.BlockSpec((1, A_V), lambda b, i: (0, 0))],
        out_specs=o_spec(0), out_shape=o_sds, scratch_shapes=kv_scratch,
        compiler_params=cp, name="lat_attn_a",
    )(p, p, p, cak, cav, cos_t, sin_t, cos_t, sin_t, dl, dg)

    o_stack = pl.pallas_call(
        _lat_b_kernel,
        grid=(nb_lat, L // TQ_ATTN),
        in_specs=[qcol(COL_BQ), kcol(COL_BK), kcol(COL_BV), cache(512), cache(512),
                  pl.BlockSpec((B_HEADS, TQ_ATTN, NA_KEY_ROWS * GRID_W), lambda b, i: (0, i, 0)), anyspec],
        out_specs=o_spec(1), out_shape=o_sds,
        scratch_shapes=[pltpu.VMEM((L + CTX_SEQ, 512), bf16), pltpu.VMEM((B_HEADS, L + CTX_SEQ, PAIR), bf16)],
        input_output_aliases={6: 0}, compiler_params=cp, name="lat_attn_b",
    )(p, p, p, cbk, cbv, na_dense, o_stack)

    return pl.pallas_call(
        _lat_c_kernel,
        grid=(nb_lat, L // TQ_ATTN_C),
        in_specs=[qcol(COL_CQ, TQ_ATTN_C), kcol(COL_CKV_256, 256), cache(PAIR), cache(PAIR),
                  tab_qc, tab_qc, tab_k128, tab_k128, smem, anyspec],
        out_specs=o_spec(2, TQ_ATTN_C), out_shape=o_sds,
        scratch_shapes=[pltpu.VMEM((L + CTX_SEQ, PAIR), bf16), pltpu.VMEM((4, L + CTX_SEQ, PAIR), bf16)],
        input_output_aliases={9: 0}, compiler_params=cp, name="lat_attn_c",
    )(p, p, cck, ccv, cos_t, sin_t, cos_t, sin_t, sink, o_stack)


def _na_key_row0(qb):
    lo, hi = NA_Q_ROWS * qb - NA_ROWS // 2, GRID_ROWS - NA_KEY_ROWS
    return min(max(lo, 0), hi) if isinstance(qb, int) else jnp.clip(lo, 0, hi)


def _na_bias_kernel(rb_ref, o_ref):
    h = pl.program_id(0)
    W = GRID_W
    n_dc = 2 * NA_COLS - 1
    n_dr = 2 * NA_ROWS - 1
    qc = lax.broadcasted_iota(jnp.int32, (W, 2 * W), 0)
    lane = lax.broadcasted_iota(jnp.int32, (W, 2 * W), 1)
    second = lane >= W
    kc = jnp.where(second, lane - W, lane)
    dc = jnp.clip(kc - qc, -(NA_COLS - 1), NA_COLS - 1) + NA_COLS - 1
    c0 = jnp.clip(qc - NA_COLS // 2, 0, W - NA_COLS)
    col_ok = (kc >= c0) & (kc < c0 + NA_COLS)
    base = h * (n_dr * n_dc)

    def pair_tile(dr0):
        t = jnp.zeros((W, 2 * W), f32)
        for j in range(n_dc):
            lo = rb_ref[base + dr0 * n_dc + j] if 0 <= dr0 < n_dr else 0.0
            hi = rb_ref[base + (dr0 + 1) * n_dc + j] if 0 <= dr0 + 1 < n_dr else 0.0
            t = jnp.where(dc == j, jnp.where(second, hi, lo), t)
        return jnp.where(col_ok, t * LOG2E, NEG_INF)

    tiles = {dr0: pair_tile(dr0) for dr0 in range(-1, n_dr)}
    neg = jnp.full((W, 2 * W), NEG_INF, f32)
    for qr in range(GRID_ROWS):
        r0 = min(max(qr - NA_ROWS // 2, 0), GRID_ROWS - NA_ROWS)
        k0 = _na_key_row0(qr // NA_Q_ROWS)
        assert k0 % 2 == 0 and k0 <= r0 and r0 + NA_ROWS <= k0 + NA_KEY_ROWS
        for pr in range(NA_KEY_ROWS // 2):
            kr = k0 + 2 * pr
            ok0 = r0 <= kr < r0 + NA_ROWS
            ok1 = r0 <= kr + 1 < r0 + NA_ROWS
            if not (ok0 or ok1):
                t = neg
            else:
                t = tiles[kr - qr + NA_ROWS - 1]
                if not ok0:
                    t = jnp.where(second, t, NEG_INF)
                if not ok1:
                    t = jnp.where(second, NEG_INF, t)
            o_ref[qr * W:(qr + 1) * W, 2 * pr * W:(2 * pr + 2) * W] = t.astype(bf16)


def _na_bias_dense(rel_bias):
    kw = NA_KEY_ROWS * GRID_W
    return pl.pallas_call(
        _na_bias_kernel,
        grid=(B_HEADS,),
        in_specs=[pl.BlockSpec(memory_space=pltpu.SMEM)],
        out_specs=pl.BlockSpec((None, LAT_SEQ, kw), lambda h: (h, 0, 0)),
        out_shape=jax.ShapeDtypeStruct((B_HEADS, LAT_SEQ, kw), bf16),
        compiler_params=_params("parallel"),
        name="na_bias",
    )(rel_bias.reshape(-1))


def _hy_filter_kernel(z_ref, w1_ref, b1_ref, w2_ref, b2_ref, fr_ref, w3f_ref, w3b_ref, dcf_ref, dcb_ref,
                      bias_ref, fc_ref, fs_ref, o_ref, *, L):
    z = z_ref[...]
    fr = fr_ref[...]
    h = jnp.sin(fr * (_dot_hi(z, w1_ref[...]) + b1_ref[...]))
    h = jnp.sin(fr * (_dot_hi(h, w2_ref[...]) + b2_ref[...]))
    t = z[:, 0:1]
    hf = _dot_hi(h, w3f_ref[...]) * jnp.exp(-t * jnp.abs(dcf_ref[...]))
    hb = _dot_hi(h, w3b_ref[...]) * jnp.exp(-t * jnp.abs(dcb_ref[...]))
    ssum = hf + hb
    bias = bias_ref[...]
    ga = _dot_3pass(fc_ref[...], ssum) + bias
    gb = _dot_3pass(fs_ref[...], hb - hf)
    g_nyq = _dot_hi(fs_ref[0:8, :], ssum)[0:1] + bias
    row0 = lax.broadcasted_iota(jnp.int32, ga.shape, 0) == 0
    inv = 1.0 / L
    o_ref[0] = jnp.where(row0, 0.5 * ga, ga) * inv
    o_ref[1] = jnp.where(row0, 0.0, gb) * inv
    o_ref[2] = jnp.where(row0, 0.5 * g_nyq, ga) * inv


def _hy_filter_tables(L, w1p, b1, w2, b2, w3, freq, decay, hy_bias):
    cb = HY_CB
    ncb = HY_WIDTH // cb
    z = jnp.asarray(_hyena_features(L))
    fc, fs = (jnp.asarray(a) for a in _dft_tables(L))
    full = lambda shape: pl.BlockSpec(shape, lambda o, c: (0,) * len(shape))
    fwd = lambda rows: pl.BlockSpec((rows, cb), lambda o, c: (0, o * 2 * ncb + c))
    bwd = lambda rows: pl.BlockSpec((rows, cb), lambda o, c: (0, o * 2 * ncb + ncb + c))
    return pl.pallas_call(
        functools.partial(_hy_filter_kernel, L=L),
        grid=(HY_ORDER, ncb),
        in_specs=[full((L, HY_EMB_PAD)), full((HY_EMB_PAD, HY_FFN)), full((1, HY_FFN)),
                  full((HY_FFN, HY_FFN)), full((1, HY_FFN)), full((1, HY_FFN)),
                  fwd(HY_FFN), bwd(HY_FFN), fwd(1), bwd(1),
                  pl.BlockSpec((None, 1, cb), lambda o, c: (o, 0, c)),
                  full((L, L)), full((L, L))],
        out_specs=pl.BlockSpec((None, 3, L, cb), lambda o, c: (o, 0, 0, c)),
        out_shape=jax.ShapeDtypeStruct((HY_ORDER, 3, L, HY_WIDTH), f32),
        compiler_params=_params("parallel", "parallel"),
        name="hy_filter",
    )(z, w1p, b1, w2, b2, freq, w3, w3, decay, decay, hy_bias.reshape(HY_ORDER, 1, HY_WIDTH), fc, fs)


def _hyena_kernel(v_ref, x1_ref, x2_ref, wv_ref, w1_ref, w2_ref, fwd_ref, inv_ref, tab_ref, _, o_ref):
    L = inv_ref.shape[0]
    cb = v_ref.shape[1]
    n_seq = v_ref.shape[0] // L
    row = lax.broadcasted_iota(jnp.int32, (L, cb), 0)

    def short_conv(u_ref, w_ref):
        w = w_ref[...]
        outs = []
        for s in range(n_seq):
            u = u_ref[s * L:(s + 1) * L, :].astype(f32)
            prev = jnp.where(row == 0, 0.0, pltpu.roll(u, 1, 0))
            nxt = jnp.where(row == L - 1, 0.0, pltpu.roll(u, L - 1, 0))
            outs.append(prev * w[0:1] + u * w[1:2] + nxt * w[2:3])
        return jnp.concatenate(outs, axis=1)

    def long_conv(u, order):
        ab = _dot(fwd_ref[...], u.astype(bf16))
        t0, t1, t2 = tab_ref[order, 0], tab_ref[order, 1], tab_ref[order, 2]
        pq = []
        for s in range(n_seq):
            a, b = ab[:L, s * cb:(s + 1) * cb], ab[L:, s * cb:(s + 1) * cb]
            pq.append(jnp.concatenate([(a * t0 + b * t1).astype(bf16), (b * t2 - a * t1).astype(bf16)], axis=0))
        return _dot(inv_ref[...], jnp.concatenate(pq, axis=1))

    z = short_conv(x1_ref, w1_ref) * long_conv(short_conv(v_ref, wv_ref), 0)
    y = (short_conv(x2_ref, w2_ref) * long_conv(z, 1)).astype(bf16)
    o_ref[...] = jnp.concatenate([y[:, s * cb:(s + 1) * cb] for s in range(n_seq)], axis=0)


def _hyena(p, o_stack, hy_short, tabs, L):
    cb = HY_CB
    ncb = HY_WIDTH // cb
    c0 = HY_COL0 // cb
    rows = min(HY_SEQS_PER_STEP[L] * L, p.shape[0])
    nb = p.shape[0] // rows
    fc, fs = _dft_tables(L)
    fwd = jnp.asarray(np.concatenate([fc, fs], axis=0), bf16)
    inv = jnp.asarray(np.concatenate([fc, fs.T], axis=1), bf16)

    def part(k):
        return pl.BlockSpec((rows, cb), lambda c, b, k=k: (b, c0 + k * ncb + c))

    def wpart(k):
        return pl.BlockSpec((3, cb), lambda c, b, k=k: (0, k * ncb + c))

    return pl.pallas_call(
        _hyena_kernel,
        grid=(ncb, nb),
        in_specs=[part(0), part(1), part(2), wpart(0), wpart(1), wpart(2),
                  pl.BlockSpec((2 * L, L), lambda c, b: (0, 0)), pl.BlockSpec((L, 2 * L), lambda c, b: (0, 0)),
                  pl.BlockSpec((HY_ORDER, 3, L, cb), lambda c, b: (0, 0, 0, c)),
                  pl.BlockSpec(memory_space=pl.ANY)],
        out_specs=pl.BlockSpec((None, rows, cb), lambda c, b: (N_BRANCH - 1, b, c)),
        out_shape=jax.ShapeDtypeStruct(o_stack.shape, o_stack.dtype),
        input_output_aliases={9: 0},
        compiler_params=_params("parallel", "arbitrary"),
        name="hyena_%d" % L,
    )(p, p, p, hy_short, hy_short, hy_short, fwd, inv, tabs, o_stack)


def kernel(x_prompt, x_sample, cache_a_k, cache_a_v, cache_b_k, cache_b_v, cache_c_k, cache_c_v, c, c_ctx, w_ada, b_ada, g_mix, w_in, diff_lambda, diff_norm_g, na_bias, swa_sink, hy_short, hy_w1, hy_b1, hy_w2, hy_b2, hy_w3, hy_freq, hy_decay, hy_bias, w_branch, w_out, g_mlp, w_up, w_down, g_final):
    nb_ctx, nb_lat = x_prompt.shape[0], x_sample.shape[0]
    assert x_prompt.shape[1:] == (CTX_SEQ, D_MODEL) and x_sample.shape[1:] == (LAT_SEQ, D_MODEL)
    assert nb_lat <= CTX_MOD_ROW and (nb_ctx * CTX_SEQ) % TM_INPROJ == 0

    x_ctx = x_prompt.reshape(nb_ctx * CTX_SEQ, D_MODEL)
    x_lat = x_sample.reshape(nb_lat * LAT_SEQ, D_MODEL)
    cv = jnp.zeros((N_MOD_ROWS, D_MODEL), f32).at[:nb_lat].set(c).at[CTX_MOD_ROW].set(c_ctx)
    mod = _adaln_all(cv, w_ada, b_ada).reshape(DEPTH * N_MOD_ROWS * 6, 1, D_MODEL)
    cos_t, sin_t = (jnp.asarray(a) for a in _rope_tables())
    caches = (cache_a_k.reshape(nb_lat, DEPTH, CTX_SEQ, 512), cache_a_v.reshape(nb_lat, DEPTH, CTX_SEQ, 512),
              cache_b_k.reshape(nb_lat, DEPTH, CTX_SEQ, 512), cache_b_v.reshape(nb_lat, DEPTH, CTX_SEQ, 512),
              cache_c_k.reshape(nb_lat, DEPTH, CTX_SEQ, 128), cache_c_v.reshape(nb_lat, DEPTH, CTX_SEQ, 128))

    kv = None
    for li in range(DEPTH):
        w_mix = _cast_mix_weights(w_in, li)
        w5, wu, wd = _cast_layer_weights(w_in, w_out, w_up, w_down, li)
        wb = w_branch[li].astype(bf16)
        g1 = g_mix[li].reshape(1, D_MODEL)
        g2 = g_mlp[li].reshape(1, D_MODEL)
        dl = diff_lambda[li]
        dg = diff_norm_g[li].reshape(1, A_V)
        sink = swa_sink[li]

        w1p = jnp.pad(hy_w1[li], ((0, HY_EMB_PAD - HY_EMB), (0, 0)))
        hy_args = (w1p, hy_b1[li].reshape(1, HY_FFN), hy_w2[li], hy_b2[li].reshape(1, HY_FFN), hy_w3[li],
                   hy_freq[li].reshape(1, HY_FFN), hy_decay[li].reshape(1, -1), hy_bias[li])
        gf = g_final.reshape(1, D_MODEL)
        final = li == DEPTH - 1

        p, kv, h = _inproj(x_ctx, mod, g1, w_mix, li, True, kv)
        o_stack = _ctx_attn(p, dl, dg, sink, li)
        o_stack = _hyena(p, o_stack, hy_short[li], _hy_filter_tables(CTX_SEQ, *hy_args), CTX_SEQ)
        x_ctx, h2 = _merge(x_ctx, mod, h, o_stack, w5, wb, g2, li, True)
        x_ctx = _mlp(x_ctx, h2, mod, wu, wd, gf, li, True, final_norm=final)

        p, h = _inproj(x_lat, mod, g1, w_mix, li, False)
        o_stack = _lat_attn(p, caches, _na_bias_dense(na_bias[li]), dl, dg, sink, cos_t, sin_t, li)
        o_stack = _hyena(p, o_stack, hy_short[li], _hy_filter_tables(LAT_SEQ, *hy_args), LAT_SEQ)
        x_lat, h2 = _merge(x_lat, mod, h, o_stack, w5, wb, g2, li, False)
        x_lat = _mlp(x_lat, h2, mod, wu, wd, gf, li, False, final_norm=final)

    y_prompt = x_ctx.reshape(nb_ctx, CTX_SEQ, D_MODEL)
    y_sample = x_lat.reshape(nb_lat, LAT_SEQ, D_MODEL)
    kv_shapes = ((2, A_HEADS, A_QK), (A_HEADS, A_V), (B_HEADS, HEAD_DIM), (B_HEADS, HEAD_DIM),
                 (C_KV_HEADS, HEAD_DIM), (C_KV_HEADS, HEAD_DIM))
    new_kv = tuple(a.reshape((nb_ctx, DEPTH, CTX_SEQ) + s) for a, s in zip(kv, kv_shapes))
    return (y_prompt, y_sample) + new_kv
```

```python
import functools
import math

import numpy as np
import jax
import jax.numpy as jnp
from jax import lax
from jax.experimental import pallas as pl
from jax.experimental.pallas import tpu as pltpu

f32 = jnp.float32
bf16 = jnp.bfloat16

D_MODEL = 2048
DEPTH = 2
CTX_SEQ = 256
LAT_SEQ = 1024
GRID_W = 64
GRID_ROWS = LAT_SEQ // GRID_W
BRANCH_WIDTH = 512
N_BRANCH = 4
HEAD_DIM = 64
A_QK = 64
A_V = 128
A_HEADS = 4
B_HEADS = 8
NA_ROWS = 8
NA_COLS = 16
C_HEADS = 8
C_KV_HEADS = 2
C_GROUP = 4
SWA_WINDOW = 128
SWA_KEYS = 512
NA_Q_ROWS = 8
NA_KEY_ROWS = 12
HY_WIDTH = 512
HY_ORDER = 2
HY_BANDS = 16
HY_EMB = 1 + 2 * HY_BANDS
HY_EMB_PAD = 128
HY_FFN = 64
D_FF = 4 * D_MODEL
ROPE_BASE = 10000.0
EPS = 1e-6
NEG_INF = -1e30
MIX_COLS = 5376
GATE_COLS = N_BRANCH * D_MODEL
COL_AQ, COL_BQ, COL_CQ, COL_AK, COL_AV, COL_BK, COL_BV = 0, 1, 2, 3, 4, 5, 6
COL_CKV_256 = 14
KV_PARTS = ((1536, 2048), (2048, 2560), (2560, 3072), (3072, 3584), (3584, 3712), (3712, 3840))
HY_COL0 = 3840
N_MOD_ROWS = 16
CTX_MOD_ROW = 8

VMEM_LIMIT = 56 * 1024 * 1024
TM_INPROJ = 256
TM_MERGE = 512
TM_MLP = 512
TF_MLP = 1024
TQ_ATTN = 512
TQ_ATTN_C = 256
CTX_SEQS_PER_STEP = 1
TN_ADA = 1024
HY_CB = 256
HY_SEQS_PER_STEP = {CTX_SEQ: 8, LAT_SEQ: 2}
CAST_ROWS = 512


def _params(*sem):
    return pltpu.CompilerParams(dimension_semantics=sem, vmem_limit_bytes=VMEM_LIMIT)


def _dot(a, b):
    return jnp.dot(a, b, preferred_element_type=f32)


def _dot_nt(a, b):
    return lax.dot_general(a, b, (((1,), (1,)), ((), ())), preferred_element_type=f32)


def _dot_hi(a, b):
    return jnp.dot(a, b, preferred_element_type=f32, precision=lax.Precision.HIGHEST)


def _dot_3pass(a, b):
    a_hi, b_hi = a.astype(bf16), b.astype(bf16)
    a_lo = (a - a_hi.astype(f32)).astype(bf16)
    b_lo = (b - b_hi.astype(f32)).astype(bf16)
    return _dot(a_hi, b_hi) + (_dot(a_hi, b_lo) + _dot(a_lo, b_hi))


@functools.lru_cache(maxsize=None)
def _rope_tables():
    half = HEAD_DIM // 2
    nf = half // 2
    inv = ROPE_BASE ** (-np.arange(nf, dtype=np.float64) / nf)
    t = np.arange(LAT_SEQ)
    pos = np.stack([t // GRID_W, t % GRID_W], axis=1).astype(np.float64)
    lane = np.arange(HEAD_DIM)
    ang = pos[:, lane // half] * inv[lane % nf][None, :]
    first = (lane % half) < nf
    cos = np.cos(ang)
    sin = np.where(first[None, :], -np.sin(ang), np.sin(ang))
    reps = 512 // HEAD_DIM
    return (np.tile(cos, (1, reps)).astype(np.float32), np.tile(sin, (1, reps)).astype(np.float32))


@functools.lru_cache(maxsize=None)
def _dft_tables(L):
    f = np.arange(L, dtype=np.int64)
    prod = (f[:, None] * f[None, :]) % (2 * L)
    ang = np.pi * prod.astype(np.float64) / L
    fc = np.cos(ang)
    fs = np.sin(ang)
    fs[0, :] = np.where(f % 2 == 0, 1.0, -1.0)
    return fc.astype(np.float32), fs.astype(np.float32)


@functools.lru_cache(maxsize=None)
def _hyena_features(L):
    n = np.arange(L, dtype=np.float64)[:, None]
    t = n / max(L - 1, 1)
    w = 2.0 * math.pi * n / L
    bands = np.linspace(1e-4, HY_BANDS - 1, HY_BANDS, dtype=np.float64)[None, :]
    z = np.concatenate([t, np.cos(bands * w), -np.sin(bands * w)], axis=-1)
    z = np.pad(z, ((0, 0), (0, HY_EMB_PAD - HY_EMB)))
    return z.astype(np.float32)


def _modulated_norm(x, g, scale, shift):
    ms = jnp.mean(x * x, axis=-1, keepdims=True)
    return x * lax.rsqrt(ms + EPS) * (g * (1.0 + scale)) + shift


def _rope(x, cos, sin_signed):
    x = x.astype(f32)
    n = x.shape[-1]
    lane = lax.broadcasted_iota(jnp.int32, x.shape, 1)
    first = (lane & (HEAD_DIM // 2 - 1)) < (HEAD_DIM // 4)
    partner = jnp.where(first, pltpu.roll(x, n - HEAD_DIM // 4, 1), pltpu.roll(x, HEAD_DIM // 4, 1))
    return x * cos + partner * sin_signed


LOG2E = 1.4426950408889634
QK_SCALE2 = HEAD_DIM ** -0.5 * LOG2E
PAIR = 2 * HEAD_DIM


def _lo_half(shape):
    return lax.broadcasted_iota(jnp.int32, shape, 1) < HEAD_DIM


def _keep_half(x, half, fill):
    lo = _lo_half(x.shape)
    return jnp.where(lo if half == 0 else jnp.logical_not(lo), x, fill)


def _head_q(q_tile, half):
    return _keep_half(q_tile, half, 0.0).astype(bf16)


def _values_with_ones(v_tile, half):
    return _keep_half(v_tile.astype(f32), half, 1.0).astype(bf16)


def _exp2_parts(scores, sink2=None):
    m = jnp.max(scores[0], axis=-1, keepdims=True)
    for s in scores[1:]:
        m = jnp.maximum(m, jnp.max(s, axis=-1, keepdims=True))
    if sink2 is not None:
        m = jnp.maximum(m, sink2)
    return [jnp.exp2(s - m) for s in scores], m


def _pair_out(acc_even, acc_odd, extra_even=None, extra_odd=None):
    lo = _lo_half(acc_even.shape)
    num = jnp.where(lo, acc_even, acc_odd)
    den = pltpu.roll(jnp.where(lo, acc_odd, acc_even), HEAD_DIM, 1)
    if extra_even is not None:
        den = den + jnp.where(lo, extra_even, extra_odd)
    return num / den


def _swap_halves_variants(v_tile):
    sw = pltpu.roll(v_tile, HEAD_DIM, 1)
    lo = _lo_half(v_tile.shape)
    tiles = (jnp.where(lo, v_tile, 1.0), jnp.where(lo, 1.0, sw), jnp.where(lo, sw, 1.0), jnp.where(lo, 1.0, v_tile))
    return [t.astype(bf16) for t in tiles]


def _gqa_head_q(q_tile, q_swapped, half, g):
    return _keep_half(q_tile if half == g else q_swapped, g, 0.0).astype(bf16)


def _diff_lambda(dl_ref, lam_init):
    dl = dl_ref[...]
    a = jnp.sum(dl[0:1] * dl[1:2], axis=-1, keepdims=True)
    b = jnp.sum(dl[2:3] * dl[3:4], axis=-1, keepdims=True)
    return jnp.exp(a) - jnp.exp(b) + lam_init


def _diff_head(q, k_of, v_of, h, lam, dg, lam_init):
    res = []
    for c in range(2):
        t = c * (A_HEADS // 2) + h // 2
        s = _dot_nt(_head_q(q[:, t * PAIR:(t + 1) * PAIR], h % 2), k_of(t))
        (e,), _ = _exp2_parts([s])
        res.append((_dot(e.astype(bf16), v_of(h)), jnp.sum(e, axis=-1, keepdims=True)))
    o = res[0][0] * (1.0 / res[0][1]) - res[1][0] * (lam / res[1][1])
    o = o * lax.rsqrt(jnp.mean(o * o, axis=-1, keepdims=True) + EPS) * dg
    return o * (1.0 - lam_init)


def _cast_kernel(src_ref, *rest):
    o_ref = rest[-1]
    x = src_ref[...]
    o_ref[...] = x.reshape(o_ref.shape).astype(bf16)


def _cast_call(name, grid, src, in_spec, out_spec, out_shape, dst=None):
    in_specs, args, aliases = [in_spec], [src], {}
    if dst is not None:
        in_specs.append(pl.BlockSpec(memory_space=pl.ANY))
        args.append(dst)
        aliases = {1: 0}
    return pl.pallas_call(
        _cast_kernel, grid=grid, in_specs=in_specs, out_specs=out_spec,
        out_shape=jax.ShapeDtypeStruct(out_shape, bf16), input_output_aliases=aliases,
        compiler_params=_params(*(("parallel",) * len(grid))), name=name,
    )(*args)


def _cast_mix_kernel(unit_ref, src_ref, o_ref):
    o_ref[...] = src_ref[0].astype(bf16)


def _cast_mix_weights(w_in):
    U = 256
    src_unit = (0, 1, 6, 7, 12, 13, 2, 3, 4, 5, 8, 9, 10, 11) + tuple(range(14, MIX_COLS // U))
    return pl.pallas_call(
        _cast_mix_kernel,
        grid_spec=pltpu.PrefetchScalarGridSpec(
            num_scalar_prefetch=1, grid=(DEPTH, len(src_unit)),
            in_specs=[pl.BlockSpec((pl.Element(1), pl.Element(D_MODEL), pl.Element(U)),
                                   lambda l, u, unit: (l, 0, pl.multiple_of(unit[u] * U, U)))],
            out_specs=pl.BlockSpec((None, D_MODEL, U), lambda l, u, unit: (l, 0, u))),
        out_shape=jax.ShapeDtypeStruct((DEPTH, D_MODEL, MIX_COLS), bf16),
        compiler_params=_params("parallel", "parallel"), name="cast_mix",
    )(jnp.asarray(src_unit, jnp.int32), w_in)


def _cast_layer_weights(w_in, w_out, w_up, w_down):
    D, R, tf = D_MODEL, CAST_ROWS, TF_MLP
    nr = D // R
    w5 = _cast_call(
        "cast_gate", (DEPTH, N_BRANCH, nr), w_in,
        pl.BlockSpec((pl.Element(1), pl.Element(R), pl.Element(D)),
                     lambda l, n, r: (l, pl.multiple_of(r * R, R), pl.multiple_of(MIX_COLS + n * D, 128))),
        pl.BlockSpec((None, None, R, D), lambda l, n, r: (l, n, r, 0)), (DEPTH, N_BRANCH + 1, D, D))
    w5 = _cast_call(
        "cast_wout", (DEPTH, nr), w_out,
        pl.BlockSpec((None, R, D), lambda l, r: (l, r, 0)),
        pl.BlockSpec((None, None, R, D), lambda l, r: (l, N_BRANCH, r, 0)), (DEPTH, N_BRANCH + 1, D, D), dst=w5)
    wu = _cast_call(
        "cast_wu", (DEPTH, D_FF // tf, nr), w_up,
        pl.BlockSpec((None, R, tf), lambda l, j, r: (l, r, j)),
        pl.BlockSpec((None, None, R, tf), lambda l, j, r: (l, j, r, 0)), (DEPTH, D_FF // tf, D, tf))
    wd = _cast_call(
        "cast_wd", (DEPTH, D_FF // R), w_down,
        pl.BlockSpec((None, R, D), lambda l, r: (l, r, 0)),
        pl.BlockSpec((None, R, D), lambda l, r: (l, r, 0)), (DEPTH, D_FF, D))
    return w5, wu, wd


def _ada_kernel(cv_ref, w_ref, b_ref, o_ref):
    cv = cv_ref[...]
    s = (cv * jax.nn.sigmoid(cv)).astype(bf16)
    o_ref[...] = _dot(s, w_ref[...].astype(bf16)) + b_ref[...]


def _adaln_all(cv, w_ada, b_ada):
    n6 = 6 * D_MODEL
    return pl.pallas_call(
        _ada_kernel,
        grid=(DEPTH, n6 // TN_ADA),
        in_specs=[pl.BlockSpec((N_MOD_ROWS, D_MODEL), lambda l, j: (0, 0)),
                  pl.BlockSpec((None, D_MODEL, TN_ADA), lambda l, j: (l, 0, j)),
                  pl.BlockSpec((None, 1, TN_ADA), lambda l, j: (l, 0, j))],
        out_specs=pl.BlockSpec((None, N_MOD_ROWS, TN_ADA), lambda l, j: (l, 0, j)),
        out_shape=jax.ShapeDtypeStruct((DEPTH, N_MOD_ROWS, n6), f32),
        compiler_params=_params("parallel", "parallel"),
        name="adaln",
    )(cv, w_ada, b_ada.reshape(DEPTH, 1, n6))


def _mod_spec(li, k, tm, is_ctx):
    def index(i, *_):
        r = CTX_MOD_ROW if is_ctx else (i * tm) // LAT_SEQ
        return ((li * N_MOD_ROWS + r) * 6 + k, 0, 0)

    return pl.BlockSpec((None, 1, D_MODEL), index)


def _inproj_kernel(x_ref, sh_ref, sc_ref, g_ref, w_ref, *rest, emit_kv):
    outs = rest[-(2 + len(KV_PARTS)) if emit_kv else -2:]
    h = _modulated_norm(x_ref[...], g_ref[...], sc_ref[...], sh_ref[...]).astype(bf16)
    outs[-1][...] = h
    res = _dot(h, w_ref[...])
    outs[0][...] = res.astype(bf16)
    if emit_kv:
        for ref, (lo, hi) in zip(outs[1:-1], KV_PARTS):
            ref[...] = res[:, lo:hi]


def _inproj(x, mod, g, w, li, is_ctx, kv_prev=None):
    T = x.shape[0]
    tm = TM_INPROJ
    assert tm == CTX_SEQ
    row = lambda width: pl.BlockSpec((tm, width), lambda i: (i, 0))
    in_specs = [row(D_MODEL), _mod_spec(li, 0, tm, is_ctx), _mod_spec(li, 1, tm, is_ctx),
                pl.BlockSpec((1, D_MODEL), lambda i: (0, 0)),
                pl.BlockSpec((None, D_MODEL, MIX_COLS), lambda i: (li, 0, 0), pipeline_mode=pl.Buffered(1))]
    args = [x, mod, mod, g, w]
    out_specs = [row(MIX_COLS), row(D_MODEL)]
    out_shape = [jax.ShapeDtypeStruct((T, MIX_COLS), bf16), jax.ShapeDtypeStruct((T, D_MODEL), bf16)]
    aliases = {}
    if is_ctx:
        for k, (lo, hi) in enumerate(KV_PARTS):
            out_specs.insert(1 + k, pl.BlockSpec((None, None, CTX_SEQ, hi - lo), lambda i: (i, li, 0, 0)))
            out_shape.insert(1 + k, jax.ShapeDtypeStruct((T // CTX_SEQ, DEPTH, CTX_SEQ, hi - lo), f32))
        if kv_prev is not None:
            for k, a in enumerate(kv_prev):
                in_specs.append(pl.BlockSpec(memory_space=pl.ANY))
                aliases[len(args)] = 1 + k
                args.append(a)
    outs = pl.pallas_call(
        functools.partial(_inproj_kernel, emit_kv=is_ctx),
        grid=(T // tm,),
        in_specs=in_specs,
        out_specs=out_specs,
        out_shape=out_shape,
        input_output_aliases=aliases,
        compiler_params=_params("parallel"),
        name="inproj",
    )(*args)
    return (outs[0], tuple(outs[1:-1]), outs[-1]) if is_ctx else (outs[0], outs[-1])


def _merge_kernel(x_ref, gt_ref, h_ref, o_ref, w_ref, wb_ref, sh2_ref, sc2_ref, g2_ref, out_ref, h2_ref, acc_ref):
    n = pl.program_id(1)

    def contribution():
        return jax.nn.sigmoid(_dot(h_ref[...], w_ref[...])) * _dot(o_ref[...], wb_ref[...])

    @pl.when(n == 0)
    def _():
        acc_ref[...] = contribution()

    @pl.when((n > 0) & (n < N_BRANCH))
    def _():
        acc_ref[...] += contribution()

    @pl.when(n == N_BRANCH)
    def _():
        half = x_ref.shape[0] // 2
        for r in range(2):
            rows = slice(r * half, (r + 1) * half)
            y = x_ref[rows, :] + gt_ref[...] * _dot(acc_ref[rows, :].astype(bf16), w_ref[...])
            out_ref[rows, :] = y
            h2_ref[rows, :] = _modulated_norm(y, g2_ref[...], sc2_ref[...], sh2_ref[...]).astype(bf16)


def _merge(x, mod, h, o_stack, w5, wb, g2, li, is_ctx):
    T = x.shape[0]
    tm = TM_MERGE
    last = N_BRANCH - 1
    row = pl.BlockSpec((tm, D_MODEL), lambda i, n: (i, 0))
    return pl.pallas_call(
        _merge_kernel,
        grid=(T // tm, N_BRANCH + 1),
        in_specs=[row, _mod_spec(li, 2, tm, is_ctx), row,
                  pl.BlockSpec((None, tm, BRANCH_WIDTH), lambda i, n: (jnp.minimum(n, last), i, 0)),
                  pl.BlockSpec((None, None, D_MODEL, D_MODEL), lambda i, n: (li, n, 0, 0)),
                  pl.BlockSpec((None, BRANCH_WIDTH, D_MODEL), lambda i, n: (jnp.minimum(n, last), 0, 0)),
                  _mod_spec(li, 3, tm, is_ctx), _mod_spec(li, 4, tm, is_ctx),
                  pl.BlockSpec((1, D_MODEL), lambda i, n: (0, 0))],
        out_specs=[row, row],
        out_shape=[jax.ShapeDtypeStruct((T, D_MODEL), f32), jax.ShapeDtypeStruct((T, D_MODEL), bf16)],
        scratch_shapes=[pltpu.VMEM((tm, D_MODEL), f32)],
        compiler_params=_params("parallel", "arbitrary"),
        name="merge",
    )(x, mod, h, o_stack, w5, wb, mod, mod, g2)


def _mlp_kernel(x_ref, h_ref, gt_ref, wu_ref, wd_ref, gf_ref, out_ref, acc_ref, *, final_norm):
    j = pl.program_id(1)

    def chunk():
        a = jnp.maximum(_dot(h_ref[...], wu_ref[...]), 0.0)
        return _dot((a * a).astype(bf16), wd_ref[...])

    @pl.when(j == 0)
    def _():
        acc_ref[...] = chunk()

    @pl.when(j > 0)
    def _():
        acc_ref[...] += chunk()

    @pl.when(j == pl.num_programs(1) - 1)
    def _():
        y = x_ref[...] + gt_ref[...] * acc_ref[...]
        if final_norm:
            y = y * lax.rsqrt(jnp.mean(y * y, axis=-1, keepdims=True) + EPS) * gf_ref[...]
        out_ref[...] = y


def _mlp(x, h2, mod, w_up, w_down, g_final, li, is_ctx, final_norm):
    T = x.shape[0]
    tm, tf = TM_MLP, TF_MLP
    row = pl.BlockSpec((tm, D_MODEL), lambda i, j: (i, 0))
    return pl.pallas_call(
        functools.partial(_mlp_kernel, final_norm=final_norm),
        grid=(T // tm, D_FF // tf),
        in_specs=[row, row, _mod_spec(li, 5, tm, is_ctx),
                  pl.BlockSpec((None, None, D_MODEL, tf), lambda i, j: (li, j, 0, 0)),
                  pl.BlockSpec((None, tf, D_MODEL), lambda i, j: (li, j, 0)),
                  pl.BlockSpec((1, D_MODEL), lambda i, j: (0, 0))],
        out_specs=row,
        out_shape=jax.ShapeDtypeStruct((T, D_MODEL), f32),
        scratch_shapes=[pltpu.VMEM((tm, D_MODEL), f32)],
        compiler_params=_params("parallel", "arbitrary"),
        name="mlp",
    )(x, h2, mod, w_up, w_down, g_final)


def _ctx_attn_kernel(aq_ref, ak_ref, av_ref, bq_ref, bk_ref, bv_ref, cq_ref, ckv_ref,
                     dl_ref, dg_ref, sink_ref, o_ref, *, lam_init):
    S = CTX_SEQ
    lam = _diff_lambda(dl_ref, lam_init)
    out_a, out_b, out_c = [], [], []
    for s in range(aq_ref.shape[0] // S):
        rows = slice(s * S, (s + 1) * S)
        q = aq_ref[rows, :].astype(f32) * QK_SCALE2
        outs = [_diff_head(q, lambda t: ak_ref[rows, t * PAIR:(t + 1) * PAIR],
                           lambda h: av_ref[rows, h * A_V:(h + 1) * A_V], h, lam, dg_ref[...], lam_init)
                for h in range(A_HEADS)]
        out_a.append(jnp.concatenate(outs, axis=1))
        q = bq_ref[rows, :].astype(f32) * QK_SCALE2
        tiles = []
        for t in range(B_HEADS // 2):
            cols = slice(t * PAIR, (t + 1) * PAIR)
            accs = []
            for half in range(2):
                (e,), _ = _exp2_parts([_dot_nt(_head_q(q[:, cols], half), bk_ref[rows, cols])])
                accs.append(_dot(e.astype(bf16), _values_with_ones(bv_ref[rows, cols], half)))
            tiles.append(_pair_out(*accs))
        out_b.append(jnp.concatenate(tiles, axis=1))
        q = cq_ref[rows, :].astype(f32) * QK_SCALE2
        k_tile = ckv_ref[rows, 0:PAIR]
        v_variants = _swap_halves_variants(ckv_ref[rows, PAIR:2 * PAIR].astype(f32))
        tiles = []
        for t in range(C_HEADS // 2):
            q_tile = q[:, t * PAIR:(t + 1) * PAIR]
            q_swapped = pltpu.roll(q_tile, HEAD_DIM, 1)
            accs, extras = [], []
            for half in range(2):
                h = 2 * t + half
                g = h // C_GROUP
                sink2 = sink_ref[h] * LOG2E
                (e,), m = _exp2_parts([_dot_nt(_gqa_head_q(q_tile, q_swapped, half, g), k_tile)], sink2)
                accs.append(_dot(e.astype(bf16), v_variants[g * 2 + half]))
                extras.append(jnp.exp2(sink2 - m))
            tiles.append(_pair_out(accs[0], accs[1], extras[0], extras[1]))
        out_c.append(jnp.concatenate(tiles, axis=1))
    for k, outs in enumerate((out_a, out_b, out_c)):
        o_ref[k] = jnp.concatenate(outs, axis=0).astype(bf16)


def _ctx_attn(p, dl, dg, sink, li):
    S = CTX_SEQS_PER_STEP * CTX_SEQ
    nb = p.shape[0] // S
    lam_init = 0.8 - 0.6 * math.exp(-0.3 * li)

    def col(c):
        return pl.BlockSpec((S, 512), lambda b, c=c: (b, c))

    return pl.pallas_call(
        functools.partial(_ctx_attn_kernel, lam_init=lam_init),
        grid=(nb,),
        in_specs=[col(COL_AQ), col(COL_AK), col(COL_AV), col(COL_BQ), col(COL_BK), col(COL_BV), col(COL_CQ),
                  pl.BlockSpec((S, 256), lambda b: (b, COL_CKV_256)),
                  pl.BlockSpec((4, A_QK), lambda b: (0, 0)),
                  pl.BlockSpec((1, A_V), lambda b: (0, 0)),
                  pl.BlockSpec(memory_space=pltpu.SMEM)],
        out_specs=pl.BlockSpec((N_BRANCH - 1, S, BRANCH_WIDTH), lambda b: (0, b, 0)),
        out_shape=jax.ShapeDtypeStruct((N_BRANCH, p.shape[0], BRANCH_WIDTH), bf16),
        compiler_params=_params("parallel"),
        name="ctx_attn",
    )(p, p, p, p, p, p, p, p, dl, dg, sink)


def _lat_a_kernel(q_ref, k_ref, v_ref, ck_ref, cv_ref, cosq_ref, sinq_ref, cosk_ref, sink_ref,
                  dl_ref, dg_ref, o_ref, kk_ref, vv_ref, *, lam_init):
    L = LAT_SEQ

    @pl.when(pl.program_id(1) == 0)
    def _():
        kk_ref[0:L, :] = _rope(k_ref[...], cosk_ref[...], sink_ref[...]).astype(bf16)
        kk_ref[L:, :] = ck_ref[...].astype(bf16)
        vv_ref[0:L, :] = v_ref[...].astype(bf16)
        vv_ref[L:, :] = cv_ref[...].astype(bf16)

    lam = _diff_lambda(dl_ref, lam_init)
    q = _rope(q_ref[...], cosq_ref[...], sinq_ref[...]) * QK_SCALE2
    outs = [_diff_head(q, lambda t: kk_ref[:, t * PAIR:(t + 1) * PAIR], lambda h: vv_ref[:, h * A_V:(h + 1) * A_V],
                       h, lam, dg_ref[...], lam_init) for h in range(A_HEADS)]
    o_ref[...] = jnp.concatenate(outs, axis=1).astype(bf16)


def _lat_b_kernel(q_ref, k_ref, v_ref, ck_ref, cv_ref, bias_ref, _, o_ref, kk_ref, va_ref):
    L, W = LAT_SEQ, NA_KEY_ROWS * GRID_W
    qb = pl.program_id(1)

    @pl.when(qb == 0)
    def _():
        kk_ref[0:L, :] = k_ref[...]
        kk_ref[L:, :] = ck_ref[...].astype(bf16)
        for t in range(B_HEADS // 2):
            cols = slice(t * PAIR, (t + 1) * PAIR)
            for half in range(2):
                va_ref[2 * t + half, 0:L, :] = _values_with_ones(v_ref[:, cols], half)
                va_ref[2 * t + half, L:, :] = _values_with_ones(cv_ref[:, cols], half)

    start = pl.multiple_of(_na_key_row0(qb) * GRID_W, 256)
    q = q_ref[...].astype(f32) * QK_SCALE2
    tiles = []
    for t in range(B_HEADS // 2):
        cols = slice(t * PAIR, (t + 1) * PAIR)
        accs = []
        for half in range(2):
            h = 2 * t + half
            qh = _head_q(q[:, cols], half)
            s_loc = _dot_nt(qh, kk_ref[pl.ds(start, W), cols]) + bias_ref[h].astype(f32)
            s_ctx = _dot_nt(qh, kk_ref[L:, cols])
            (e_loc, e_ctx), _m = _exp2_parts([s_loc, s_ctx])
            accs.append(_dot(e_loc.astype(bf16), va_ref[h, pl.ds(start, W), :])
                        + _dot(e_ctx.astype(bf16), va_ref[h, L:, :]))
        tiles.append(_pair_out(*accs))
    o_ref[...] = jnp.concatenate(tiles, axis=1).astype(bf16)


def _lat_c_kernel(q_ref, kv_ref, ck_ref, cv_ref, cosq_ref, sinq_ref, cosk_ref, sink_ref, snk_ref,
                  _, o_ref, kk_ref, va_ref):
    L, W = LAT_SEQ, SWA_KEYS
    tq = q_ref.shape[0]
    qb = pl.program_id(1)

    @pl.when(qb == 0)
    def _():
        kk_ref[0:L, :] = _rope(kv_ref[:, 0:PAIR], cosk_ref[...], sink_ref[...]).astype(bf16)
        kk_ref[L:, :] = ck_ref[...].astype(bf16)
        lat = _swap_halves_variants(kv_ref[:, PAIR:2 * PAIR].astype(f32))
        ctx = _swap_halves_variants(cv_ref[...])
        for idx in range(4):
            va_ref[idx, 0:L, :] = lat[idx]
            va_ref[idx, L:, :] = ctx[idx]

    start = pl.multiple_of(jnp.clip(qb * tq - SWA_WINDOW, 0, L - W), 128)
    q = _rope(q_ref[...], cosq_ref[...], sinq_ref[...]) * QK_SCALE2
    qpos = qb * tq + lax.broadcasted_iota(jnp.int32, (tq, W), 0)
    kpos = start + lax.broadcasted_iota(jnp.int32, (tq, W), 1)
    valid = jnp.abs(qpos - kpos) <= SWA_WINDOW
    tiles = []
    for t in range(C_HEADS // 2):
        q_tile = q[:, t * PAIR:(t + 1) * PAIR]
        q_swapped = pltpu.roll(q_tile, HEAD_DIM, 1)
        accs, extras = [], []
        for half in range(2):
            h = 2 * t + half
            g = h // C_GROUP
            qh = _gqa_head_q(q_tile, q_swapped, half, g)
            s_loc = jnp.where(valid, _dot_nt(qh, kk_ref[pl.ds(start, W), :]), NEG_INF)
            s_ctx = _dot_nt(qh, kk_ref[L:, :])
            sink2 = snk_ref[h] * LOG2E
            (e_loc, e_ctx), m = _exp2_parts([s_loc, s_ctx], sink2)
            idx = g * 2 + half
            accs.append(_dot(e_loc.astype(bf16), va_ref[idx, pl.ds(start, W), :])
                        + _dot(e_ctx.astype(bf16), va_ref[idx, L:, :]))
            extras.append(jnp.exp2(sink2 - m))
        tiles.append(_pair_out(accs[0], accs[1], extras[0], extras[1]))
    o_ref[...] = jnp.concatenate(tiles, axis=1).astype(bf16)


def _lat_attn(p, caches, na_dense, dl, dg, sink, cos_t, sin_t, li):
    cak, cav, cbk, cbv, cck, ccv = caches
    L = LAT_SEQ
    assert TQ_ATTN == NA_Q_ROWS * GRID_W and SWA_KEYS >= TQ_ATTN_C + 2 * SWA_WINDOW
    nb_lat = p.shape[0] // L
    lam_init = 0.8 - 0.6 * math.exp(-0.3 * li)

    def qcol(c, tq=TQ_ATTN):
        return pl.BlockSpec((tq, 512), lambda b, i, c=c: (b * (L // tq) + i, c))

    def kcol(c, width=512):
        return pl.BlockSpec((L, width), lambda b, i, c=c: (b, c))

    def cache(width):
        return pl.BlockSpec((None, None, CTX_SEQ, width), lambda b, i: (b, li, 0, 0))

    tab_q = pl.BlockSpec((TQ_ATTN, 512), lambda b, i: (i, 0))
    tab_qc = pl.BlockSpec((TQ_ATTN_C, 512), lambda b, i: (i, 0))
    tab_k = pl.BlockSpec((L, 512), lambda b, i: (0, 0))
    tab_k128 = pl.BlockSpec((L, 128), lambda b, i: (0, 0))
    o_sds = jax.ShapeDtypeStruct((N_BRANCH, p.shape[0], BRANCH_WIDTH), bf16)
    anyspec = pl.BlockSpec(memory_space=pl.ANY)

    def o_spec(branch, tq=TQ_ATTN):
        return pl.BlockSpec((None, tq, BRANCH_WIDTH), lambda b, i: (branch, b * (L // tq) + i, 0))

    kv_scratch = [pltpu.VMEM((L + CTX_SEQ, 512), bf16), pltpu.VMEM((L + CTX_SEQ, 512), bf16)]
    smem = pl.BlockSpec(memory_space=pltpu.SMEM)
    cp = _params("parallel", "arbitrary")

    o_stack = pl.pallas_call(
        functools.partial(_lat_a_kernel, lam_init=lam_init),
        grid=(nb_lat, L // TQ_ATTN),
        in_specs=[qcol(COL_AQ), kcol(COL_AK), kcol(COL_AV), cache(512), cache(512),
                  tab_q, tab_q, tab_k, tab_k,
                  pl.BlockSpec((4, A_QK), lambda b, i: (0, 0)), pl.BlockSpec((1, A_V), lambda b, i: (0, 0))],
        out_specs=o_spec(0), out_shape=o_sds, scratch_shapes=kv_scratch,
        compiler_params=cp, name="lat_attn_a",
    )(p, p, p, cak, cav, cos_t, sin_t, cos_t, sin_t, dl, dg)

    o_stack = pl.pallas_call(
        _lat_b_kernel,
        grid=(nb_lat, L // TQ_ATTN),
        in_specs=[qcol(COL_BQ), kcol(COL_BK), kcol(COL_BV), cache(512), cache(512),
                  pl.BlockSpec((B_HEADS, TQ_ATTN, NA_KEY_ROWS * GRID_W), lambda b, i: (0, i, 0)), anyspec],
        out_specs=o_spec(1), out_shape=o_sds,
        scratch_shapes=[pltpu.VMEM((L + CTX_SEQ, 512), bf16), pltpu.VMEM((B_HEADS, L + CTX_SEQ, PAIR), bf16)],
        input_output_aliases={6: 0}, compiler_params=cp, name="lat_attn_b",
    )(p, p, p, cbk, cbv, na_dense, o_stack)

    return pl.pallas_call(
        _lat_c_kernel,
        grid=(nb_lat, L // TQ_ATTN_C),
        in_specs=[qcol(COL_CQ, TQ_ATTN_C), kcol(COL_CKV_256, 256), cache(PAIR), cache(PAIR),
                  tab_qc, tab_qc, tab_k128, tab_k128, smem, anyspec],
        out_specs=o_spec(2, TQ_ATTN_C), out_shape=o_sds,
        scratch_shapes=[pltpu.VMEM((L + CTX_SEQ, PAIR), bf16), pltpu.VMEM((4, L + CTX_SEQ, PAIR), bf16)],
        input_output_aliases={9: 0}, compiler_params=cp, name="lat_attn_c",
    )(p, p, cck, ccv, cos_t, sin_t, cos_t, sin_t, sink, o_stack)


def _na_key_row0(qb):
    lo, hi = NA_Q_ROWS * qb - NA_ROWS // 2, GRID_ROWS - NA_KEY_ROWS
    return min(max(lo, 0), hi) if isinstance(qb, int) else jnp.clip(lo, 0, hi)


def _na_bias_kernel(rb_ref, o_ref):
    h = pl.program_id(0)
    W = GRID_W
    n_dc = 2 * NA_COLS - 1
    n_dr = 2 * NA_ROWS - 1
    qc = lax.broadcasted_iota(jnp.int32, (W, 2 * W), 0)
    lane = lax.broadcasted_iota(jnp.int32, (W, 2 * W), 1)
    second = lane >= W
    kc = jnp.where(second, lane - W, lane)
    dc = jnp.clip(kc - qc, -(NA_COLS - 1), NA_COLS - 1) + NA_COLS - 1
    c0 = jnp.clip(qc - NA_COLS // 2, 0, W - NA_COLS)
    col_ok = (kc >= c0) & (kc < c0 + NA_COLS)
    base = h * (n_dr * n_dc)

    def pair_tile(dr0):
        t = jnp.zeros((W, 2 * W), f32)
        for j in range(n_dc):
            lo = rb_ref[base + dr0 * n_dc + j] if 0 <= dr0 < n_dr else 0.0
            hi = rb_ref[base + (dr0 + 1) * n_dc + j] if 0 <= dr0 + 1 < n_dr else 0.0
            t = jnp.where(dc == j, jnp.where(second, hi, lo), t)
        return jnp.where(col_ok, t * LOG2E, NEG_INF)

    tiles = {dr0: pair_tile(dr0) for dr0 in range(-1, n_dr)}
    neg = jnp.full((W, 2 * W), NEG_INF, f32)
    for qr in range(GRID_ROWS):
        r0 = min(max(qr - NA_ROWS // 2, 0), GRID_ROWS - NA_ROWS)
        k0 = _na_key_row0(qr // NA_Q_ROWS)
        assert k0 % 2 == 0 and k0 <= r0 and r0 + NA_ROWS <= k0 + NA_KEY_ROWS
        for pr in range(NA_KEY_ROWS // 2):
            kr = k0 + 2 * pr
            ok0 = r0 <= kr < r0 + NA_ROWS
            ok1 = r0 <= kr + 1 < r0 + NA_ROWS
            if not (ok0 or ok1):
                t = neg
            else:
                t = tiles[kr - qr + NA_ROWS - 1]
                if not ok0:
                    t = jnp.where(second, t, NEG_INF)
                if not ok1:
                    t = jnp.where(second, NEG_INF, t)
            o_ref[qr * W:(qr + 1) * W, 2 * pr * W:(2 * pr + 2) * W] = t.astype(bf16)


def _na_bias_dense(rel_bias):
    kw = NA_KEY_ROWS * GRID_W
    return pl.pallas_call(
        _na_bias_kernel,
        grid=(B_HEADS,),
        in_specs=[pl.BlockSpec(memory_space=pltpu.SMEM)],
        out_specs=pl.BlockSpec((None, LAT_SEQ, kw), lambda h: (h, 0, 0)),
        out_shape=jax.ShapeDtypeStruct((B_HEADS, LAT_SEQ, kw), bf16),
        compiler_params=_params("parallel"),
        name="na_bias",
    )(rel_bias.reshape(-1))


def _hy_filter_kernel(z_ref, w1_ref, b1_ref, w2_ref, b2_ref, fr_ref, w3f_ref, w3b_ref, dcf_ref, dcb_ref,
                      bias_ref, fc_ref, fs_ref, o_ref, *, L):
    z = z_ref[...]
    fr = fr_ref[...]
    h = jnp.sin(fr * (_dot_hi(z, w1_ref[...]) + b1_ref[...]))
    h = jnp.sin(fr * (_dot_hi(h, w2_ref[...]) + b2_ref[...]))
    t = z[:, 0:1]
    hf = _dot_hi(h, w3f_ref[...]) * jnp.exp(-t * jnp.abs(dcf_ref[...]))
    hb = _dot_hi(h, w3b_ref[...]) * jnp.exp(-t * jnp.abs(dcb_ref[...]))
    ssum = hf + hb
    bias = bias_ref[...]
    ga = _dot_3pass(fc_ref[...], ssum) + bias
    gb = _dot_3pass(fs_ref[...], hb - hf)
    g_nyq = _dot_hi(fs_ref[0:8, :], ssum)[0:1] + bias
    row0 = lax.broadcasted_iota(jnp.int32, ga.shape, 0) == 0
    inv = 1.0 / L
    o_ref[0] = jnp.where(row0, 0.5 * ga, ga) * inv
    o_ref[1] = jnp.where(row0, 0.0, gb) * inv
    o_ref[2] = jnp.where(row0, 0.5 * g_nyq, ga) * inv


def _hy_filter_tables(L, w1p, b1, w2, b2, w3, freq, decay, hy_bias):
    cb = HY_CB
    ncb = HY_WIDTH // cb
    z = jnp.asarray(_hyena_features(L))
    fc, fs = (jnp.asarray(a) for a in _dft_tables(L))
    full = lambda shape: pl.BlockSpec(shape, lambda o, c: (0,) * len(shape))
    fwd = lambda rows: pl.BlockSpec((rows, cb), lambda o, c: (0, o * 2 * ncb + c))
    bwd = lambda rows: pl.BlockSpec((rows, cb), lambda o, c: (0, o * 2 * ncb + ncb + c))
    return pl.pallas_call(
        functools.partial(_hy_filter_kernel, L=L),
        grid=(HY_ORDER, ncb),
        in_specs=[full((L, HY_EMB_PAD)), full((HY_EMB_PAD, HY_FFN)), full((1, HY_FFN)),
                  full((HY_FFN, HY_FFN)), full((1, HY_FFN)), full((1, HY_FFN)),
                  fwd(HY_FFN), bwd(HY_FFN), fwd(1), bwd(1),
                  pl.BlockSpec((None, 1, cb), lambda o, c: (o, 0, c)),
                  full((L, L)), full((L, L))],
        out_specs=pl.BlockSpec((None, 3, L, cb), lambda o, c: (o, 0, 0, c)),
        out_shape=jax.ShapeDtypeStruct((HY_ORDER, 3, L, HY_WIDTH), f32),
        compiler_params=_params("parallel", "parallel"),
        name="hy_filter",
    )(z, w1p, b1, w2, b2, freq, w3, w3, decay, decay, hy_bias.reshape(HY_ORDER, 1, HY_WIDTH), fc, fs)


def _hyena_kernel(v_ref, x1_ref, x2_ref, wv_ref, w1_ref, w2_ref, fwd_ref, inv_ref, tab_ref, _, o_ref):
    L = inv_ref.shape[0]
    cb = v_ref.shape[1]
    n_seq = v_ref.shape[0] // L
    row = lax.broadcasted_iota(jnp.int32, (L, cb), 0)

    def short_conv(u_ref, w_ref):
        w = w_ref[...]
        outs = []
        for s in range(n_seq):
            u = u_ref[s * L:(s + 1) * L, :].astype(f32)
            prev = jnp.where(row == 0, 0.0, pltpu.roll(u, 1, 0))
            nxt = jnp.where(row == L - 1, 0.0, pltpu.roll(u, L - 1, 0))
            outs.append(prev * w[0:1] + u * w[1:2] + nxt * w[2:3])
        return jnp.concatenate(outs, axis=1)

    def long_conv(u, order):
        ab = _dot(fwd_ref[...], u.astype(bf16))
        t0, t1, t2 = tab_ref[order, 0], tab_ref[order, 1], tab_ref[order, 2]
        pq = []
        for s in range(n_seq):
            a, b = ab[:L, s * cb:(s + 1) * cb], ab[L:, s * cb:(s + 1) * cb]
            pq.append(jnp.concatenate([(a * t0 + b * t1).astype(bf16), (b * t2 - a * t1).astype(bf16)], axis=0))
        return _dot(inv_ref[...], jnp.concatenate(pq, axis=1))

    z = short_conv(x1_ref, w1_ref) * long_conv(short_conv(v_ref, wv_ref), 0)
    y = (short_conv(x2_ref, w2_ref) * long_conv(z, 1)).astype(bf16)
    o_ref[...] = jnp.concatenate([y[:, s * cb:(s + 1) * cb] for s in range(n_seq)], axis=0)


def _hyena(p, o_stack, hy_short, tabs, L):
    cb = HY_CB
    ncb = HY_WIDTH // cb
    c0 = HY_COL0 // cb
    rows = min(HY_SEQS_PER_STEP[L] * L, p.shape[0])
    nb = p.shape[0] // rows
    fc, fs = _dft_tables(L)
    fwd = jnp.asarray(np.concatenate([fc, fs], axis=0), bf16)
    inv = jnp.asarray(np.concatenate([fc, fs.T], axis=1), bf16)

    def part(k):
        return pl.BlockSpec((rows, cb), lambda c, b, k=k: (b, c0 + k * ncb + c))

    def wpart(k):
        return pl.BlockSpec((3, cb), lambda c, b, k=k: (0, k * ncb + c))

    return pl.pallas_call(
        _hyena_kernel,
        grid=(ncb, nb),
        in_specs=[part(0), part(1), part(2), wpart(0), wpart(1), wpart(2),
                  pl.BlockSpec((2 * L, L), lambda c, b: (0, 0)), pl.BlockSpec((L, 2 * L), lambda c, b: (0, 0)),
                  pl.BlockSpec((HY_ORDER, 3, L, cb), lambda c, b: (0, 0, 0, c)),
                  pl.BlockSpec(memory_space=pl.ANY)],
        out_specs=pl.BlockSpec((None, rows, cb), lambda c, b: (N_BRANCH - 1, b, c)),
        out_shape=jax.ShapeDtypeStruct(o_stack.shape, o_stack.dtype),
        input_output_aliases={9: 0},
        compiler_params=_params("parallel", "arbitrary"),
        name="hyena_%d" % L,
    )(p, p, p, hy_short, hy_short, hy_short, fwd, inv, tabs, o_stack)


def kernel(x_prompt, x_sample, cache_a_k, cache_a_v, cache_b_k, cache_b_v, cache_c_k, cache_c_v, c, c_ctx, w_ada, b_ada, g_mix, w_in, diff_lambda, diff_norm_g, na_bias, swa_sink, hy_short, hy_w1, hy_b1, hy_w2, hy_b2, hy_w3, hy_freq, hy_decay, hy_bias, w_branch, w_out, g_mlp, w_up, w_down, g_final):
    nb_ctx, nb_lat = x_prompt.shape[0], x_sample.shape[0]
    assert x_prompt.shape[1:] == (CTX_SEQ, D_MODEL) and x_sample.shape[1:] == (LAT_SEQ, D_MODEL)
    assert nb_lat <= CTX_MOD_ROW and (nb_ctx * CTX_SEQ) % TM_INPROJ == 0

    x_ctx = x_prompt.reshape(nb_ctx * CTX_SEQ, D_MODEL)
    x_lat = x_sample.reshape(nb_lat * LAT_SEQ, D_MODEL)
    cv = jnp.zeros((N_MOD_ROWS, D_MODEL), f32).at[:nb_lat].set(c).at[CTX_MOD_ROW].set(c_ctx)
    mod = _adaln_all(cv, w_ada, b_ada).reshape(DEPTH * N_MOD_ROWS * 6, 1, D_MODEL)
    cos_t, sin_t = (jnp.asarray(a) for a in _rope_tables())
    caches = (cache_a_k.reshape(nb_lat, DEPTH, CTX_SEQ, 512), cache_a_v.reshape(nb_lat, DEPTH, CTX_SEQ, 512),
              cache_b_k.reshape(nb_lat, DEPTH, CTX_SEQ, 512), cache_b_v.reshape(nb_lat, DEPTH, CTX_SEQ, 512),
              cache_c_k.reshape(nb_lat, DEPTH, CTX_SEQ, 128), cache_c_v.reshape(nb_lat, DEPTH, CTX_SEQ, 128))

    w_mix = _cast_mix_weights(w_in)
    w5, wu, wd = _cast_layer_weights(w_in, w_out, w_up, w_down)
    kv = None
    for li in range(DEPTH):
        wb = w_branch[li].astype(bf16)
        g1 = g_mix[li].reshape(1, D_MODEL)
        g2 = g_mlp[li].reshape(1, D_MODEL)
        dl = diff_lambda[li]
        dg = diff_norm_g[li].reshape(1, A_V)
        sink = swa_sink[li]

        w1p = jnp.pad(hy_w1[li], ((0, HY_EMB_PAD - HY_EMB), (0, 0)))
        hy_args = (w1p, hy_b1[li].reshape(1, HY_FFN), hy_w2[li], hy_b2[li].reshape(1, HY_FFN), hy_w3[li],
                   hy_freq[li].reshape(1, HY_FFN), hy_decay[li].reshape(1, -1), hy_bias[li])
        gf = g_final.reshape(1, D_MODEL)
        final = li == DEPTH - 1

        p, kv, h = _inproj(x_ctx, mod, g1, w_mix, li, True, kv)
        o_stack = _ctx_attn(p, dl, dg, sink, li)
        o_stack = _hyena(p, o_stack, hy_short[li], _hy_filter_tables(CTX_SEQ, *hy_args), CTX_SEQ)
        x_ctx, h2 = _merge(x_ctx, mod, h, o_stack, w5, wb, g2, li, True)
        x_ctx = _mlp(x_ctx, h2, mod, wu, wd, gf, li, True, final_norm=final)

        p, h = _inproj(x_lat, mod, g1, w_mix, li, False)
        o_stack = _lat_attn(p, caches, _na_bias_dense(na_bias[li]), dl, dg, sink, cos_t, sin_t, li)
        o_stack = _hyena(p, o_stack, hy_short[li], _hy_filter_tables(LAT_SEQ, *hy_args), LAT_SEQ)
        x_lat, h2 = _merge(x_lat, mod, h, o_stack, w5, wb, g2, li, False)
        x_lat = _mlp(x_lat, h2, mod, wu, wd, gf, li, False, final_norm=final)

    y_prompt = x_ctx.reshape(nb_ctx, CTX_SEQ, D_MODEL)
    y_sample = x_lat.reshape(nb_lat, LAT_SEQ, D_MODEL)
    kv_shapes = ((2, A_HEADS, A_QK), (A_HEADS, A_V), (B_HEADS, HEAD_DIM), (B_HEADS, HEAD_DIM),
                 (C_KV_HEADS, HEAD_DIM), (C_KV_HEADS, HEAD_DIM))
    new_kv = tuple(a.reshape((nb_ctx, DEPTH, CTX_SEQ) + s) for a, s in zip(kv, kv_shapes))
    return (y_prompt, y_sample) + new_kv
```

```python
import functools
import math

import numpy as np
import jax
import jax.numpy as jnp
from jax import lax
from jax.experimental import pallas as pl
from jax.experimental.pallas import tpu as pltpu

f32 = jnp.float32
bf16 = jnp.bfloat16

D_MODEL = 2048
DEPTH = 2
CTX_SEQ = 256
LAT_SEQ = 1024
GRID_W = 64
GRID_ROWS = LAT_SEQ // GRID_W
BRANCH_WIDTH = 512
N_BRANCH = 4
HEAD_DIM = 64
A_QK = 64
A_V = 128
A_HEADS = 4
B_HEADS = 8
NA_ROWS = 8
NA_COLS = 16
C_HEADS = 8
C_KV_HEADS = 2
C_GROUP = 4
SWA_WINDOW = 128
SWA_KEYS = 512
NA_Q_ROWS = 8
NA_KEY_ROWS = 12
HY_WIDTH = 512
HY_ORDER = 2
HY_BANDS = 16
HY_EMB = 1 + 2 * HY_BANDS
HY_EMB_PAD = 128
HY_FFN = 64
D_FF = 4 * D_MODEL
ROPE_BASE = 10000.0
EPS = 1e-6
NEG_INF = -1e30
MIX_COLS = 5376
COL_AQ, COL_BQ, COL_CQ, COL_AK, COL_AV, COL_BK, COL_BV = 0, 1, 2, 3, 4, 5, 6
COL_CKV_256 = 14
KV_PARTS = ((1536, 2048), (2048, 2560), (2560, 3072), (3072, 3584), (3584, 3712), (3712, 3840))
HY_COL0 = 3840
N_MOD_ROWS = 16
CTX_MOD_ROW = 8

VMEM_LIMIT = 56 * 1024 * 1024
TM_INPROJ = 256
TM_MERGE = 512
TM_MLP = 512
TF_MLP = 1024
TQ_ATTN = 512
TQ_ATTN_C = 256
CTX_SEQS_PER_STEP = 1
TN_ADA = 2048
HY_CB = 256
HY_SEQS_PER_STEP = {CTX_SEQ: 8, LAT_SEQ: 2}
CAST_ROWS = 1024


def _params(*sem):
    return pltpu.CompilerParams(dimension_semantics=sem, vmem_limit_bytes=VMEM_LIMIT)


def _dot(a, b):
    return jnp.dot(a, b, preferred_element_type=f32)


def _dot_nt(a, b):
    return lax.dot_general(a, b, (((1,), (1,)), ((), ())), preferred_element_type=f32)


def _dot_hi(a, b):
    return jnp.dot(a, b, preferred_element_type=f32, precision=lax.Precision.HIGHEST)


def _dot_3pass(a, b):
    a_hi, b_hi = a.astype(bf16), b.astype(bf16)
    a_lo = (a - a_hi.astype(f32)).astype(bf16)
    b_lo = (b - b_hi.astype(f32)).astype(bf16)
    return _dot(a_hi, b_hi) + (_dot(a_hi, b_lo) + _dot(a_lo, b_hi))


@functools.lru_cache(maxsize=None)
def _rope_tables():
    half = HEAD_DIM // 2
    nf = half // 2
    inv = ROPE_BASE ** (-np.arange(nf, dtype=np.float64) / nf)
    t = np.arange(LAT_SEQ)
    pos = np.stack([t // GRID_W, t % GRID_W], axis=1).astype(np.float64)
    lane = np.arange(HEAD_DIM)
    ang = pos[:, lane // half] * inv[lane % nf][None, :]
    first = (lane % half) < nf
    cos = np.cos(ang)
    sin = np.where(first[None, :], -np.sin(ang), np.sin(ang))
    reps = 512 // HEAD_DIM
    return (np.tile(cos, (1, reps)).astype(np.float32), np.tile(sin, (1, reps)).astype(np.float32))


@functools.lru_cache(maxsize=None)
def _dft_tables(L):
    f = np.arange(L, dtype=np.int64)
    prod = (f[:, None] * f[None, :]) % (2 * L)
    ang = np.pi * prod.astype(np.float64) / L
    fc = np.cos(ang)
    fs = np.sin(ang)
    fs[0, :] = np.where(f % 2 == 0, 1.0, -1.0)
    return fc.astype(np.float32), fs.astype(np.float32)


@functools.lru_cache(maxsize=None)
def _hyena_features(L):
    n = np.arange(L, dtype=np.float64)[:, None]
    t = n / max(L - 1, 1)
    w = 2.0 * math.pi * n / L
    bands = np.linspace(1e-4, HY_BANDS - 1, HY_BANDS, dtype=np.float64)[None, :]
    z = np.concatenate([t, np.cos(bands * w), -np.sin(bands * w)], axis=-1)
    z = np.pad(z, ((0, 0), (0, HY_EMB_PAD - HY_EMB)))
    return z.astype(np.float32)


def _modulated_norm(x, g, scale, shift):
    ms = jnp.mean(x * x, axis=-1, keepdims=True)
    return x * lax.rsqrt(ms + EPS) * (g * (1.0 + scale)) + shift


def _rope(x, cos, sin_signed):
    x = x.astype(f32)
    n = x.shape[-1]
    lane = lax.broadcasted_iota(jnp.int32, x.shape, 1)
    first = (lane & (HEAD_DIM // 2 - 1)) < (HEAD_DIM // 4)
    partner = jnp.where(first, pltpu.roll(x, n - HEAD_DIM // 4, 1), pltpu.roll(x, HEAD_DIM // 4, 1))
    return x * cos + partner * sin_signed


LOG2E = 1.4426950408889634
QK_SCALE2 = HEAD_DIM ** -0.5 * LOG2E
PAIR = 2 * HEAD_DIM


def _lo_half(shape):
    return lax.broadcasted_iota(jnp.int32, shape, 1) < HEAD_DIM


def _keep_half(x, half, fill):
    lo = _lo_half(x.shape)
    return jnp.where(lo if half == 0 else jnp.logical_not(lo), x, fill)


def _head_q(q_tile, half):
    return _keep_half(q_tile, half, 0.0).astype(bf16)


def _values_with_ones(v_tile, half):
    return _keep_half(v_tile.astype(f32), half, 1.0).astype(bf16)


def _exp2_parts(scores, sink2=None):
    m = jnp.max(scores[0], axis=-1, keepdims=True)
    for s in scores[1:]:
        m = jnp.maximum(m, jnp.max(s, axis=-1, keepdims=True))
    if sink2 is not None:
        m = jnp.maximum(m, sink2)
    return [jnp.exp2(s - m) for s in scores], m


def _pair_out(acc_even, acc_odd, extra_even=None, extra_odd=None):
    lo = _lo_half(acc_even.shape)
    num = jnp.where(lo, acc_even, acc_odd)
    den = pltpu.roll(jnp.where(lo, acc_odd, acc_even), HEAD_DIM, 1)
    if extra_even is not None:
        den = den + jnp.where(lo, extra_even, extra_odd)
    return num / den


def _swap_halves_variants(v_tile):
    sw = pltpu.roll(v_tile, HEAD_DIM, 1)
    lo = _lo_half(v_tile.shape)
    tiles = (jnp.where(lo, v_tile, 1.0), jnp.where(lo, 1.0, sw), jnp.where(lo, sw, 1.0), jnp.where(lo, 1.0, v_tile))
    return [t.astype(bf16) for t in tiles]


def _gqa_head_q(q_tile, q_swapped, half, g):
    return _keep_half(q_tile if half == g else q_swapped, g, 0.0).astype(bf16)


def _diff_lambda(dl_ref, lam_init):
    dl = dl_ref[...]
    a = jnp.sum(dl[0:1] * dl[1:2], axis=-1, keepdims=True)
    b = jnp.sum(dl[2:3] * dl[3:4], axis=-1, keepdims=True)
    return jnp.exp(a) - jnp.exp(b) + lam_init


def _diff_head(q, k_of, v_of, h, lam, dg, lam_init):
    res = []
    for c in range(2):
        t = c * (A_HEADS // 2) + h // 2
        s = _dot_nt(_head_q(q[:, t * PAIR:(t + 1) * PAIR], h % 2), k_of(t))
        (e,), _ = _exp2_parts([s])
        res.append((_dot(e.astype(bf16), v_of(h)), jnp.sum(e, axis=-1, keepdims=True)))
    o = res[0][0] * (1.0 / res[0][1]) - res[1][0] * (lam / res[1][1])
    o = o * lax.rsqrt(jnp.mean(o * o, axis=-1, keepdims=True) + EPS) * dg
    return o * (1.0 - lam_init)


def _cast_kernel(src_ref, *rest):
    o_ref = rest[-1]
    x = src_ref[...]
    o_ref[...] = x.reshape(o_ref.shape).astype(bf16)


def _cast_call(name, grid, src, in_spec, out_spec, out_shape, dst=None):
    in_specs, args, aliases = [in_spec], [src], {}
    if dst is not None:
        in_specs.append(pl.BlockSpec(memory_space=pl.ANY))
        args.append(dst)
        aliases = {1: 0}
    return pl.pallas_call(
        _cast_kernel, grid=grid, in_specs=in_specs, out_specs=out_spec,
        out_shape=jax.ShapeDtypeStruct(out_shape, bf16), input_output_aliases=aliases,
        compiler_params=_params(*(("parallel",) * len(grid))), name=name,
    )(*args)


def _cast_mix_kernel(unit_ref, src_ref, o_ref):
    o_ref[...] = src_ref[0].astype(bf16)


def _cast_mix_weights(w_in):
    U = 256
    src_unit = (0, 1, 6, 7, 12, 13, 2, 3, 4, 5, 8, 9, 10, 11) + tuple(range(14, MIX_COLS // U))
    return pl.pallas_call(
        _cast_mix_kernel,
        grid_spec=pltpu.PrefetchScalarGridSpec(
            num_scalar_prefetch=1, grid=(DEPTH, len(src_unit)),
            in_specs=[pl.BlockSpec((pl.Element(1), pl.Element(D_MODEL), pl.Element(U)),
                                   lambda l, u, unit: (l, 0, pl.multiple_of(unit[u] * U, U)))],
            out_specs=pl.BlockSpec((None, D_MODEL, U), lambda l, u, unit: (l, 0, u))),
        out_shape=jax.ShapeDtypeStruct((DEPTH, D_MODEL, MIX_COLS), bf16),
        compiler_params=_params("parallel", "parallel"), name="cast_mix",
    )(jnp.asarray(src_unit, jnp.int32), w_in)


def _cast_layer_weights(w_in, w_out, w_up, w_down):
    D, R, tf = D_MODEL, CAST_ROWS, TF_MLP
    nr = D // R
    w5 = _cast_call(
        "cast_gate", (DEPTH, N_BRANCH, nr), w_in,
        pl.BlockSpec((pl.Element(1), pl.Element(R), pl.Element(D)),
                     lambda l, n, r: (l, pl.multiple_of(r * R, R), pl.multiple_of(MIX_COLS + n * D, 128))),
        pl.BlockSpec((None, None, R, D), lambda l, n, r: (l, n, r, 0)), (DEPTH, N_BRANCH + 1, D, D))
    w5 = _cast_call(
        "cast_wout", (DEPTH, nr), w_out,
        pl.BlockSpec((None, R, D), lambda l, r: (l, r, 0)),
        pl.BlockSpec((None, None, R, D), lambda l, r: (l, N_BRANCH, r, 0)), (DEPTH, N_BRANCH + 1, D, D), dst=w5)
    wu = _cast_call(
        "cast_wu", (DEPTH, D_FF // tf, nr), w_up,
        pl.BlockSpec((None, R, tf), lambda l, j, r: (l, r, j)),
        pl.BlockSpec((None, None, R, tf), lambda l, j, r: (l, j, r, 0)), (DEPTH, D_FF // tf, D, tf))
    wd = _cast_call(
        "cast_wd", (DEPTH, D_FF // R), w_down,
        pl.BlockSpec((None, R, D), lambda l, r: (l, r, 0)),
        pl.BlockSpec((None, R, D), lambda l, r: (l, r, 0)), (DEPTH, D_FF, D))
    return w5, wu, wd


def _ada_kernel(cv_ref, w_ref, b_ref, o_ref):
    cv = cv_ref[...]
    s = (cv * jax.nn.sigmoid(cv)).astype(bf16)
    o_ref[...] = _dot(s, w_ref[...].astype(bf16)) + b_ref[...]


def _adaln_all(cv, w_ada, b_ada):
    n6 = 6 * D_MODEL
    return pl.pallas_call(
        _ada_kernel,
        grid=(DEPTH, n6 // TN_ADA),
        in_specs=[pl.BlockSpec((N_MOD_ROWS, D_MODEL), lambda l, j: (0, 0)),
                  pl.BlockSpec((None, D_MODEL, TN_ADA), lambda l, j: (l, 0, j)),
                  pl.BlockSpec((None, 1, TN_ADA), lambda l, j: (l, 0, j))],
        out_specs=pl.BlockSpec((None, N_MOD_ROWS, TN_ADA), lambda l, j: (l, 0, j)),
        out_shape=jax.ShapeDtypeStruct((DEPTH, N_MOD_ROWS, n6), f32),
        compiler_params=_params("parallel", "parallel"),
        name="adaln",
    )(cv, w_ada, b_ada.reshape(DEPTH, 1, n6))


def _mod_spec(li, k, tm, is_ctx):
    def index(i, *_):
        r = CTX_MOD_ROW if is_ctx else (i * tm) // LAT_SEQ
        return ((li * N_MOD_ROWS + r) * 6 + k, 0, 0)

    return pl.BlockSpec((None, 1, D_MODEL), index)


def _inproj_kernel(x_ref, sh_ref, sc_ref, g_ref, w_ref, *rest, emit_kv):
    outs = rest[-(2 + len(KV_PARTS)) if emit_kv else -2:]
    h = _modulated_norm(x_ref[...], g_ref[...], sc_ref[...], sh_ref[...]).astype(bf16)
    outs[-1][...] = h
    res = _dot(h, w_ref[...])
    outs[0][...] = res.astype(bf16)
    if emit_kv:
        for ref, (lo, hi) in zip(outs[1:-1], KV_PARTS):
            ref[...] = res[:, lo:hi]


def _inproj(x, mod, g, w, li, is_ctx, kv_prev=None):
    T = x.shape[0]
    tm = TM_INPROJ
    assert tm == CTX_SEQ
    row = lambda width: pl.BlockSpec((tm, width), lambda i: (i, 0))
    in_specs = [row(D_MODEL), _mod_spec(li, 0, tm, is_ctx), _mod_spec(li, 1, tm, is_ctx),
                pl.BlockSpec((1, D_MODEL), lambda i: (0, 0)),
                pl.BlockSpec((None, D_MODEL, MIX_COLS), lambda i: (li, 0, 0), pipeline_mode=pl.Buffered(1))]
    args = [x, mod, mod, g, w]
    out_specs = [row(MIX_COLS), row(D_MODEL)]
    out_shape = [jax.ShapeDtypeStruct((T, MIX_COLS), bf16), jax.ShapeDtypeStruct((T, D_MODEL), bf16)]
    aliases = {}
    if is_ctx:
        for k, (lo, hi) in enumerate(KV_PARTS):
            out_specs.insert(1 + k, pl.BlockSpec((None, None, CTX_SEQ, hi - lo), lambda i: (i, li, 0, 0)))
            out_shape.insert(1 + k, jax.ShapeDtypeStruct((T // CTX_SEQ, DEPTH, CTX_SEQ, hi - lo), f32))
        if kv_prev is not None:
            for k, a in enumerate(kv_prev):
                in_specs.append(pl.BlockSpec(memory_space=pl.ANY))
                aliases[len(args)] = 1 + k
                args.append(a)
    outs = pl.pallas_call(
        functools.partial(_inproj_kernel, emit_kv=is_ctx),
        grid=(T // tm,),
        in_specs=in_specs,
        out_specs=out_specs,
        out_shape=out_shape,
        input_output_aliases=aliases,
        compiler_params=_params("parallel"),
        name="inproj",
    )(*args)
    return (outs[0], tuple(outs[1:-1]), outs[-1]) if is_ctx else (outs[0], outs[-1])


def _merge_kernel(x_ref, gt_ref, h_ref, o_ref, w_ref, wb_ref, sh2_ref, sc2_ref, g2_ref, out_ref, h2_ref, acc_ref):
    n = pl.program_id(1)

    def contribution():
        return jax.nn.sigmoid(_dot(h_ref[...], w_ref[...])) * _dot(o_ref[...], wb_ref[...])

    @pl.when(n == 0)
    def _():
        acc_ref[...] = contribution()

    @pl.when((n > 0) & (n < N_BRANCH))
    def _():
        acc_ref[...] += contribution()

    @pl.when(n == N_BRANCH)
    def _():
        half = x_ref.shape[0] // 2
        for r in range(2):
            rows = slice(r * half, (r + 1) * half)
            y = x_ref[rows, :] + gt_ref[...] * _dot(acc_ref[rows, :].astype(bf16), w_ref[...])
            out_ref[rows, :] = y
            h2_ref[rows, :] = _modulated_norm(y, g2_ref[...], sc2_ref[...], sh2_ref[...]).astype(bf16)


def _merge(x, mod, h, o_stack, w5, wb, g2, li, is_ctx):
    T = x.shape[0]
    tm = TM_MERGE
    last = N_BRANCH - 1
    row = pl.BlockSpec((tm, D_MODEL), lambda i, n: (i, 0))
    return pl.pallas_call(
        _merge_kernel,
        grid=(T // tm, N_BRANCH + 1),
        in_specs=[row, _mod_spec(li, 2, tm, is_ctx), row,
                  pl.BlockSpec((None, tm, BRANCH_WIDTH), lambda i, n: (jnp.minimum(n, last), i, 0)),
                  pl.BlockSpec((None, None, D_MODEL, D_MODEL), lambda i, n: (li, n, 0, 0)),
                  pl.BlockSpec((None, BRANCH_WIDTH, D_MODEL), lambda i, n: (jnp.minimum(n, last), 0, 0)),
                  _mod_spec(li, 3, tm, is_ctx), _mod_spec(li, 4, tm, is_ctx),
                  pl.BlockSpec((1, D_MODEL), lambda i, n: (0, 0))],
        out_specs=[row, row],
        out_shape=[jax.ShapeDtypeStruct((T, D_MODEL), f32), jax.ShapeDtypeStruct((T, D_MODEL), bf16)],
        scratch_shapes=[pltpu.VMEM((tm, D_MODEL), f32)],
        compiler_params=_params("parallel", "arbitrary"),
        name="merge",
    )(x, mod, h, o_stack, w5, wb, mod, mod, g2)


def _mlp_kernel(x_ref, h_ref, gt_ref, wu_ref, wd_ref, gf_ref, out_ref, acc_ref, *, final_norm):
    j = pl.program_id(1)

    def chunk():
        a = jnp.maximum(_dot(h_ref[...], wu_ref[...]), 0.0)
        return _dot((a * a).astype(bf16), wd_ref[...])

    @pl.when(j == 0)
    def _():
        acc_ref[...] = chunk()

    @pl.when(j > 0)
    def _():
        acc_ref[...] += chunk()

    @pl.when(j == pl.num_programs(1) - 1)
    def _():
        y = x_ref[...] + gt_ref[...] * acc_ref[...]
        if final_norm:
            y = y * lax.rsqrt(jnp.mean(y * y, axis=-1, keepdims=True) + EPS) * gf_ref[...]
        out_ref[...] = y


def _mlp(x, h2, mod, w_up, w_down, g_final, li, is_ctx, final_norm):
    T = x.shape[0]
    tm, tf = TM_MLP, TF_MLP
    row = pl.BlockSpec((tm, D_MODEL), lambda i, j: (i, 0))
    return pl.pallas_call(
        functools.partial(_mlp_kernel, final_norm=final_norm),
        grid=(T // tm, D_FF // tf),
        in_specs=[row, row, _mod_spec(li, 5, tm, is_ctx),
                  pl.BlockSpec((None, None, D_MODEL, tf), lambda i, j: (li, j, 0, 0)),
                  pl.BlockSpec((None, tf, D_MODEL), lambda i, j: (li, j, 0)),
                  pl.BlockSpec((1, D_MODEL), lambda i, j: (0, 0))],
        out_specs=row,
        out_shape=jax.ShapeDtypeStruct((T, D_MODEL), f32),
        scratch_shapes=[pltpu.VMEM((tm, D_MODEL), f32)],
        compiler_params=_params("parallel", "arbitrary"),
        name="mlp",
    )(x, h2, mod, w_up, w_down, g_final)


def _ctx_attn_kernel(aq_ref, ak_ref, av_ref, bq_ref, bk_ref, bv_ref, cq_ref, ckv_ref,
                     dl_ref, dg_ref, sink_ref, o_ref, *, lam_init):
    S = CTX_SEQ
    lam = _diff_lambda(dl_ref, lam_init)
    out_a, out_b, out_c = [], [], []
    for s in range(aq_ref.shape[0] // S):
        rows = slice(s * S, (s + 1) * S)
        q = aq_ref[rows, :].astype(f32) * QK_SCALE2
        outs = [_diff_head(q, lambda t: ak_ref[rows, t * PAIR:(t + 1) * PAIR],
                           lambda h: av_ref[rows, h * A_V:(h + 1) * A_V], h, lam, dg_ref[...], lam_init)
                for h in range(A_HEADS)]
        out_a.append(jnp.concatenate(outs, axis=1))
        q = bq_ref[rows, :].astype(f32) * QK_SCALE2
        tiles = []
        for t in range(B_HEADS // 2):
            cols = slice(t * PAIR, (t + 1) * PAIR)
            accs = []
            for half in range(2):
                (e,), _ = _exp2_parts([_dot_nt(_head_q(q[:, cols], half), bk_ref[rows, cols])])
                accs.append(_dot(e.astype(bf16), _values_with_ones(bv_ref[rows, cols], half)))
            tiles.append(_pair_out(*accs))
        out_b.append(jnp.concatenate(tiles, axis=1))
        q = cq_ref[rows, :].astype(f32) * QK_SCALE2
        k_tile = ckv_ref[rows, 0:PAIR]
        v_variants = _swap_halves_variants(ckv_ref[rows, PAIR:2 * PAIR].astype(f32))
        tiles = []
        for t in range(C_HEADS // 2):
            q_tile = q[:, t * PAIR:(t + 1) * PAIR]
            q_swapped = pltpu.roll(q_tile, HEAD_DIM, 1)
            accs, extras = [], []
            for half in range(2):
                h = 2 * t + half
                g = h // C_GROUP
                sink2 = sink_ref[h] * LOG2E
                (e,), m = _exp2_parts([_dot_nt(_gqa_head_q(q_tile, q_swapped, half, g), k_tile)], sink2)
                accs.append(_dot(e.astype(bf16), v_variants[g * 2 + half]))
                extras.append(jnp.exp2(sink2 - m))
            tiles.append(_pair_out(accs[0], accs[1], extras[0], extras[1]))
        out_c.append(jnp.concatenate(tiles, axis=1))
    for k, outs in enumerate((out_a, out_b, out_c)):
        o_ref[k] = jnp.concatenate(outs, axis=0).astype(bf16)


def _ctx_attn(p, dl, dg, sink, li):
    S = CTX_SEQS_PER_STEP * CTX_SEQ
    nb = p.shape[0] // S
    lam_init = 0.8 - 0.6 * math.exp(-0.3 * li)

    def col(c):
        return pl.BlockSpec((S, 512), lambda b, c=c: (b, c))

    return pl.pallas_call(
        functools.partial(_ctx_attn_kernel, lam_init=lam_init),
        grid=(nb,),
        in_specs=[col(COL_AQ), col(COL_AK), col(COL_AV), col(COL_BQ), col(COL_BK), col(COL_BV), col(COL_CQ),
                  pl.BlockSpec((S, 256), lambda b: (b, COL_CKV_256)),
                  pl.BlockSpec((4, A_QK), lambda b: (0, 0)),
                  pl.BlockSpec((1, A_V), lambda b: (0, 0)),
                  pl.BlockSpec(memory_space=pltpu.SMEM)],
        out_specs=pl.BlockSpec((N_BRANCH - 1, S, BRANCH_WIDTH), lambda b: (0, b, 0)),
        out_shape=jax.ShapeDtypeStruct((N_BRANCH, p.shape[0], BRANCH_WIDTH), bf16),
        compiler_params=_params("parallel"),
        name="ctx_attn",
    )(p, p, p, p, p, p, p, p, dl, dg, sink)


def _lat_a_kernel(q_ref, k_ref, v_ref, ck_ref, cv_ref, cosq_ref, sinq_ref, cosk_ref, sink_ref,
                  dl_ref, dg_ref, o_ref, kk_ref, vv_ref, *, lam_init):
    L = LAT_SEQ

    @pl.when(pl.program_id(1) == 0)
    def _():
        kk_ref[0:L, :] = _rope(k_ref[...], cosk_ref[...], sink_ref[...]).astype(bf16)
        kk_ref[L:, :] = ck_ref[...].astype(bf16)
        vv_ref[0:L, :] = v_ref[...].astype(bf16)
        vv_ref[L:, :] = cv_ref[...].astype(bf16)

    lam = _diff_lambda(dl_ref, lam_init)
    q = _rope(q_ref[...], cosq_ref[...], sinq_ref[...]) * QK_SCALE2
    outs = [_diff_head(q, lambda t: kk_ref[:, t * PAIR:(t + 1) * PAIR], lambda h: vv_ref[:, h * A_V:(h + 1) * A_V],
                       h, lam, dg_ref[...], lam_init) for h in range(A_HEADS)]
    o_ref[...] = jnp.concatenate(outs, axis=1).astype(bf16)


def _lat_b_kernel(q_ref, k_ref, v_ref, ck_ref, cv_ref, bias_ref, _, o_ref, kk_ref, va_ref):
    L, W = LAT_SEQ, NA_KEY_ROWS * GRID_W
    qb = pl.program_id(1)

    @pl.when(qb == 0)
    def _():
        kk_ref[0:L, :] = k_ref[...]
        kk_ref[L:, :] = ck_ref[...].astype(bf16)
        for t in range(B_HEADS // 2):
            cols = slice(t * PAIR, (t + 1) * PAIR)
            for half in range(2):
                va_ref[2 * t + half, 0:L, :] = _values_with_ones(v_ref[:, cols], half)
                va_ref[2 * t + half, L:, :] = _values_with_ones(cv_ref[:, cols], half)

    start = pl.multiple_of(_na_key_row0(qb) * GRID_W, 256)
    q = q_ref[...].astype(f32) * QK_SCALE2
    tiles = []
    for t in range(B_HEADS // 2):
        cols = slice(t * PAIR, (t + 1) * PAIR)
        accs = []
        for half in range(2):
            h = 2 * t + half
            qh = _head_q(q[:, cols], half)
            s_loc = _dot_nt(qh, kk_ref[pl.ds(start, W), cols]) + bias_ref[h].astype(f32)
            s_ctx = _dot_nt(qh, kk_ref[L:, cols])
            (e_loc, e_ctx), _m = _exp2_parts([s_loc, s_ctx])
            accs.append(_dot(e_loc.astype(bf16), va_ref[h, pl.ds(start, W), :])
                        + _dot(e_ctx.astype(bf16), va_ref[h, L:, :]))
        tiles.append(_pair_out(*accs))
    o_ref[...] = jnp.concatenate(tiles, axis=1).astype(bf16)


def _lat_c_kernel(q_ref, kv_ref, ck_ref, cv_ref, cosq_ref, sinq_ref, cosk_ref, sink_ref, snk_ref,
                  _, o_ref, kk_ref, va_ref):
    L, W = LAT_SEQ, SWA_KEYS
    tq = q_ref.shape[0]
    qb = pl.program_id(1)

    @pl.when(qb == 0)
    def _():
        kk_ref[0:L, :] = _rope(kv_ref[:, 0:PAIR], cosk_ref[...], sink_ref[...]).astype(bf16)
        kk_ref[L:, :] = ck_ref[...].astype(bf16)
        lat = _swap_halves_variants(kv_ref[:, PAIR:2 * PAIR].astype(f32))
        ctx = _swap_halves_variants(cv_ref[...])
        for idx in range(4):
            va_ref[idx, 0:L, :] = lat[idx]
            va_ref[idx, L:, :] = ctx[idx]

    start = pl.multiple_of(jnp.clip(qb * tq - SWA_WINDOW, 0, L - W), 128)
    q = _rope(q_ref[...], cosq_ref[...], sinq_ref[...]) * QK_SCALE2
    qpos = qb * tq + lax.broadcasted_iota(jnp.int32, (tq, W), 0)
    kpos = start + lax.broadcasted_iota(jnp.int32, (tq, W), 1)
    valid = jnp.abs(qpos - kpos) <= SWA_WINDOW
    tiles = []
    for t in range(C_HEADS // 2):
        q_tile = q[:, t * PAIR:(t + 1) * PAIR]
        q_swapped = pltpu.roll(q_tile, HEAD_DIM, 1)
        accs, extras = [], []
        for half in range(2):
            h = 2 * t + half
            g = h // C_GROUP
            qh = _gqa_head_q(q_tile, q_swapped, half, g)
            s_loc = jnp.where(valid, _dot_nt(qh, kk_ref[pl.ds(start, W), :]), NEG_INF)
            s_ctx = _dot_nt(qh, kk_ref[L:, :])
            sink2 = snk_ref[h] * LOG2E
            (e_loc, e_ctx), m = _exp2_parts([s_loc, s_ctx], sink2)
            idx = g * 2 + half
            accs.append(_dot(e_loc.astype(bf16), va_ref[idx, pl.ds(start, W), :])
                        + _dot(e_ctx.astype(bf16), va_ref[idx, L:, :]))
            extras.append(jnp.exp2(sink2 - m))
        tiles.append(_pair_out(accs[0], accs[1], extras[0], extras[1]))
    o_ref[...] = jnp.concatenate(tiles, axis=1).astype(bf16)


def _lat_attn(p, caches, na_dense, dl, dg, sink, cos_t, sin_t, li):
    cak, cav, cbk, cbv, cck, ccv = caches
    L = LAT_SEQ
    assert TQ_ATTN == NA_Q_ROWS * GRID_W and SWA_KEYS >= TQ_ATTN_C + 2 * SWA_WINDOW
    nb_lat = p.shape[0] // L
    lam_init = 0.8 - 0.6 * math.exp(-0.3 * li)

    def qcol(c, tq=TQ_ATTN):
        return pl.BlockSpec((tq, 512), lambda b, i, c=c: (b * (L // tq) + i, c))

    def kcol(c, width=512):
        return pl.BlockSpec((L, width), lambda b, i, c=c: (b, c))

    def cache(width):
        return pl.BlockSpec((None, None, CTX_SEQ, width), lambda b, i: (b, li, 0, 0))

    tab_q = pl.BlockSpec((TQ_ATTN, 512), lambda b, i: (i, 0))
    tab_qc = pl.BlockSpec((TQ_ATTN_C, 512), lambda b, i: (i, 0))
    tab_k = pl.BlockSpec((L, 512), lambda b, i: (0, 0))
    tab_k128 = pl.BlockSpec((L, 128), lambda b, i: (0, 0))
    o_sds = jax.ShapeDtypeStruct((N_BRANCH, p.shape[0], BRANCH_WIDTH), bf16)
    anyspec = pl.BlockSpec(memory_space=pl.ANY)

    def o_spec(branch, tq=TQ_ATTN):
        return pl.BlockSpec((None, tq, BRANCH_WIDTH), lambda b, i: (branch, b * (L // tq) + i, 0))

    kv_scratch = [pltpu.VMEM((L + CTX_SEQ, 512), bf16), pltpu.VMEM((L + CTX_SEQ, 512), bf16)]
    smem = pl.BlockSpec(memory_space=pltpu.SMEM)
    cp = _params("parallel", "arbitrary")

    o_stack = pl.pallas_call(
        functools.partial(_lat_a_kernel, lam_init=lam_init),
        grid=(nb_lat, L // TQ_ATTN),
        in_specs=[qcol(COL_AQ), kcol(COL_AK), kcol(COL_AV), cache(512), cache(512),
                  tab_q, tab_q, tab_k, tab_k,
                  pl.BlockSpec((4, A_QK), lambda b, i: (0, 0)), pl.BlockSpec((1, A_V), lambda b, i: (0, 0))],
        out_specs=o_spec(0), out_shape=o_sds, scratch_shapes=kv_scratch,
        compiler_params=cp, name="lat_attn_a",
    )(p, p, p, cak, cav, cos_t, sin_t, cos_t, sin_t, dl, dg)

    o_stack = pl.pallas_call(
        _lat_b_kernel,
        grid=(nb_lat, L // TQ_ATTN),
        in_specs=[qcol(COL_BQ), kcol(COL_BK), kcol(COL_BV), cache(512), cache(512),
                  pl.BlockSpec((B_HEADS, TQ_ATTN, NA_KEY_ROWS * GRID_W), lambda b, i: (0, i, 0)), anyspec],
        out_specs=o_spec(1), out_shape=o_sds,
        scratch_shapes=[pltpu.VMEM((L + CTX_SEQ, 512), bf16), pltpu.VMEM((B_HEADS, L + CTX_SEQ, PAIR), bf16)],
        input_output_aliases={6: 0}, compiler_params=cp, name="lat_attn_b",
    )(p, p, p, cbk, cbv, na_dense, o_stack)

    return pl.pallas_call(
        _lat_c_kernel,
        grid=(nb_lat, L // TQ_ATTN_C),
        in_specs=[qcol(COL_CQ, TQ_ATTN_C), kcol(COL_CKV_256, 256), cache(PAIR), cache(PAIR),
                  tab_qc, tab_qc, tab_k128, tab_k128, smem, anyspec],
        out_specs=o_spec(2, TQ_ATTN_C), out_shape=o_sds,
        scratch_shapes=[pltpu.VMEM((L + CTX_SEQ, PAIR), bf16), pltpu.VMEM((4, L + CTX_SEQ, PAIR), bf16)],
        input_output_aliases={9: 0}, compiler_params=cp, name="lat_attn_c",
    )(p, p, cck, ccv, cos_t, sin_t, cos_t, sin_t, sink, o_stack)


def _na_key_row0(qb):
    lo, hi = NA_Q_ROWS * qb - NA_ROWS // 2, GRID_ROWS - NA_KEY_ROWS
    return min(max(lo, 0), hi) if isinstance(qb, int) else jnp.clip(lo, 0, hi)


def _na_bias_kernel(rb_ref, o_ref):
    h = pl.program_id(0)
    W = GRID_W
    n_dc = 2 * NA_COLS - 1
    n_dr = 2 * NA_ROWS - 1
    qc = lax.broadcasted_iota(jnp.int32, (W, 2 * W), 0)
    lane = lax.broadcasted_iota(jnp.int32, (W, 2 * W), 1)
    second = lane >= W
    kc = jnp.where(second, lane - W, lane)
    dc = jnp.clip(kc - qc, -(NA_COLS - 1), NA_COLS - 1) + NA_COLS - 1
    c0 = jnp.clip(qc - NA_COLS // 2, 0, W - NA_COLS)
    col_ok = (kc >= c0) & (kc < c0 + NA_COLS)
    base = h * (n_dr * n_dc)

    def pair_tile(dr0):
        t = jnp.zeros((W, 2 * W), f32)
        for j in range(n_dc):
            lo = rb_ref[base + dr0 * n_dc + j] if 0 <= dr0 < n_dr else 0.0
            hi = rb_ref[base + (dr0 + 1) * n_dc + j] if 0 <= dr0 + 1 < n_dr else 0.0
            t = jnp.where(dc == j, jnp.where(second, hi, lo), t)
        return jnp.where(col_ok, t * LOG2E, NEG_INF)

    tiles = {dr0: pair_tile(dr0) for dr0 in range(-1, n_dr)}
    neg = jnp.full((W, 2 * W), NEG_INF, f32)
    for qr in range(GRID_ROWS):
        r0 = min(max(qr - NA_ROWS // 2, 0), GRID_ROWS - NA_ROWS)
        k0 = _na_key_row0(qr // NA_Q_ROWS)
        assert k0 % 2 == 0 and k0 <= r0 and r0 + NA_ROWS <= k0 + NA_KEY_ROWS
        for pr in range(NA_KEY_ROWS // 2):
            kr = k0 + 2 * pr
            ok0 = r0 <= kr < r0 + NA_ROWS
            ok1 = r0 <= kr + 1 < r0 + NA_ROWS
            if not (ok0 or ok1):
                t = neg
            else:
                t = tiles[kr - qr + NA_ROWS - 1]
                if not ok0:
                    t = jnp.where(second, t, NEG_INF)
                if not ok1:
                    t = jnp.where(second, NEG_INF, t)
            o_ref[qr * W:(qr + 1) * W, 2 * pr * W:(2 * pr + 2) * W] = t.astype(bf16)


def _na_bias_dense(rel_bias):
    kw = NA_KEY_ROWS * GRID_W
    return pl.pallas_call(
        _na_bias_kernel,
        grid=(B_HEADS,),
        in_specs=[pl.BlockSpec(memory_space=pltpu.SMEM)],
        out_specs=pl.BlockSpec((None, LAT_SEQ, kw), lambda h: (h, 0, 0)),
        out_shape=jax.ShapeDtypeStruct((B_HEADS, LAT_SEQ, kw), bf16),
        compiler_params=_params("parallel"),
        name="na_bias",
    )(rel_bias.reshape(-1))


def _hy_filter_kernel(z_ref, w1_ref, b1_ref, w2_ref, b2_ref, fr_ref, w3f_ref, w3b_ref, dcf_ref, dcb_ref,
                      bias_ref, fc_ref, fs_ref, o_ref, *, L):
    z = z_ref[...]
    fr = fr_ref[...]
    h = jnp.sin(fr * (_dot_hi(z, w1_ref[...]) + b1_ref[...]))
    h = jnp.sin(fr * (_dot_hi(h, w2_ref[...]) + b2_ref[...]))
    t = z[:, 0:1]
    hf = _dot_hi(h, w3f_ref[...]) * jnp.exp(-t * jnp.abs(dcf_ref[...]))
    hb = _dot_hi(h, w3b_ref[...]) * jnp.exp(-t * jnp.abs(dcb_ref[...]))
    ssum = hf + hb
    bias = bias_ref[...]
    ga = _dot_3pass(fc_ref[...], ssum) + bias
    gb = _dot_3pass(fs_ref[...], hb - hf)
    g_nyq = _dot_hi(fs_ref[0:8, :], ssum)[0:1] + bias
    row0 = lax.broadcasted_iota(jnp.int32, ga.shape, 0) == 0
    inv = 1.0 / L
    o_ref[0] = jnp.where(row0, 0.5 * ga, ga) * inv
    o_ref[1] = jnp.where(row0, 0.0, gb) * inv
    o_ref[2] = jnp.where(row0, 0.5 * g_nyq, ga) * inv


def _hy_filter_tables(L, w1p, b1, w2, b2, w3, freq, decay, hy_bias):
    cb = HY_CB
    ncb = HY_WIDTH // cb
    z = jnp.asarray(_hyena_features(L))
    fc, fs = (jnp.asarray(a) for a in _dft_tables(L))
    full = lambda shape: pl.BlockSpec(shape, lambda o, c: (0,) * len(shape))
    fwd = lambda rows: pl.BlockSpec((rows, cb), lambda o, c: (0, o * 2 * ncb + c))
    bwd = lambda rows: pl.BlockSpec((rows, cb), lambda o, c: (0, o * 2 * ncb + ncb + c))
    return pl.pallas_call(
        functools.partial(_hy_filter_kernel, L=L),
        grid=(HY_ORDER, ncb),
        in_specs=[full((L, HY_EMB_PAD)), full((HY_EMB_PAD, HY_FFN)), full((1, HY_FFN)),
                  full((HY_FFN, HY_FFN)), full((1, HY_FFN)), full((1, HY_FFN)),
                  fwd(HY_FFN), bwd(HY_FFN), fwd(1), bwd(1),
                  pl.BlockSpec((None, 1, cb), lambda o, c: (o, 0, c)),
                  full((L, L)), full((L, L))],
        out_specs=pl.BlockSpec((None, 3, L, cb), lambda o, c: (o, 0, 0, c)),
        out_shape=jax.ShapeDtypeStruct((HY_ORDER, 3, L, HY_WIDTH), f32),
        compiler_params=_params("parallel", "parallel"),
        name="hy_filter",
    )(z, w1p, b1, w2, b2, freq, w3, w3, decay, decay, hy_bias.reshape(HY_ORDER, 1, HY_WIDTH), fc, fs)


def _hyena_kernel(v_ref, x1_ref, x2_ref, wv_ref, w1_ref, w2_ref, fwd_ref, inv_ref, tab_ref, _, o_ref):
    L = inv_ref.shape[0]
    cb = v_ref.shape[1]
    n_seq = v_ref.shape[0] // L
    row = lax.broadcasted_iota(jnp.int32, (L, cb), 0)

    def short_conv(u_ref, w_ref):
        w = w_ref[...]
        outs = []
        for s in range(n_seq):
            u = u_ref[s * L:(s + 1) * L, :].astype(f32)
            prev = jnp.where(row == 0, 0.0, pltpu.roll(u, 1, 0))
            nxt = jnp.where(row == L - 1, 0.0, pltpu.roll(u, L - 1, 0))
            outs.append(prev * w[0:1] + u * w[1:2] + nxt * w[2:3])
        return jnp.concatenate(outs, axis=1)

    def long_conv(u, order):
        ab = _dot(fwd_ref[...], u.astype(bf16))
        t0, t1, t2 = tab_ref[order, 0], tab_ref[order, 1], tab_ref[order, 2]
        pq = []
        for s in range(n_seq):
            a, b = ab[:L, s * cb:(s + 1) * cb], ab[L:, s * cb:(s + 1) * cb]
            pq.append(jnp.concatenate([(a * t0 + b * t1).astype(bf16), (b * t2 - a * t1).astype(bf16)], axis=0))
        return _dot(inv_ref[...], jnp.concatenate(pq, axis=1))

    z = short_conv(x1_ref, w1_ref) * long_conv(short_conv(v_ref, wv_ref), 0)
    y = (short_conv(x2_ref, w2_ref) * long_conv(z, 1)).astype(bf16)
    o_ref[...] = jnp.concatenate([y[:, s * cb:(s + 1) * cb] for s in range(n_seq)], axis=0)


def _hyena(p, o_stack, hy_short, tabs, L):
    cb = HY_CB
    ncb = HY_WIDTH // cb
    c0 = HY_COL0 // cb
    rows = min(HY_SEQS_PER_STEP[L] * L, p.shape[0])
    nb = p.shape[0] // rows
    fc, fs = _dft_tables(L)
    fwd = jnp.asarray(np.concatenate([fc, fs], axis=0)).astype(bf16)
    inv = jnp.asarray(np.concatenate([fc, fs.T], axis=1)).astype(bf16)

    def part(k):
        return pl.BlockSpec((rows, cb), lambda c, b, k=k: (b, c0 + k * ncb + c))

    def wpart(k):
        return pl.BlockSpec((3, cb), lambda c, b, k=k: (0, k * ncb + c))

    return pl.pallas_call(
        _hyena_kernel,
        grid=(ncb, nb),
        in_specs=[part(0), part(1), part(2), wpart(0), wpart(1), wpart(2),
                  pl.BlockSpec((2 * L, L), lambda c, b: (0, 0)), pl.BlockSpec((L, 2 * L), lambda c, b: (0, 0)),
                  pl.BlockSpec((HY_ORDER, 3, L, cb), lambda c, b: (0, 0, 0, c)),
                  pl.BlockSpec(memory_space=pl.ANY)],
        out_specs=pl.BlockSpec((None, rows, cb), lambda c, b: (N_BRANCH - 1, b, c)),
        out_shape=jax.ShapeDtypeStruct(o_stack.shape, o_stack.dtype),
        input_output_aliases={9: 0},
        compiler_params=_params("parallel", "arbitrary"),
        name="hyena_%d" % L,
    )(p, p, p, hy_short, hy_short, hy_short, fwd, inv, tabs, o_stack)


def kernel(x_prompt, x_sample, cache_a_k, cache_a_v, cache_b_k, cache_b_v, cache_c_k, cache_c_v, c, c_ctx, w_ada, b_ada, g_mix, w_in, diff_lambda, diff_norm_g, na_bias, swa_sink, hy_short, hy_w1, hy_b1, hy_w2, hy_b2, hy_w3, hy_freq, hy_decay, hy_bias, w_branch, w_out, g_mlp, w_up, w_down, g_final):
    nb_ctx, nb_lat = x_prompt.shape[0], x_sample.shape[0]
    assert x_prompt.shape[1:] == (CTX_SEQ, D_MODEL) and x_sample.shape[1:] == (LAT_SEQ, D_MODEL)
    assert nb_lat <= CTX_MOD_ROW and (nb_ctx * CTX_SEQ) % TM_INPROJ == 0

    x_ctx = x_prompt.reshape(nb_ctx * CTX_SEQ, D_MODEL)
    x_lat = x_sample.reshape(nb_lat * LAT_SEQ, D_MODEL)
    cv = jnp.zeros((N_MOD_ROWS, D_MODEL), f32).at[:nb_lat].set(c).at[CTX_MOD_ROW].set(c_ctx)
    mod = _adaln_all(cv, w_ada, b_ada).reshape(DEPTH * N_MOD_ROWS * 6, 1, D_MODEL)
    cos_t, sin_t = (jnp.asarray(a) for a in _rope_tables())
    caches = (cache_a_k.reshape(nb_lat, DEPTH, CTX_SEQ, 512), cache_a_v.reshape(nb_lat, DEPTH, CTX_SEQ, 512),
              cache_b_k.reshape(nb_lat, DEPTH, CTX_SEQ, 512), cache_b_v.reshape(nb_lat, DEPTH, CTX_SEQ, 512),
              cache_c_k.reshape(nb_lat, DEPTH, CTX_SEQ, 128), cache_c_v.reshape(nb_lat, DEPTH, CTX_SEQ, 128))

    w_mix = _cast_mix_weights(w_in)
    w5, wu, wd = _cast_layer_weights(w_in, w_out, w_up, w_down)
    kv = None
    for li in range(DEPTH):
        wb = w_branch[li].astype(bf16)
        g1 = g_mix[li].reshape(1, D_MODEL)
        g2 = g_mlp[li].reshape(1, D_MODEL)
        dl = diff_lambda[li]
        dg = diff_norm_g[li].reshape(1, A_V)
        sink = swa_sink[li]

        w1p = jnp.pad(hy_w1[li], ((0, HY_EMB_PAD - HY_EMB), (0, 0)))
        hy_args = (w1p, hy_b1[li].reshape(1, HY_FFN), hy_w2[li], hy_b2[li].reshape(1, HY_FFN), hy_w3[li],
                   hy_freq[li].reshape(1, HY_FFN), hy_decay[li].reshape(1, -1), hy_bias[li])
        gf = g_final.reshape(1, D_MODEL)
        final = li == DEPTH - 1

        p, kv, h = _inproj(x_ctx, mod, g1, w_mix, li, True, kv)
        o_stack = _ctx_attn(p, dl, dg, sink, li)
        o_stack = _hyena(p, o_stack, hy_short[li], _hy_filter_tables(CTX_SEQ, *hy_args), CTX_SEQ)
        x_ctx, h2 = _merge(x_ctx, mod, h, o_stack, w5, wb, g2, li, True)
        x_ctx = _mlp(x_ctx, h2, mod, wu, wd, gf, li, True, final_norm=final)

        p, h = _inproj(x_lat, mod, g1, w_mix, li, False)
        o_stack = _lat_attn(p, caches, _na_bias_dense(na_bias[li]), dl, dg, sink, cos_t, sin_t, li)
        o_stack = _hyena(p, o_stack, hy_short[li], _hy_filter_tables(LAT_SEQ, *hy_args), LAT_SEQ)
        x_lat, h2 = _merge(x_lat, mod, h, o_stack, w5, wb, g2, li, False)
        x_lat = _mlp(x_lat, h2, mod, wu, wd, gf, li, False, final_norm=final)

    y_prompt = x_ctx.reshape(nb_ctx, CTX_SEQ, D_MODEL)
    y_sample = x_lat.reshape(nb_lat, LAT_SEQ, D_MODEL)
    kv_shapes = ((2, A_HEADS, A_QK), (A_HEADS, A_V), (B_HEADS, HEAD_DIM), (B_HEADS, HEAD_DIM),
                 (C_KV_HEADS, HEAD_DIM), (C_KV_HEADS, HEAD_DIM))
    new_kv = tuple(a.reshape((nb_ctx, DEPTH, CTX_SEQ) + s) for a, s in zip(kv, kv_shapes))
    return (y_prompt, y_sample) + new_kv
```
